```python
import math
import jax
import jax.numpy as jnp
from jax import lax
import numpy as np


D_MODEL = 2048
BATCH = 2
SEQ = 4096
DEPTH = 4

N_MIXERS = 4
N_SSD = (DEPTH + 3) // 4
N_FOX = (DEPTH + 2) // 4
N_MLA = (DEPTH + 1) // 4
N_S5 = DEPTH // 4
ALPHA = (2.0 * DEPTH) ** 0.25
BETA = (8.0 * DEPTH) ** -0.25
LN_EPS = 1e-5
RMS_EPS = 1e-6
Q_BLOCK = 128

SSD_EXPAND = 2
SSD_D_INNER = SSD_EXPAND * D_MODEL
SSD_HEADDIM = 64
SSD_HEADS = SSD_D_INNER // SSD_HEADDIM
SSD_GROUPS = 8
SSD_STATE = 128
SSD_CONV = 4
SSD_CHUNK = 128
SSD_CONV_DIM = SSD_D_INNER + 2 * SSD_GROUPS * SSD_STATE
SSD_IN_DIM = SSD_D_INNER + SSD_CONV_DIM + SSD_HEADS
DT_PROJ_SCALE = 0.1

FOX_HEADS = 16
FOX_HEAD_DIM = D_MODEL // FOX_HEADS
FOX_WIDTH = FOX_HEADS * FOX_HEAD_DIM
FOX_IN_DIM = 4 * FOX_WIDTH + FOX_HEADS
FGATE_PROJ_SCALE = 0.1

MLA_HEADS = 16
MLA_Q_RANK = 512
MLA_KV_RANK = 512
MLA_NOPE = 128
MLA_ROPE = 64
MLA_V = 128
MLA_WIDTH = MLA_HEADS * MLA_V
MLA_IN_DIM = MLA_Q_RANK + MLA_KV_RANK + MLA_ROPE + MLA_WIDTH
ROPE_BASE = 10000.0

S5_WIDTH = D_MODEL
S5_GROUP = 16
S5_GROUPS = S5_WIDTH // S5_GROUP
S5_STATE = 64
S5_IN_DIM = 2 * S5_WIDTH

kernel_name = 'hybrid_ssd_fox_mla_s5_deepnorm'


def layer_norm(x, g, b):
    xf = x.astype(jnp.float32)
    mu = jnp.mean(xf, -1, keepdims=True)
    var = jnp.mean(jnp.square(xf - mu), -1, keepdims=True)
    return ((xf - mu) * lax.rsqrt(var + LN_EPS) * g.astype(jnp.float32) + b.astype(jnp.float32)).astype(x.dtype)


def rms_norm(x, w):
    xf = x.astype(jnp.float32)
    y = xf * lax.rsqrt(jnp.mean(jnp.square(xf), -1, keepdims=True) + RMS_EPS)
    return (y * w.astype(jnp.float32)).astype(x.dtype)


def causal_depthwise_conv(x, w, bias):
    k = w.shape[0]
    y = lax.conv_general_dilated(x, w[:, None, :].astype(x.dtype), window_strides=(1,), padding=[(k - 1, 0)], dimension_numbers=('NWC', 'WIO', 'NWC'), feature_group_count=x.shape[-1])
    return y + bias.astype(x.dtype)


def rope_cos_sin(positions, dim):
    inv_freq = ROPE_BASE ** (-jnp.arange(0, dim, 2, dtype=jnp.float32) / dim)
    ang = positions.astype(jnp.float32)[..., None] * inv_freq
    return jnp.cos(ang), jnp.sin(ang)


def apply_rope(x, cos, sin):
    xf = x.astype(jnp.float32)
    x1, x2 = jnp.split(xf, 2, axis=-1)
    return jnp.concatenate([x1 * cos - x2 * sin, x1 * sin + x2 * cos], axis=-1).astype(x.dtype)


def causal_block_attention(q, k, v, scale, cum=None):
    bsz, T, H, dq = q.shape
    nblk = T // Q_BLOCK
    key_pos = jnp.arange(T)
    q_blocks = jnp.moveaxis(q.reshape(bsz, nblk, Q_BLOCK, H, dq), 1, 0)
    starts = jnp.arange(nblk) * Q_BLOCK
    if cum is None:
        xs = (q_blocks, starts)
    else:
        cum = cum.astype(jnp.float32)
        cum_k = jnp.swapaxes(cum, 1, 2)
        xs = (q_blocks, starts, jnp.moveaxis(cum.reshape(bsz, nblk, Q_BLOCK, H), 1, 0))

    def attend(blk):
        qb, s0 = blk[0], blk[1]
        logits = jnp.einsum('bqhd,bkhd->bhqk', qb, k, preferred_element_type=jnp.float32) * scale
        if cum is not None:
            cq = jnp.swapaxes(blk[2], 1, 2)
            logits = logits + cq[..., :, None] - cum_k[:, :, None, :]
        query_pos = s0 + jnp.arange(Q_BLOCK)
        logits = jnp.where(key_pos[None, :] <= query_pos[:, None], logits, -jnp.inf)
        probs = jax.nn.softmax(logits, axis=-1)
        return jnp.einsum('bhqk,bkhd->bqhd', probs.astype(v.dtype), v)

    out = lax.map(attend, xs)
    return jnp.moveaxis(out, 0, 1).reshape(bsz, T, H, v.shape[-1])


def ssd_mixer(h, w_in, conv_w, conv_b, dt_bias, a_log, d_skip, norm_w, w_out):
    bsz, T, _ = h.shape
    G, E, P, N, Q = SSD_GROUPS, SSD_HEADS // SSD_GROUPS, SSD_HEADDIM, SSD_STATE, SSD_CHUNK
    nc = T // Q
    f32 = jnp.float32
    z, xbc, dt = jnp.split(h @ w_in, [SSD_D_INNER, SSD_D_INNER + SSD_CONV_DIM], axis=-1)
    xbc = jax.nn.silu(causal_depthwise_conv(xbc, conv_w, conv_b)).astype(f32)
    xs, bm, cm = jnp.split(xbc, [SSD_D_INNER, SSD_D_INNER + G * N], axis=-1)
    x = xs.reshape(bsz, nc, Q, G, E, P)
    bm = bm.reshape(bsz, nc, Q, G, N)
    cm = cm.reshape(bsz, nc, Q, G, N)
    dt = jax.nn.softplus(dt.astype(f32) + dt_bias.astype(f32)).reshape(bsz, nc, Q, G, E)
    a = -jnp.exp(a_log.astype(f32)).reshape(G, E)
    xdt = x * dt[..., None]
    acs = jnp.moveaxis(jnp.cumsum(dt * a, axis=2), 2, -1)
    idx = jnp.arange(Q)
    seg = acs[..., :, None] - acs[..., None, :]
    decay = jnp.exp(jnp.where(idx[:, None] >= idx[None, :], seg, -jnp.inf))
    cb = jnp.einsum('bclgn,bcsgn->bcgls', cm, bm)
    y_diag = jnp.einsum('bcgls,bcgels,bcsgep->bclgep', cb, decay, xdt)
    decay_to_end = jnp.exp(acs[..., -1:] - acs)
    states = jnp.einsum('bclgn,bcgel,bclgep->bcgepn', bm, decay_to_end, xdt)
    chunk_decay = jnp.exp(acs[..., -1])

    def carry_state(s, inp):
        st, dec = inp
        return s * dec[..., None, None] + st, s

    _, prev = lax.scan(carry_state, jnp.zeros_like(states[:, 0]), (jnp.moveaxis(states, 1, 0), jnp.moveaxis(chunk_decay, 1, 0)))
    prev = jnp.moveaxis(prev, 0, 1)
    y_off = jnp.einsum('bclgn,bcgepn,bcgel->bclgep', cm, prev, jnp.exp(acs))
    y = y_diag + y_off + d_skip.astype(f32).reshape(G, E, 1) * x
    y = y.reshape(bsz, T, SSD_D_INNER) * jax.nn.silu(z.astype(f32))
    yg = y.reshape(bsz, T, G, SSD_D_INNER // G)
    yg = yg * lax.rsqrt(jnp.mean(jnp.square(yg), -1, keepdims=True) + RMS_EPS)
    y = yg.reshape(bsz, T, SSD_D_INNER) * norm_w.astype(f32)
    return y.astype(h.dtype) @ w_out


def fox_mixer(h, w_in, f_bias, w_out):
    bsz, T, _ = h.shape
    W = FOX_WIDTH
    q, k, v, z, f = jnp.split(h @ w_in, [W, 2 * W, 3 * W, 4 * W], axis=-1)
    shp = (bsz, T, FOX_HEADS, FOX_HEAD_DIM)
    log_f = jax.nn.log_sigmoid(f.astype(jnp.float32) + f_bias.astype(jnp.float32))
    cum = jnp.cumsum(log_f, axis=1)
    o = causal_block_attention(q.reshape(shp), k.reshape(shp), v.reshape(shp), FOX_HEAD_DIM ** -0.5, cum)
    y = o.reshape(bsz, T, W) * jax.nn.silu(z)
    return y @ w_out


def mla_mixer(h, positions, w_in, q_norm, kv_norm, w_q_up, w_kv_up, w_out):
    bsz, T, _ = h.shape
    H = MLA_HEADS
    q_lat, kv_lat, k_pe, z = jnp.split(h @ w_in, [MLA_Q_RANK, MLA_Q_RANK + MLA_KV_RANK, MLA_Q_RANK + MLA_KV_RANK + MLA_ROPE], axis=-1)
    q = (rms_norm(q_lat, q_norm) @ w_q_up).reshape(bsz, T, H, MLA_NOPE + MLA_ROPE)
    kv = (rms_norm(kv_lat, kv_norm) @ w_kv_up).reshape(bsz, T, H, MLA_NOPE + MLA_V)
    cos, sin = rope_cos_sin(positions, MLA_ROPE)
    q_pe = apply_rope(q[..., MLA_NOPE:], cos[:, :, None, :], sin[:, :, None, :])
    k_pe = apply_rope(k_pe, cos, sin)
    q = jnp.concatenate([q[..., :MLA_NOPE], q_pe], axis=-1)
    k = jnp.concatenate([kv[..., :MLA_NOPE], jnp.broadcast_to(k_pe[:, :, None, :], (bsz, T, H, MLA_ROPE))], axis=-1)
    o = causal_block_attention(q, k, kv[..., MLA_NOPE:], (MLA_NOPE + MLA_ROPE) ** -0.5)
    y = o.reshape(bsz, T, MLA_WIDTH) * jax.nn.silu(z)
    return y @ w_out


def _diag_linear_combine(left, right):
    a_l, b_l = left
    a_r, b_r = right
    return a_r * a_l, a_r * b_l + b_r


def s5_mixer(h, w_in, lam_re, lam_im, log_step, b_re, b_im, c_re, c_im, d_skip, w_glu, b_glu, w_out):
    bsz, T, _ = h.shape
    f32 = jnp.float32
    u, z = jnp.split(h @ w_in, 2, axis=-1)
    u = u.astype(f32)
    lam = lax.complex(lam_re.astype(f32), lam_im.astype(f32))
    step = jnp.exp(log_step.astype(f32))[:, None]
    lam_bar = jnp.exp(lam * step)
    b_bar = ((lam_bar - 1.0) / lam)[..., None] * lax.complex(b_re.astype(f32), b_im.astype(f32))
    ug = u.reshape(bsz, T, S5_GROUPS, S5_GROUP).astype(jnp.complex64)
    bu = jnp.einsum('gpi,btgi->btgp', b_bar, ug)
    a_seq = jnp.broadcast_to(lam_bar, (1, T, S5_GROUPS, S5_STATE))
    _, states = lax.associative_scan(_diag_linear_combine, (a_seq, bu), axis=1)
    c = lax.complex(c_re.astype(f32), c_im.astype(f32))
    y = jnp.einsum('gip,btgp->btgi', c, states).real.reshape(bsz, T, S5_WIDTH) + d_skip.astype(f32) * u
    y = jax.nn.gelu(y)
    y = y * jax.nn.sigmoid(y @ w_glu.astype(f32) + b_glu.astype(f32))
    y = y * jax.nn.silu(z.astype(f32))
    return y.astype(h.dtype) @ w_out


def _normal(key, shape, std):
    return jax.random.normal(key, shape, jnp.float32) * std


def setup_inputs(seed: int = 0) -> dict:
    key = jax.random.key(seed)
    keys = iter(jax.random.split(key, 48))
    D = D_MODEL
    x = _normal(next(keys), (BATCH, SEQ, D), 1.0)
    positions = (jax.random.randint(next(keys), (BATCH, 1), 0, 1024) + jnp.arange(SEQ)[None, :]).astype(jnp.int32)
    ln_g = 1.0 + _normal(next(keys), (DEPTH, D), 0.02)
    ln_b = _normal(next(keys), (DEPTH, D), 0.02)
    L = N_SSD
    ssd_cols = jnp.concatenate([jnp.ones((SSD_IN_DIM - SSD_HEADS,), jnp.float32), jnp.full((SSD_HEADS,), DT_PROJ_SCALE, jnp.float32)])
    ssd_w_in = _normal(next(keys), (L, D, SSD_IN_DIM), D ** -0.5) * ssd_cols
    ssd_conv_w = _normal(next(keys), (L, SSD_CONV, SSD_CONV_DIM), SSD_CONV ** -0.5)
    ssd_conv_b = _normal(next(keys), (L, SSD_CONV_DIM), 0.02)
    dt0 = jnp.exp(jax.random.uniform(next(keys), (L, SSD_HEADS), jnp.float32, math.log(1e-3), math.log(1e-1)))
    ssd_dt_bias = dt0 + jnp.log(-jnp.expm1(-dt0))
    ssd_a_log = jnp.log(jax.random.uniform(next(keys), (L, SSD_HEADS), jnp.float32, 1.0, 16.0))
    ssd_d = 1.0 + _normal(next(keys), (L, SSD_HEADS), 0.1)
    ssd_norm_w = 1.0 + _normal(next(keys), (L, SSD_D_INNER), 0.02)
    ssd_w_out = _normal(next(keys), (L, SSD_D_INNER, D), SSD_D_INNER ** -0.5 * BETA)
    L = N_FOX
    fox_cols = jnp.concatenate([jnp.ones((4 * FOX_WIDTH,), jnp.float32), jnp.full((FOX_HEADS,), FGATE_PROJ_SCALE, jnp.float32)])
    fox_w_in = _normal(next(keys), (L, D, FOX_IN_DIM), D ** -0.5) * fox_cols
    fox_f_bias = jax.random.uniform(next(keys), (L, FOX_HEADS), jnp.float32, 1.0, 6.0)
    fox_w_out = _normal(next(keys), (L, FOX_WIDTH, D), FOX_WIDTH ** -0.5 * BETA)
    L = N_MLA
    mla_w_in = _normal(next(keys), (L, D, MLA_IN_DIM), D ** -0.5)
    mla_q_norm = 1.0 + _normal(next(keys), (L, MLA_Q_RANK), 0.02)
    mla_kv_norm = 1.0 + _normal(next(keys), (L, MLA_KV_RANK), 0.02)
    mla_w_q_up = _normal(next(keys), (L, MLA_Q_RANK, MLA_HEADS * (MLA_NOPE + MLA_ROPE)), MLA_Q_RANK ** -0.5)
    mla_w_kv_up = _normal(next(keys), (L, MLA_KV_RANK, MLA_HEADS * (MLA_NOPE + MLA_V)), MLA_KV_RANK ** -0.5)
    mla_w_out = _normal(next(keys), (L, MLA_WIDTH, D), MLA_WIDTH ** -0.5 * BETA)
    L = N_S5
    s5_w_in = _normal(next(keys), (L, D, S5_IN_DIM), D ** -0.5)
    s5_lambda_re = -0.5 + _normal(next(keys), (L, S5_GROUPS, S5_STATE), 0.01)
    s5_lambda_im = math.pi * jnp.arange(S5_STATE, dtype=jnp.float32) + _normal(next(keys), (L, S5_GROUPS, S5_STATE), 0.01)
    s5_log_step = jax.random.uniform(next(keys), (L, S5_GROUPS), jnp.float32, math.log(1e-3), math.log(1e-1))
    s5_b_re = _normal(next(keys), (L, S5_GROUPS, S5_STATE, S5_GROUP), (2.0 * S5_GROUP) ** -0.5)
    s5_b_im = _normal(next(keys), (L, S5_GROUPS, S5_STATE, S5_GROUP), (2.0 * S5_GROUP) ** -0.5)
    s5_c_re = _normal(next(keys), (L, S5_GROUPS, S5_GROUP, S5_STATE), S5_STATE ** -0.5)
    s5_c_im = _normal(next(keys), (L, S5_GROUPS, S5_GROUP, S5_STATE), S5_STATE ** -0.5)
    s5_d = 1.0 + _normal(next(keys), (L, S5_WIDTH), 0.1)
    s5_w_glu = _normal(next(keys), (L, S5_WIDTH, S5_WIDTH), S5_WIDTH ** -0.5)
    s5_b_glu = _normal(next(keys), (L, S5_WIDTH), 0.02)
    s5_w_out = _normal(next(keys), (L, S5_WIDTH, D), S5_WIDTH ** -0.5 * BETA)
    return {'x': x, 'positions': positions, 'ln_g': ln_g, 'ln_b': ln_b,
            'ssd_w_in': ssd_w_in, 'ssd_conv_w': ssd_conv_w, 'ssd_conv_b': ssd_conv_b, 'ssd_dt_bias': ssd_dt_bias,
            'ssd_a_log': ssd_a_log, 'ssd_d': ssd_d, 'ssd_norm_w': ssd_norm_w, 'ssd_w_out': ssd_w_out,
            'fox_w_in': fox_w_in, 'fox_f_bias': fox_f_bias, 'fox_w_out': fox_w_out,
            'mla_w_in': mla_w_in, 'mla_q_norm': mla_q_norm, 'mla_kv_norm': mla_kv_norm, 'mla_w_q_up': mla_w_q_up,
            'mla_w_kv_up': mla_w_kv_up, 'mla_w_out': mla_w_out,
            's5_w_in': s5_w_in, 's5_lambda_re': s5_lambda_re, 's5_lambda_im': s5_lambda_im, 's5_log_step': s5_log_step,
            's5_b_re': s5_b_re, 's5_b_im': s5_b_im, 's5_c_re': s5_c_re, 's5_c_im': s5_c_im, 's5_d': s5_d,
            's5_w_glu': s5_w_glu, 's5_b_glu': s5_b_glu, 's5_w_out': s5_w_out}


def reference(x, positions, ln_g, ln_b,
              ssd_w_in, ssd_conv_w, ssd_conv_b, ssd_dt_bias, ssd_a_log, ssd_d, ssd_norm_w, ssd_w_out,
              fox_w_in, fox_f_bias, fox_w_out,
              mla_w_in, mla_q_norm, mla_kv_norm, mla_w_q_up, mla_w_kv_up, mla_w_out,
              s5_w_in, s5_lambda_re, s5_lambda_im, s5_log_step, s5_b_re, s5_b_im, s5_c_re, s5_c_im, s5_d,
              s5_w_glu, s5_b_glu, s5_w_out):
    h = x
    for i in range(DEPTH):
        j = i // N_MIXERS
        kind = i % N_MIXERS
        if kind == 0:
            out = ssd_mixer(h, ssd_w_in[j], ssd_conv_w[j], ssd_conv_b[j], ssd_dt_bias[j], ssd_a_log[j], ssd_d[j], ssd_norm_w[j], ssd_w_out[j])
        elif kind == 1:
            out = fox_mixer(h, fox_w_in[j], fox_f_bias[j], fox_w_out[j])
        elif kind == 2:
            out = mla_mixer(h, positions, mla_w_in[j], mla_q_norm[j], mla_kv_norm[j], mla_w_q_up[j], mla_w_kv_up[j], mla_w_out[j])
        else:
            out = s5_mixer(h, s5_w_in[j], s5_lambda_re[j], s5_lambda_im[j], s5_log_step[j], s5_b_re[j], s5_b_im[j], s5_c_re[j], s5_c_im[j], s5_d[j], s5_w_glu[j], s5_b_glu[j], s5_w_out[j])
        h = layer_norm(ALPHA * h + out.astype(h.dtype), ln_g[i], ln_b[i])
    return h
```

```python
import functools
import math

import jax
import jax.numpy as jnp
from jax import lax
from jax.experimental import pallas as pl
from jax.experimental.pallas import tpu as pltpu

F32 = jnp.float32
BF16 = jnp.bfloat16
HIGHEST = lax.Precision.HIGHEST

D_MODEL = 2048
DEPTH = 4
ALPHA = (2.0 * DEPTH) ** 0.25
LN_EPS = 1e-5
RMS_EPS = 1e-6

SSD_D_INNER = 4096
SSD_HEADS = 64
SSD_HEADDIM = 64
SSD_GROUPS = 8
SSD_HEADS_PER_GROUP = SSD_HEADS // SSD_GROUPS
SSD_STATE = 128
SSD_CONV = 4
SSD_CHUNK = 128
SSD_GROUP_WIDTH = SSD_D_INNER // SSD_GROUPS
SSD_BC_WIDTH = SSD_GROUPS * SSD_STATE

FOX_HEADS = 16
FOX_HEAD_DIM = 128
FOX_WIDTH = FOX_HEADS * FOX_HEAD_DIM

MLA_HEADS = 16
MLA_Q_RANK = 512
MLA_KV_RANK = 512
MLA_NOPE = 128
MLA_ROPE = 64
MLA_V = 128
MLA_QK = MLA_NOPE + MLA_ROPE
MLA_WIDTH = MLA_HEADS * MLA_V
ROPE_BASE = 10000.0

S5_WIDTH = D_MODEL
S5_GROUP = 16
S5_GROUPS = S5_WIDTH // S5_GROUP
S5_STATE = 64
S5_CHUNK = 16
S5_ROW = S5_CHUNK * S5_GROUP
S5_PAIRS = S5_GROUPS // 2

LANES = 128
VMEM_LIMIT_BYTES = 48 * 1024 * 1024


def _cparams(*sem):
    return pltpu.CompilerParams(dimension_semantics=sem, vmem_limit_bytes=VMEM_LIMIT_BYTES)


def _sigmoid(x):
    return 1.0 / (1.0 + jnp.exp(-x))


def _silu(x):
    return x * _sigmoid(x)


def _softplus(x):
    return jnp.maximum(x, 0.0) + jnp.log1p(jnp.exp(-jnp.abs(x)))


def _mm_kernel(x_ref, w_ref, o_ref, *, scale):
    acc = jnp.dot(x_ref[...].astype(BF16), w_ref[...], preferred_element_type=F32)
    if scale is not None:
        acc = acc * scale
    o_ref[...] = acc.astype(o_ref.dtype)


def _matmul(x, w, out_dtype, *, scale=None, tm=512, tn=1024):
    M, K = x.shape
    N = w.shape[1]
    tm = min(tm, M)
    tn = min(tn, N)
    assert M % tm == 0 and N % tn == 0, (M, N, tm, tn)
    return pl.pallas_call(
        functools.partial(_mm_kernel, scale=scale),
        grid=(N // tn, M // tm),
        in_specs=[pl.BlockSpec((tm, K), lambda j, i: (i, 0)),
                  pl.BlockSpec((K, tn), lambda j, i: (0, j))],
        out_specs=pl.BlockSpec((tm, tn), lambda j, i: (i, j)),
        out_shape=jax.ShapeDtypeStruct((M, N), out_dtype),
        compiler_params=_cparams("parallel", "parallel"),
        name="proj",
    )(x, w)


def _out_ln_kernel(y_ref, w_ref, h_ref, g_ref, b_ref, of_ref, ob_ref, acc_ref, *, nk):
    k = pl.program_id(1)

    @pl.when(k == 0)
    def _():
        acc_ref[...] = jnp.zeros_like(acc_ref)

    acc_ref[...] += jnp.dot(y_ref[...], w_ref[...], preferred_element_type=F32)

    @pl.when(k == nk - 1)
    def _():
        r = ALPHA * h_ref[...] + acc_ref[...]
        mu = jnp.mean(r, axis=-1, keepdims=True)
        d = r - mu
        var = jnp.mean(d * d, axis=-1, keepdims=True)
        out = d * lax.rsqrt(var + LN_EPS) * g_ref[...] + b_ref[...]
        of_ref[...] = out
        ob_ref[...] = out.astype(BF16)


def _out_proj_ln(y, w, h, g, b, *, tm=512, tk=512):
    M, K = y.shape
    D = w.shape[1]
    tm = min(tm, M)
    tk = min(tk, K)
    assert M % tm == 0 and K % tk == 0
    nk = K // tk
    return pl.pallas_call(
        functools.partial(_out_ln_kernel, nk=nk),
        grid=(M // tm, nk),
        in_specs=[pl.BlockSpec((tm, tk), lambda i, k: (i, k)),
                  pl.BlockSpec((tk, D), lambda i, k: (k, 0)),
                  pl.BlockSpec((tm, D), lambda i, k: (i, 0)),
                  pl.BlockSpec((1, D), lambda i, k: (0, 0)),
                  pl.BlockSpec((1, D), lambda i, k: (0, 0))],
        out_specs=[pl.BlockSpec((tm, D), lambda i, k: (i, 0)),
                   pl.BlockSpec((tm, D), lambda i, k: (i, 0))],
        out_shape=[jax.ShapeDtypeStruct((M, D), F32), jax.ShapeDtypeStruct((M, D), BF16)],
        scratch_shapes=[pltpu.VMEM((tm, D), F32)],
        compiler_params=_cparams("parallel", "arbitrary"),
        name="out_proj_ln",
    )(y, w, h, g.reshape(1, D), b.reshape(1, D))


def _attn_kernel(*refs, has_bias, tq, tk, nk):
    if has_bias:
        q_ref, k_ref, v_ref, z_ref, cq_ref, ck_ref, o_ref, m_ref, l_ref, acc_ref = refs
    else:
        q_ref, k_ref, v_ref, z_ref, o_ref, m_ref, l_ref, acc_ref = refs
        cq_ref = ck_ref = None
    i = pl.program_id(2)
    j = pl.program_id(3)

    @pl.when(j == 0)
    def _():
        m_ref[...] = jnp.full_like(m_ref, -jnp.inf)
        l_ref[...] = jnp.zeros_like(l_ref)
        acc_ref[...] = jnp.zeros_like(acc_ref)

    q0 = i * tq
    k0 = j * tk
    needed = k0 <= q0 + (tq - 1)
    crosses_diagonal = k0 + (tk - 1) > q0

    def step(masked):
        s = lax.dot_general(q_ref[...], k_ref[...], (((1,), (1,)), ((), ())),
                            preferred_element_type=F32)
        if has_bias:
            s = s + (cq_ref[...] - ck_ref[...])
        if masked:
            qpos = q0 + lax.broadcasted_iota(jnp.int32, (tq, tk), 0)
            kpos = k0 + lax.broadcasted_iota(jnp.int32, (tq, tk), 1)
            s = jnp.where(kpos <= qpos, s, -jnp.inf)
        m_prev = m_ref[...]
        m_new = jnp.maximum(m_prev, jnp.max(s, axis=-1, keepdims=True))
        alpha = jnp.exp(m_prev - m_new)
        p = jnp.exp(s - m_new)
        l_ref[...] = alpha * l_ref[...] + jnp.sum(p, axis=-1, keepdims=True)
        acc_ref[...] = alpha * acc_ref[...] + jnp.dot(
            p.astype(BF16), v_ref[...], preferred_element_type=F32)
        m_ref[...] = m_new

    @pl.when(jnp.logical_and(needed, jnp.logical_not(crosses_diagonal)))
    def _():
        step(False)

    @pl.when(jnp.logical_and(needed, crosses_diagonal))
    def _():
        step(True)

    @pl.when(j == nk - 1)
    def _():
        o = acc_ref[...] / l_ref[...]
        o_ref[...] = (o * _silu(z_ref[...])).astype(o_ref.dtype)


def _attention(inputs, in_specs, out_spec, *, B, T, H, dv, tq, tk, has_bias):
    nq = T // tq
    nk = T // tk
    return pl.pallas_call(
        functools.partial(_attn_kernel, has_bias=has_bias, tq=tq, tk=tk, nk=nk),
        grid=(B, H, nq, nk),
        in_specs=in_specs,
        out_specs=out_spec,
        out_shape=jax.ShapeDtypeStruct((B * T, H * dv), BF16),
        scratch_shapes=[pltpu.VMEM((tq, 1), F32), pltpu.VMEM((tq, 1), F32),
                        pltpu.VMEM((tq, dv), F32)],
        compiler_params=_cparams("parallel", "parallel", "parallel", "arbitrary"),
        name="causal_attention",
    )(*inputs)


def _attn_tiles(T):
    tq = min(256, T)
    tk = min(512, T)
    return tq, tk


def _last_kv_block(i, tq, tk):
    return (i * tq + (tq - 1)) // tk


def _ssd_kernel(x_ref, bm_ref, cm_ref, z_ref, dtc_ref, dtr_ref,
                cwx_ref, cwb_ref, cwc_ref, cbx_ref, cbb_ref, cbc_ref,
                dtbc_ref, dtbr_ref, alc_ref, alr_ref, dsk_ref, nw_ref,
                o_ref,
                extx_ref, extb_ref, extc_ref, state_ref, yz_ref):
    Q = SSD_CHUNK
    c = pl.program_id(2)

    @pl.when(c == 0)
    def _():
        extx_ref[pl.ds(0, 8), :] = jnp.zeros((8, SSD_GROUP_WIDTH), F32)
        extb_ref[pl.ds(0, 8), :] = jnp.zeros((8, SSD_STATE), F32)
        extc_ref[pl.ds(0, 8), :] = jnp.zeros((8, SSD_STATE), F32)
        state_ref[...] = jnp.zeros_like(state_ref)

    def conv_silu(raw_ref, ext_ref, w_ref, b_ref):
        ext_ref[pl.ds(8, Q), :] = raw_ref[...]
        acc = b_ref[...] + w_ref[pl.ds(0, 1), :] * ext_ref[pl.ds(5, Q), :]
        for kk in range(1, SSD_CONV):
            acc = acc + w_ref[pl.ds(kk, 1), :] * ext_ref[pl.ds(5 + kk, Q), :]
        ext_ref[pl.ds(0, 8), :] = ext_ref[pl.ds(Q, 8), :]
        return _silu(acc)

    bm = conv_silu(bm_ref, extb_ref, cwb_ref, cbb_ref)
    cm = conv_silu(cm_ref, extc_ref, cwc_ref, cbc_ref)
    bm_b = bm.astype(BF16)
    cm_b = cm.astype(BF16)
    bmt_b = bm.T.astype(BF16)
    cb = lax.dot_general(cm_b, bm_b, (((1,), (1,)), ((), ())), preferred_element_type=F32)

    row = lax.broadcasted_iota(jnp.int32, (Q, Q), 0)
    col = lax.broadcasted_iota(jnp.int32, (Q, Q), 1)
    lower = row >= col
    tri_l = lower.astype(F32)
    tri_u = (row <= col).astype(F32)
    left = col < SSD_HEADDIM

    dt_c = _softplus(dtc_ref[...] + dtbc_ref[...])
    da_c = dt_c * (-jnp.exp(alc_ref[...]))
    acs_c = jnp.dot(tri_l, da_c, precision=HIGHEST, preferred_element_type=F32)
    dt_r = _softplus(dtr_ref[...] + dtbr_ref[...])
    da_r = dt_r * (-jnp.exp(alr_ref[...]))
    acs_r = jnp.dot(da_r, tri_u, precision=HIGHEST, preferred_element_type=F32)
    dsk = dsk_ref[...]

    def pair_cols(v, e0):
        a = jnp.broadcast_to(v[:, e0:e0 + 1], (v.shape[0], Q))
        b = jnp.broadcast_to(v[:, e0 + 1:e0 + 2], (v.shape[0], Q))
        return jnp.where(left[:v.shape[0]], a, b)

    ssq = jnp.zeros((Q, 1), F32)
    for kp in range(SSD_HEADS_PER_GROUP // 2):
        e0 = 2 * kp
        cols = pl.ds(kp * LANES, LANES)
        ext_ref_rows = extx_ref
        acc = cbx_ref[:, cols]
        ext_ref_rows[pl.ds(8, Q), cols] = x_ref[:, cols]
        for kk in range(SSD_CONV):
            acc = acc + cwx_ref[pl.ds(kk, 1), cols] * ext_ref_rows[pl.ds(5 + kk, Q), cols]
        ext_ref_rows[pl.ds(0, 8), cols] = ext_ref_rows[pl.ds(Q, 8), cols]
        xp = _silu(acc)

        dt_p = pair_cols(dt_c, e0)
        acs_p = pair_cols(acs_c, e0)
        last_p = pair_cols(acs_c[Q - 1:Q, :], e0)
        d_p = pair_cols(dsk, e0)
        xdt = xp * dt_p
        xdt_b = xdt.astype(BF16)

        ys = []
        for e in (e0, e0 + 1):
            seg = acs_c[:, e:e + 1] - acs_r[e:e + 1, :]
            dec = jnp.exp(jnp.where(lower, seg, -jnp.inf))
            ys.append(jnp.dot((cb * dec).astype(BF16), xdt_b, preferred_element_type=F32))
        y_diag = jnp.where(left, ys[0], ys[1])

        xw = (xdt * jnp.exp(last_p - acs_p)).astype(BF16)
        s_loc = jnp.dot(bmt_b, xw, preferred_element_type=F32)
        s_prev = state_ref[kp]
        y_off = jnp.dot(cm_b, s_prev.astype(BF16), preferred_element_type=F32) * jnp.exp(acs_p)
        state_ref[kp] = s_prev * jnp.exp(last_p) + s_loc

        y = (y_diag + y_off + d_p * xp) * _silu(z_ref[:, cols])
        yz_ref[:, cols] = y
        ssq = ssq + jnp.sum(y * y, axis=-1, keepdims=True)

    inv = lax.rsqrt(ssq * (1.0 / SSD_GROUP_WIDTH) + RMS_EPS)
    o_ref[...] = (yz_ref[...] * inv * nw_ref[...]).astype(o_ref.dtype)


def _ssd_core(z, xbc, dt, conv_w, conv_b, dt_bias, a_log, d_skip, norm_w, *, B, T):
    G, E, Q = SSD_GROUPS, SSD_HEADS_PER_GROUP, SSD_CHUNK
    NC = T // Q
    GW, N = SSD_GROUP_WIDTH, SSD_STATE
    xblocks = SSD_D_INNER // N
    dt4 = dt[:, :SSD_HEADS].reshape(B, T, G, E)
    dtc = jnp.transpose(dt4, (0, 2, 1, 3))
    dtr = jnp.transpose(dt4, (0, 2, 3, 1))
    cw = conv_w
    cb = conv_b.reshape(1, -1)
    per_group = lambda v: v.reshape(G, E)
    dtb, al, dsk = per_group(dt_bias), per_group(a_log), per_group(d_skip)
    col3 = lambda v: v.reshape(G, 1, E)
    row3 = lambda v: v.reshape(G, E, 1)

    rowblk = lambda b, g, c: b * NC + c
    in_specs = [
        pl.BlockSpec((Q, GW), lambda b, g, c: (rowblk(b, g, c), g)),
        pl.BlockSpec((Q, N), lambda b, g, c: (rowblk(b, g, c), xblocks + g)),
        pl.BlockSpec((Q, N), lambda b, g, c: (rowblk(b, g, c), xblocks + G + g)),
        pl.BlockSpec((Q, GW), lambda b, g, c: (rowblk(b, g, c), g)),
        pl.BlockSpec((None, None, Q, E), lambda b, g, c: (b, g, c, 0)),
        pl.BlockSpec((None, None, E, Q), lambda b, g, c: (b, g, 0, c)),
        pl.BlockSpec((SSD_CONV, GW), lambda b, g, c: (0, g)),
        pl.BlockSpec((SSD_CONV, N), lambda b, g, c: (0, xblocks + g)),
        pl.BlockSpec((SSD_CONV, N), lambda b, g, c: (0, xblocks + G + g)),
        pl.BlockSpec((1, GW), lambda b, g, c: (0, g)),
        pl.BlockSpec((1, N), lambda b, g, c: (0, xblocks + g)),
        pl.BlockSpec((1, N), lambda b, g, c: (0, xblocks + G + g)),
        pl.BlockSpec((None, 1, E), lambda b, g, c: (g, 0, 0)),
        pl.BlockSpec((None, E, 1), lambda b, g, c: (g, 0, 0)),
        pl.BlockSpec((None, 1, E), lambda b, g, c: (g, 0, 0)),
        pl.BlockSpec((None, E, 1), lambda b, g, c: (g, 0, 0)),
        pl.BlockSpec((None, 1, E), lambda b, g, c: (g, 0, 0)),
        pl.BlockSpec((1, GW), lambda b, g, c: (0, g)),
    ]
    return pl.pallas_call(
        _ssd_kernel,
        grid=(B, G, NC),
        in_specs=in_specs,
        out_specs=pl.BlockSpec((Q, GW), lambda b, g, c: (rowblk(b, g, c), g)),
        out_shape=jax.ShapeDtypeStruct((B * T, SSD_D_INNER), BF16),
        scratch_shapes=[pltpu.VMEM((Q + 8, GW), F32), pltpu.VMEM((Q + 8, N), F32),
                        pltpu.VMEM((Q + 8, N), F32),
                        pltpu.VMEM((E // 2, N, LANES), F32), pltpu.VMEM((Q, GW), F32)],
        compiler_params=_cparams("parallel", "parallel", "arbitrary"),
        name="ssd_chunk_scan",
    )(xbc, xbc, xbc, z, dtc, dtr, cw, cw, cw, cb, cb, cb,
      col3(dtb), row3(dtb), col3(al), row3(al), col3(dsk), norm_w.reshape(1, -1))


def _ssd_layer(hf, hb, w_in, conv_w, conv_b, dt_bias, a_log, d_skip, norm_w, w_out, g, b, *, B, T):
    di = SSD_D_INNER
    cd = di + 2 * SSD_BC_WIDTH
    z = _matmul(hb, w_in[:, :di].astype(BF16), F32)
    xbc = _matmul(hb, w_in[:, di:di + cd].astype(BF16), F32)
    w_dt = jnp.pad(w_in[:, di + cd:], ((0, 0), (0, LANES - SSD_HEADS))).astype(BF16)
    dt = _matmul(hb, w_dt, F32)
    y = _ssd_core(z, xbc, dt, conv_w, conv_b, dt_bias, a_log, d_skip, norm_w, B=B, T=T)
    return _out_proj_ln(y, w_out.astype(BF16), hf, g, b)


def _fox_cum_kernel(f_ref, b_ref, o_ref, *, nblk):
    blk = LANES
    row = lax.broadcasted_iota(jnp.int32, (blk, blk), 0)
    col = lax.broadcasted_iota(jnp.int32, (blk, blk), 1)
    tri = (row >= col).astype(F32)
    bias = b_ref[...]

    def body(t, carry):
        x = f_ref[pl.ds(t * blk, blk), :] + bias
        logf = jnp.minimum(x, 0.0) - jnp.log1p(jnp.exp(-jnp.abs(x)))
        cum = jnp.dot(tri, logf, precision=HIGHEST, preferred_element_type=F32) + carry
        o_ref[pl.ds(t * blk, blk), :] = cum
        return cum[blk - 1:blk, :]

    lax.fori_loop(0, nblk, body, jnp.zeros((1, LANES), F32))


def _fox_cum(f, f_bias, *, B, T):
    bias = jnp.pad(f_bias, (0, LANES - FOX_HEADS)).reshape(1, LANES)
    return pl.pallas_call(
        functools.partial(_fox_cum_kernel, nblk=T // LANES),
        grid=(B,),
        in_specs=[pl.BlockSpec((None, T, LANES), lambda b: (b, 0, 0)),
                  pl.BlockSpec((1, LANES), lambda b: (0, 0))],
        out_specs=pl.BlockSpec((None, T, LANES), lambda b: (b, 0, 0)),
        out_shape=jax.ShapeDtypeStruct((B, T, LANES), F32),
        compiler_params=_cparams("parallel"),
        name="fox_cum_log_forget",
    )(f.reshape(B, T, LANES), bias)


def _fox_layer(hf, hb, w_in, f_bias, w_out, g, b, *, B, T):
    W, H, dh = FOX_WIDTH, FOX_HEADS, FOX_HEAD_DIM
    q = _matmul(hb, w_in[:, :W].astype(BF16), BF16, scale=dh ** -0.5)
    kv = _matmul(hb, w_in[:, W:3 * W].astype(BF16), BF16)
    z = _matmul(hb, w_in[:, 3 * W:4 * W].astype(BF16), F32)
    w_f = jnp.pad(w_in[:, 4 * W:], ((0, 0), (0, LANES - H))).astype(BF16)
    f = _matmul(hb, w_f, F32)
    cum = _fox_cum(f, f_bias, B=B, T=T)[:, :, :H]
    cum_h = jnp.transpose(cum, (0, 2, 1))
    cq = cum_h.reshape(B, H, T, 1)
    ck = cum_h.reshape(B, H, 1, T)

    tq, tk = _attn_tiles(T)
    nq, nk = T // tq, T // tk
    kvb = lambda i, j: jnp.minimum(j, _last_kv_block(i, tq, tk))
    in_specs = [
        pl.BlockSpec((tq, dh), lambda bb, h, i, j: (bb * nq + i, h)),
        pl.BlockSpec((tk, dh), lambda bb, h, i, j: (bb * nk + kvb(i, j), h)),
        pl.BlockSpec((tk, dh), lambda bb, h, i, j: (bb * nk + kvb(i, j), H + h)),
        pl.BlockSpec((tq, dh), lambda bb, h, i, j: (bb * nq + i, h)),
        pl.BlockSpec((None, None, tq, 1), lambda bb, h, i, j: (bb, h, i, 0)),
        pl.BlockSpec((None, None, 1, tk), lambda bb, h, i, j: (bb, h, 0, kvb(i, j))),
    ]
    out_spec = pl.BlockSpec((tq, dh), lambda bb, h, i, j: (bb * nq + i, h))
    y = _attention((q, kv, kv, z, cq, ck), in_specs, out_spec,
                   B=B, T=T, H=H, dv=dh, tq=tq, tk=tk, has_bias=True)
    return _out_proj_ln(y, w_out.astype(BF16), hf, g, b)


def _rope_table_kernel(pos_ref, freq_ref, sign_ref, cos_ref, sin_ref):
    ang = pos_ref[...].astype(F32) * freq_ref[...]
    cos_ref[...] = jnp.cos(ang)
    sin_ref[...] = jnp.sin(ang) * sign_ref[...]


def _rope_tables(positions, *, B, T):
    half = MLA_ROPE // 2
    inv_freq = ROPE_BASE ** (-jnp.arange(0, MLA_ROPE, 2, dtype=F32) / MLA_ROPE)
    freq = jnp.tile(inv_freq, LANES // half).reshape(1, LANES)
    sign = jnp.tile(jnp.concatenate([-jnp.ones((half,), F32), jnp.ones((half,), F32)]),
                    LANES // MLA_ROPE).reshape(1, LANES)
    M = B * T
    tm = min(1024, M)
    return pl.pallas_call(
        _rope_table_kernel,
        grid=(M // tm,),
        in_specs=[pl.BlockSpec((tm, 1), lambda i: (i, 0)),
                  pl.BlockSpec((1, LANES), lambda i: (0, 0)),
                  pl.BlockSpec((1, LANES), lambda i: (0, 0))],
        out_specs=[pl.BlockSpec((tm, LANES), lambda i: (i, 0)),
                   pl.BlockSpec((tm, LANES), lambda i: (i, 0))],
        out_shape=[jax.ShapeDtypeStruct((M, LANES), F32), jax.ShapeDtypeStruct((M, LANES), F32)],
        compiler_params=_cparams("parallel"),
        name="rope_tables",
    )(positions.reshape(M, 1), freq, sign)


def _rms_to_bf16(x, w):
    y = x * lax.rsqrt(jnp.mean(x * x, axis=-1, keepdims=True) + RMS_EPS)
    return (y * w).astype(BF16)


def _rope_pair(c2, cos, sin):
    return c2 * cos + pltpu.roll(c2, MLA_ROPE, 1) * sin


def _mla_q_kernel(x_ref, nw_ref, w_ref, cos_ref, sin_ref, o_ref, xn_ref, *, scale):
    @pl.when(pl.program_id(1) == 0)
    def _():
        xn_ref[...] = _rms_to_bf16(x_ref[...], nw_ref[...])

    r = jnp.dot(xn_ref[...], w_ref[...], preferred_element_type=F32)
    pe = _rope_pair(r[:, MLA_NOPE:], cos_ref[...], sin_ref[...])
    o_ref[:, :MLA_NOPE] = (r[:, :MLA_NOPE] * scale).astype(o_ref.dtype)
    o_ref[:, MLA_NOPE:] = (pe[:, :MLA_ROPE] * scale).astype(o_ref.dtype)


def _mla_kv_kernel(x_ref, nw_ref, w_ref, kpe_ref, cos_ref, sin_ref, k_ref, v_ref, xn_ref, pe_ref):
    @pl.when(pl.program_id(1) == 0)
    def _():
        xn_ref[...] = _rms_to_bf16(x_ref[...], nw_ref[...])
        pe_ref[...] = _rope_pair(kpe_ref[...], cos_ref[...], sin_ref[...])

    r = jnp.dot(xn_ref[...], w_ref[...], preferred_element_type=F32)
    k_ref[:, :MLA_NOPE] = r[:, :MLA_NOPE].astype(k_ref.dtype)
    k_ref[:, MLA_NOPE:] = pe_ref[:, :MLA_ROPE].astype(k_ref.dtype)
    v_ref[...] = r[:, MLA_NOPE:].astype(v_ref.dtype)


def _swap_halves(w):
    half = w.shape[-1] // 2
    return jnp.concatenate([w[..., half:], w[..., :half]], axis=-1)


def _mla_layer(hf, hb, positions, w_in, q_norm, kv_norm, w_q_up, w_kv_up, w_out, g, b, *, B, T):
    H, M = MLA_HEADS, B * T
    qr, kr = MLA_Q_RANK, MLA_KV_RANK
    q_lat = _matmul(hb, w_in[:, :qr].astype(BF16), F32)
    kv_lat = _matmul(hb, w_in[:, qr:qr + kr].astype(BF16), F32)
    w_pe = w_in[:, qr + kr:qr + kr + MLA_ROPE]
    kpe2 = _matmul(hb, jnp.concatenate([w_pe, _swap_halves(w_pe)], axis=1).astype(BF16), F32)
    z = _matmul(hb, w_in[:, qr + kr + MLA_ROPE:].astype(BF16), F32)
    cos, sin = _rope_tables(positions, B=B, T=T)

    wq = w_q_up.reshape(qr, H, MLA_QK)
    wq = jnp.concatenate([wq, _swap_halves(wq[..., MLA_NOPE:])], axis=-1)
    wq = jnp.transpose(wq, (1, 0, 2)).astype(BF16)
    wkv = jnp.transpose(w_kv_up.reshape(kr, H, MLA_NOPE + MLA_V), (1, 0, 2)).astype(BF16)

    tm = min(512, M)
    wide = MLA_NOPE + LANES
    row = lambda i, h: (i, 0)
    q = pl.pallas_call(
        functools.partial(_mla_q_kernel, scale=MLA_QK ** -0.5),
        grid=(M // tm, H),
        in_specs=[pl.BlockSpec((tm, qr), row),
                  pl.BlockSpec((1, qr), lambda i, h: (0, 0)),
                  pl.BlockSpec((None, qr, wide), lambda i, h: (h, 0, 0)),
                  pl.BlockSpec((tm, LANES), row),
                  pl.BlockSpec((tm, LANES), row)],
        out_specs=pl.BlockSpec((None, tm, MLA_QK), lambda i, h: (h, i, 0)),
        out_shape=jax.ShapeDtypeStruct((H, M, MLA_QK), BF16),
        scratch_shapes=[pltpu.VMEM((tm, qr), BF16)],
        compiler_params=_cparams("parallel", "arbitrary"),
        name="mla_q_up_rope",
    )(q_lat, q_norm.reshape(1, qr), wq, cos, sin)
    k, v = pl.pallas_call(
        _mla_kv_kernel,
        grid=(M // tm, H),
        in_specs=[pl.BlockSpec((tm, kr), row),
                  pl.BlockSpec((1, kr), lambda i, h: (0, 0)),
                  pl.BlockSpec((None, kr, MLA_NOPE + MLA_V), lambda i, h: (h, 0, 0)),
                  pl.BlockSpec((tm, LANES), row),
                  pl.BlockSpec((tm, LANES), row),
                  pl.BlockSpec((tm, LANES), row)],
        out_specs=[pl.BlockSpec((None, tm, MLA_QK), lambda i, h: (h, i, 0)),
                   pl.BlockSpec((None, tm, MLA_V), lambda i, h: (h, i, 0))],
        out_shape=[jax.ShapeDtypeStruct((H, M, MLA_QK), BF16),
                   jax.ShapeDtypeStruct((H, M, MLA_V), BF16)],
        scratch_shapes=[pltpu.VMEM((tm, kr), BF16), pltpu.VMEM((tm, LANES), F32)],
        compiler_params=_cparams("parallel", "arbitrary"),
        name="mla_kv_up_rope",
    )(kv_lat, kv_norm.reshape(1, kr), wkv, kpe2, cos, sin)

    tq, tk = _attn_tiles(T)
    nq, nk = T // tq, T // tk
    kvb = lambda i, j: jnp.minimum(j, _last_kv_block(i, tq, tk))
    in_specs = [
        pl.BlockSpec((None, tq, MLA_QK), lambda bb, h, i, j: (h, bb * nq + i, 0)),
        pl.BlockSpec((None, tk, MLA_QK), lambda bb, h, i, j: (h, bb * nk + kvb(i, j), 0)),
        pl.BlockSpec((None, tk, MLA_V), lambda bb, h, i, j: (h, bb * nk + kvb(i, j), 0)),
        pl.BlockSpec((tq, MLA_V), lambda bb, h, i, j: (bb * nq + i, h)),
    ]
    out_spec = pl.BlockSpec((tq, MLA_V), lambda bb, h, i, j: (bb * nq + i, h))
    y = _attention((q, k, v, z), in_specs, out_spec,
                   B=B, T=T, H=H, dv=MLA_V, tq=tq, tk=tk, has_bias=False)
    return _out_proj_ln(y, w_out.astype(BF16), hf, g, b)


def _s5_matrices(lam_re, lam_im, log_step, b_re, b_im, c_re, c_im):
    L = S5_CHUNK
    step = jnp.exp(log_step.astype(F32))[:, None]
    mag = jnp.exp(lam_re * step)
    ar = mag * jnp.cos(lam_im * step)
    ai = mag * jnp.sin(lam_im * step)
    den = lam_re * lam_re + lam_im * lam_im
    fr = ((ar - 1.0) * lam_re + ai * lam_im) / den
    fi = (ai * lam_re - (ar - 1.0) * lam_im) / den
    bbr = fr[..., None] * b_re - fi[..., None] * b_im
    bbi = fr[..., None] * b_im + fi[..., None] * b_re
    pr, pi = [jnp.ones_like(ar)], [jnp.zeros_like(ar)]
    for _ in range(L):
        pr_next = pr[-1] * ar - pi[-1] * ai
        pi_next = pr[-1] * ai + pi[-1] * ar
        pr.append(pr_next)
        pi.append(pi_next)
    pwr = jnp.stack(pr, axis=1)
    pwi = jnp.stack(pi, axis=1)

    cpr = c_re[:, None] * pwr[:, :L, None, :] - c_im[:, None] * pwi[:, :L, None, :]
    cpi = c_re[:, None] * pwi[:, :L, None, :] + c_im[:, None] * pwr[:, :L, None, :]
    kern = jnp.einsum('gdip,gpj->gdij', cpr, bbr, precision=HIGHEST) \
        - jnp.einsum('gdip,gpj->gdij', cpi, bbi, precision=HIGHEST)
    r = jnp.arange(L)
    lag = r[:, None] - r[None, :]
    toe = kern[:, jnp.clip(lag, 0, L - 1)]
    toe = jnp.where((lag >= 0)[None, :, :, None, None], toe, 0.0)
    tmt = jnp.transpose(toe, (0, 2, 4, 1, 3)).reshape(S5_GROUPS, S5_ROW, S5_ROW)

    qr = pwr[:, L - 1::-1][:, :L]
    qi = pwi[:, L - 1::-1][:, :L]
    bpr = qr[:, :, :, None] * bbr[:, None] - qi[:, :, :, None] * bbi[:, None]
    bpi = qr[:, :, :, None] * bbi[:, None] + qi[:, :, :, None] * bbr[:, None]
    to_rows = lambda m: jnp.transpose(m, (0, 1, 3, 2)).reshape(S5_GROUPS, S5_ROW, S5_STATE)
    bpr, bpi = to_rows(bpr), to_rows(bpi)

    c1r = c_re[:, None] * pwr[:, 1:, None, :] - c_im[:, None] * pwi[:, 1:, None, :]
    c1i = c_re[:, None] * pwi[:, 1:, None, :] + c_im[:, None] * pwr[:, 1:, None, :]
    to_cols = lambda m: jnp.transpose(m, (0, 3, 1, 2)).reshape(S5_GROUPS, S5_STATE, S5_ROW)
    cpw_r, cpw_i = to_cols(c1r), -to_cols(c1i)

    zs = jnp.zeros((S5_PAIRS, S5_ROW, S5_STATE), F32)
    ev, od = slice(0, None, 2), slice(1, None, 2)
    bp_pair = jnp.concatenate([
        jnp.concatenate([bpr[ev], zs, bpi[ev], zs], axis=2),
        jnp.concatenate([zs, bpr[od], zs, bpi[od]], axis=2)], axis=1)
    zc = jnp.zeros((S5_PAIRS, S5_STATE, S5_ROW), F32)
    cp_pair = jnp.concatenate([
        jnp.concatenate([cpw_r[ev], zc], axis=2),
        jnp.concatenate([zc, cpw_r[od]], axis=2),
        jnp.concatenate([cpw_i[ev], zc], axis=2),
        jnp.concatenate([zc, cpw_i[od]], axis=2)], axis=1)
    alr = pwr[:, L].reshape(S5_PAIRS, 2 * S5_STATE)
    ali = pwi[:, L].reshape(S5_PAIRS, 2 * S5_STATE)
    a1 = jnp.concatenate([alr, alr], axis=1).reshape(1, -1)
    a2 = jnp.concatenate([-ali, ali], axis=1).reshape(1, -1)
    return tmt.astype(BF16), bp_pair.astype(BF16), cp_pair.astype(BF16), a1, a2


def _s5_local_state_kernel(x_ref, bp_ref, s_ref):
    s_ref[...] = jnp.dot(x_ref[...].astype(BF16), bp_ref[...], preferred_element_type=F32)


def _s5_scan_kernel(s_ref, a1_ref, a2_ref, o_ref, *, B, nch):
    W = s_ref.shape[1]
    a1 = a1_ref[...]
    a2 = a2_ref[...]

    def swap_re_im(c):
        parts = []
        for q in range(W // (2 * LANES)):
            lo = q * 2 * LANES
            parts += [c[:, lo + LANES:lo + 2 * LANES], c[:, lo:lo + LANES]]
        return jnp.concatenate(parts, axis=1)

    def body(n, carry):
        new = []
        for bb in range(B):
            row = bb * nch + n
            c = carry[bb]
            o_ref[pl.ds(row, 1), :] = c
            new.append(a1 * c + a2 * swap_re_im(c) + s_ref[pl.ds(row, 1), :])
        return tuple(new)

    lax.fori_loop(0, nch, body, tuple(jnp.zeros((1, W), F32) for _ in range(B)))


def _gelu_tanh(y):
    return 0.5 * y * (1.0 + jnp.tanh(math.sqrt(2.0 / math.pi) * (y + 0.044715 * (y * y * y))))


def _s5_out_kernel(x_ref, st_ref, tmt_ref, cp_ref, d_ref, o_ref):
    x = x_ref[...]
    xb = x.astype(BF16)
    off = jnp.dot(st_ref[...].astype(BF16), cp_ref[...], preferred_element_type=F32)
    for gi in range(2):
        cols = slice(gi * S5_ROW, (gi + 1) * S5_ROW)
        y = jnp.dot(xb[:, cols], tmt_ref[gi], preferred_element_type=F32)
        y = y + off[:, cols] + d_ref[:, cols] * x[:, cols]
        o_ref[:, cols] = _gelu_tanh(y)


def _glu_kernel(y_ref, w_ref, b_ref, yb_ref, z_ref, o_ref):
    t = jnp.dot(y_ref[...].astype(BF16), w_ref[...], preferred_element_type=F32) + b_ref[...]
    o_ref[...] = (yb_ref[...] * _sigmoid(t) * _silu(z_ref[...])).astype(o_ref.dtype)


def _s5_layer(hf, hb, w_in, lam_re, lam_im, log_step, b_re, b_im, c_re, c_im, d_skip,
              w_glu, b_glu, w_out, g, b, *, B, T):
    M, W, G, L = B * T, S5_WIDTH, S5_GROUPS, S5_CHUNK
    nch = T // L
    rows = B * nch
    u = _matmul(hb, w_in[:, :W].astype(BF16), F32)
    z = _matmul(hb, w_in[:, W:].astype(BF16), F32)
    tmt, bp_pair, cp_pair, a1, a2 = _s5_matrices(lam_re, lam_im, log_step, b_re, b_im, c_re, c_im)

    x = jnp.transpose(u.reshape(rows, L, G, S5_GROUP), (0, 2, 1, 3)).reshape(rows, G * S5_ROW)
    pw_x, pw_s = 2 * S5_ROW, 4 * S5_STATE
    s_loc = pl.pallas_call(
        _s5_local_state_kernel,
        grid=(S5_PAIRS,),
        in_specs=[pl.BlockSpec((rows, pw_x), lambda p: (0, p)),
                  pl.BlockSpec((None, pw_x, pw_s), lambda p: (p, 0, 0))],
        out_specs=pl.BlockSpec((rows, pw_s), lambda p: (0, p)),
        out_shape=jax.ShapeDtypeStruct((rows, S5_PAIRS * pw_s), F32),
        compiler_params=_cparams("parallel"),
        name="s5_chunk_state",
    )(x, bp_pair)

    sw = 8 * pw_s
    st_in = pl.pallas_call(
        functools.partial(_s5_scan_kernel, B=B, nch=nch),
        grid=(S5_PAIRS * pw_s // sw,),
        in_specs=[pl.BlockSpec((rows, sw), lambda p: (0, p)),
                  pl.BlockSpec((1, sw), lambda p: (0, p)),
                  pl.BlockSpec((1, sw), lambda p: (0, p))],
        out_specs=pl.BlockSpec((rows, sw), lambda p: (0, p)),
        out_shape=jax.ShapeDtypeStruct((rows, S5_PAIRS * pw_s), F32),
        compiler_params=_cparams("parallel"),
        name="s5_state_scan",
    )(s_loc, a1, a2)

    d_rows = jnp.broadcast_to(d_skip.reshape(G, 1, S5_GROUP), (G, L, S5_GROUP)).reshape(1, G * S5_ROW)
    y = pl.pallas_call(
        _s5_out_kernel,
        grid=(S5_PAIRS,),
        in_specs=[pl.BlockSpec((rows, pw_x), lambda p: (0, p)),
                  pl.BlockSpec((rows, pw_s), lambda p: (0, p)),
                  pl.BlockSpec((2, S5_ROW, S5_ROW), lambda p: (p, 0, 0)),
                  pl.BlockSpec((None, pw_s, pw_x), lambda p: (p, 0, 0)),
                  pl.BlockSpec((1, pw_x), lambda p: (0, p))],
        out_specs=pl.BlockSpec((rows, pw_x), lambda p: (0, p)),
        out_shape=jax.ShapeDtypeStruct((rows, G * S5_ROW), F32),
        compiler_params=_cparams("parallel"),
        name="s5_chunk_output",
    )(x, st_in, tmt, cp_pair, d_rows)
    y = jnp.transpose(y.reshape(rows, G, L, S5_GROUP), (0, 2, 1, 3)).reshape(M, W)

    tm, tn = min(512, M), 512
    y3 = pl.pallas_call(
        _glu_kernel,
        grid=(M // tm, W // tn),
        in_specs=[pl.BlockSpec((tm, W), lambda i, j: (i, 0)),
                  pl.BlockSpec((W, tn), lambda i, j: (0, j)),
                  pl.BlockSpec((1, tn), lambda i, j: (0, j)),
                  pl.BlockSpec((tm, tn), lambda i, j: (i, j)),
                  pl.BlockSpec((tm, tn), lambda i, j: (i, j))],
        out_specs=pl.BlockSpec((tm, tn), lambda i, j: (i, j)),
        out_shape=jax.ShapeDtypeStruct((M, W), BF16),
        compiler_params=_cparams("parallel", "parallel"),
        name="s5_glu_gate",
    )(y, w_glu.astype(BF16), b_glu.reshape(1, W), y, z)
    return _out_proj_ln(y3, w_out.astype(BF16), hf, g, b)


def kernel(x, positions, ln_g, ln_b, ssd_w_in, ssd_conv_w, ssd_conv_b, ssd_dt_bias, ssd_a_log, ssd_d, ssd_norm_w, ssd_w_out, fox_w_in, fox_f_bias, fox_w_out, mla_w_in, mla_q_norm, mla_kv_norm, mla_w_q_up, mla_w_kv_up, mla_w_out, s5_w_in, s5_lambda_re, s5_lambda_im, s5_log_step, s5_b_re, s5_b_im, s5_c_re, s5_c_im, s5_d, s5_w_glu, s5_b_glu, s5_w_out):
    B, T, D = x.shape
    hf = x.reshape(B * T, D)
    hb = hf
    for i in range(DEPTH):
        j = i // 4
        kind = i % 4
        g, b = ln_g[i], ln_b[i]
        if kind == 0:
            hf, hb = _ssd_layer(hf, hb, ssd_w_in[j], ssd_conv_w[j], ssd_conv_b[j], ssd_dt_bias[j],
                                ssd_a_log[j], ssd_d[j], ssd_norm_w[j], ssd_w_out[j], g, b, B=B, T=T)
        elif kind == 1:
            hf, hb = _fox_layer(hf, hb, fox_w_in[j], fox_f_bias[j], fox_w_out[j], g, b, B=B, T=T)
        elif kind == 2:
            hf, hb = _mla_layer(hf, hb, positions, mla_w_in[j], mla_q_norm[j], mla_kv_norm[j],
                                mla_w_q_up[j], mla_w_kv_up[j], mla_w_out[j], g, b, B=B, T=T)
        else:
            hf, hb = _s5_layer(hf, hb, s5_w_in[j], s5_lambda_re[j], s5_lambda_im[j], s5_log_step[j],
                               s5_b_re[j], s5_b_im[j], s5_c_re[j], s5_c_im[j], s5_d[j],
                               s5_w_glu[j], s5_b_glu[j], s5_w_out[j], g, b, B=B, T=T)
    return hf.reshape(B, T, D)
```

```python
import functools
import math

import jax
import jax.numpy as jnp
from jax import lax
from jax.experimental import pallas as pl
from jax.experimental.pallas import tpu as pltpu

F32 = jnp.float32
BF16 = jnp.bfloat16
HIGHEST = lax.Precision.HIGHEST

D_MODEL = 2048
DEPTH = 4
ALPHA = (2.0 * DEPTH) ** 0.25
LN_EPS = 1e-5
RMS_EPS = 1e-6

SSD_D_INNER = 4096
SSD_HEADS = 64
SSD_HEADDIM = 64
SSD_GROUPS = 8
SSD_HEADS_PER_GROUP = SSD_HEADS // SSD_GROUPS
SSD_STATE = 128
SSD_CONV = 4
SSD_CHUNK = 128
SSD_GROUP_WIDTH = SSD_D_INNER // SSD_GROUPS
SSD_BC_WIDTH = SSD_GROUPS * SSD_STATE

FOX_HEADS = 16
FOX_HEAD_DIM = 128
FOX_WIDTH = FOX_HEADS * FOX_HEAD_DIM

MLA_HEADS = 16
MLA_Q_RANK = 512
MLA_KV_RANK = 512
MLA_NOPE = 128
MLA_ROPE = 64
MLA_V = 128
MLA_QK = MLA_NOPE + MLA_ROPE
MLA_WIDTH = MLA_HEADS * MLA_V
ROPE_BASE = 10000.0

S5_WIDTH = D_MODEL
S5_GROUP = 16
S5_GROUPS = S5_WIDTH // S5_GROUP
S5_STATE = 64
S5_CHUNK = 16
S5_ROW = S5_CHUNK * S5_GROUP
S5_PAIRS = S5_GROUPS // 2

LANES = 128
VMEM_LIMIT_BYTES = 48 * 1024 * 1024


def _cparams(*sem):
    return pltpu.CompilerParams(dimension_semantics=sem, vmem_limit_bytes=VMEM_LIMIT_BYTES)


def _sigmoid(x):
    return 1.0 / (1.0 + jnp.exp(-x))


def _silu(x):
    return x * _sigmoid(x)


def _softplus(x):
    return jnp.maximum(x, 0.0) + jnp.log1p(jnp.exp(-jnp.abs(x)))


def _mm_kernel(x_ref, w_ref, o_ref, *, scale):
    acc = jnp.dot(x_ref[...].astype(BF16), w_ref[...], preferred_element_type=F32)
    if scale is not None:
        acc = acc * scale
    o_ref[...] = acc.astype(o_ref.dtype)


def _matmul(x, w, out_dtype, *, scale=None, tm=512, tn=1024):
    M, K = x.shape
    N = w.shape[1]
    tm = min(tm, M)
    tn = min(tn, N)
    assert M % tm == 0 and N % tn == 0, (M, N, tm, tn)
    return pl.pallas_call(
        functools.partial(_mm_kernel, scale=scale),
        grid=(N // tn, M // tm),
        in_specs=[pl.BlockSpec((tm, K), lambda j, i: (i, 0)),
                  pl.BlockSpec((K, tn), lambda j, i: (0, j))],
        out_specs=pl.BlockSpec((tm, tn), lambda j, i: (i, j)),
        out_shape=jax.ShapeDtypeStruct((M, N), out_dtype),
        compiler_params=_cparams("parallel", "parallel"),
        name="proj",
    )(x, w)


def _out_ln_kernel(y_ref, w_ref, h_ref, g_ref, b_ref, of_ref, ob_ref, acc_ref, *, nk):
    k = pl.program_id(1)

    @pl.when(k == 0)
    def _():
        acc_ref[...] = jnp.zeros_like(acc_ref)

    acc_ref[...] += jnp.dot(y_ref[...], w_ref[...], preferred_element_type=F32)

    @pl.when(k == nk - 1)
    def _():
        r = ALPHA * h_ref[...] + acc_ref[...]
        mu = jnp.mean(r, axis=-1, keepdims=True)
        d = r - mu
        var = jnp.mean(d * d, axis=-1, keepdims=True)
        out = d * lax.rsqrt(var + LN_EPS) * g_ref[...] + b_ref[...]
        of_ref[...] = out
        ob_ref[...] = out.astype(BF16)


def _out_proj_ln(y, w, h, g, b, *, tm=512, tk=512):
    M, K = y.shape
    D = w.shape[1]
    tm = min(tm, M)
    tk = min(tk, K)
    assert M % tm == 0 and K % tk == 0
    nk = K // tk
    return pl.pallas_call(
        functools.partial(_out_ln_kernel, nk=nk),
        grid=(M // tm, nk),
        in_specs=[pl.BlockSpec((tm, tk), lambda i, k: (i, k)),
                  pl.BlockSpec((tk, D), lambda i, k: (k, 0)),
                  pl.BlockSpec((tm, D), lambda i, k: (i, 0)),
                  pl.BlockSpec((1, D), lambda i, k: (0, 0)),
                  pl.BlockSpec((1, D), lambda i, k: (0, 0))],
        out_specs=[pl.BlockSpec((tm, D), lambda i, k: (i, 0)),
                   pl.BlockSpec((tm, D), lambda i, k: (i, 0))],
        out_shape=[jax.ShapeDtypeStruct((M, D), F32), jax.ShapeDtypeStruct((M, D), BF16)],
        scratch_shapes=[pltpu.VMEM((tm, D), F32)],
        compiler_params=_cparams("parallel", "arbitrary"),
        name="out_proj_ln",
    )(y, w, h, g.reshape(1, D), b.reshape(1, D))


ATTN_HEADS_PER_STEP = 4


def _attn_kernel(*refs, nparts, tq, tk, dv, head_major):
    q_refs = refs[:nparts]
    k_refs = refs[nparts:2 * nparts]
    v_ref, z_ref, o_ref, m_ref, acc_ref = refs[2 * nparts:]
    nh = ATTN_HEADS_PER_STEP
    i = pl.program_id(2)
    ones_cols = jnp.ones((tk, LANES), BF16)

    def head(ref, hh, rows=slice(None)):
        if head_major:
            return ref[hh, rows, :]
        w = ref.shape[-1] // nh
        return ref[rows, hh * w:(hh + 1) * w]

    def cat(pieces):
        return pieces[0] if len(pieces) == 1 else jnp.concatenate(pieces, axis=1)

    qs = [cat([head(r, hh) for r in q_refs]) for hh in range(nh)]
    m_ref[...] = jnp.full_like(m_ref, -jnp.inf)
    acc_ref[...] = jnp.zeros_like(acc_ref)

    def block(j, diag_offset):
        rows = pl.ds(pl.multiple_of(j * tk, tk), tk)
        scores = []
        for hh in range(nh):
            k = cat([head(r, hh, rows) for r in k_refs])
            scores.append(lax.dot_general(qs[hh], k, (((1,), (1,)), ((), ())),
                                          preferred_element_type=F32))
        probs, alphas = [], []
        for hh in range(nh):
            s = scores[hh]
            if diag_offset is not None:
                r = lax.broadcasted_iota(jnp.int32, (tq, tk), 0)
                c = lax.broadcasted_iota(jnp.int32, (tq, tk), 1)
                s = jnp.where(c + diag_offset <= r, s, -jnp.inf)
            m_prev = m_ref[hh]
            m_new = jnp.maximum(m_prev, jnp.max(s, axis=-1, keepdims=True))
            alpha = jnp.exp(m_prev - m_new)
            p = jnp.exp(s - jnp.concatenate([m_new] * (tk // LANES), axis=1))
            m_ref[hh] = m_new
            probs.append(p.astype(BF16))
            alphas.append(jnp.concatenate([alpha] * ((dv + LANES) // LANES), axis=1))
        for hh in range(nh):
            v_aug = jnp.concatenate([head(v_ref, hh, rows), ones_cols], axis=1)
            acc_ref[hh] = alphas[hh] * acc_ref[hh] + jnp.dot(
                probs[hh], v_aug, preferred_element_type=F32)

    n_full = i * (tq // tk)

    def full_block(j, carry):
        block(j, None)
        return carry

    lax.fori_loop(0, n_full, full_block, 0)
    for d in range(tq // tk):
        block(n_full + d, d * tk)

    for hh in range(nh):
        cols = slice(hh * dv, (hh + 1) * dv)
        o = acc_ref[hh, :, :dv] / acc_ref[hh, :, dv:]
        o_ref[:, cols] = (o * _silu(z_ref[:, cols])).astype(o_ref.dtype)


def _attention(inputs, in_specs, out_spec, *, B, T, H, dv, tq, tk, nparts, head_major):
    nh = ATTN_HEADS_PER_STEP
    assert tq % tk == 0 and T % tq == 0 and H % nh == 0
    return pl.pallas_call(
        functools.partial(_attn_kernel, nparts=nparts, tq=tq, tk=tk, dv=dv,
                          head_major=head_major),
        grid=(B, H // nh, T // tq),
        in_specs=in_specs,
        out_specs=out_spec,
        out_shape=jax.ShapeDtypeStruct((B * T, H * dv), BF16),
        scratch_shapes=[pltpu.VMEM((nh, tq, LANES), F32),
                        pltpu.VMEM((nh, tq, dv + LANES), F32)],
        compiler_params=_cparams("parallel", "parallel", "arbitrary"),
        name="causal_attention",
    )(*inputs)


def _attn_tiles(T):
    tq = min(512, T)
    tk = min(512, T)
    return tq, tk


def _ssd_kernel(x_ref, bm_ref, cm_ref, z_ref, dtc_ref, dtr_ref,
                cwx_ref, cwb_ref, cwc_ref, cbx_ref, cbb_ref, cbc_ref,
                dtbc_ref, dtbr_ref, alc_ref, alr_ref, dsk_ref, nw_ref,
                o_ref,
                extx_ref, extb_ref, extc_ref, state_ref, yz_ref):
    Q = SSD_CHUNK
    c = pl.program_id(2)

    @pl.when(c == 0)
    def _():
        extx_ref[pl.ds(0, 8), :] = jnp.zeros((8, SSD_GROUP_WIDTH), F32)
        extb_ref[pl.ds(0, 8), :] = jnp.zeros((8, SSD_STATE), F32)
        extc_ref[pl.ds(0, 8), :] = jnp.zeros((8, SSD_STATE), F32)
        state_ref[...] = jnp.zeros_like(state_ref)

    def conv_silu(raw_ref, ext_ref, w_ref, b_ref):
        ext_ref[pl.ds(8, Q), :] = raw_ref[...]
        acc = b_ref[...] + w_ref[pl.ds(0, 1), :] * ext_ref[pl.ds(5, Q), :]
        for kk in range(1, SSD_CONV):
            acc = acc + w_ref[pl.ds(kk, 1), :] * ext_ref[pl.ds(5 + kk, Q), :]
        ext_ref[pl.ds(0, 8), :] = ext_ref[pl.ds(Q, 8), :]
        return _silu(acc)

    bm = conv_silu(bm_ref, extb_ref, cwb_ref, cbb_ref)
    cm = conv_silu(cm_ref, extc_ref, cwc_ref, cbc_ref)
    bm_b = bm.astype(BF16)
    cm_b = cm.astype(BF16)
    bmt_b = bm.T.astype(BF16)
    cb = lax.dot_general(cm_b, bm_b, (((1,), (1,)), ((), ())), preferred_element_type=F32)

    row = lax.broadcasted_iota(jnp.int32, (Q, Q), 0)
    col = lax.broadcasted_iota(jnp.int32, (Q, Q), 1)
    lower = row >= col
    tri_l = lower.astype(F32)
    tri_u = (row <= col).astype(F32)
    left = col < SSD_HEADDIM

    dt_c = _softplus(dtc_ref[...] + dtbc_ref[...])
    da_c = dt_c * (-jnp.exp(alc_ref[...]))
    acs_c = jnp.dot(tri_l, da_c, precision=HIGHEST, preferred_element_type=F32)
    dt_r = _softplus(dtr_ref[...] + dtbr_ref[...])
    da_r = dt_r * (-jnp.exp(alr_ref[...]))
    acs_r = jnp.dot(da_r, tri_u, precision=HIGHEST, preferred_element_type=F32)
    dsk = dsk_ref[...]

    def pair_cols(v, e0):
        a = jnp.broadcast_to(v[:, e0:e0 + 1], (v.shape[0], Q))
        b = jnp.broadcast_to(v[:, e0 + 1:e0 + 2], (v.shape[0], Q))
        return jnp.where(left[:v.shape[0]], a, b)

    ssq = jnp.zeros((Q, 1), F32)
    for kp in range(SSD_HEADS_PER_GROUP // 2):
        e0 = 2 * kp
        cols = pl.ds(kp * LANES, LANES)
        ext_ref_rows = extx_ref
        acc = cbx_ref[:, cols]
        ext_ref_rows[pl.ds(8, Q), cols] = x_ref[:, cols]
        for kk in range(SSD_CONV):
            acc = acc + cwx_ref[pl.ds(kk, 1), cols] * ext_ref_rows[pl.ds(5 + kk, Q), cols]
        ext_ref_rows[pl.ds(0, 8), cols] = ext_ref_rows[pl.ds(Q, 8), cols]
        xp = _silu(acc)

        dt_p = pair_cols(dt_c, e0)
        acs_p = pair_cols(acs_c, e0)
        last_p = pair_cols(acs_c[Q - 1:Q, :], e0)
        d_p = pair_cols(dsk, e0)
        xdt = xp * dt_p
        xdt_b = xdt.astype(BF16)

        ys = []
        for e in (e0, e0 + 1):
            seg = acs_c[:, e:e + 1] - acs_r[e:e + 1, :]
            dec = jnp.exp(jnp.where(lower, seg, -jnp.inf))
            ys.append(jnp.dot((cb * dec).astype(BF16), xdt_b, preferred_element_type=F32))
        y_diag = jnp.where(left, ys[0], ys[1])

        xw = (xdt * jnp.exp(last_p - acs_p)).astype(BF16)
        s_loc = jnp.dot(bmt_b, xw, preferred_element_type=F32)
        s_prev = state_ref[kp]
        y_off = jnp.dot(cm_b, s_prev.astype(BF16), preferred_element_type=F32) * jnp.exp(acs_p)
        state_ref[kp] = s_prev * jnp.exp(last_p) + s_loc

        y = (y_diag + y_off + d_p * xp) * _silu(z_ref[:, cols])
        yz_ref[:, cols] = y
        ssq = ssq + jnp.sum(y * y, axis=-1, keepdims=True)

    inv = lax.rsqrt(ssq * (1.0 / SSD_GROUP_WIDTH) + RMS_EPS)
    o_ref[...] = (yz_ref[...] * inv * nw_ref[...]).astype(o_ref.dtype)


def _ssd_core(z, xbc, dt, conv_w, conv_b, dt_bias, a_log, d_skip, norm_w, *, B, T):
    G, E, Q = SSD_GROUPS, SSD_HEADS_PER_GROUP, SSD_CHUNK
    NC = T // Q
    GW, N = SSD_GROUP_WIDTH, SSD_STATE
    xblocks = SSD_D_INNER // N
    dt4 = dt[:, :SSD_HEADS].reshape(B, T, G, E)
    dtc = jnp.transpose(dt4, (0, 2, 1, 3))
    dtr = jnp.transpose(dt4, (0, 2, 3, 1))
    cw = conv_w
    cb = conv_b.reshape(1, -1)
    per_group = lambda v: v.reshape(G, E)
    dtb, al, dsk = per_group(dt_bias), per_group(a_log), per_group(d_skip)
    col3 = lambda v: v.reshape(G, 1, E)
    row3 = lambda v: v.reshape(G, E, 1)

    rowblk = lambda b, g, c: b * NC + c
    in_specs = [
        pl.BlockSpec((Q, GW), lambda b, g, c: (rowblk(b, g, c), g)),
        pl.BlockSpec((Q, N), lambda b, g, c: (rowblk(b, g, c), xblocks + g)),
        pl.BlockSpec((Q, N), lambda b, g, c: (rowblk(b, g, c), xblocks + G + g)),
        pl.BlockSpec((Q, GW), lambda b, g, c: (rowblk(b, g, c), g)),
        pl.BlockSpec((None, None, Q, E), lambda b, g, c: (b, g, c, 0)),
        pl.BlockSpec((None, None, E, Q), lambda b, g, c: (b, g, 0, c)),
        pl.BlockSpec((SSD_CONV, GW), lambda b, g, c: (0, g)),
        pl.BlockSpec((SSD_CONV, N), lambda b, g, c: (0, xblocks + g)),
        pl.BlockSpec((SSD_CONV, N), lambda b, g, c: (0, xblocks + G + g)),
        pl.BlockSpec((1, GW), lambda b, g, c: (0, g)),
        pl.BlockSpec((1, N), lambda b, g, c: (0, xblocks + g)),
        pl.BlockSpec((1, N), lambda b, g, c: (0, xblocks + G + g)),
        pl.BlockSpec((None, 1, E), lambda b, g, c: (g, 0, 0)),
        pl.BlockSpec((None, E, 1), lambda b, g, c: (g, 0, 0)),
        pl.BlockSpec((None, 1, E), lambda b, g, c: (g, 0, 0)),
        pl.BlockSpec((None, E, 1), lambda b, g, c: (g, 0, 0)),
        pl.BlockSpec((None, 1, E), lambda b, g, c: (g, 0, 0)),
        pl.BlockSpec((1, GW), lambda b, g, c: (0, g)),
    ]
    return pl.pallas_call(
        _ssd_kernel,
        grid=(B, G, NC),
        in_specs=in_specs,
        out_specs=pl.BlockSpec((Q, GW), lambda b, g, c: (rowblk(b, g, c), g)),
        out_shape=jax.ShapeDtypeStruct((B * T, SSD_D_INNER), BF16),
        scratch_shapes=[pltpu.VMEM((Q + 8, GW), F32), pltpu.VMEM((Q + 8, N), F32),
                        pltpu.VMEM((Q + 8, N), F32),
                        pltpu.VMEM((E // 2, N, LANES), F32), pltpu.VMEM((Q, GW), F32)],
        compiler_params=_cparams("parallel", "parallel", "arbitrary"),
        name="ssd_chunk_scan",
    )(xbc, xbc, xbc, z, dtc, dtr, cw, cw, cw, cb, cb, cb,
      col3(dtb), row3(dtb), col3(al), row3(al), col3(dsk), norm_w.reshape(1, -1))


def _ssd_layer(hf, hb, w_in, conv_w, conv_b, dt_bias, a_log, d_skip, norm_w, w_out, g, b, *, B, T):
    di = SSD_D_INNER
    cd = di + 2 * SSD_BC_WIDTH
    z = _matmul(hb, w_in[:, :di].astype(BF16), F32)
    xbc = _matmul(hb, w_in[:, di:di + cd].astype(BF16), F32)
    w_dt = jnp.pad(w_in[:, di + cd:], ((0, 0), (0, LANES - SSD_HEADS))).astype(BF16)
    dt = _matmul(hb, w_dt, F32)
    y = _ssd_core(z, xbc, dt, conv_w, conv_b, dt_bias, a_log, d_skip, norm_w, B=B, T=T)
    return _out_proj_ln(y, w_out.astype(BF16), hf, g, b)


FOX_BIAS_PIECES = 3


def _fox_bias_kernel(f_ref, b_ref, qx_ref, kx_ref, carry_ref, *, nblk):
    blk, W, P = LANES, FOX_WIDTH, FOX_BIAS_PIECES
    row = lax.broadcasted_iota(jnp.int32, (blk, blk), 0)
    col = lax.broadcasted_iota(jnp.int32, (blk, blk), 1)
    tri = (row >= col).astype(F32)
    bias = b_ref[...]
    r = lax.broadcasted_iota(jnp.int32, (P * blk, W), 0)
    c = lax.broadcasted_iota(jnp.int32, (P * blk, W), 1)
    head, piece = r % blk, r // blk
    is_head = head < FOX_HEADS
    place_q = jnp.logical_and(is_head, c == head * blk + piece).astype(BF16)
    place_k = jnp.logical_and(is_head, c == head * blk + P + piece).astype(BF16)
    cmod = lax.broadcasted_iota(jnp.int32, (1, W), 1) % blk
    ones_q = jnp.logical_and(cmod >= P, cmod < 2 * P).astype(F32)
    ones_k = (cmod < P).astype(F32)

    def body(t, carry):
        rows = pl.ds(pl.multiple_of(t * blk, blk), blk)
        x = f_ref[rows, :] + bias
        logf = jnp.minimum(x, 0.0) - jnp.log1p(jnp.exp(-jnp.abs(x)))
        cum = jnp.dot(tri, logf, precision=HIGHEST, preferred_element_type=F32) + carry
        hi = cum.astype(BF16)
        r1 = cum - hi.astype(F32)
        mid = r1.astype(BF16)
        lo = (r1 - mid.astype(F32)).astype(BF16)
        pieces = jnp.concatenate([hi, mid, lo], axis=1)
        qx_ref[rows, :] = (jnp.dot(pieces, place_q, preferred_element_type=F32)
                           + ones_q).astype(BF16)
        kx_ref[rows, :] = (ones_k - jnp.dot(pieces, place_k, preferred_element_type=F32)
                           ).astype(BF16)
        return cum[blk - 1:blk, :]

    @pl.when(pl.program_id(1) == 0)
    def _():
        carry_ref[...] = jnp.zeros_like(carry_ref)

    carry_ref[...] = lax.fori_loop(0, nblk, body, carry_ref[...])


def _fox_bias_columns(f, f_bias, *, B, T):
    bias = jnp.pad(f_bias, (0, LANES - FOX_HEADS)).reshape(1, LANES)
    tt = min(512, T)
    out = jax.ShapeDtypeStruct((B, T, FOX_WIDTH), BF16)
    spec = pl.BlockSpec((None, tt, FOX_WIDTH), lambda b, t: (b, t, 0))
    qx, kx = pl.pallas_call(
        functools.partial(_fox_bias_kernel, nblk=tt // LANES),
        grid=(B, T // tt),
        in_specs=[pl.BlockSpec((None, tt, LANES), lambda b, t: (b, t, 0)),
                  pl.BlockSpec((1, LANES), lambda b, t: (0, 0))],
        out_specs=[spec, spec],
        out_shape=[out, out],
        scratch_shapes=[pltpu.VMEM((1, LANES), F32)],
        compiler_params=_cparams("parallel", "arbitrary"),
        name="fox_cum_log_forget",
    )(f.reshape(B, T, LANES), bias)
    return qx.reshape(B * T, FOX_WIDTH), kx.reshape(B * T, FOX_WIDTH)


def _fox_layer(hf, hb, w_in, f_bias, w_out, g, b, *, B, T):
    W, H, dh = FOX_WIDTH, FOX_HEADS, FOX_HEAD_DIM
    q = _matmul(hb, w_in[:, :W].astype(BF16), BF16, scale=dh ** -0.5)
    kv = _matmul(hb, w_in[:, W:3 * W].astype(BF16), BF16)
    z = _matmul(hb, w_in[:, 3 * W:4 * W].astype(BF16), F32)
    w_f = jnp.pad(w_in[:, 4 * W:], ((0, 0), (0, LANES - H))).astype(BF16)
    f = _matmul(hb, w_f, F32)
    qx, kx = _fox_bias_columns(f, f_bias, B=B, T=T)

    tq, tk = _attn_tiles(T)
    nq = T // tq
    hw = ATTN_HEADS_PER_STEP * dh
    q_tile = pl.BlockSpec((tq, hw), lambda bb, h, i: (bb * nq + i, h))
    seq_k = pl.BlockSpec((T, hw), lambda bb, h, i: (bb, h))
    seq_v = pl.BlockSpec((T, hw), lambda bb, h, i: (bb, W // hw + h))
    in_specs = [q_tile, q_tile, seq_k, seq_k, seq_v, q_tile]
    y = _attention((q, qx, kv, kx, kv, z), in_specs, q_tile,
                   B=B, T=T, H=H, dv=dh, tq=tq, tk=tk, nparts=2, head_major=False)
    return _out_proj_ln(y, w_out.astype(BF16), hf, g, b)


def _rope_table_kernel(pos_ref, freq_ref, sign_ref, cos_ref, sin_ref):
    ang = pos_ref[...].astype(F32) * freq_ref[...]
    cos_ref[...] = jnp.cos(ang)
    sin_ref[...] = jnp.sin(ang) * sign_ref[...]


def _rope_tables(positions, *, B, T):
    half = MLA_ROPE // 2
    inv_freq = ROPE_BASE ** (-jnp.arange(0, MLA_ROPE, 2, dtype=F32) / MLA_ROPE)
    freq = jnp.tile(inv_freq, LANES // half).reshape(1, LANES)
    sign = jnp.tile(jnp.concatenate([-jnp.ones((half,), F32), jnp.ones((half,), F32)]),
                    LANES // MLA_ROPE).reshape(1, LANES)
    M = B * T
    tm = min(1024, M)
    return pl.pallas_call(
        _rope_table_kernel,
        grid=(M // tm,),
        in_specs=[pl.BlockSpec((tm, 1), lambda i: (i, 0)),
                  pl.BlockSpec((1, LANES), lambda i: (0, 0)),
                  pl.BlockSpec((1, LANES), lambda i: (0, 0))],
        out_specs=[pl.BlockSpec((tm, LANES), lambda i: (i, 0)),
                   pl.BlockSpec((tm, LANES), lambda i: (i, 0))],
        out_shape=[jax.ShapeDtypeStruct((M, LANES), F32), jax.ShapeDtypeStruct((M, LANES), F32)],
        compiler_params=_cparams("parallel"),
        name="rope_tables",
    )(positions.reshape(M, 1), freq, sign)


def _rms_to_bf16(x, w):
    y = x * lax.rsqrt(jnp.mean(x * x, axis=-1, keepdims=True) + RMS_EPS)
    return (y * w).astype(BF16)


def _rope_pair(c2, cos, sin):
    return c2 * cos + pltpu.roll(c2, MLA_ROPE, 1) * sin


def _mla_q_kernel(x_ref, nw_ref, w_ref, cos_ref, sin_ref, o_ref, xn_ref, *, scale):
    @pl.when(pl.program_id(1) == 0)
    def _():
        xn_ref[...] = _rms_to_bf16(x_ref[...], nw_ref[...])

    r = jnp.dot(xn_ref[...], w_ref[...], preferred_element_type=F32)
    pe = _rope_pair(r[:, MLA_NOPE:], cos_ref[...], sin_ref[...])
    o_ref[:, :MLA_NOPE] = (r[:, :MLA_NOPE] * scale).astype(o_ref.dtype)
    o_ref[:, MLA_NOPE:] = (pe[:, :MLA_ROPE] * scale).astype(o_ref.dtype)


def _mla_kv_kernel(x_ref, nw_ref, w_ref, kpe_ref, cos_ref, sin_ref, k_ref, v_ref, xn_ref, pe_ref):
    @pl.when(pl.program_id(1) == 0)
    def _():
        xn_ref[...] = _rms_to_bf16(x_ref[...], nw_ref[...])
        pe_ref[...] = _rope_pair(kpe_ref[...], cos_ref[...], sin_ref[...])

    r = jnp.dot(xn_ref[...], w_ref[...], preferred_element_type=F32)
    k_ref[:, :MLA_NOPE] = r[:, :MLA_NOPE].astype(k_ref.dtype)
    k_ref[:, MLA_NOPE:] = pe_ref[:, :MLA_ROPE].astype(k_ref.dtype)
    v_ref[...] = r[:, MLA_NOPE:].astype(v_ref.dtype)


def _swap_halves(w):
    half = w.shape[-1] // 2
    return jnp.concatenate([w[..., half:], w[..., :half]], axis=-1)


def _mla_layer(hf, hb, positions, w_in, q_norm, kv_norm, w_q_up, w_kv_up, w_out, g, b, *, B, T):
    H, M = MLA_HEADS, B * T
    qr, kr = MLA_Q_RANK, MLA_KV_RANK
    q_lat = _matmul(hb, w_in[:, :qr].astype(BF16), F32)
    kv_lat = _matmul(hb, w_in[:, qr:qr + kr].astype(BF16), F32)
    w_pe = w_in[:, qr + kr:qr + kr + MLA_ROPE]
    kpe2 = _matmul(hb, jnp.concatenate([w_pe, _swap_halves(w_pe)], axis=1).astype(BF16), F32)
    z = _matmul(hb, w_in[:, qr + kr + MLA_ROPE:].astype(BF16), F32)
    cos, sin = _rope_tables(positions, B=B, T=T)

    wq = w_q_up.reshape(qr, H, MLA_QK)
    wq = jnp.concatenate([wq, _swap_halves(wq[..., MLA_NOPE:])], axis=-1)
    wq = jnp.transpose(wq, (1, 0, 2)).astype(BF16)
    wkv = jnp.transpose(w_kv_up.reshape(kr, H, MLA_NOPE + MLA_V), (1, 0, 2)).astype(BF16)

    tm = min(512, M)
    wide = MLA_NOPE + LANES
    row = lambda i, h: (i, 0)
    q = pl.pallas_call(
        functools.partial(_mla_q_kernel, scale=MLA_QK ** -0.5),
        grid=(M // tm, H),
        in_specs=[pl.BlockSpec((tm, qr), row),
                  pl.BlockSpec((1, qr), lambda i, h: (0, 0)),
                  pl.BlockSpec((None, qr, wide), lambda i, h: (h, 0, 0)),
                  pl.BlockSpec((tm, LANES), row),
                  pl.BlockSpec((tm, LANES), row)],
        out_specs=pl.BlockSpec((None, tm, MLA_QK), lambda i, h: (h, i, 0)),
        out_shape=jax.ShapeDtypeStruct((H, M, MLA_QK), BF16),
        scratch_shapes=[pltpu.VMEM((tm, qr), BF16)],
        compiler_params=_cparams("parallel", "arbitrary"),
        name="mla_q_up_rope",
    )(q_lat, q_norm.reshape(1, qr), wq, cos, sin)
    k, v = pl.pallas_call(
        _mla_kv_kernel,
        grid=(M // tm, H),
        in_specs=[pl.BlockSpec((tm, kr), row),
                  pl.BlockSpec((1, kr), lambda i, h: (0, 0)),
                  pl.BlockSpec((None, kr, MLA_NOPE + MLA_V), lambda i, h: (h, 0, 0)),
                  pl.BlockSpec((tm, LANES), row),
                  pl.BlockSpec((tm, LANES), row),
                  pl.BlockSpec((tm, LANES), row)],
        out_specs=[pl.BlockSpec((None, tm, MLA_QK), lambda i, h: (h, i, 0)),
                   pl.BlockSpec((None, tm, MLA_V), lambda i, h: (h, i, 0))],
        out_shape=[jax.ShapeDtypeStruct((H, M, MLA_QK), BF16),
                   jax.ShapeDtypeStruct((H, M, MLA_V), BF16)],
        scratch_shapes=[pltpu.VMEM((tm, kr), BF16), pltpu.VMEM((tm, LANES), F32)],
        compiler_params=_cparams("parallel", "arbitrary"),
        name="mla_kv_up_rope",
    )(kv_lat, kv_norm.reshape(1, kr), wkv, kpe2, cos, sin)

    tq, tk = _attn_tiles(T)
    nq = T // tq
    nh = ATTN_HEADS_PER_STEP
    in_specs = [
        pl.BlockSpec((nh, tq, MLA_QK), lambda bb, h, i: (h, bb * nq + i, 0)),
        pl.BlockSpec((nh, T, MLA_QK), lambda bb, h, i: (h, bb, 0)),
        pl.BlockSpec((nh, T, MLA_V), lambda bb, h, i: (h, bb, 0)),
        pl.BlockSpec((tq, nh * MLA_V), lambda bb, h, i: (bb * nq + i, h)),
    ]
    out_spec = pl.BlockSpec((tq, nh * MLA_V), lambda bb, h, i: (bb * nq + i, h))
    y = _attention((q, k, v, z), in_specs, out_spec,
                   B=B, T=T, H=H, dv=MLA_V, tq=tq, tk=tk, nparts=1, head_major=True)
    return _out_proj_ln(y, w_out.astype(BF16), hf, g, b)


def _s5_matrices(lam_re, lam_im, log_step, b_re, b_im, c_re, c_im):
    L = S5_CHUNK
    step = jnp.exp(log_step.astype(F32))[:, None]
    mag = jnp.exp(lam_re * step)
    ar = mag * jnp.cos(lam_im * step)
    ai = mag * jnp.sin(lam_im * step)
    den = lam_re * lam_re + lam_im * lam_im
    fr = ((ar - 1.0) * lam_re + ai * lam_im) / den
    fi = (ai * lam_re - (ar - 1.0) * lam_im) / den
    bbr = fr[..., None] * b_re - fi[..., None] * b_im
    bbi = fr[..., None] * b_im + fi[..., None] * b_re
    pr, pi = [jnp.ones_like(ar)], [jnp.zeros_like(ar)]
    for _ in range(L):
        pr_next = pr[-1] * ar - pi[-1] * ai
        pi_next = pr[-1] * ai + pi[-1] * ar
        pr.append(pr_next)
        pi.append(pi_next)
    pwr = jnp.stack(pr, axis=1)
    pwi = jnp.stack(pi, axis=1)

    cpr = c_re[:, None] * pwr[:, :L, None, :] - c_im[:, None] * pwi[:, :L, None, :]
    cpi = c_re[:, None] * pwi[:, :L, None, :] + c_im[:, None] * pwr[:, :L, None, :]
    kern = jnp.einsum('gdip,gpj->gdij', cpr, bbr, precision=HIGHEST) \
        - jnp.einsum('gdip,gpj->gdij', cpi, bbi, precision=HIGHEST)
    r = jnp.arange(L)
    lag = r[:, None] - r[None, :]
    toe = kern[:, jnp.clip(lag, 0, L - 1)]
    toe = jnp.where((lag >= 0)[None, :, :, None, None], toe, 0.0)
    tmt = jnp.transpose(toe, (0, 2, 4, 1, 3)).reshape(S5_GROUPS, S5_ROW, S5_ROW)

    qr = pwr[:, L - 1::-1][:, :L]
    qi = pwi[:, L - 1::-1][:, :L]
    bpr = qr[:, :, :, None] * bbr[:, None] - qi[:, :, :, None] * bbi[:, None]
    bpi = qr[:, :, :, None] * bbi[:, None] + qi[:, :, :, None] * bbr[:, None]
    to_rows = lambda m: jnp.transpose(m, (0, 1, 3, 2)).reshape(S5_GROUPS, S5_ROW, S5_STATE)
    bpr, bpi = to_rows(bpr), to_rows(bpi)

    c1r = c_re[:, None] * pwr[:, 1:, None, :] - c_im[:, None] * pwi[:, 1:, None, :]
    c1i = c_re[:, None] * pwi[:, 1:, None, :] + c_im[:, None] * pwr[:, 1:, None, :]
    to_cols = lambda m: jnp.transpose(m, (0, 3, 1, 2)).reshape(S5_GROUPS, S5_STATE, S5_ROW)
    cpw_r, cpw_i = to_cols(c1r), -to_cols(c1i)

    zs = jnp.zeros((S5_PAIRS, S5_ROW, S5_STATE), F32)
    ev, od = slice(0, None, 2), slice(1, None, 2)
    bp_pair = jnp.concatenate([
        jnp.concatenate([bpr[ev], zs, bpi[ev], zs], axis=2),
        jnp.concatenate([zs, bpr[od], zs, bpi[od]], axis=2)], axis=1)
    zc = jnp.zeros((S5_PAIRS, S5_STATE, S5_ROW), F32)
    cp_pair = jnp.concatenate([
        jnp.concatenate([cpw_r[ev], zc], axis=2),
        jnp.concatenate([zc, cpw_r[od]], axis=2),
        jnp.concatenate([cpw_i[ev], zc], axis=2),
        jnp.concatenate([zc, cpw_i[od]], axis=2)], axis=1)
    alr = pwr[:, L].reshape(S5_PAIRS, 2 * S5_STATE)
    ali = pwi[:, L].reshape(S5_PAIRS, 2 * S5_STATE)
    a1 = jnp.concatenate([alr, alr], axis=1).reshape(1, -1)
    a2 = jnp.concatenate([-ali, ali], axis=1).reshape(1, -1)
    return tmt.astype(BF16), bp_pair.astype(BF16), cp_pair.astype(BF16), a1, a2


def _s5_local_state_kernel(x_ref, bp_ref, s_ref):
    s_ref[...] = jnp.dot(x_ref[...].astype(BF16), bp_ref[...], preferred_element_type=F32)


def _s5_scan_kernel(s_ref, a1_ref, a2_ref, o_ref, *, B, nch):
    W = s_ref.shape[1]
    a1 = a1_ref[...]
    a2 = a2_ref[...]

    def swap_re_im(c):
        parts = []
        for q in range(W // (2 * LANES)):
            lo = q * 2 * LANES
            parts += [c[:, lo + LANES:lo + 2 * LANES], c[:, lo:lo + LANES]]
        return jnp.concatenate(parts, axis=1)

    def body(n, carry):
        new = []
        for bb in range(B):
            row = bb * nch + n
            c = carry[bb]
            o_ref[pl.ds(row, 1), :] = c
            new.append(a1 * c + a2 * swap_re_im(c) + s_ref[pl.ds(row, 1), :])
        return tuple(new)

    lax.fori_loop(0, nch, body, tuple(jnp.zeros((1, W), F32) for _ in range(B)))


def _gelu_tanh(y):
    return 0.5 * y * (1.0 + jnp.tanh(math.sqrt(2.0 / math.pi) * (y + 0.044715 * (y * y * y))))


def _s5_out_kernel(x_ref, st_ref, tmt_ref, cp_ref, d_ref, o_ref):
    x = x_ref[...]
    xb = x.astype(BF16)
    off = jnp.dot(st_ref[...].astype(BF16), cp_ref[...], preferred_element_type=F32)
    for gi in range(2):
        cols = slice(gi * S5_ROW, (gi + 1) * S5_ROW)
        y = jnp.dot(xb[:, cols], tmt_ref[gi], preferred_element_type=F32)
        y = y + off[:, cols] + d_ref[:, cols] * x[:, cols]
        o_ref[:, cols] = _gelu_tanh(y)


def _glu_kernel(y_ref, w_ref, b_ref, yb_ref, z_ref, o_ref):
    t = jnp.dot(y_ref[...].astype(BF16), w_ref[...], preferred_element_type=F32) + b_ref[...]
    o_ref[...] = (yb_ref[...] * _sigmoid(t) * _silu(z_ref[...])).astype(o_ref.dtype)


def _s5_layer(hf, hb, w_in, lam_re, lam_im, log_step, b_re, b_im, c_re, c_im, d_skip,
              w_glu, b_glu, w_out, g, b, *, B, T):
    M, W, G, L = B * T, S5_WIDTH, S5_GROUPS, S5_CHUNK
    nch = T // L
    rows = B * nch
    u = _matmul(hb, w_in[:, :W].astype(BF16), F32)
    z = _matmul(hb, w_in[:, W:].astype(BF16), F32)
    tmt, bp_pair, cp_pair, a1, a2 = _s5_matrices(lam_re, lam_im, log_step, b_re, b_im, c_re, c_im)

    x = jnp.transpose(u.reshape(rows, L, G, S5_GROUP), (0, 2, 1, 3)).reshape(rows, G * S5_ROW)
    pw_x, pw_s = 2 * S5_ROW, 4 * S5_STATE
    s_loc = pl.pallas_call(
        _s5_local_state_kernel,
        grid=(S5_PAIRS,),
        in_specs=[pl.BlockSpec((rows, pw_x), lambda p: (0, p)),
                  pl.BlockSpec((None, pw_x, pw_s), lambda p: (p, 0, 0))],
        out_specs=pl.BlockSpec((rows, pw_s), lambda p: (0, p)),
        out_shape=jax.ShapeDtypeStruct((rows, S5_PAIRS * pw_s), F32),
        compiler_params=_cparams("parallel"),
        name="s5_chunk_state",
    )(x, bp_pair)

    sw = 8 * pw_s
    st_in = pl.pallas_call(
        functools.partial(_s5_scan_kernel, B=B, nch=nch),
        grid=(S5_PAIRS * pw_s // sw,),
        in_specs=[pl.BlockSpec((rows, sw), lambda p: (0, p)),
                  pl.BlockSpec((1, sw), lambda p: (0, p)),
                  pl.BlockSpec((1, sw), lambda p: (0, p))],
        out_specs=pl.BlockSpec((rows, sw), lambda p: (0, p)),
        out_shape=jax.ShapeDtypeStruct((rows, S5_PAIRS * pw_s), F32),
        compiler_params=_cparams("parallel"),
        name="s5_state_scan",
    )(s_loc, a1, a2)

    d_rows = jnp.broadcast_to(d_skip.reshape(G, 1, S5_GROUP), (G, L, S5_GROUP)).reshape(1, G * S5_ROW)
    y = pl.pallas_call(
        _s5_out_kernel,
        grid=(S5_PAIRS,),
        in_specs=[pl.BlockSpec((rows, pw_x), lambda p: (0, p)),
                  pl.BlockSpec((rows, pw_s), lambda p: (0, p)),
                  pl.BlockSpec((2, S5_ROW, S5_ROW), lambda p: (p, 0, 0)),
                  pl.BlockSpec((None, pw_s, pw_x), lambda p: (p, 0, 0)),
                  pl.BlockSpec((1, pw_x), lambda p: (0, p))],
        out_specs=pl.BlockSpec((rows, pw_x), lambda p: (0, p)),
        out_shape=jax.ShapeDtypeStruct((rows, G * S5_ROW), F32),
        compiler_params=_cparams("parallel"),
        name="s5_chunk_output",
    )(x, st_in, tmt, cp_pair, d_rows)
    y = jnp.transpose(y.reshape(rows, G, L, S5_GROUP), (0, 2, 1, 3)).reshape(M, W)

    tm, tn = min(512, M), 512
    y3 = pl.pallas_call(
        _glu_kernel,
        grid=(M // tm, W // tn),
        in_specs=[pl.BlockSpec((tm, W), lambda i, j: (i, 0)),
                  pl.BlockSpec((W, tn), lambda i, j: (0, j)),
                  pl.BlockSpec((1, tn), lambda i, j: (0, j)),
                  pl.BlockSpec((tm, tn), lambda i, j: (i, j)),
                  pl.BlockSpec((tm, tn), lambda i, j: (i, j))],
        out_specs=pl.BlockSpec((tm, tn), lambda i, j: (i, j)),
        out_shape=jax.ShapeDtypeStruct((M, W), BF16),
        compiler_params=_cparams("parallel", "parallel"),
        name="s5_glu_gate",
    )(y, w_glu.astype(BF16), b_glu.reshape(1, W), y, z)
    return _out_proj_ln(y3, w_out.astype(BF16), hf, g, b)


def kernel(x, positions, ln_g, ln_b, ssd_w_in, ssd_conv_w, ssd_conv_b, ssd_dt_bias, ssd_a_log, ssd_d, ssd_norm_w, ssd_w_out, fox_w_in, fox_f_bias, fox_w_out, mla_w_in, mla_q_norm, mla_kv_norm, mla_w_q_up, mla_w_kv_up, mla_w_out, s5_w_in, s5_lambda_re, s5_lambda_im, s5_log_step, s5_b_re, s5_b_im, s5_c_re, s5_c_im, s5_d, s5_w_glu, s5_b_glu, s5_w_out):
    B, T, D = x.shape
    hf = x.reshape(B * T, D)
    hb = hf
    for i in range(DEPTH):
        j = i // 4
        kind = i % 4
        g, b = ln_g[i], ln_b[i]
        if kind == 0:
            hf, hb = _ssd_layer(hf, hb, ssd_w_in[j], ssd_conv_w[j], ssd_conv_b[j], ssd_dt_bias[j],
                                ssd_a_log[j], ssd_d[j], ssd_norm_w[j], ssd_w_out[j], g, b, B=B, T=T)
        elif kind == 1:
            hf, hb = _fox_layer(hf, hb, fox_w_in[j], fox_f_bias[j], fox_w_out[j], g, b, B=B, T=T)
        elif kind == 2:
            hf, hb = _mla_layer(hf, hb, positions, mla_w_in[j], mla_q_norm[j], mla_kv_norm[j],
                                mla_w_q_up[j], mla_w_kv_up[j], mla_w_out[j], g, b, B=B, T=T)
        else:
            hf, hb = _s5_layer(hf, hb, s5_w_in[j], s5_lambda_re[j], s5_lambda_im[j], s5_log_step[j],
                               s5_b_re[j], s5_b_im[j], s5_c_re[j], s5_c_im[j], s5_d[j],
                               s5_w_glu[j], s5_b_glu[j], s5_w_out[j], g, b, B=B, T=T)
    return hf.reshape(B, T, D)
```

```python
import functools
import math

import jax
import jax.numpy as jnp
from jax import lax
from jax.experimental import pallas as pl
from jax.experimental.pallas import tpu as pltpu

F32 = jnp.float32
BF16 = jnp.bfloat16
HIGHEST = lax.Precision.HIGHEST

D_MODEL = 2048
DEPTH = 4
ALPHA = (2.0 * DEPTH) ** 0.25
LN_EPS = 1e-5
RMS_EPS = 1e-6

SSD_D_INNER = 4096
SSD_HEADS = 64
SSD_HEADDIM = 64
SSD_GROUPS = 8
SSD_HEADS_PER_GROUP = SSD_HEADS // SSD_GROUPS
SSD_STATE = 128
SSD_CONV = 4
SSD_CHUNK = 128
SSD_GROUP_WIDTH = SSD_D_INNER // SSD_GROUPS
SSD_BC_WIDTH = SSD_GROUPS * SSD_STATE

FOX_HEADS = 16
FOX_HEAD_DIM = 128
FOX_WIDTH = FOX_HEADS * FOX_HEAD_DIM

MLA_HEADS = 16
MLA_Q_RANK = 512
MLA_KV_RANK = 512
MLA_NOPE = 128
MLA_ROPE = 64
MLA_V = 128
MLA_QK = MLA_NOPE + MLA_ROPE
MLA_WIDTH = MLA_HEADS * MLA_V
ROPE_BASE = 10000.0

S5_WIDTH = D_MODEL
S5_GROUP = 16
S5_GROUPS = S5_WIDTH // S5_GROUP
S5_STATE = 64
S5_CHUNK = 16
S5_ROW = S5_CHUNK * S5_GROUP
S5_PAIRS = S5_GROUPS // 2

LANES = 128
VMEM_LIMIT_BYTES = 48 * 1024 * 1024


def _cparams(*sem):
    return pltpu.CompilerParams(dimension_semantics=sem, vmem_limit_bytes=VMEM_LIMIT_BYTES)


def _resident(block_shape, index_map):
    return pl.BlockSpec(block_shape, index_map, pipeline_mode=pl.Buffered(1))


def _sigmoid(x):
    return 0.5 * (jnp.tanh(0.5 * x) + 1.0)


def _silu(x):
    h = 0.5 * x
    return h + h * jnp.tanh(h)


def _softplus(x):
    return jnp.maximum(x, 0.0) + jnp.log1p(jnp.exp(-jnp.abs(x)))


def _deepnorm_ln(h, branch, g, b):
    r = ALPHA * h + branch
    mu = jnp.mean(r, axis=-1, keepdims=True)
    d = r - mu
    var = jnp.mean(d * d, axis=-1, keepdims=True)
    return d * lax.rsqrt(var + LN_EPS) * g + b


def _mm_kernel(x_ref, w_ref, o_ref, wb_ref, *, scale):
    @pl.when(pl.program_id(1) == 0)
    def _():
        wb_ref[...] = w_ref[...].astype(BF16)

    acc = jnp.dot(x_ref[...].astype(BF16), wb_ref[...], preferred_element_type=F32)
    if scale is not None:
        acc = acc * scale
    o_ref[...] = acc.astype(o_ref.dtype)


def _matmul(x, w, out_dtype, *, col0=0, ncols=None, scale=None, tm=512, tn=1024):
    M, K = x.shape
    N = w.shape[1] - col0 if ncols is None else ncols
    tm = min(tm, M)
    tn = min(tn, N)
    assert M % tm == 0 and N % tn == 0 and col0 % tn == 0, (M, N, col0, tm, tn)
    j0 = col0 // tn
    return pl.pallas_call(
        functools.partial(_mm_kernel, scale=scale),
        grid=(N // tn, M // tm),
        in_specs=[pl.BlockSpec((tm, K), lambda j, i: (i, 0)),
                  pl.BlockSpec((K, tn), lambda j, i: (0, j0 + j))],
        out_specs=pl.BlockSpec((tm, tn), lambda j, i: (i, j)),
        out_shape=jax.ShapeDtypeStruct((M, N), out_dtype),
        scratch_shapes=[pltpu.VMEM((K, tn), BF16)],
        compiler_params=_cparams("parallel", "arbitrary"),
        name="proj",
    )(x, w)


def _out_ln_kernel(y_ref, w_ref, h_ref, g_ref, b_ref, of_ref, ob_ref):
    branch = jnp.dot(y_ref[...], w_ref[...], preferred_element_type=F32)
    out = _deepnorm_ln(h_ref[...], branch, g_ref[...], b_ref[...])
    of_ref[...] = out
    ob_ref[...] = out.astype(BF16)


def _out_proj_ln(y, w, h, g, b, *, tm=256):
    M, K = y.shape
    D = w.shape[1]
    tm = min(tm, M)
    assert M % tm == 0
    return pl.pallas_call(
        _out_ln_kernel,
        grid=(M // tm,),
        in_specs=[pl.BlockSpec((tm, K), lambda i: (i, 0)),
                  _resident((K, D), lambda i: (0, 0)),
                  pl.BlockSpec((tm, D), lambda i: (i, 0)),
                  pl.BlockSpec((1, D), lambda i: (0, 0)),
                  pl.BlockSpec((1, D), lambda i: (0, 0))],
        out_specs=[pl.BlockSpec((tm, D), lambda i: (i, 0)),
                   pl.BlockSpec((tm, D), lambda i: (i, 0))],
        out_shape=[jax.ShapeDtypeStruct((M, D), F32), jax.ShapeDtypeStruct((M, D), BF16)],
        compiler_params=_cparams("parallel"),
        name="out_proj_ln",
    )(y, w, h, g.reshape(1, D), b.reshape(1, D))


ATTN_HEADS_PER_STEP = 4


def _attn_kernel(*refs, nparts, tq, tk, dv, head_major):
    q_refs = refs[:nparts]
    k_refs = refs[nparts:2 * nparts]
    v_ref, z_ref, o_ref, m_ref, acc_ref = refs[2 * nparts:]
    nh = ATTN_HEADS_PER_STEP
    i = pl.program_id(2)
    ones_cols = jnp.ones((tk, LANES), BF16)

    def head(ref, hh, rows=slice(None)):
        if head_major:
            return ref[hh, rows, :]
        w = ref.shape[-1] // nh
        return ref[rows, hh * w:(hh + 1) * w]

    def cat(pieces):
        return pieces[0] if len(pieces) == 1 else jnp.concatenate(pieces, axis=1)

    qs = [cat([head(r, hh) for r in q_refs]) for hh in range(nh)]
    m_ref[...] = jnp.full_like(m_ref, -jnp.inf)
    acc_ref[...] = jnp.zeros_like(acc_ref)

    def block(j, diag_offset):
        rows = pl.ds(pl.multiple_of(j * tk, tk), tk)
        scores = []
        for hh in range(nh):
            k = cat([head(r, hh, rows) for r in k_refs])
            scores.append(lax.dot_general(qs[hh], k, (((1,), (1,)), ((), ())),
                                          preferred_element_type=F32))
        probs, alphas = [], []
        for hh in range(nh):
            s = scores[hh]
            if diag_offset is not None:
                r = lax.broadcasted_iota(jnp.int32, (tq, tk), 0)
                c = lax.broadcasted_iota(jnp.int32, (tq, tk), 1)
                s = jnp.where(c + diag_offset <= r, s, -jnp.inf)
            m_prev = m_ref[hh]
            m_new = jnp.maximum(m_prev, jnp.max(s, axis=-1, keepdims=True))
            alpha = jnp.exp(m_prev - m_new)
            p = jnp.exp(s - jnp.concatenate([m_new] * (tk // LANES), axis=1))
            m_ref[hh] = m_new
            probs.append(p.astype(BF16))
            alphas.append(jnp.concatenate([alpha] * ((dv + LANES) // LANES), axis=1))
        for hh in range(nh):
            v_aug = jnp.concatenate([head(v_ref, hh, rows), ones_cols], axis=1)
            acc_ref[hh] = alphas[hh] * acc_ref[hh] + jnp.dot(
                probs[hh], v_aug, preferred_element_type=F32)

    n_full = i * (tq // tk)

    def full_block(j, carry):
        block(j, None)
        return carry

    lax.fori_loop(0, n_full, full_block, 0)
    for d in range(tq // tk):
        block(n_full + d, d * tk)

    for hh in range(nh):
        cols = slice(hh * dv, (hh + 1) * dv)
        o = acc_ref[hh, :, :dv] / acc_ref[hh, :, dv:]
        o_ref[:, cols] = (o * _silu(z_ref[:, cols])).astype(o_ref.dtype)


def _attention(inputs, in_specs, out_spec, *, B, T, H, dv, tq, tk, nparts, head_major):
    nh = ATTN_HEADS_PER_STEP
    assert tq % tk == 0 and T % tq == 0 and H % nh == 0
    return pl.pallas_call(
        functools.partial(_attn_kernel, nparts=nparts, tq=tq, tk=tk, dv=dv,
                          head_major=head_major),
        grid=(B, H // nh, T // tq),
        in_specs=in_specs,
        out_specs=out_spec,
        out_shape=jax.ShapeDtypeStruct((B * T, H * dv), BF16),
        scratch_shapes=[pltpu.VMEM((nh, tq, LANES), F32),
                        pltpu.VMEM((nh, tq, dv + LANES), F32)],
        compiler_params=_cparams("parallel", "parallel", "arbitrary"),
        name="causal_attention",
    )(*inputs)


def _attn_tiles(T):
    tq = min(512, T)
    tk = min(512, T)
    return tq, tk


def _ssd_kernel(x_ref, bm_ref, cm_ref, z_ref, dtr_ref,
                cwx_ref, cwb_ref, cwc_ref, cbx_ref, cbb_ref, cbc_ref,
                dtb_ref, al_ref, dsk_ref, nw_ref, spread_ref,
                o_ref,
                extx_ref, extb_ref, extc_ref, state_ref, yz_ref):
    Q, E, P, GW = SSD_CHUNK, SSD_HEADS_PER_GROUP, SSD_HEADDIM, SSD_GROUP_WIDTH
    c = pl.program_id(2)

    @pl.when(c == 0)
    def _():
        extx_ref[pl.ds(0, 8), :] = jnp.zeros((8, GW), F32)
        extb_ref[pl.ds(0, 8), :] = jnp.zeros((8, SSD_STATE), F32)
        extc_ref[pl.ds(0, 8), :] = jnp.zeros((8, SSD_STATE), F32)
        state_ref[...] = jnp.zeros_like(state_ref)

    def conv_silu(raw_ref, ext_ref, w_ref, b_ref, cols=slice(None)):
        ext_ref[pl.ds(8, Q), cols] = raw_ref[:, cols]
        acc = b_ref[:, cols] + w_ref[pl.ds(0, 1), cols] * ext_ref[pl.ds(5, Q), cols]
        for kk in range(1, SSD_CONV):
            acc = acc + w_ref[pl.ds(kk, 1), cols] * ext_ref[pl.ds(5 + kk, Q), cols]
        ext_ref[pl.ds(0, 8), cols] = ext_ref[pl.ds(Q, 8), cols]
        return _silu(acc)

    row = lax.broadcasted_iota(jnp.int32, (Q, Q), 0)
    col = lax.broadcasted_iota(jnp.int32, (Q, Q), 1)
    lower = row >= col
    tri_u = (row <= col).astype(BF16)
    left = col < P

    def pieces(v):
        hi = v.astype(BF16).astype(F32)
        mid = (v - hi).astype(BF16).astype(F32)
        lo = ((v - hi) - mid).astype(BF16).astype(F32)
        return jnp.concatenate([hi, mid, lo], axis=0).astype(BF16)

    dt_r = _softplus(dtr_ref[...] + dtb_ref[...])
    cum3 = jnp.dot(pieces(dt_r * (-jnp.exp(al_ref[...]))), tri_u, preferred_element_type=F32)
    acs_r = cum3[:E] + cum3[E:2 * E] + cum3[2 * E:]

    bm = conv_silu(bm_ref, extb_ref, cwb_ref, cbb_ref)
    cm = conv_silu(cm_ref, extc_ref, cwc_ref, cbc_ref)
    bm_b = bm.astype(BF16)
    cm_b = cm.astype(BF16)
    bmt_b = bm.T.astype(BF16)
    cb = lax.dot_general(cm_b, bm_b, (((1,), (1,)), ((), ())), preferred_element_type=F32)
    npairs = E // 2
    carried = [jnp.dot(cm_b, state_ref[kp].astype(BF16), preferred_element_type=F32)
               for kp in range(npairs)]
    xps = [conv_silu(x_ref, extx_ref, cwx_ref, cbx_ref, pl.ds(kp * LANES, LANES))
           for kp in range(npairs)]
    gates = [_silu(z_ref[:, pl.ds(kp * LANES, LANES)]) for kp in range(npairs)]

    spread = lax.dot_general(pieces(jnp.concatenate([dt_r, acs_r], axis=0)), spread_ref[...],
                             (((0,), (0,)), ((), ())), preferred_element_type=F32)
    dt_ch = spread[:, :GW]
    acs_ch = spread[:, GW:2 * GW]
    acs_col = spread[:, 2 * GW:]

    ssq = jnp.zeros((Q, 1), F32)
    for kp in range(npairs):
        cols = pl.ds(kp * LANES, LANES)
        xp = xps[kp]
        dt_p = dt_ch[:, kp * LANES:(kp + 1) * LANES]
        acs_p = acs_ch[:, kp * LANES:(kp + 1) * LANES]
        last_p = acs_p[Q - 1:Q, :]
        xdt = xp * dt_p
        xdt_b = xdt.astype(BF16)

        ys = []
        for e in (2 * kp, 2 * kp + 1):
            seg = acs_col[:, e * Q:(e + 1) * Q] - acs_r[e:e + 1, :]
            dec = jnp.exp(jnp.where(lower, seg, -jnp.inf))
            ys.append(jnp.dot((cb * dec).astype(BF16), xdt_b, preferred_element_type=F32))
        y_diag = jnp.where(left, ys[0], ys[1])

        xw = (xdt * jnp.exp(last_p - acs_p)).astype(BF16)
        s_loc = jnp.dot(bmt_b, xw, preferred_element_type=F32)
        y_off = carried[kp] * jnp.exp(acs_p)
        state_ref[kp] = state_ref[kp] * jnp.exp(last_p) + s_loc

        y = (y_diag + y_off + dsk_ref[:, cols] * xp) * gates[kp]
        yz_ref[:, cols] = y
        ssq = ssq + jnp.sum(y * y, axis=-1, keepdims=True)

    inv = lax.rsqrt(ssq * (1.0 / GW) + RMS_EPS)
    o_ref[...] = (yz_ref[...] * inv * nw_ref[...]).astype(o_ref.dtype)


def _ssd_core(z, xbc, dt, conv_w, conv_b, dt_bias, a_log, d_skip, norm_w, *, B, T):
    G, E, Q = SSD_GROUPS, SSD_HEADS_PER_GROUP, SSD_CHUNK
    NC = T // Q
    GW, N = SSD_GROUP_WIDTH, SSD_STATE
    xblocks = SSD_D_INNER // N
    dtr = jnp.transpose(dt[:, :SSD_HEADS].reshape(B, T, G, E), (0, 2, 3, 1))
    cw = conv_w
    cb = conv_b.reshape(1, -1)
    head_rows = lambda v: v.reshape(G, E, 1)
    d_ch = jnp.repeat(d_skip, SSD_HEADDIM).reshape(1, -1)
    heads = jnp.arange(E)
    per_channel = (jnp.arange(GW)[None, :] // SSD_HEADDIM == heads[:, None]).astype(BF16)
    per_head_block = (jnp.arange(E * Q)[None, :] // Q == heads[:, None]).astype(BF16)
    none = lambda n: jnp.zeros((E, n), BF16)
    spread = jnp.concatenate([
        jnp.concatenate([per_channel, none(GW), none(E * Q)], axis=1),
        jnp.concatenate([none(GW), per_channel, per_head_block], axis=1)], axis=0)
    spread = jnp.tile(spread, (3, 1))

    rowblk = lambda b, g, c: b * NC + c
    in_specs = [
        pl.BlockSpec((Q, GW), lambda b, g, c: (rowblk(b, g, c), g)),
        pl.BlockSpec((Q, N), lambda b, g, c: (rowblk(b, g, c), xblocks + g)),
        pl.BlockSpec((Q, N), lambda b, g, c: (rowblk(b, g, c), xblocks + G + g)),
        pl.BlockSpec((Q, GW), lambda b, g, c: (rowblk(b, g, c), g)),
        pl.BlockSpec((None, None, E, Q), lambda b, g, c: (b, g, 0, c)),
        pl.BlockSpec((SSD_CONV, GW), lambda b, g, c: (0, g)),
        pl.BlockSpec((SSD_CONV, N), lambda b, g, c: (0, xblocks + g)),
        pl.BlockSpec((SSD_CONV, N), lambda b, g, c: (0, xblocks + G + g)),
        pl.BlockSpec((1, GW), lambda b, g, c: (0, g)),
        pl.BlockSpec((1, N), lambda b, g, c: (0, xblocks + g)),
        pl.BlockSpec((1, N), lambda b, g, c: (0, xblocks + G + g)),
        pl.BlockSpec((None, E, 1), lambda b, g, c: (g, 0, 0)),
        pl.BlockSpec((None, E, 1), lambda b, g, c: (g, 0, 0)),
        pl.BlockSpec((1, GW), lambda b, g, c: (0, g)),
        pl.BlockSpec((1, GW), lambda b, g, c: (0, g)),
        pl.BlockSpec(spread.shape, lambda b, g, c: (0, 0)),
    ]
    return pl.pallas_call(
        _ssd_kernel,
        grid=(B, G, NC),
        in_specs=in_specs,
        out_specs=pl.BlockSpec((Q, GW), lambda b, g, c: (rowblk(b, g, c), g)),
        out_shape=jax.ShapeDtypeStruct((B * T, SSD_D_INNER), BF16),
        scratch_shapes=[pltpu.VMEM((Q + 8, GW), F32), pltpu.VMEM((Q + 8, N), F32),
                        pltpu.VMEM((Q + 8, N), F32),
                        pltpu.VMEM((E // 2, N, LANES), F32), pltpu.VMEM((Q, GW), F32)],
        compiler_params=_cparams("parallel", "parallel", "arbitrary"),
        name="ssd_chunk_scan",
    )(xbc, xbc, xbc, z, dtr, cw, cw, cw, cb, cb, cb,
      head_rows(dt_bias), head_rows(a_log), d_ch, norm_w.reshape(1, -1), spread)


def _pad_cols(w, n):
    return jnp.pad(w, ((0, 0), (0, n - w.shape[1])))


def _ssd_layer(hf, hb, w_in, conv_w, conv_b, dt_bias, a_log, d_skip, norm_w, w_out, g, b, *, B, T):
    di = SSD_D_INNER
    cd = di + 2 * SSD_BC_WIDTH
    z = _matmul(hb, w_in, F32, col0=0, ncols=di)
    xbc = _matmul(hb, w_in, F32, col0=di, ncols=cd)
    dt = _matmul(hb, _pad_cols(w_in[:, di + cd:], LANES), F32)
    y = _ssd_core(z, xbc, dt, conv_w, conv_b, dt_bias, a_log, d_skip, norm_w, B=B, T=T)
    return _out_proj_ln(y, w_out.astype(BF16), hf, g, b)


FOX_BIAS_PIECES = 3


def _fox_bias_kernel(f_ref, b_ref, qx_ref, kx_ref, carry_ref, *, nblk):
    blk, W, P = LANES, FOX_WIDTH, FOX_BIAS_PIECES
    row = lax.broadcasted_iota(jnp.int32, (blk, blk), 0)
    col = lax.broadcasted_iota(jnp.int32, (blk, blk), 1)
    tri = (row >= col).astype(F32)
    bias = b_ref[...]
    r = lax.broadcasted_iota(jnp.int32, (P * blk, W), 0)
    c = lax.broadcasted_iota(jnp.int32, (P * blk, W), 1)
    head, piece = r % blk, r // blk
    is_head = head < FOX_HEADS
    place_q = jnp.logical_and(is_head, c == head * blk + piece).astype(BF16)
    place_k = jnp.logical_and(is_head, c == head * blk + P + piece).astype(BF16)
    cmod = lax.broadcasted_iota(jnp.int32, (1, W), 1) % blk
    ones_q = jnp.logical_and(cmod >= P, cmod < 2 * P).astype(F32)
    ones_k = (cmod < P).astype(F32)

    def body(t, carry):
        rows = pl.ds(pl.multiple_of(t * blk, blk), blk)
        x = f_ref[rows, :] + bias
        logf = jnp.minimum(x, 0.0) - jnp.log1p(jnp.exp(-jnp.abs(x)))
        cum = jnp.dot(tri, logf, precision=HIGHEST, preferred_element_type=F32) + carry
        hi = cum.astype(BF16)
        r1 = cum - hi.astype(F32)
        mid = r1.astype(BF16)
        lo = (r1 - mid.astype(F32)).astype(BF16)
        pieces = jnp.concatenate([hi, mid, lo], axis=1)
        qx_ref[rows, :] = (jnp.dot(pieces, place_q, preferred_element_type=F32)
                           + ones_q).astype(BF16)
        kx_ref[rows, :] = (ones_k - jnp.dot(pieces, place_k, preferred_element_type=F32)
                           ).astype(BF16)
        return cum[blk - 1:blk, :]

    @pl.when(pl.program_id(1) == 0)
    def _():
        carry_ref[...] = jnp.zeros_like(carry_ref)

    carry_ref[...] = lax.fori_loop(0, nblk, body, carry_ref[...])


def _fox_bias_columns(f, f_bias, *, B, T):
    bias = jnp.pad(f_bias, (0, LANES - FOX_HEADS)).reshape(1, LANES)
    tt = min(512, T)
    out = jax.ShapeDtypeStruct((B, T, FOX_WIDTH), BF16)
    spec = pl.BlockSpec((None, tt, FOX_WIDTH), lambda b, t: (b, t, 0))
    qx, kx = pl.pallas_call(
        functools.partial(_fox_bias_kernel, nblk=tt // LANES),
        grid=(B, T // tt),
        in_specs=[pl.BlockSpec((None, tt, LANES), lambda b, t: (b, t, 0)),
                  pl.BlockSpec((1, LANES), lambda b, t: (0, 0))],
        out_specs=[spec, spec],
        out_shape=[out, out],
        scratch_shapes=[pltpu.VMEM((1, LANES), F32)],
        compiler_params=_cparams("parallel", "arbitrary"),
        name="fox_cum_log_forget",
    )(f.reshape(B, T, LANES), bias)
    return qx.reshape(B * T, FOX_WIDTH), kx.reshape(B * T, FOX_WIDTH)


def _fox_layer(hf, hb, w_in, f_bias, w_out, g, b, *, B, T):
    W, H, dh = FOX_WIDTH, FOX_HEADS, FOX_HEAD_DIM
    q = _matmul(hb, w_in, BF16, col0=0, ncols=W, scale=dh ** -0.5)
    kv = _matmul(hb, w_in, BF16, col0=W, ncols=2 * W)
    z = _matmul(hb, w_in, F32, col0=3 * W, ncols=W)
    f = _matmul(hb, _pad_cols(w_in[:, 4 * W:], LANES), F32)
    qx, kx = _fox_bias_columns(f, f_bias, B=B, T=T)

    tq, tk = _attn_tiles(T)
    nq = T // tq
    hw = ATTN_HEADS_PER_STEP * dh
    q_tile = pl.BlockSpec((tq, hw), lambda bb, h, i: (bb * nq + i, h))
    seq_k = pl.BlockSpec((T, hw), lambda bb, h, i: (bb, h))
    seq_v = pl.BlockSpec((T, hw), lambda bb, h, i: (bb, W // hw + h))
    in_specs = [q_tile, q_tile, seq_k, seq_k, seq_v, q_tile]
    y = _attention((q, qx, kv, kx, kv, z), in_specs, q_tile,
                   B=B, T=T, H=H, dv=dh, tq=tq, tk=tk, nparts=2, head_major=False)
    return _out_proj_ln(y, w_out.astype(BF16), hf, g, b)


def _rope_table_kernel(pos_ref, freq_ref, sign_ref, cos_ref, sin_ref):
    ang = pos_ref[...].astype(F32) * freq_ref[...]
    cos_ref[...] = jnp.cos(ang)
    sin_ref[...] = jnp.sin(ang) * sign_ref[...]


def _rope_tables(positions, *, B, T):
    half = MLA_ROPE // 2
    inv_freq = ROPE_BASE ** (-jnp.arange(0, MLA_ROPE, 2, dtype=F32) / MLA_ROPE)
    freq = jnp.tile(inv_freq, LANES // half).reshape(1, LANES)
    sign = jnp.tile(jnp.concatenate([-jnp.ones((half,), F32), jnp.ones((half,), F32)]),
                    LANES // MLA_ROPE).reshape(1, LANES)
    M = B * T
    tm = min(1024, M)
    return pl.pallas_call(
        _rope_table_kernel,
        grid=(M // tm,),
        in_specs=[pl.BlockSpec((tm, 1), lambda i: (i, 0)),
                  pl.BlockSpec((1, LANES), lambda i: (0, 0)),
                  pl.BlockSpec((1, LANES), lambda i: (0, 0))],
        out_specs=[pl.BlockSpec((tm, LANES), lambda i: (i, 0)),
                   pl.BlockSpec((tm, LANES), lambda i: (i, 0))],
        out_shape=[jax.ShapeDtypeStruct((M, LANES), F32), jax.ShapeDtypeStruct((M, LANES), F32)],
        compiler_params=_cparams("parallel"),
        name="rope_tables",
    )(positions.reshape(M, 1), freq, sign)


def _rms_to_bf16(x, w):
    y = x * lax.rsqrt(jnp.mean(x * x, axis=-1, keepdims=True) + RMS_EPS)
    return (y * w).astype(BF16)


def _rope_pair(c2, cos, sin):
    return c2 * cos + pltpu.roll(c2, MLA_ROPE, 1) * sin


def _mla_qkv_kernel(x_ref, wl_ref, qn_ref, kn_ref, wq_ref, wkv_ref, cos_ref, sin_ref,
                    q_ref, k_ref, v_ref, *, scale):
    qr, kr = MLA_Q_RANK, MLA_KV_RANK
    cos, sin = cos_ref[...], sin_ref[...]
    lat = jnp.dot(x_ref[...], wl_ref[...], preferred_element_type=F32)
    qn = _rms_to_bf16(lat[:, :qr], qn_ref[...])
    kn = _rms_to_bf16(lat[:, qr:qr + kr], kn_ref[...])
    k_pe = _rope_pair(lat[:, qr + kr:], cos, sin)[:, :MLA_ROPE].astype(k_ref.dtype)
    for h in range(MLA_HEADS):
        r = jnp.dot(qn, wq_ref[h], preferred_element_type=F32)
        pe = _rope_pair(r[:, MLA_NOPE:], cos, sin)
        q_ref[h, :, :MLA_NOPE] = (r[:, :MLA_NOPE] * scale).astype(q_ref.dtype)
        q_ref[h, :, MLA_NOPE:] = (pe[:, :MLA_ROPE] * scale).astype(q_ref.dtype)
        r = jnp.dot(kn, wkv_ref[h], preferred_element_type=F32)
        k_ref[h, :, :MLA_NOPE] = r[:, :MLA_NOPE].astype(k_ref.dtype)
        k_ref[h, :, MLA_NOPE:] = k_pe
        v_ref[h] = r[:, MLA_NOPE:].astype(v_ref.dtype)


def _swap_halves(w):
    half = w.shape[-1] // 2
    return jnp.concatenate([w[..., half:], w[..., :half]], axis=-1)


def _mla_layer(hf, hb, positions, w_in, q_norm, kv_norm, w_q_up, w_kv_up, w_out, g, b, *, B, T):
    H, M = MLA_HEADS, B * T
    qr, kr = MLA_Q_RANK, MLA_KV_RANK
    lat_end = qr + kr + MLA_ROPE
    w_pe = w_in[:, qr + kr:lat_end]
    w_lat = jnp.concatenate([w_in[:, :qr + kr], w_pe, _swap_halves(w_pe)], axis=1).astype(BF16)
    nlat = w_lat.shape[1]
    z = _matmul(hb, w_in[:, lat_end:], F32)
    cos, sin = _rope_tables(positions, B=B, T=T)

    wq = w_q_up.reshape(qr, H, MLA_QK)
    wq = jnp.concatenate([wq, _swap_halves(wq[..., MLA_NOPE:])], axis=-1)
    wq = jnp.transpose(wq, (1, 0, 2)).astype(BF16)
    wkv = jnp.transpose(w_kv_up.reshape(kr, H, MLA_NOPE + MLA_V), (1, 0, 2)).astype(BF16)

    tm = min(256, M)
    wide = MLA_NOPE + LANES
    row = lambda i: (i, 0)
    q, k, v = pl.pallas_call(
        functools.partial(_mla_qkv_kernel, scale=MLA_QK ** -0.5),
        grid=(M // tm,),
        in_specs=[pl.BlockSpec((tm, D_MODEL), row),
                  _resident((D_MODEL, nlat), lambda i: (0, 0)),
                  pl.BlockSpec((1, qr), lambda i: (0, 0)),
                  pl.BlockSpec((1, kr), lambda i: (0, 0)),
                  _resident((H, qr, wide), lambda i: (0, 0, 0)),
                  _resident((H, kr, MLA_NOPE + MLA_V), lambda i: (0, 0, 0)),
                  pl.BlockSpec((tm, LANES), row),
                  pl.BlockSpec((tm, LANES), row)],
        out_specs=[pl.BlockSpec((H, tm, MLA_QK), lambda i: (0, i, 0)),
                   pl.BlockSpec((H, tm, MLA_QK), lambda i: (0, i, 0)),
                   pl.BlockSpec((H, tm, MLA_V), lambda i: (0, i, 0))],
        out_shape=[jax.ShapeDtypeStruct((H, M, MLA_QK), BF16),
                   jax.ShapeDtypeStruct((H, M, MLA_QK), BF16),
                   jax.ShapeDtypeStruct((H, M, MLA_V), BF16)],
        compiler_params=_cparams("parallel"),
        name="mla_qkv_rope",
    )(hb, w_lat, q_norm.reshape(1, qr), kv_norm.reshape(1, kr), wq, wkv, cos, sin)

    tq, tk = _attn_tiles(T)
    nq = T // tq
    nh = ATTN_HEADS_PER_STEP
    in_specs = [
        pl.BlockSpec((nh, tq, MLA_QK), lambda bb, h, i: (h, bb * nq + i, 0)),
        pl.BlockSpec((nh, T, MLA_QK), lambda bb, h, i: (h, bb, 0)),
        pl.BlockSpec((nh, T, MLA_V), lambda bb, h, i: (h, bb, 0)),
        pl.BlockSpec((tq, nh * MLA_V), lambda bb, h, i: (bb * nq + i, h)),
    ]
    out_spec = pl.BlockSpec((tq, nh * MLA_V), lambda bb, h, i: (bb * nq + i, h))
    y = _attention((q, k, v, z), in_specs, out_spec,
                   B=B, T=T, H=H, dv=MLA_V, tq=tq, tk=tk, nparts=1, head_major=True)
    return _out_proj_ln(y, w_out.astype(BF16), hf, g, b)


def _s5_matrices(lam_re, lam_im, log_step, b_re, b_im, c_re, c_im):
    L = S5_CHUNK
    step = jnp.exp(log_step.astype(F32))[:, None]
    mag = jnp.exp(lam_re * step)
    ar = mag * jnp.cos(lam_im * step)
    ai = mag * jnp.sin(lam_im * step)
    den = lam_re * lam_re + lam_im * lam_im
    fr = ((ar - 1.0) * lam_re + ai * lam_im) / den
    fi = (ai * lam_re - (ar - 1.0) * lam_im) / den
    bbr = fr[..., None] * b_re - fi[..., None] * b_im
    bbi = fr[..., None] * b_im + fi[..., None] * b_re
    pr, pi = [jnp.ones_like(ar)], [jnp.zeros_like(ar)]
    for _ in range(L):
        pr_next = pr[-1] * ar - pi[-1] * ai
        pi_next = pr[-1] * ai + pi[-1] * ar
        pr.append(pr_next)
        pi.append(pi_next)
    pwr = jnp.stack(pr, axis=1)
    pwi = jnp.stack(pi, axis=1)

    cpr = c_re[:, None] * pwr[:, :L, None, :] - c_im[:, None] * pwi[:, :L, None, :]
    cpi = c_re[:, None] * pwi[:, :L, None, :] + c_im[:, None] * pwr[:, :L, None, :]
    kern = jnp.einsum('gdip,gpj->gdij', cpr, bbr, precision=HIGHEST) \
        - jnp.einsum('gdip,gpj->gdij', cpi, bbi, precision=HIGHEST)
    r = jnp.arange(L)
    lag = r[:, None] - r[None, :]
    toe = kern[:, jnp.clip(lag, 0, L - 1)]
    toe = jnp.where((lag >= 0)[None, :, :, None, None], toe, 0.0)
    tmt = jnp.transpose(toe, (0, 2, 4, 1, 3)).reshape(S5_GROUPS, S5_ROW, S5_ROW)

    qr = pwr[:, L - 1::-1][:, :L]
    qi = pwi[:, L - 1::-1][:, :L]
    bpr = qr[:, :, :, None] * bbr[:, None] - qi[:, :, :, None] * bbi[:, None]
    bpi = qr[:, :, :, None] * bbi[:, None] + qi[:, :, :, None] * bbr[:, None]
    to_rows = lambda m: jnp.transpose(m, (0, 1, 3, 2)).reshape(S5_GROUPS, S5_ROW, S5_STATE)
    bpr, bpi = to_rows(bpr), to_rows(bpi)

    c1r = c_re[:, None] * pwr[:, 1:, None, :] - c_im[:, None] * pwi[:, 1:, None, :]
    c1i = c_re[:, None] * pwi[:, 1:, None, :] + c_im[:, None] * pwr[:, 1:, None, :]
    to_cols = lambda m: jnp.transpose(m, (0, 3, 1, 2)).reshape(S5_GROUPS, S5_STATE, S5_ROW)
    cpw_r, cpw_i = to_cols(c1r), -to_cols(c1i)

    zs = jnp.zeros((S5_PAIRS, S5_ROW, S5_STATE), F32)
    ev, od = slice(0, None, 2), slice(1, None, 2)
    bp_pair = jnp.concatenate([
        jnp.concatenate([bpr[ev], zs, bpi[ev], zs], axis=2),
        jnp.concatenate([zs, bpr[od], zs, bpi[od]], axis=2)], axis=1)
    zc = jnp.zeros((S5_PAIRS, S5_STATE, S5_ROW), F32)
    cp_pair = jnp.concatenate([
        jnp.concatenate([cpw_r[ev], zc], axis=2),
        jnp.concatenate([zc, cpw_r[od]], axis=2),
        jnp.concatenate([cpw_i[ev], zc], axis=2),
        jnp.concatenate([zc, cpw_i[od]], axis=2)], axis=1)
    alr = pwr[:, L].reshape(S5_PAIRS, 2 * S5_STATE)
    ali = pwi[:, L].reshape(S5_PAIRS, 2 * S5_STATE)
    a1 = jnp.concatenate([alr, alr], axis=1).reshape(1, -1)
    a2 = jnp.concatenate([-ali, ali], axis=1).reshape(1, -1)
    return tmt.astype(BF16), bp_pair.astype(BF16), cp_pair.astype(BF16), a1, a2


def _s5_local_state_kernel(x_ref, bp_ref, s_ref):
    s_ref[...] = jnp.dot(x_ref[...], bp_ref[...], preferred_element_type=F32)


def _s5_scan_kernel(s_ref, a1_ref, a2_ref, o_ref, *, B, nch):
    W = s_ref.shape[1]
    a1 = a1_ref[...]
    a2 = a2_ref[...]

    def swap_re_im(c):
        parts = []
        for q in range(W // (2 * LANES)):
            lo = q * 2 * LANES
            parts += [c[:, lo + LANES:lo + 2 * LANES], c[:, lo:lo + LANES]]
        return jnp.concatenate(parts, axis=1)

    def body(n, carry):
        new = []
        for bb in range(B):
            row = bb * nch + n
            c = carry[bb]
            o_ref[pl.ds(row, 1), :] = c
            new.append(a1 * c + a2 * swap_re_im(c) + s_ref[pl.ds(row, 1), :])
        return tuple(new)

    lax.fori_loop(0, nch, body, tuple(jnp.zeros((1, W), F32) for _ in range(B)))


def _gelu_tanh(y):
    return 0.5 * y * (1.0 + jnp.tanh(math.sqrt(2.0 / math.pi) * (y + 0.044715 * (y * y * y))))


def _s5_out_kernel(x_ref, st_ref, tmt_ref, cp_ref, o_ref):
    x = x_ref[...]
    off = jnp.dot(st_ref[...].astype(BF16), cp_ref[...], preferred_element_type=F32)
    for gi in range(2):
        cols = slice(gi * S5_ROW, (gi + 1) * S5_ROW)
        y = jnp.dot(x[:, cols], tmt_ref[gi], preferred_element_type=F32)
        o_ref[:, cols] = (y + off[:, cols]).astype(o_ref.dtype)


def _s5_glu_out_kernel(ys_ref, u_ref, d_ref, wg_ref, bg_ref, z_ref, wo_ref, h_ref, g_ref, b_ref,
                       of_ref, ob_ref):
    y = _gelu_tanh(ys_ref[...].astype(F32) + d_ref[...] * u_ref[...])
    t = jnp.dot(y.astype(BF16), wg_ref[...], preferred_element_type=F32) + bg_ref[...]
    y = (y * _sigmoid(t) * _silu(z_ref[...])).astype(BF16)
    branch = jnp.dot(y, wo_ref[...], preferred_element_type=F32)
    out = _deepnorm_ln(h_ref[...], branch, g_ref[...], b_ref[...])
    of_ref[...] = out
    ob_ref[...] = out.astype(BF16)


def _s5_layer(hf, hb, w_in, lam_re, lam_im, log_step, b_re, b_im, c_re, c_im, d_skip,
              w_glu, b_glu, w_out, g, b, *, B, T):
    M, W, G, L = B * T, S5_WIDTH, S5_GROUPS, S5_CHUNK
    nch = T // L
    rows = B * nch
    u = _matmul(hb, w_in, F32, col0=0, ncols=W)
    z = _matmul(hb, w_in, F32, col0=W, ncols=W)
    tmt, bp_pair, cp_pair, a1, a2 = _s5_matrices(lam_re, lam_im, log_step, b_re, b_im, c_re, c_im)

    x = jnp.transpose(u.astype(BF16).reshape(rows, L, G, S5_GROUP), (0, 2, 1, 3))
    x = x.reshape(rows, G * S5_ROW)
    pw_x, pw_s = 2 * S5_ROW, 4 * S5_STATE
    s_loc = pl.pallas_call(
        _s5_local_state_kernel,
        grid=(S5_PAIRS,),
        in_specs=[pl.BlockSpec((rows, pw_x), lambda p: (0, p)),
                  pl.BlockSpec((None, pw_x, pw_s), lambda p: (p, 0, 0))],
        out_specs=pl.BlockSpec((rows, pw_s), lambda p: (0, p)),
        out_shape=jax.ShapeDtypeStruct((rows, S5_PAIRS * pw_s), F32),
        compiler_params=_cparams("parallel"),
        name="s5_chunk_state",
    )(x, bp_pair)

    sw = 8 * pw_s
    st_in = pl.pallas_call(
        functools.partial(_s5_scan_kernel, B=B, nch=nch),
        grid=(S5_PAIRS * pw_s // sw,),
        in_specs=[pl.BlockSpec((rows, sw), lambda p: (0, p)),
                  pl.BlockSpec((1, sw), lambda p: (0, p)),
                  pl.BlockSpec((1, sw), lambda p: (0, p))],
        out_specs=pl.BlockSpec((rows, sw), lambda p: (0, p)),
        out_shape=jax.ShapeDtypeStruct((rows, S5_PAIRS * pw_s), F32),
        compiler_params=_cparams("parallel"),
        name="s5_state_scan",
    )(s_loc, a1, a2)

    y = pl.pallas_call(
        _s5_out_kernel,
        grid=(S5_PAIRS,),
        in_specs=[pl.BlockSpec((rows, pw_x), lambda p: (0, p)),
                  pl.BlockSpec((rows, pw_s), lambda p: (0, p)),
                  pl.BlockSpec((2, S5_ROW, S5_ROW), lambda p: (p, 0, 0)),
                  pl.BlockSpec((None, pw_s, pw_x), lambda p: (p, 0, 0))],
        out_specs=pl.BlockSpec((rows, pw_x), lambda p: (0, p)),
        out_shape=jax.ShapeDtypeStruct((rows, G * S5_ROW), BF16),
        compiler_params=_cparams("parallel"),
        name="s5_chunk_output",
    )(x, st_in, tmt, cp_pair)
    y = jnp.transpose(y.reshape(rows, G, L, S5_GROUP), (0, 2, 1, 3)).reshape(M, W)

    tm = min(256, M)
    row = lambda i: (i, 0)
    fixed = lambda i: (0, 0)
    return pl.pallas_call(
        _s5_glu_out_kernel,
        grid=(M // tm,),
        in_specs=[pl.BlockSpec((tm, W), row),
                  pl.BlockSpec((tm, W), row),
                  pl.BlockSpec((1, W), fixed),
                  _resident((W, W), fixed),
                  pl.BlockSpec((1, W), fixed),
                  pl.BlockSpec((tm, W), row),
                  _resident((W, D_MODEL), fixed),
                  pl.BlockSpec((tm, D_MODEL), row),
                  pl.BlockSpec((1, D_MODEL), fixed),
                  pl.BlockSpec((1, D_MODEL), fixed)],
        out_specs=[pl.BlockSpec((tm, D_MODEL), row), pl.BlockSpec((tm, D_MODEL), row)],
        out_shape=[jax.ShapeDtypeStruct((M, D_MODEL), F32),
                   jax.ShapeDtypeStruct((M, D_MODEL), BF16)],
        compiler_params=_cparams("parallel"),
        name="s5_glu_out_ln",
    )(y, u, d_skip.reshape(1, W), w_glu.astype(BF16), b_glu.reshape(1, W), z,
      w_out.astype(BF16), hf, g.reshape(1, D_MODEL), b.reshape(1, D_MODEL))


def kernel(x, positions, ln_g, ln_b, ssd_w_in, ssd_conv_w, ssd_conv_b, ssd_dt_bias, ssd_a_log, ssd_d, ssd_norm_w, ssd_w_out, fox_w_in, fox_f_bias, fox_w_out, mla_w_in, mla_q_norm, mla_kv_norm, mla_w_q_up, mla_w_kv_up, mla_w_out, s5_w_in, s5_lambda_re, s5_lambda_im, s5_log_step, s5_b_re, s5_b_im, s5_c_re, s5_c_im, s5_d, s5_w_glu, s5_b_glu, s5_w_out):
    B, T, D = x.shape
    hf = x.reshape(B * T, D)
    hb = hf
    for i in range(DEPTH):
        j = i // 4
        kind = i % 4
        g, b = ln_g[i], ln_b[i]
        if kind == 0:
            hf, hb = _ssd_layer(hf, hb, ssd_w_in[j], ssd_conv_w[j], ssd_conv_b[j], ssd_dt_bias[j],
                                ssd_a_log[j], ssd_d[j], ssd_norm_w[j], ssd_w_out[j], g, b, B=B, T=T)
        elif kind == 1:
            hf, hb = _fox_layer(hf, hb, fox_w_in[j], fox_f_bias[j], fox_w_out[j], g, b, B=B, T=T)
        elif kind == 2:
            hf, hb = _mla_layer(hf, hb.astype(BF16), positions, mla_w_in[j], mla_q_norm[j],
                                mla_kv_norm[j], mla_w_q_up[j], mla_w_kv_up[j], mla_w_out[j],
                                g, b, B=B, T=T)
        else:
            hf, hb = _s5_layer(hf, hb, s5_w_in[j], s5_lambda_re[j], s5_lambda_im[j], s5_log_step[j],
                               s5_b_re[j], s5_b_im[j], s5_c_re[j], s5_c_im[j], s5_d[j],
                               s5_w_glu[j], s5_b_glu[j], s5_w_out[j], g, b, B=B, T=T)
    return hf.reshape(B, T, D)
```

```python
import functools
import math

import jax
import jax.numpy as jnp
from jax import lax
from jax.experimental import pallas as pl
from jax.experimental.pallas import tpu as pltpu

F32 = jnp.float32
BF16 = jnp.bfloat16
HIGHEST = lax.Precision.HIGHEST

D_MODEL = 2048
DEPTH = 4
ALPHA = (2.0 * DEPTH) ** 0.25
LN_EPS = 1e-5
RMS_EPS = 1e-6

SSD_D_INNER = 4096
SSD_HEADS = 64
SSD_HEADDIM = 64
SSD_GROUPS = 8
SSD_HEADS_PER_GROUP = SSD_HEADS // SSD_GROUPS
SSD_STATE = 128
SSD_CONV = 4
SSD_CHUNK = 128
SSD_GROUP_WIDTH = SSD_D_INNER // SSD_GROUPS
SSD_BC_WIDTH = SSD_GROUPS * SSD_STATE
SSD_GROUPS_PER_STEP = 2

FOX_HEADS = 16
FOX_HEAD_DIM = 128
FOX_WIDTH = FOX_HEADS * FOX_HEAD_DIM

MLA_HEADS = 16
MLA_Q_RANK = 512
MLA_KV_RANK = 512
MLA_NOPE = 128
MLA_ROPE = 64
MLA_V = 128
MLA_QK = MLA_NOPE + MLA_ROPE
MLA_WIDTH = MLA_HEADS * MLA_V
ROPE_BASE = 10000.0

S5_WIDTH = D_MODEL
S5_GROUP = 16
S5_GROUPS = S5_WIDTH // S5_GROUP
S5_STATE = 64
S5_CHUNK = 16
S5_ROW = S5_CHUNK * S5_GROUP
S5_PAIRS = S5_GROUPS // 2

LANES = 128
VMEM_LIMIT_BYTES = 48 * 1024 * 1024


def _cparams(*sem):
    return pltpu.CompilerParams(dimension_semantics=sem, vmem_limit_bytes=VMEM_LIMIT_BYTES)


def _resident(block_shape, index_map):
    return pl.BlockSpec(block_shape, index_map, pipeline_mode=pl.Buffered(1))


def _sigmoid(x):
    return 0.5 * (jnp.tanh(0.5 * x) + 1.0)


def _silu(x):
    h = 0.5 * x
    return h + h * jnp.tanh(h)


def _softplus(x):
    return jnp.maximum(x, 0.0) + jnp.log1p(jnp.exp(-jnp.abs(x)))


def _deepnorm_ln(h, branch, g, b):
    r = ALPHA * h + branch
    mu = jnp.mean(r, axis=-1, keepdims=True)
    d = r - mu
    var = jnp.mean(d * d, axis=-1, keepdims=True)
    return d * lax.rsqrt(var + LN_EPS) * g + b


def _mm_kernel(x_ref, w_ref, o_ref, wb_ref, *, scale):
    @pl.when(pl.program_id(1) == 0)
    def _():
        wb_ref[...] = w_ref[...].astype(BF16)

    acc = jnp.dot(x_ref[...].astype(BF16), wb_ref[...], preferred_element_type=F32)
    if scale is not None:
        acc = acc * scale
    o_ref[...] = acc.astype(o_ref.dtype)


def _matmul(x, w, out_dtype, *, col0=0, ncols=None, scale=None, tn=1024):
    M, K = x.shape
    N = w.shape[1] - col0 if ncols is None else ncols
    tm = min(1024 if x.dtype == BF16 else 512, M)
    tn = min(tn, N)
    assert M % tm == 0 and N % tn == 0 and col0 % tn == 0, (M, N, col0, tm, tn)
    j0 = col0 // tn
    return pl.pallas_call(
        functools.partial(_mm_kernel, scale=scale),
        grid=(N // tn, M // tm),
        in_specs=[pl.BlockSpec((tm, K), lambda j, i: (i, 0)),
                  pl.BlockSpec((K, tn), lambda j, i: (0, j0 + j))],
        out_specs=pl.BlockSpec((tm, tn), lambda j, i: (i, j)),
        out_shape=jax.ShapeDtypeStruct((M, N), out_dtype),
        scratch_shapes=[pltpu.VMEM((K, tn), BF16)],
        compiler_params=_cparams("parallel", "arbitrary"),
        name="proj",
    )(x, w)


def _out_ln_kernel(y_ref, w_ref, h_ref, g_ref, b_ref, of_ref, ob_ref):
    branch = jnp.dot(y_ref[...], w_ref[...], preferred_element_type=F32)
    out = _deepnorm_ln(h_ref[...], branch, g_ref[...], b_ref[...])
    of_ref[...] = out
    ob_ref[...] = out.astype(BF16)


def _out_proj_ln(y, w, h, g, b, *, tm=256):
    M, K = y.shape
    D = w.shape[1]
    tm = min(tm, M)
    assert M % tm == 0
    return pl.pallas_call(
        _out_ln_kernel,
        grid=(M // tm,),
        in_specs=[pl.BlockSpec((tm, K), lambda i: (i, 0)),
                  _resident((K, D), lambda i: (0, 0)),
                  pl.BlockSpec((tm, D), lambda i: (i, 0)),
                  pl.BlockSpec((1, D), lambda i: (0, 0)),
                  pl.BlockSpec((1, D), lambda i: (0, 0))],
        out_specs=[pl.BlockSpec((tm, D), lambda i: (i, 0)),
                   pl.BlockSpec((tm, D), lambda i: (i, 0))],
        out_shape=[jax.ShapeDtypeStruct((M, D), F32), jax.ShapeDtypeStruct((M, D), BF16)],
        compiler_params=_cparams("parallel"),
        name="out_proj_ln",
    )(y, w, h, g.reshape(1, D), b.reshape(1, D))


ATTN_HEADS_PER_STEP = 4


def _attn_kernel(*refs, nparts, tq, tk, dv, head_major):
    q_refs = refs[:nparts]
    k_refs = refs[nparts:2 * nparts]
    v_ref, z_ref, o_ref, m_ref, acc_ref = refs[2 * nparts:]
    nh = ATTN_HEADS_PER_STEP
    i = pl.program_id(2)
    ones_cols = jnp.ones((tk, LANES), BF16)

    def head(ref, hh, rows=slice(None)):
        if head_major:
            return ref[hh, rows, :]
        w = ref.shape[-1] // nh
        return ref[rows, hh * w:(hh + 1) * w]

    def cat(pieces):
        return pieces[0] if len(pieces) == 1 else jnp.concatenate(pieces, axis=1)

    qs = [cat([head(r, hh) for r in q_refs]) for hh in range(nh)]
    m_ref[...] = jnp.full_like(m_ref, -jnp.inf)
    acc_ref[...] = jnp.zeros_like(acc_ref)

    def block(j, diag_offset):
        rows = pl.ds(pl.multiple_of(j * tk, tk), tk)
        scores = []
        for hh in range(nh):
            k = cat([head(r, hh, rows) for r in k_refs])
            scores.append(lax.dot_general(qs[hh], k, (((1,), (1,)), ((), ())),
                                          preferred_element_type=F32))
        probs, alphas = [], []
        for hh in range(nh):
            s = scores[hh]
            if diag_offset is not None:
                r = lax.broadcasted_iota(jnp.int32, (tq, tk), 0)
                c = lax.broadcasted_iota(jnp.int32, (tq, tk), 1)
                s = jnp.where(c + diag_offset <= r, s, -jnp.inf)
            m_prev = m_ref[hh]
            m_new = jnp.maximum(m_prev, jnp.max(s, axis=-1, keepdims=True))
            alpha = jnp.exp(m_prev - m_new)
            p = jnp.exp(s - jnp.concatenate([m_new] * (tk // LANES), axis=1))
            m_ref[hh] = m_new
            probs.append(p.astype(BF16))
            alphas.append(jnp.concatenate([alpha] * ((dv + LANES) // LANES), axis=1))
        for hh in range(nh):
            v_aug = jnp.concatenate([head(v_ref, hh, rows), ones_cols], axis=1)
            acc_ref[hh] = alphas[hh] * acc_ref[hh] + jnp.dot(
                probs[hh], v_aug, preferred_element_type=F32)

    n_full = i * (tq // tk)

    def full_block(j, carry):
        block(j, None)
        return carry

    lax.fori_loop(0, n_full, full_block, 0)
    for d in range(tq // tk):
        block(n_full + d, d * tk)

    for hh in range(nh):
        cols = slice(hh * dv, (hh + 1) * dv)
        o = acc_ref[hh, :, :dv] / acc_ref[hh, :, dv:]
        o_ref[:, cols] = (o * _silu(z_ref[:, cols])).astype(o_ref.dtype)


def _attention(inputs, in_specs, out_spec, *, B, T, H, dv, tq, tk, nparts, head_major):
    nh = ATTN_HEADS_PER_STEP
    assert tq % tk == 0 and T % tq == 0 and H % nh == 0
    return pl.pallas_call(
        functools.partial(_attn_kernel, nparts=nparts, tq=tq, tk=tk, dv=dv,
                          head_major=head_major),
        grid=(B, H // nh, T // tq),
        in_specs=in_specs,
        out_specs=out_spec,
        out_shape=jax.ShapeDtypeStruct((B * T, H * dv), BF16),
        scratch_shapes=[pltpu.VMEM((nh, tq, LANES), F32),
                        pltpu.VMEM((nh, tq, dv + LANES), F32)],
        compiler_params=_cparams("parallel", "parallel", "arbitrary"),
        name="causal_attention",
    )(*inputs)


def _attn_tiles(T):
    tq = min(512, T)
    tk = min(512, T)
    return tq, tk


def _ssd_kernel(x_ref, bm_ref, cm_ref, z_ref, dtr_ref,
                cwx_ref, cwb_ref, cwc_ref, cbx_ref, cbb_ref, cbc_ref,
                dtb_ref, al_ref, dsk_ref, nw_ref, spread_ref,
                o_ref,
                extx_ref, extb_ref, extc_ref, state_ref, yz_ref):
    Q, E, P, GW, N = SSD_CHUNK, SSD_HEADS_PER_GROUP, SSD_HEADDIM, SSD_GROUP_WIDTH, SSD_STATE
    NG = SSD_GROUPS_PER_STEP
    groups = range(NG)
    c = pl.program_id(2)

    @pl.when(c == 0)
    def _():
        extx_ref[pl.ds(0, 8), :] = jnp.zeros((8, NG * GW), F32)
        extb_ref[pl.ds(0, 8), :] = jnp.zeros((8, NG * N), F32)
        extc_ref[pl.ds(0, 8), :] = jnp.zeros((8, NG * N), F32)
        state_ref[...] = jnp.zeros_like(state_ref)

    def conv_silu(raw_ref, ext_ref, w_ref, b_ref, cols=slice(None)):
        ext_ref[pl.ds(8, Q), cols] = raw_ref[:, cols]
        acc = b_ref[:, cols] + w_ref[pl.ds(0, 1), cols] * ext_ref[pl.ds(5, Q), cols]
        for kk in range(1, SSD_CONV):
            acc = acc + w_ref[pl.ds(kk, 1), cols] * ext_ref[pl.ds(5 + kk, Q), cols]
        ext_ref[pl.ds(0, 8), cols] = ext_ref[pl.ds(Q, 8), cols]
        return _silu(acc)

    row = lax.broadcasted_iota(jnp.int32, (Q, Q), 0)
    col = lax.broadcasted_iota(jnp.int32, (Q, Q), 1)
    lower = row >= col
    tri_u = (row <= col).astype(BF16)
    left = col < P

    def pieces(v):
        hi = v.astype(BF16).astype(F32)
        mid = (v - hi).astype(BF16).astype(F32)
        lo = ((v - hi) - mid).astype(BF16).astype(F32)
        return jnp.concatenate([hi, mid, lo], axis=0).astype(BF16)

    npairs = E // 2
    dt_r, acs_r = [], []
    for gg in groups:
        dt = _softplus(dtr_ref[gg] + dtb_ref[gg])
        cum3 = jnp.dot(pieces(dt * (-jnp.exp(al_ref[gg]))), tri_u, preferred_element_type=F32)
        dt_r.append(dt)
        acs_r.append(cum3[:E] + cum3[E:2 * E] + cum3[2 * E:])

    bm_b, cm_b, bmt_b, cb, carried = [], [], [], [], []
    for gg in groups:
        ncols = pl.ds(gg * N, N)
        bm = conv_silu(bm_ref, extb_ref, cwb_ref, cbb_ref, ncols)
        cm = conv_silu(cm_ref, extc_ref, cwc_ref, cbc_ref, ncols)
        bm_b.append(bm.astype(BF16))
        cm_b.append(cm.astype(BF16))
        bmt_b.append(bm.T.astype(BF16))
        cb.append(lax.dot_general(cm_b[gg], bm_b[gg], (((1,), (1,)), ((), ())),
                                  preferred_element_type=F32))
        carried.append([jnp.dot(cm_b[gg], state_ref[gg * npairs + kp].astype(BF16),
                                preferred_element_type=F32)
                        for kp in range(npairs)])

    pair_cols = lambda gg, kp: pl.ds(gg * GW + kp * LANES, LANES)
    xps = [[conv_silu(x_ref, extx_ref, cwx_ref, cbx_ref, pair_cols(gg, kp))
            for kp in range(npairs)] for gg in groups]
    gates = [[_silu(z_ref[:, pair_cols(gg, kp)]) for kp in range(npairs)] for gg in groups]

    spreads = [lax.dot_general(pieces(jnp.concatenate([dt_r[gg], acs_r[gg]], axis=0)),
                               spread_ref[...], (((0,), (0,)), ((), ())),
                               preferred_element_type=F32) for gg in groups]

    for gg in groups:
        dt_ch = spreads[gg][:, :GW]
        acs_ch = spreads[gg][:, GW:2 * GW]
        acs_col = spreads[gg][:, 2 * GW:]
        ssq = jnp.zeros((Q, 1), F32)
        for kp in range(npairs):
            cols = pair_cols(gg, kp)
            xp = xps[gg][kp]
            dt_p = dt_ch[:, kp * LANES:(kp + 1) * LANES]
            acs_p = acs_ch[:, kp * LANES:(kp + 1) * LANES]
            last_p = acs_p[Q - 1:Q, :]
            xdt = xp * dt_p
            xdt_b = xdt.astype(BF16)

            ys = []
            for e in (2 * kp, 2 * kp + 1):
                seg = acs_col[:, e * Q:(e + 1) * Q] - acs_r[gg][e:e + 1, :]
                dec = jnp.exp(jnp.where(lower, seg, -jnp.inf))
                ys.append(jnp.dot((cb[gg] * dec).astype(BF16), xdt_b,
                                  preferred_element_type=F32))
            y_diag = jnp.where(left, ys[0], ys[1])

            xw = (xdt * jnp.exp(last_p - acs_p)).astype(BF16)
            s_loc = jnp.dot(bmt_b[gg], xw, preferred_element_type=F32)
            y_off = carried[gg][kp] * jnp.exp(acs_p)
            slot = gg * npairs + kp
            state_ref[slot] = state_ref[slot] * jnp.exp(last_p) + s_loc

            y = (y_diag + y_off + dsk_ref[:, cols] * xp) * gates[gg][kp]
            yz_ref[:, cols] = y
            ssq = ssq + jnp.sum(y * y, axis=-1, keepdims=True)

        gcols = pl.ds(gg * GW, GW)
        inv = lax.rsqrt(ssq * (1.0 / GW) + RMS_EPS)
        o_ref[:, gcols] = (yz_ref[:, gcols] * inv * nw_ref[:, gcols]).astype(o_ref.dtype)


def _ssd_core(z, xbc, dt, conv_w, conv_b, dt_bias, a_log, d_skip, norm_w, *, B, T):
    G, E, Q = SSD_GROUPS, SSD_HEADS_PER_GROUP, SSD_CHUNK
    NC = T // Q
    GW, N = SSD_GROUP_WIDTH, SSD_STATE
    xblocks = SSD_D_INNER // N
    dtr = jnp.transpose(dt[:, :SSD_HEADS].reshape(B, T, G, E), (0, 2, 3, 1))
    cw = conv_w
    cb = conv_b.reshape(1, -1)
    head_rows = lambda v: v.reshape(G, E, 1)
    d_ch = jnp.repeat(d_skip, SSD_HEADDIM).reshape(1, -1)
    heads = jnp.arange(E)
    per_channel = (jnp.arange(GW)[None, :] // SSD_HEADDIM == heads[:, None]).astype(BF16)
    per_head_block = (jnp.arange(E * Q)[None, :] // Q == heads[:, None]).astype(BF16)
    none = lambda n: jnp.zeros((E, n), BF16)
    spread = jnp.concatenate([
        jnp.concatenate([per_channel, none(GW), none(E * Q)], axis=1),
        jnp.concatenate([none(GW), per_channel, per_head_block], axis=1)], axis=0)
    spread = jnp.tile(spread, (3, 1))

    NG = SSD_GROUPS_PER_STEP
    XW, BW = NG * GW, NG * N
    assert G % NG == 0
    bblk = SSD_D_INNER // BW
    cblk = (SSD_D_INNER + SSD_BC_WIDTH) // BW
    rowblk = lambda b, g, c: b * NC + c
    in_specs = [
        pl.BlockSpec((Q, XW), lambda b, g, c: (rowblk(b, g, c), g)),
        pl.BlockSpec((Q, BW), lambda b, g, c: (rowblk(b, g, c), bblk + g)),
        pl.BlockSpec((Q, BW), lambda b, g, c: (rowblk(b, g, c), cblk + g)),
        pl.BlockSpec((Q, XW), lambda b, g, c: (rowblk(b, g, c), g)),
        pl.BlockSpec((None, NG, E, Q), lambda b, g, c: (b, g, 0, c)),
        pl.BlockSpec((SSD_CONV, XW), lambda b, g, c: (0, g)),
        pl.BlockSpec((SSD_CONV, BW), lambda b, g, c: (0, bblk + g)),
        pl.BlockSpec((SSD_CONV, BW), lambda b, g, c: (0, cblk + g)),
        pl.BlockSpec((1, XW), lambda b, g, c: (0, g)),
        pl.BlockSpec((1, BW), lambda b, g, c: (0, bblk + g)),
        pl.BlockSpec((1, BW), lambda b, g, c: (0, cblk + g)),
        pl.BlockSpec((NG, E, 1), lambda b, g, c: (g, 0, 0)),
        pl.BlockSpec((NG, E, 1), lambda b, g, c: (g, 0, 0)),
        pl.BlockSpec((1, XW), lambda b, g, c: (0, g)),
        pl.BlockSpec((1, XW), lambda b, g, c: (0, g)),
        pl.BlockSpec(spread.shape, lambda b, g, c: (0, 0)),
    ]
    return pl.pallas_call(
        _ssd_kernel,
        grid=(B, G // NG, NC),
        in_specs=in_specs,
        out_specs=pl.BlockSpec((Q, XW), lambda b, g, c: (rowblk(b, g, c), g)),
        out_shape=jax.ShapeDtypeStruct((B * T, SSD_D_INNER), BF16),
        scratch_shapes=[pltpu.VMEM((Q + 8, XW), F32), pltpu.VMEM((Q + 8, BW), F32),
                        pltpu.VMEM((Q + 8, BW), F32),
                        pltpu.VMEM((NG * E // 2, N, LANES), F32), pltpu.VMEM((Q, XW), F32)],
        compiler_params=_cparams("parallel", "parallel", "arbitrary"),
        name="ssd_chunk_scan",
    )(xbc, xbc, xbc, z, dtr, cw, cw, cw, cb, cb, cb,
      head_rows(dt_bias), head_rows(a_log), d_ch, norm_w.reshape(1, -1), spread)


def _pad_cols(w, n):
    return jnp.pad(w, ((0, 0), (0, n - w.shape[1])))


def _ssd_layer(hf, hb, w_in, conv_w, conv_b, dt_bias, a_log, d_skip, norm_w, w_out, g, b, *, B, T):
    di = SSD_D_INNER
    cd = di + 2 * SSD_BC_WIDTH
    z = _matmul(hb, w_in, F32, col0=0, ncols=di)
    xbc = _matmul(hb, w_in, F32, col0=di, ncols=cd)
    dt = _matmul(hb, _pad_cols(w_in[:, di + cd:], LANES), F32)
    y = _ssd_core(z, xbc, dt, conv_w, conv_b, dt_bias, a_log, d_skip, norm_w, B=B, T=T)
    return _out_proj_ln(y, w_out.astype(BF16), hf, g, b)


FOX_BIAS_PIECES = 3


def _fox_bias_kernel(f_ref, b_ref, qx_ref, kx_ref, carry_ref, *, nblk):
    blk, W, P = LANES, FOX_WIDTH, FOX_BIAS_PIECES
    row = lax.broadcasted_iota(jnp.int32, (blk, blk), 0)
    col = lax.broadcasted_iota(jnp.int32, (blk, blk), 1)
    tri = (row >= col).astype(F32)
    bias = b_ref[...]
    r = lax.broadcasted_iota(jnp.int32, (P * blk, W), 0)
    c = lax.broadcasted_iota(jnp.int32, (P * blk, W), 1)
    head, piece = r % blk, r // blk
    is_head = head < FOX_HEADS
    place_q = jnp.logical_and(is_head, c == head * blk + piece).astype(BF16)
    place_k = jnp.logical_and(is_head, c == head * blk + P + piece).astype(BF16)
    cmod = lax.broadcasted_iota(jnp.int32, (1, W), 1) % blk
    ones_q = jnp.logical_and(cmod >= P, cmod < 2 * P).astype(F32)
    ones_k = (cmod < P).astype(F32)

    def body(t, carry):
        rows = pl.ds(pl.multiple_of(t * blk, blk), blk)
        x = f_ref[rows, :] + bias
        logf = jnp.minimum(x, 0.0) - jnp.log1p(jnp.exp(-jnp.abs(x)))
        cum = jnp.dot(tri, logf, precision=HIGHEST, preferred_element_type=F32) + carry
        hi = cum.astype(BF16)
        r1 = cum - hi.astype(F32)
        mid = r1.astype(BF16)
        lo = (r1 - mid.astype(F32)).astype(BF16)
        pieces = jnp.concatenate([hi, mid, lo], axis=1)
        qx_ref[rows, :] = (jnp.dot(pieces, place_q, preferred_element_type=F32)
                           + ones_q).astype(BF16)
        kx_ref[rows, :] = (ones_k - jnp.dot(pieces, place_k, preferred_element_type=F32)
                           ).astype(BF16)
        return cum[blk - 1:blk, :]

    @pl.when(pl.program_id(1) == 0)
    def _():
        carry_ref[...] = jnp.zeros_like(carry_ref)

    carry_ref[...] = lax.fori_loop(0, nblk, body, carry_ref[...])


def _fox_bias_columns(f, f_bias, *, B, T):
    bias = jnp.pad(f_bias, (0, LANES - FOX_HEADS)).reshape(1, LANES)
    tt = min(512, T)
    out = jax.ShapeDtypeStruct((B, T, FOX_WIDTH), BF16)
    spec = pl.BlockSpec((None, tt, FOX_WIDTH), lambda b, t: (b, t, 0))
    qx, kx = pl.pallas_call(
        functools.partial(_fox_bias_kernel, nblk=tt // LANES),
        grid=(B, T // tt),
        in_specs=[pl.BlockSpec((None, tt, LANES), lambda b, t: (b, t, 0)),
                  pl.BlockSpec((1, LANES), lambda b, t: (0, 0))],
        out_specs=[spec, spec],
        out_shape=[out, out],
        scratch_shapes=[pltpu.VMEM((1, LANES), F32)],
        compiler_params=_cparams("parallel", "arbitrary"),
        name="fox_cum_log_forget",
    )(f.reshape(B, T, LANES), bias)
    return qx.reshape(B * T, FOX_WIDTH), kx.reshape(B * T, FOX_WIDTH)


def _fox_layer(hf, hb, w_in, f_bias, w_out, g, b, *, B, T):
    W, H, dh = FOX_WIDTH, FOX_HEADS, FOX_HEAD_DIM
    q = _matmul(hb, w_in, BF16, col0=0, ncols=W, scale=dh ** -0.5)
    kv = _matmul(hb, w_in, BF16, col0=W, ncols=2 * W)
    z = _matmul(hb, w_in, F32, col0=3 * W, ncols=W)
    f = _matmul(hb, _pad_cols(w_in[:, 4 * W:], LANES), F32)
    qx, kx = _fox_bias_columns(f, f_bias, B=B, T=T)

    tq, tk = _attn_tiles(T)
    nq = T // tq
    hw = ATTN_HEADS_PER_STEP * dh
    q_tile = pl.BlockSpec((tq, hw), lambda bb, h, i: (bb * nq + i, h))
    seq_k = pl.BlockSpec((T, hw), lambda bb, h, i: (bb, h))
    seq_v = pl.BlockSpec((T, hw), lambda bb, h, i: (bb, W // hw + h))
    in_specs = [q_tile, q_tile, seq_k, seq_k, seq_v, q_tile]
    y = _attention((q, qx, kv, kx, kv, z), in_specs, q_tile,
                   B=B, T=T, H=H, dv=dh, tq=tq, tk=tk, nparts=2, head_major=False)
    return _out_proj_ln(y, w_out.astype(BF16), hf, g, b)


def _rope_table_kernel(pos_ref, freq_ref, sign_ref, cos_ref, sin_ref):
    ang = pos_ref[...].astype(F32) * freq_ref[...]
    cos_ref[...] = jnp.cos(ang)
    sin_ref[...] = jnp.sin(ang) * sign_ref[...]


def _rope_tables(positions, *, B, T):
    half = MLA_ROPE // 2
    inv_freq = ROPE_BASE ** (-jnp.arange(0, MLA_ROPE, 2, dtype=F32) / MLA_ROPE)
    freq = jnp.tile(inv_freq, LANES // half).reshape(1, LANES)
    sign = jnp.tile(jnp.concatenate([-jnp.ones((half,), F32), jnp.ones((half,), F32)]),
                    LANES // MLA_ROPE).reshape(1, LANES)
    M = B * T
    tm = min(1024, M)
    return pl.pallas_call(
        _rope_table_kernel,
        grid=(M // tm,),
        in_specs=[pl.BlockSpec((tm, 1), lambda i: (i, 0)),
                  pl.BlockSpec((1, LANES), lambda i: (0, 0)),
                  pl.BlockSpec((1, LANES), lambda i: (0, 0))],
        out_specs=[pl.BlockSpec((tm, LANES), lambda i: (i, 0)),
                   pl.BlockSpec((tm, LANES), lambda i: (i, 0))],
        out_shape=[jax.ShapeDtypeStruct((M, LANES), F32), jax.ShapeDtypeStruct((M, LANES), F32)],
        compiler_params=_cparams("parallel"),
        name="rope_tables",
    )(positions.reshape(M, 1), freq, sign)


def _rms_to_bf16(x, w):
    y = x * lax.rsqrt(jnp.mean(x * x, axis=-1, keepdims=True) + RMS_EPS)
    return (y * w).astype(BF16)


def _rope_pair(c2, cos, sin):
    return c2 * cos + pltpu.roll(c2, MLA_ROPE, 1) * sin


def _mla_qkv_kernel(x_ref, wl_ref, qn_ref, kn_ref, wq_ref, wkv_ref, cos_ref, sin_ref,
                    q_ref, k_ref, v_ref, *, scale):
    qr, kr = MLA_Q_RANK, MLA_KV_RANK
    cos, sin = cos_ref[...], sin_ref[...]
    lat = jnp.dot(x_ref[...], wl_ref[...], preferred_element_type=F32)
    qn = _rms_to_bf16(lat[:, :qr], qn_ref[...])
    kn = _rms_to_bf16(lat[:, qr:qr + kr], kn_ref[...])
    k_pe = _rope_pair(lat[:, qr + kr:], cos, sin)[:, :MLA_ROPE].astype(k_ref.dtype)
    for h in range(MLA_HEADS):
        r = jnp.dot(qn, wq_ref[h], preferred_element_type=F32)
        pe = _rope_pair(r[:, MLA_NOPE:], cos, sin)
        q_ref[h, :, :MLA_NOPE] = (r[:, :MLA_NOPE] * scale).astype(q_ref.dtype)
        q_ref[h, :, MLA_NOPE:] = (pe[:, :MLA_ROPE] * scale).astype(q_ref.dtype)
        r = jnp.dot(kn, wkv_ref[h], preferred_element_type=F32)
        k_ref[h, :, :MLA_NOPE] = r[:, :MLA_NOPE].astype(k_ref.dtype)
        k_ref[h, :, MLA_NOPE:] = k_pe
        v_ref[h] = r[:, MLA_NOPE:].astype(v_ref.dtype)


def _swap_halves(w):
    half = w.shape[-1] // 2
    return jnp.concatenate([w[..., half:], w[..., :half]], axis=-1)


def _mla_layer(hf, hb, positions, w_in, q_norm, kv_norm, w_q_up, w_kv_up, w_out, g, b, *, B, T):
    H, M = MLA_HEADS, B * T
    qr, kr = MLA_Q_RANK, MLA_KV_RANK
    lat_end = qr + kr + MLA_ROPE
    w_pe = w_in[:, qr + kr:lat_end]
    w_lat = jnp.concatenate([w_in[:, :qr + kr], w_pe, _swap_halves(w_pe)], axis=1).astype(BF16)
    nlat = w_lat.shape[1]
    z = _matmul(hb, w_in[:, lat_end:], F32)
    cos, sin = _rope_tables(positions, B=B, T=T)

    wq = w_q_up.reshape(qr, H, MLA_QK)
    wq = jnp.concatenate([wq, _swap_halves(wq[..., MLA_NOPE:])], axis=-1)
    wq = jnp.transpose(wq, (1, 0, 2)).astype(BF16)
    wkv = jnp.transpose(w_kv_up.reshape(kr, H, MLA_NOPE + MLA_V), (1, 0, 2)).astype(BF16)

    tm = min(256, M)
    wide = MLA_NOPE + LANES
    row = lambda i: (i, 0)
    q, k, v = pl.pallas_call(
        functools.partial(_mla_qkv_kernel, scale=MLA_QK ** -0.5),
        grid=(M // tm,),
        in_specs=[pl.BlockSpec((tm, D_MODEL), row),
                  _resident((D_MODEL, nlat), lambda i: (0, 0)),
                  pl.BlockSpec((1, qr), lambda i: (0, 0)),
                  pl.BlockSpec((1, kr), lambda i: (0, 0)),
                  _resident((H, qr, wide), lambda i: (0, 0, 0)),
                  _resident((H, kr, MLA_NOPE + MLA_V), lambda i: (0, 0, 0)),
                  pl.BlockSpec((tm, LANES), row),
                  pl.BlockSpec((tm, LANES), row)],
        out_specs=[pl.BlockSpec((H, tm, MLA_QK), lambda i: (0, i, 0)),
                   pl.BlockSpec((H, tm, MLA_QK), lambda i: (0, i, 0)),
                   pl.BlockSpec((H, tm, MLA_V), lambda i: (0, i, 0))],
        out_shape=[jax.ShapeDtypeStruct((H, M, MLA_QK), BF16),
                   jax.ShapeDtypeStruct((H, M, MLA_QK), BF16),
                   jax.ShapeDtypeStruct((H, M, MLA_V), BF16)],
        compiler_params=_cparams("parallel"),
        name="mla_qkv_rope",
    )(hb, w_lat, q_norm.reshape(1, qr), kv_norm.reshape(1, kr), wq, wkv, cos, sin)

    tq, tk = _attn_tiles(T)
    nq = T // tq
    nh = ATTN_HEADS_PER_STEP
    in_specs = [
        pl.BlockSpec((nh, tq, MLA_QK), lambda bb, h, i: (h, bb * nq + i, 0)),
        pl.BlockSpec((nh, T, MLA_QK), lambda bb, h, i: (h, bb, 0)),
        pl.BlockSpec((nh, T, MLA_V), lambda bb, h, i: (h, bb, 0)),
        pl.BlockSpec((tq, nh * MLA_V), lambda bb, h, i: (bb * nq + i, h)),
    ]
    out_spec = pl.BlockSpec((tq, nh * MLA_V), lambda bb, h, i: (bb * nq + i, h))
    y = _attention((q, k, v, z), in_specs, out_spec,
                   B=B, T=T, H=H, dv=MLA_V, tq=tq, tk=tk, nparts=1, head_major=True)
    return _out_proj_ln(y, w_out.astype(BF16), hf, g, b)


def _s5_matrices(lam_re, lam_im, log_step, b_re, b_im, c_re, c_im):
    L = S5_CHUNK
    step = jnp.exp(log_step.astype(F32))[:, None]
    mag = jnp.exp(lam_re * step)
    ar = mag * jnp.cos(lam_im * step)
    ai = mag * jnp.sin(lam_im * step)
    den = lam_re * lam_re + lam_im * lam_im
    fr = ((ar - 1.0) * lam_re + ai * lam_im) / den
    fi = (ai * lam_re - (ar - 1.0) * lam_im) / den
    bbr = fr[..., None] * b_re - fi[..., None] * b_im
    bbi = fr[..., None] * b_im + fi[..., None] * b_re
    pr, pi = [jnp.ones_like(ar)], [jnp.zeros_like(ar)]
    for _ in range(L):
        pr_next = pr[-1] * ar - pi[-1] * ai
        pi_next = pr[-1] * ai + pi[-1] * ar
        pr.append(pr_next)
        pi.append(pi_next)
    pwr = jnp.stack(pr, axis=1)
    pwi = jnp.stack(pi, axis=1)

    G, P, I, ROW, NP = S5_GROUPS, S5_STATE, S5_GROUP, S5_ROW, S5_PAIRS

    cpr = c_re[:, None] * pwr[:, :L, None, :] - c_im[:, None] * pwi[:, :L, None, :]
    cpi = c_re[:, None] * pwi[:, :L, None, :] + c_im[:, None] * pwr[:, :L, None, :]
    kern_t = jnp.einsum('gdip,gpj->gjdi', cpr, bbr, precision=HIGHEST) \
        - jnp.einsum('gdip,gpj->gjdi', cpi, bbi, precision=HIGHEST)
    kpad = jnp.pad(kern_t, ((0, 0), (0, 0), (L - 1, 0), (0, 0))).reshape(G, I, (2 * L - 1) * I)
    tmt = jnp.stack([kpad[:, :, (L - 1 - s) * I:(2 * L - 1 - s) * I] for s in range(L)], axis=1)
    tmt = tmt.reshape(G, ROW, ROW)

    qr = pwr[:, :L][:, ::-1]
    qi = pwi[:, :L][:, ::-1]
    bbr_t = jnp.transpose(bbr, (0, 2, 1))[:, None]
    bbi_t = jnp.transpose(bbi, (0, 2, 1))[:, None]
    bpr = (qr[:, :, None] * bbr_t - qi[:, :, None] * bbi_t).reshape(G, ROW, P)
    bpi = (qr[:, :, None] * bbi_t + qi[:, :, None] * bbr_t).reshape(G, ROW, P)

    by_chan = lambda c: jnp.tile(jnp.transpose(c, (0, 2, 1)), (1, 1, L))
    by_step = lambda w: jnp.repeat(jnp.transpose(w[:, 1:], (0, 2, 1)), I, axis=2)
    cr, ci, wr, wi = by_chan(c_re), by_chan(c_im), by_step(pwr), by_step(pwi)
    cpw_r = cr * wr - ci * wi
    cpw_i = -(cr * wi + ci * wr)

    eye2 = jnp.eye(2, dtype=F32)
    bp = jnp.stack([bpr, bpi], axis=2).reshape(NP, 2, ROW, 2, 1, P)
    bp_pair = (bp * eye2[None, :, None, None, :, None]).reshape(NP, 2 * ROW, 4 * P)
    cp = jnp.stack([cpw_r, cpw_i], axis=1).reshape(NP, 2, 2, P, ROW)
    cp = jnp.transpose(cp, (0, 2, 3, 1, 4))[:, :, None]
    cp_pair = (cp * eye2[None, None, :, None, :, None]).reshape(NP, 4 * P, 2 * ROW)
    alr = pwr[:, L].reshape(S5_PAIRS, 2 * S5_STATE)
    ali = pwi[:, L].reshape(S5_PAIRS, 2 * S5_STATE)
    a1 = jnp.concatenate([alr, alr], axis=1).reshape(1, -1)
    a2 = jnp.concatenate([-ali, ali], axis=1).reshape(1, -1)
    return tmt.astype(BF16), bp_pair.astype(BF16), cp_pair.astype(BF16), a1, a2


def _s5_local_state_kernel(x_ref, bp_ref, s_ref):
    s_ref[...] = jnp.dot(x_ref[...], bp_ref[...], preferred_element_type=F32)


def _s5_scan_kernel(s_ref, a1_ref, a2_ref, o_ref, *, B, nch):
    W = s_ref.shape[1]
    a1 = a1_ref[...]
    a2 = a2_ref[...]

    def swap_re_im(c):
        parts = []
        for q in range(W // (2 * LANES)):
            lo = q * 2 * LANES
            parts += [c[:, lo + LANES:lo + 2 * LANES], c[:, lo:lo + LANES]]
        return jnp.concatenate(parts, axis=1)

    def body(n, carry):
        new = []
        for bb in range(B):
            row = bb * nch + n
            c = carry[bb]
            o_ref[pl.ds(row, 1), :] = c
            new.append(a1 * c + a2 * swap_re_im(c) + s_ref[pl.ds(row, 1), :])
        return tuple(new)

    lax.fori_loop(0, nch, body, tuple(jnp.zeros((1, W), F32) for _ in range(B)))


def _gelu_tanh(y):
    return 0.5 * y * (1.0 + jnp.tanh(math.sqrt(2.0 / math.pi) * (y + 0.044715 * (y * y * y))))


def _s5_out_kernel(x_ref, st_ref, tmt_ref, cp_ref, o_ref):
    x = x_ref[...]
    off = jnp.dot(st_ref[...].astype(BF16), cp_ref[...], preferred_element_type=F32)
    for gi in range(2):
        cols = slice(gi * S5_ROW, (gi + 1) * S5_ROW)
        y = jnp.dot(x[:, cols], tmt_ref[gi], preferred_element_type=F32)
        o_ref[:, cols] = (y + off[:, cols]).astype(o_ref.dtype)


def _s5_glu_out_kernel(ys_ref, u_ref, d_ref, wg_ref, bg_ref, z_ref, wo_ref, h_ref, g_ref, b_ref,
                       of_ref, ob_ref):
    y = _gelu_tanh(ys_ref[...].astype(F32) + d_ref[...] * u_ref[...])
    t = jnp.dot(y.astype(BF16), wg_ref[...], preferred_element_type=F32) + bg_ref[...]
    y = (y * _sigmoid(t) * _silu(z_ref[...])).astype(BF16)
    branch = jnp.dot(y, wo_ref[...], preferred_element_type=F32)
    out = _deepnorm_ln(h_ref[...], branch, g_ref[...], b_ref[...])
    of_ref[...] = out
    ob_ref[...] = out.astype(BF16)


def _s5_layer(hf, hb, w_in, lam_re, lam_im, log_step, b_re, b_im, c_re, c_im, d_skip,
              w_glu, b_glu, w_out, g, b, *, B, T):
    M, W, G, L = B * T, S5_WIDTH, S5_GROUPS, S5_CHUNK
    nch = T // L
    rows = B * nch
    u = _matmul(hb, w_in, F32, col0=0, ncols=W)
    z = _matmul(hb, w_in, F32, col0=W, ncols=W)
    tmt, bp_pair, cp_pair, a1, a2 = _s5_matrices(lam_re, lam_im, log_step, b_re, b_im, c_re, c_im)

    x = jnp.transpose(u.astype(BF16).reshape(rows, L, G, S5_GROUP), (0, 2, 1, 3))
    x = x.reshape(rows, G * S5_ROW)
    pw_x, pw_s = 2 * S5_ROW, 4 * S5_STATE
    s_loc = pl.pallas_call(
        _s5_local_state_kernel,
        grid=(S5_PAIRS,),
        in_specs=[pl.BlockSpec((rows, pw_x), lambda p: (0, p)),
                  pl.BlockSpec((None, pw_x, pw_s), lambda p: (p, 0, 0))],
        out_specs=pl.BlockSpec((rows, pw_s), lambda p: (0, p)),
        out_shape=jax.ShapeDtypeStruct((rows, S5_PAIRS * pw_s), F32),
        compiler_params=_cparams("parallel"),
        name="s5_chunk_state",
    )(x, bp_pair)

    sw = 8 * pw_s
    st_in = pl.pallas_call(
        functools.partial(_s5_scan_kernel, B=B, nch=nch),
        grid=(S5_PAIRS * pw_s // sw,),
        in_specs=[pl.BlockSpec((rows, sw), lambda p: (0, p)),
                  pl.BlockSpec((1, sw), lambda p: (0, p)),
                  pl.BlockSpec((1, sw), lambda p: (0, p))],
        out_specs=pl.BlockSpec((rows, sw), lambda p: (0, p)),
        out_shape=jax.ShapeDtypeStruct((rows, S5_PAIRS * pw_s), F32),
        compiler_params=_cparams("parallel"),
        name="s5_state_scan",
    )(s_loc, a1, a2)

    y = pl.pallas_call(
        _s5_out_kernel,
        grid=(S5_PAIRS,),
        in_specs=[pl.BlockSpec((rows, pw_x), lambda p: (0, p)),
                  pl.BlockSpec((rows, pw_s), lambda p: (0, p)),
                  pl.BlockSpec((2, S5_ROW, S5_ROW), lambda p: (p, 0, 0)),
                  pl.BlockSpec((None, pw_s, pw_x), lambda p: (p, 0, 0))],
        out_specs=pl.BlockSpec((rows, pw_x), lambda p: (0, p)),
        out_shape=jax.ShapeDtypeStruct((rows, G * S5_ROW), BF16),
        compiler_params=_cparams("parallel"),
        name="s5_chunk_output",
    )(x, st_in, tmt, cp_pair)
    y = jnp.transpose(y.reshape(rows, G, L, S5_GROUP), (0, 2, 1, 3)).reshape(M, W)

    tm = min(256, M)
    row = lambda i: (i, 0)
    fixed = lambda i: (0, 0)
    return pl.pallas_call(
        _s5_glu_out_kernel,
        grid=(M // tm,),
        in_specs=[pl.BlockSpec((tm, W), row),
                  pl.BlockSpec((tm, W), row),
                  pl.BlockSpec((1, W), fixed),
                  _resident((W, W), fixed),
                  pl.BlockSpec((1, W), fixed),
                  pl.BlockSpec((tm, W), row),
                  _resident((W, D_MODEL), fixed),
                  pl.BlockSpec((tm, D_MODEL), row),
                  pl.BlockSpec((1, D_MODEL), fixed),
                  pl.BlockSpec((1, D_MODEL), fixed)],
        out_specs=[pl.BlockSpec((tm, D_MODEL), row), pl.BlockSpec((tm, D_MODEL), row)],
        out_shape=[jax.ShapeDtypeStruct((M, D_MODEL), F32),
                   jax.ShapeDtypeStruct((M, D_MODEL), BF16)],
        compiler_params=_cparams("parallel"),
        name="s5_glu_out_ln",
    )(y, u, d_skip.reshape(1, W), w_glu.astype(BF16), b_glu.reshape(1, W), z,
      w_out.astype(BF16), hf, g.reshape(1, D_MODEL), b.reshape(1, D_MODEL))


def kernel(x, positions, ln_g, ln_b, ssd_w_in, ssd_conv_w, ssd_conv_b, ssd_dt_bias, ssd_a_log, ssd_d, ssd_norm_w, ssd_w_out, fox_w_in, fox_f_bias, fox_w_out, mla_w_in, mla_q_norm, mla_kv_norm, mla_w_q_up, mla_w_kv_up, mla_w_out, s5_w_in, s5_lambda_re, s5_lambda_im, s5_log_step, s5_b_re, s5_b_im, s5_c_re, s5_c_im, s5_d, s5_w_glu, s5_b_glu, s5_w_out):
    B, T, D = x.shape
    hf = x.reshape(B * T, D)
    hb = hf
    for i in range(DEPTH):
        j = i // 4
        kind = i % 4
        g, b = ln_g[i], ln_b[i]
        if kind == 0:
            hf, hb = _ssd_layer(hf, hb, ssd_w_in[j], ssd_conv_w[j], ssd_conv_b[j], ssd_dt_bias[j],
                                ssd_a_log[j], ssd_d[j], ssd_norm_w[j], ssd_w_out[j], g, b, B=B, T=T)
        elif kind == 1:
            hf, hb = _fox_layer(hf, hb, fox_w_in[j], fox_f_bias[j], fox_w_out[j], g, b, B=B, T=T)
        elif kind == 2:
            hf, hb = _mla_layer(hf, hb.astype(BF16), positions, mla_w_in[j], mla_q_norm[j],
                                mla_kv_norm[j], mla_w_q_up[j], mla_w_kv_up[j], mla_w_out[j],
                                g, b, B=B, T=T)
        else:
            hf, hb = _s5_layer(hf, hb, s5_w_in[j], s5_lambda_re[j], s5_lambda_im[j], s5_log_step[j],
                               s5_b_re[j], s5_b_im[j], s5_c_re[j], s5_c_im[j], s5_d[j],
                               s5_w_glu[j], s5_b_glu[j], s5_w_out[j], g, b, B=B, T=T)
    return hf.reshape(B, T, D)
```

```python
import functools
import math

import jax
import jax.numpy as jnp
from jax import lax
from jax.experimental import pallas as pl
from jax.experimental.pallas import tpu as pltpu

F32 = jnp.float32
BF16 = jnp.bfloat16
HIGHEST = lax.Precision.HIGHEST

D_MODEL = 2048
DEPTH = 4
ALPHA = (2.0 * DEPTH) ** 0.25
LN_EPS = 1e-5
RMS_EPS = 1e-6

SSD_D_INNER = 4096
SSD_HEADS = 64
SSD_HEADDIM = 64
SSD_GROUPS = 8
SSD_HEADS_PER_GROUP = SSD_HEADS // SSD_GROUPS
SSD_STATE = 128
SSD_CONV = 4
SSD_CHUNK = 128
SSD_GROUP_WIDTH = SSD_D_INNER // SSD_GROUPS
SSD_BC_WIDTH = SSD_GROUPS * SSD_STATE
SSD_GROUPS_PER_STEP = 2

FOX_HEADS = 16
FOX_HEAD_DIM = 128
FOX_WIDTH = FOX_HEADS * FOX_HEAD_DIM

MLA_HEADS = 16
MLA_Q_RANK = 512
MLA_KV_RANK = 512
MLA_NOPE = 128
MLA_ROPE = 64
MLA_V = 128
MLA_QK = MLA_NOPE + MLA_ROPE
MLA_WIDTH = MLA_HEADS * MLA_V
ROPE_BASE = 10000.0

S5_WIDTH = D_MODEL
S5_GROUP = 16
S5_GROUPS = S5_WIDTH // S5_GROUP
S5_STATE = 64
S5_CHUNK = 16
S5_TOEP_GROUPS = 16
S5_STATE_GROUPS = 8
S5_STEP_GROUP = 4

LANES = 128
VMEM_LIMIT_BYTES = 48 * 1024 * 1024


def _cparams(*sem):
    return pltpu.CompilerParams(dimension_semantics=sem, vmem_limit_bytes=VMEM_LIMIT_BYTES)


def _resident(block_shape, index_map):
    return pl.BlockSpec(block_shape, index_map, pipeline_mode=pl.Buffered(1))


def _sigmoid(x):
    return 0.5 * (jnp.tanh(0.5 * x) + 1.0)


def _silu(x):
    h = 0.5 * x
    return h + h * jnp.tanh(h)


def _softplus(x):
    return jnp.maximum(x, 0.0) + jnp.log1p(jnp.exp(-jnp.abs(x)))


def _deepnorm_ln(h, branch, g, b):
    r = ALPHA * h + branch
    mu = jnp.mean(r, axis=-1, keepdims=True)
    d = r - mu
    var = jnp.mean(d * d, axis=-1, keepdims=True)
    return d * lax.rsqrt(var + LN_EPS) * g + b


def _mm_kernel(x_ref, w_ref, o_ref, wb_ref, *, scale):
    @pl.when(pl.program_id(1) == 0)
    def _():
        wb_ref[...] = w_ref[...].astype(BF16)

    acc = jnp.dot(x_ref[...].astype(BF16), wb_ref[...], preferred_element_type=F32)
    if scale is not None:
        acc = acc * scale
    o_ref[...] = acc.astype(o_ref.dtype)


def _matmul(x, w, out_dtype, *, col0=0, ncols=None, scale=None, tn=1024):
    M, K = x.shape
    N = w.shape[1] - col0 if ncols is None else ncols
    tm = min(1024 if x.dtype == BF16 else 512, M)
    tn = min(tn, N)
    assert M % tm == 0 and N % tn == 0 and col0 % tn == 0, (M, N, col0, tm, tn)
    j0 = col0 // tn
    return pl.pallas_call(
        functools.partial(_mm_kernel, scale=scale),
        grid=(N // tn, M // tm),
        in_specs=[pl.BlockSpec((tm, K), lambda j, i: (i, 0)),
                  pl.BlockSpec((K, tn), lambda j, i: (0, j0 + j))],
        out_specs=pl.BlockSpec((tm, tn), lambda j, i: (i, j)),
        out_shape=jax.ShapeDtypeStruct((M, N), out_dtype),
        scratch_shapes=[pltpu.VMEM((K, tn), BF16)],
        compiler_params=_cparams("parallel", "arbitrary"),
        name="proj",
    )(x, w)


def _out_ln_kernel(y_ref, w_ref, h_ref, g_ref, b_ref, of_ref, ob_ref):
    branch = jnp.dot(y_ref[...], w_ref[...], preferred_element_type=F32)
    out = _deepnorm_ln(h_ref[...], branch, g_ref[...], b_ref[...])
    of_ref[...] = out
    ob_ref[...] = out.astype(BF16)


def _out_proj_ln(y, w, h, g, b, *, tm=256):
    M, K = y.shape
    D = w.shape[1]
    tm = min(tm, M)
    assert M % tm == 0
    return pl.pallas_call(
        _out_ln_kernel,
        grid=(M // tm,),
        in_specs=[pl.BlockSpec((tm, K), lambda i: (i, 0)),
                  _resident((K, D), lambda i: (0, 0)),
                  pl.BlockSpec((tm, D), lambda i: (i, 0)),
                  pl.BlockSpec((1, D), lambda i: (0, 0)),
                  pl.BlockSpec((1, D), lambda i: (0, 0))],
        out_specs=[pl.BlockSpec((tm, D), lambda i: (i, 0)),
                   pl.BlockSpec((tm, D), lambda i: (i, 0))],
        out_shape=[jax.ShapeDtypeStruct((M, D), F32), jax.ShapeDtypeStruct((M, D), BF16)],
        compiler_params=_cparams("parallel"),
        name="out_proj_ln",
    )(y, w, h, g.reshape(1, D), b.reshape(1, D))


ATTN_HEADS_PER_STEP = 4


def _attn_kernel(*refs, nparts, tq, tk, dv, head_major):
    q_refs = refs[:nparts]
    k_refs = refs[nparts:2 * nparts]
    v_ref, z_ref, o_ref, m_ref, acc_ref = refs[2 * nparts:]
    nh = ATTN_HEADS_PER_STEP
    i = pl.program_id(2)
    ones_cols = jnp.ones((tk, LANES), BF16)

    def head(ref, hh, rows=slice(None)):
        if head_major:
            return ref[hh, rows, :]
        w = ref.shape[-1] // nh
        return ref[rows, hh * w:(hh + 1) * w]

    def cat(pieces):
        return pieces[0] if len(pieces) == 1 else jnp.concatenate(pieces, axis=1)

    qs = [cat([head(r, hh) for r in q_refs]) for hh in range(nh)]
    m_ref[...] = jnp.full_like(m_ref, -jnp.inf)
    acc_ref[...] = jnp.zeros_like(acc_ref)

    def block(j, diag_offset):
        rows = pl.ds(pl.multiple_of(j * tk, tk), tk)
        scores = []
        for hh in range(nh):
            k = cat([head(r, hh, rows) for r in k_refs])
            scores.append(lax.dot_general(qs[hh], k, (((1,), (1,)), ((), ())),
                                          preferred_element_type=F32))
        probs, alphas = [], []
        for hh in range(nh):
            s = scores[hh]
            if diag_offset is not None:
                r = lax.broadcasted_iota(jnp.int32, (tq, tk), 0)
                c = lax.broadcasted_iota(jnp.int32, (tq, tk), 1)
                s = jnp.where(c + diag_offset <= r, s, -jnp.inf)
            m_prev = m_ref[hh]
            m_new = jnp.maximum(m_prev, jnp.max(s, axis=-1, keepdims=True))
            alpha = jnp.exp(m_prev - m_new)
            p = jnp.exp(s - jnp.concatenate([m_new] * (tk // LANES), axis=1))
            m_ref[hh] = m_new
            probs.append(p.astype(BF16))
            alphas.append(jnp.concatenate([alpha] * ((dv + LANES) // LANES), axis=1))
        for hh in range(nh):
            v_aug = jnp.concatenate([head(v_ref, hh, rows), ones_cols], axis=1)
            acc_ref[hh] = alphas[hh] * acc_ref[hh] + jnp.dot(
                probs[hh], v_aug, preferred_element_type=F32)

    n_full = i * (tq // tk)

    def full_block(j, carry):
        block(j, None)
        return carry

    lax.fori_loop(0, n_full, full_block, 0)
    for d in range(tq // tk):
        block(n_full + d, d * tk)

    for hh in range(nh):
        cols = slice(hh * dv, (hh + 1) * dv)
        o = acc_ref[hh, :, :dv] / acc_ref[hh, :, dv:]
        o_ref[:, cols] = (o * _silu(z_ref[:, cols])).astype(o_ref.dtype)


def _attention(inputs, in_specs, out_spec, *, B, T, H, dv, tq, tk, nparts, head_major):
    nh = ATTN_HEADS_PER_STEP
    assert tq % tk == 0 and T % tq == 0 and H % nh == 0
    return pl.pallas_call(
        functools.partial(_attn_kernel, nparts=nparts, tq=tq, tk=tk, dv=dv,
                          head_major=head_major),
        grid=(B, H // nh, T // tq),
        in_specs=in_specs,
        out_specs=out_spec,
        out_shape=jax.ShapeDtypeStruct((B * T, H * dv), BF16),
        scratch_shapes=[pltpu.VMEM((nh, tq, LANES), F32),
                        pltpu.VMEM((nh, tq, dv + LANES), F32)],
        compiler_params=_cparams("parallel", "parallel", "arbitrary"),
        name="causal_attention",
    )(*inputs)


def _attn_tiles(T):
    tq = min(512, T)
    tk = min(512, T)
    return tq, tk


def _ssd_kernel(x_ref, bm_ref, cm_ref, z_ref, dtr_ref,
                cwx_ref, cwb_ref, cwc_ref, cbx_ref, cbb_ref, cbc_ref,
                dtb_ref, al_ref, dsk_ref, nw_ref, spread_ref,
                o_ref,
                extx_ref, extb_ref, extc_ref, state_ref, yz_ref):
    Q, E, P, GW, N = SSD_CHUNK, SSD_HEADS_PER_GROUP, SSD_HEADDIM, SSD_GROUP_WIDTH, SSD_STATE
    NG = SSD_GROUPS_PER_STEP
    groups = range(NG)
    c = pl.program_id(2)

    @pl.when(c == 0)
    def _():
        extx_ref[pl.ds(0, 8), :] = jnp.zeros((8, NG * GW), F32)
        extb_ref[pl.ds(0, 8), :] = jnp.zeros((8, NG * N), F32)
        extc_ref[pl.ds(0, 8), :] = jnp.zeros((8, NG * N), F32)
        state_ref[...] = jnp.zeros_like(state_ref)

    def conv_silu(raw_ref, ext_ref, w_ref, b_ref, cols=slice(None)):
        ext_ref[pl.ds(8, Q), cols] = raw_ref[:, cols]
        acc = b_ref[:, cols] + w_ref[pl.ds(0, 1), cols] * ext_ref[pl.ds(5, Q), cols]
        for kk in range(1, SSD_CONV):
            acc = acc + w_ref[pl.ds(kk, 1), cols] * ext_ref[pl.ds(5 + kk, Q), cols]
        ext_ref[pl.ds(0, 8), cols] = ext_ref[pl.ds(Q, 8), cols]
        return _silu(acc)

    row = lax.broadcasted_iota(jnp.int32, (Q, Q), 0)
    col = lax.broadcasted_iota(jnp.int32, (Q, Q), 1)
    lower = row >= col
    tri_u = (row <= col).astype(BF16)
    left = col < P

    def pieces(v):
        hi = v.astype(BF16).astype(F32)
        mid = (v - hi).astype(BF16).astype(F32)
        lo = ((v - hi) - mid).astype(BF16).astype(F32)
        return jnp.concatenate([hi, mid, lo], axis=0).astype(BF16)

    npairs = E // 2
    dt_r, acs_r = [], []
    for gg in groups:
        dt = _softplus(dtr_ref[gg] + dtb_ref[gg])
        cum3 = jnp.dot(pieces(dt * (-jnp.exp(al_ref[gg]))), tri_u, preferred_element_type=F32)
        dt_r.append(dt)
        acs_r.append(cum3[:E] + cum3[E:2 * E] + cum3[2 * E:])

    bm_b, cm_b, bmt_b, cb, carried = [], [], [], [], []
    for gg in groups:
        ncols = pl.ds(gg * N, N)
        bm = conv_silu(bm_ref, extb_ref, cwb_ref, cbb_ref, ncols)
        cm = conv_silu(cm_ref, extc_ref, cwc_ref, cbc_ref, ncols)
        bm_b.append(bm.astype(BF16))
        cm_b.append(cm.astype(BF16))
        bmt_b.append(bm.T.astype(BF16))
        cb.append(lax.dot_general(cm_b[gg], bm_b[gg], (((1,), (1,)), ((), ())),
                                  preferred_element_type=F32))
        carried.append([jnp.dot(cm_b[gg], state_ref[gg * npairs + kp].astype(BF16),
                                preferred_element_type=F32)
                        for kp in range(npairs)])

    pair_cols = lambda gg, kp: pl.ds(gg * GW + kp * LANES, LANES)
    xps = [[conv_silu(x_ref, extx_ref, cwx_ref, cbx_ref, pair_cols(gg, kp))
            for kp in range(npairs)] for gg in groups]
    gates = [[_silu(z_ref[:, pair_cols(gg, kp)]) for kp in range(npairs)] for gg in groups]

    spreads = [lax.dot_general(pieces(jnp.concatenate([dt_r[gg], acs_r[gg]], axis=0)),
                               spread_ref[...], (((0,), (0,)), ((), ())),
                               preferred_element_type=F32) for gg in groups]

    for gg in groups:
        dt_ch = spreads[gg][:, :GW]
        acs_ch = spreads[gg][:, GW:2 * GW]
        acs_col = spreads[gg][:, 2 * GW:]
        ssq = jnp.zeros((Q, 1), F32)
        for kp in range(npairs):
            cols = pair_cols(gg, kp)
            xp = xps[gg][kp]
            dt_p = dt_ch[:, kp * LANES:(kp + 1) * LANES]
            acs_p = acs_ch[:, kp * LANES:(kp + 1) * LANES]
            last_p = acs_p[Q - 1:Q, :]
            xdt = xp * dt_p
            xdt_b = xdt.astype(BF16)

            ys = []
            for e in (2 * kp, 2 * kp + 1):
                seg = acs_col[:, e * Q:(e + 1) * Q] - acs_r[gg][e:e + 1, :]
                dec = jnp.exp(jnp.where(lower, seg, -jnp.inf))
                ys.append(jnp.dot((cb[gg] * dec).astype(BF16), xdt_b,
                                  preferred_element_type=F32))
            y_diag = jnp.where(left, ys[0], ys[1])

            xw = (xdt * jnp.exp(last_p - acs_p)).astype(BF16)
            s_loc = jnp.dot(bmt_b[gg], xw, preferred_element_type=F32)
            y_off = carried[gg][kp] * jnp.exp(acs_p)
            slot = gg * npairs + kp
            state_ref[slot] = state_ref[slot] * jnp.exp(last_p) + s_loc

            y = (y_diag + y_off + dsk_ref[:, cols] * xp) * gates[gg][kp]
            yz_ref[:, cols] = y
            ssq = ssq + jnp.sum(y * y, axis=-1, keepdims=True)

        gcols = pl.ds(gg * GW, GW)
        inv = lax.rsqrt(ssq * (1.0 / GW) + RMS_EPS)
        o_ref[:, gcols] = (yz_ref[:, gcols] * inv * nw_ref[:, gcols]).astype(o_ref.dtype)


def _ssd_core(z, xbc, dt, conv_w, conv_b, dt_bias, a_log, d_skip, norm_w, *, B, T):
    G, E, Q = SSD_GROUPS, SSD_HEADS_PER_GROUP, SSD_CHUNK
    NC = T // Q
    GW, N = SSD_GROUP_WIDTH, SSD_STATE
    dtr = jnp.transpose(dt[:, :SSD_HEADS].reshape(B, T, G, E), (0, 2, 3, 1))
    cw = conv_w
    cb = conv_b.reshape(1, -1)
    head_rows = lambda v: v.reshape(G, E, 1)
    d_ch = jnp.repeat(d_skip, SSD_HEADDIM).reshape(1, -1)
    heads = jnp.arange(E)
    per_channel = (jnp.arange(GW)[None, :] // SSD_HEADDIM == heads[:, None]).astype(BF16)
    per_head_block = (jnp.arange(E * Q)[None, :] // Q == heads[:, None]).astype(BF16)
    none = lambda n: jnp.zeros((E, n), BF16)
    spread = jnp.concatenate([
        jnp.concatenate([per_channel, none(GW), none(E * Q)], axis=1),
        jnp.concatenate([none(GW), per_channel, per_head_block], axis=1)], axis=0)
    spread = jnp.tile(spread, (3, 1))

    NG = SSD_GROUPS_PER_STEP
    XW, BW = NG * GW, NG * N
    assert G % NG == 0
    bblk = SSD_D_INNER // BW
    cblk = (SSD_D_INNER + SSD_BC_WIDTH) // BW
    rowblk = lambda b, g, c: b * NC + c
    in_specs = [
        pl.BlockSpec((Q, XW), lambda b, g, c: (rowblk(b, g, c), g)),
        pl.BlockSpec((Q, BW), lambda b, g, c: (rowblk(b, g, c), bblk + g)),
        pl.BlockSpec((Q, BW), lambda b, g, c: (rowblk(b, g, c), cblk + g)),
        pl.BlockSpec((Q, XW), lambda b, g, c: (rowblk(b, g, c), g)),
        pl.BlockSpec((None, NG, E, Q), lambda b, g, c: (b, g, 0, c)),
        pl.BlockSpec((SSD_CONV, XW), lambda b, g, c: (0, g)),
        pl.BlockSpec((SSD_CONV, BW), lambda b, g, c: (0, bblk + g)),
        pl.BlockSpec((SSD_CONV, BW), lambda b, g, c: (0, cblk + g)),
        pl.BlockSpec((1, XW), lambda b, g, c: (0, g)),
        pl.BlockSpec((1, BW), lambda b, g, c: (0, bblk + g)),
        pl.BlockSpec((1, BW), lambda b, g, c: (0, cblk + g)),
        pl.BlockSpec((NG, E, 1), lambda b, g, c: (g, 0, 0)),
        pl.BlockSpec((NG, E, 1), lambda b, g, c: (g, 0, 0)),
        pl.BlockSpec((1, XW), lambda b, g, c: (0, g)),
        pl.BlockSpec((1, XW), lambda b, g, c: (0, g)),
        pl.BlockSpec(spread.shape, lambda b, g, c: (0, 0)),
    ]
    return pl.pallas_call(
        _ssd_kernel,
        grid=(B, G // NG, NC),
        in_specs=in_specs,
        out_specs=pl.BlockSpec((Q, XW), lambda b, g, c: (rowblk(b, g, c), g)),
        out_shape=jax.ShapeDtypeStruct((B * T, SSD_D_INNER), BF16),
        scratch_shapes=[pltpu.VMEM((Q + 8, XW), F32), pltpu.VMEM((Q + 8, BW), F32),
                        pltpu.VMEM((Q + 8, BW), F32),
                        pltpu.VMEM((NG * E // 2, N, LANES), F32), pltpu.VMEM((Q, XW), F32)],
        compiler_params=_cparams("parallel", "parallel", "arbitrary"),
        name="ssd_chunk_scan",
    )(xbc, xbc, xbc, z, dtr, cw, cw, cw, cb, cb, cb,
      head_rows(dt_bias), head_rows(a_log), d_ch, norm_w.reshape(1, -1), spread)


def _pad_cols(w, n):
    return jnp.pad(w, ((0, 0), (0, n - w.shape[1])))


def _ssd_layer(hf, hb, w_in, conv_w, conv_b, dt_bias, a_log, d_skip, norm_w, w_out, g, b, *, B, T):
    di = SSD_D_INNER
    cd = di + 2 * SSD_BC_WIDTH
    z = _matmul(hb, w_in, F32, col0=0, ncols=di)
    xbc = _matmul(hb, w_in, F32, col0=di, ncols=cd)
    dt = _matmul(hb, _pad_cols(w_in[:, di + cd:], LANES), F32)
    y = _ssd_core(z, xbc, dt, conv_w, conv_b, dt_bias, a_log, d_skip, norm_w, B=B, T=T)
    return _out_proj_ln(y, w_out.astype(BF16), hf, g, b)


FOX_BIAS_PIECES = 3


def _fox_bias_kernel(f_ref, b_ref, qx_ref, kx_ref, carry_ref, *, nblk):
    blk, W, P = LANES, FOX_WIDTH, FOX_BIAS_PIECES
    row = lax.broadcasted_iota(jnp.int32, (blk, blk), 0)
    col = lax.broadcasted_iota(jnp.int32, (blk, blk), 1)
    tri = (row >= col).astype(F32)
    bias = b_ref[...]
    r = lax.broadcasted_iota(jnp.int32, (P * blk, W), 0)
    c = lax.broadcasted_iota(jnp.int32, (P * blk, W), 1)
    head, piece = r % blk, r // blk
    is_head = head < FOX_HEADS
    place_q = jnp.logical_and(is_head, c == head * blk + piece).astype(BF16)
    place_k = jnp.logical_and(is_head, c == head * blk + P + piece).astype(BF16)
    cmod = lax.broadcasted_iota(jnp.int32, (1, W), 1) % blk
    ones_q = jnp.logical_and(cmod >= P, cmod < 2 * P).astype(F32)
    ones_k = (cmod < P).astype(F32)

    def body(t, carry):
        rows = pl.ds(pl.multiple_of(t * blk, blk), blk)
        x = f_ref[rows, :] + bias
        logf = jnp.minimum(x, 0.0) - jnp.log1p(jnp.exp(-jnp.abs(x)))
        cum = jnp.dot(tri, logf, precision=HIGHEST, preferred_element_type=F32) + carry
        hi = cum.astype(BF16)
        r1 = cum - hi.astype(F32)
        mid = r1.astype(BF16)
        lo = (r1 - mid.astype(F32)).astype(BF16)
        pieces = jnp.concatenate([hi, mid, lo], axis=1)
        qx_ref[rows, :] = (jnp.dot(pieces, place_q, preferred_element_type=F32)
                           + ones_q).astype(BF16)
        kx_ref[rows, :] = (ones_k - jnp.dot(pieces, place_k, preferred_element_type=F32)
                           ).astype(BF16)
        return cum[blk - 1:blk, :]

    @pl.when(pl.program_id(1) == 0)
    def _():
        carry_ref[...] = jnp.zeros_like(carry_ref)

    carry_ref[...] = lax.fori_loop(0, nblk, body, carry_ref[...])


def _fox_bias_columns(f, f_bias, *, B, T):
    bias = jnp.pad(f_bias, (0, LANES - FOX_HEADS)).reshape(1, LANES)
    tt = min(512, T)
    out = jax.ShapeDtypeStruct((B, T, FOX_WIDTH), BF16)
    spec = pl.BlockSpec((None, tt, FOX_WIDTH), lambda b, t: (b, t, 0))
    qx, kx = pl.pallas_call(
        functools.partial(_fox_bias_kernel, nblk=tt // LANES),
        grid=(B, T // tt),
        in_specs=[pl.BlockSpec((None, tt, LANES), lambda b, t: (b, t, 0)),
                  pl.BlockSpec((1, LANES), lambda b, t: (0, 0))],
        out_specs=[spec, spec],
        out_shape=[out, out],
        scratch_shapes=[pltpu.VMEM((1, LANES), F32)],
        compiler_params=_cparams("parallel", "arbitrary"),
        name="fox_cum_log_forget",
    )(f.reshape(B, T, LANES), bias)
    return qx.reshape(B * T, FOX_WIDTH), kx.reshape(B * T, FOX_WIDTH)


def _fox_layer(hf, hb, w_in, f_bias, w_out, g, b, *, B, T):
    W, H, dh = FOX_WIDTH, FOX_HEADS, FOX_HEAD_DIM
    q = _matmul(hb, w_in, BF16, col0=0, ncols=W, scale=dh ** -0.5)
    kv = _matmul(hb, w_in, BF16, col0=W, ncols=2 * W)
    z = _matmul(hb, w_in, F32, col0=3 * W, ncols=W)
    f = _matmul(hb, _pad_cols(w_in[:, 4 * W:], LANES), F32)
    qx, kx = _fox_bias_columns(f, f_bias, B=B, T=T)

    tq, tk = _attn_tiles(T)
    nq = T // tq
    hw = ATTN_HEADS_PER_STEP * dh
    q_tile = pl.BlockSpec((tq, hw), lambda bb, h, i: (bb * nq + i, h))
    seq_k = pl.BlockSpec((T, hw), lambda bb, h, i: (bb, h))
    seq_v = pl.BlockSpec((T, hw), lambda bb, h, i: (bb, W // hw + h))
    in_specs = [q_tile, q_tile, seq_k, seq_k, seq_v, q_tile]
    y = _attention((q, qx, kv, kx, kv, z), in_specs, q_tile,
                   B=B, T=T, H=H, dv=dh, tq=tq, tk=tk, nparts=2, head_major=False)
    return _out_proj_ln(y, w_out.astype(BF16), hf, g, b)


def _rope_table_kernel(pos_ref, freq_ref, sign_ref, cos_ref, sin_ref):
    ang = pos_ref[...].astype(F32) * freq_ref[...]
    cos_ref[...] = jnp.cos(ang)
    sin_ref[...] = jnp.sin(ang) * sign_ref[...]


def _rope_tables(positions, *, B, T):
    half = MLA_ROPE // 2
    inv_freq = ROPE_BASE ** (-jnp.arange(0, MLA_ROPE, 2, dtype=F32) / MLA_ROPE)
    freq = jnp.tile(inv_freq, LANES // half).reshape(1, LANES)
    sign = jnp.tile(jnp.concatenate([-jnp.ones((half,), F32), jnp.ones((half,), F32)]),
                    LANES // MLA_ROPE).reshape(1, LANES)
    M = B * T
    tm = min(1024, M)
    return pl.pallas_call(
        _rope_table_kernel,
        grid=(M // tm,),
        in_specs=[pl.BlockSpec((tm, 1), lambda i: (i, 0)),
                  pl.BlockSpec((1, LANES), lambda i: (0, 0)),
                  pl.BlockSpec((1, LANES), lambda i: (0, 0))],
        out_specs=[pl.BlockSpec((tm, LANES), lambda i: (i, 0)),
                   pl.BlockSpec((tm, LANES), lambda i: (i, 0))],
        out_shape=[jax.ShapeDtypeStruct((M, LANES), F32), jax.ShapeDtypeStruct((M, LANES), F32)],
        compiler_params=_cparams("parallel"),
        name="rope_tables",
    )(positions.reshape(M, 1), freq, sign)


def _rms_to_bf16(x, w):
    y = x * lax.rsqrt(jnp.mean(x * x, axis=-1, keepdims=True) + RMS_EPS)
    return (y * w).astype(BF16)


def _rope_pair(c2, cos, sin):
    return c2 * cos + pltpu.roll(c2, MLA_ROPE, 1) * sin


def _mla_qkv_kernel(x_ref, wl_ref, qn_ref, kn_ref, wq_ref, wkv_ref, cos_ref, sin_ref,
                    q_ref, k_ref, v_ref, *, scale):
    qr, kr = MLA_Q_RANK, MLA_KV_RANK
    cos, sin = cos_ref[...], sin_ref[...]
    lat = jnp.dot(x_ref[...], wl_ref[...], preferred_element_type=F32)
    qn = _rms_to_bf16(lat[:, :qr], qn_ref[...])
    kn = _rms_to_bf16(lat[:, qr:qr + kr], kn_ref[...])
    k_pe = _rope_pair(lat[:, qr + kr:], cos, sin)[:, :MLA_ROPE].astype(k_ref.dtype)
    for h in range(MLA_HEADS):
        r = jnp.dot(qn, wq_ref[h], preferred_element_type=F32)
        pe = _rope_pair(r[:, MLA_NOPE:], cos, sin)
        q_ref[h, :, :MLA_NOPE] = (r[:, :MLA_NOPE] * scale).astype(q_ref.dtype)
        q_ref[h, :, MLA_NOPE:] = (pe[:, :MLA_ROPE] * scale).astype(q_ref.dtype)
        r = jnp.dot(kn, wkv_ref[h], preferred_element_type=F32)
        k_ref[h, :, :MLA_NOPE] = r[:, :MLA_NOPE].astype(k_ref.dtype)
        k_ref[h, :, MLA_NOPE:] = k_pe
        v_ref[h] = r[:, MLA_NOPE:].astype(v_ref.dtype)


def _swap_halves(w):
    half = w.shape[-1] // 2
    return jnp.concatenate([w[..., half:], w[..., :half]], axis=-1)


def _mla_layer(hf, hb, positions, w_in, q_norm, kv_norm, w_q_up, w_kv_up, w_out, g, b, *, B, T):
    H, M = MLA_HEADS, B * T
    qr, kr = MLA_Q_RANK, MLA_KV_RANK
    lat_end = qr + kr + MLA_ROPE
    w_pe = w_in[:, qr + kr:lat_end]
    w_lat = jnp.concatenate([w_in[:, :qr + kr], w_pe, _swap_halves(w_pe)], axis=1).astype(BF16)
    nlat = w_lat.shape[1]
    z = _matmul(hb, w_in[:, lat_end:], F32)
    cos, sin = _rope_tables(positions, B=B, T=T)

    wq = w_q_up.reshape(qr, H, MLA_QK)
    wq = jnp.concatenate([wq, _swap_halves(wq[..., MLA_NOPE:])], axis=-1)
    wq = jnp.transpose(wq, (1, 0, 2)).astype(BF16)
    wkv = jnp.transpose(w_kv_up.reshape(kr, H, MLA_NOPE + MLA_V), (1, 0, 2)).astype(BF16)

    tm = min(256, M)
    wide = MLA_NOPE + LANES
    row = lambda i: (i, 0)
    q, k, v = pl.pallas_call(
        functools.partial(_mla_qkv_kernel, scale=MLA_QK ** -0.5),
        grid=(M // tm,),
        in_specs=[pl.BlockSpec((tm, D_MODEL), row),
                  _resident((D_MODEL, nlat), lambda i: (0, 0)),
                  pl.BlockSpec((1, qr), lambda i: (0, 0)),
                  pl.BlockSpec((1, kr), lambda i: (0, 0)),
                  _resident((H, qr, wide), lambda i: (0, 0, 0)),
                  _resident((H, kr, MLA_NOPE + MLA_V), lambda i: (0, 0, 0)),
                  pl.BlockSpec((tm, LANES), row),
                  pl.BlockSpec((tm, LANES), row)],
        out_specs=[pl.BlockSpec((H, tm, MLA_QK), lambda i: (0, i, 0)),
                   pl.BlockSpec((H, tm, MLA_QK), lambda i: (0, i, 0)),
                   pl.BlockSpec((H, tm, MLA_V), lambda i: (0, i, 0))],
        out_shape=[jax.ShapeDtypeStruct((H, M, MLA_QK), BF16),
                   jax.ShapeDtypeStruct((H, M, MLA_QK), BF16),
                   jax.ShapeDtypeStruct((H, M, MLA_V), BF16)],
        compiler_params=_cparams("parallel"),
        name="mla_qkv_rope",
    )(hb, w_lat, q_norm.reshape(1, qr), kv_norm.reshape(1, kr), wq, wkv, cos, sin)

    tq, tk = _attn_tiles(T)
    nq = T // tq
    nh = ATTN_HEADS_PER_STEP
    in_specs = [
        pl.BlockSpec((nh, tq, MLA_QK), lambda bb, h, i: (h, bb * nq + i, 0)),
        pl.BlockSpec((nh, T, MLA_QK), lambda bb, h, i: (h, bb, 0)),
        pl.BlockSpec((nh, T, MLA_V), lambda bb, h, i: (h, bb, 0)),
        pl.BlockSpec((tq, nh * MLA_V), lambda bb, h, i: (bb * nq + i, h)),
    ]
    out_spec = pl.BlockSpec((tq, nh * MLA_V), lambda bb, h, i: (bb * nq + i, h))
    y = _attention((q, k, v, z), in_specs, out_spec,
                   B=B, T=T, H=H, dv=MLA_V, tq=tq, tk=tk, nparts=1, head_major=True)
    return _out_proj_ln(y, w_out.astype(BF16), hf, g, b)


def _s5_matrices(lam_re, lam_im, log_step, b_re, b_im, c_re, c_im):
    L = S5_CHUNK
    step = jnp.exp(log_step.astype(F32))[:, None]
    mag = jnp.exp(lam_re * step)
    ar = mag * jnp.cos(lam_im * step)
    ai = mag * jnp.sin(lam_im * step)
    den = lam_re * lam_re + lam_im * lam_im
    fr = ((ar - 1.0) * lam_re + ai * lam_im) / den
    fi = (ai * lam_re - (ar - 1.0) * lam_im) / den
    bbr = fr[..., None] * b_re - fi[..., None] * b_im
    bbi = fr[..., None] * b_im + fi[..., None] * b_re
    pr, pi = [jnp.ones_like(ar)], [jnp.zeros_like(ar)]
    for _ in range(L):
        pr_next = pr[-1] * ar - pi[-1] * ai
        pi_next = pr[-1] * ai + pi[-1] * ar
        pr.append(pr_next)
        pi.append(pi_next)
    pwr = jnp.stack(pr, axis=1)
    pwi = jnp.stack(pi, axis=1)

    G, P, I = S5_GROUPS, S5_STATE, S5_GROUP
    TG, SG = S5_TOEP_GROUPS, S5_STATE_GROUPS
    NTB, NSB = G // TG, G // SG

    cpr = c_re[:, None] * pwr[:, :L, None, :] - c_im[:, None] * pwi[:, :L, None, :]
    cpi = c_re[:, None] * pwi[:, :L, None, :] + c_im[:, None] * pwr[:, :L, None, :]
    kern_t = jnp.einsum('gdip,gpj->gjdi', cpr, bbr, precision=HIGHEST) \
        - jnp.einsum('gdip,gpj->gjdi', cpi, bbi, precision=HIGHEST)
    kd = jnp.transpose(kern_t.reshape(NTB, TG, I, L, I), (3, 0, 1, 2, 4))
    kd = kd[:, :, :, :, None, :] * jnp.eye(TG, dtype=F32)[None, None, :, None, :, None]
    kd = kd.reshape(L, NTB, TG * I, TG * I)
    kd = jnp.pad(kd, ((L - 1, 0), (0, 0), (0, 0), (0, 0)))

    qr = pwr[:, :L][:, ::-1]
    qi = pwi[:, :L][:, ::-1]
    bbr_t = jnp.transpose(bbr, (0, 2, 1))[:, None]
    bbi_t = jnp.transpose(bbi, (0, 2, 1))[:, None]
    bp = jnp.stack([qr[:, :, None] * bbr_t - qi[:, :, None] * bbi_t,
                    qr[:, :, None] * bbi_t + qi[:, :, None] * bbr_t], axis=3)
    bp = jnp.transpose(bp.reshape(NSB, SG, L, I, 2, P), (2, 0, 1, 3, 4, 5))
    bd = bp[:, :, :, :, :, None, :] * jnp.eye(SG, dtype=F32)[None, None, :, None, None, :, None]
    bd = bd.reshape(L, NSB, SG * I, 2 * SG * P)

    chan = lambda c: jnp.transpose(c.reshape(NSB, SG, I, P), (0, 3, 1, 2)).reshape(1, NSB, P, SG * I)
    step_pow = lambda w: jnp.repeat(
        jnp.transpose(w[:, 1:].reshape(NSB, SG, L, P), (2, 0, 3, 1)), I, axis=3)
    cr, ci, wr, wi = chan(c_re), chan(c_im), step_pow(pwr), step_pow(pwi)
    cp = jnp.stack([cr * wr - ci * wi, -(cr * wi + ci * wr)], axis=2)
    own = (jnp.arange(SG * I)[None, :] // I == jnp.arange(SG)[:, None]).astype(F32)
    cd = cp[:, :, :, None, :, :] * own[None, None, None, :, None, :]
    cd = cd.reshape(L, NSB, 2 * SG * P, SG * I)

    alr = pwr[:, L].reshape(NSB, SG * P)
    ali = pwi[:, L].reshape(NSB, SG * P)
    a1 = jnp.concatenate([alr, alr], axis=1).reshape(1, -1)
    a2 = jnp.concatenate([-ali, ali], axis=1).reshape(1, -1)
    return kd.astype(BF16), bd.astype(BF16), cd.astype(BF16), a1, a2


def _s5_local_state_kernel(*refs):
    u_refs, (bd_ref, s_ref) = refs[:S5_CHUNK], refs[S5_CHUNK:]
    acc = jnp.dot(u_refs[0][...].astype(BF16), bd_ref[0], preferred_element_type=F32)
    for s in range(1, S5_CHUNK):
        acc = acc + jnp.dot(u_refs[s][...].astype(BF16), bd_ref[s], preferred_element_type=F32)
    s_ref[...] = acc


def _s5_scan_kernel(s_ref, a1_ref, a2_ref, o_ref, *, B, nch):
    W = s_ref.shape[1]
    a1 = a1_ref[...]
    a2 = a2_ref[...]
    unit = 2 * S5_STATE_GROUPS * S5_STATE

    def swap_re_im(c):
        parts = []
        for q in range(W // unit):
            lo = q * unit
            parts += [c[:, lo + unit // 2:lo + unit], c[:, lo:lo + unit // 2]]
        return jnp.concatenate(parts, axis=1)

    def body(n, carry):
        new = []
        for bb in range(B):
            row = bb * nch + n
            c = carry[bb]
            o_ref[pl.ds(row, 1), :] = c
            new.append(a1 * c + a2 * swap_re_im(c) + s_ref[pl.ds(row, 1), :])
        return tuple(new)

    lax.fori_loop(0, nch, body, tuple(jnp.zeros((1, W), F32) for _ in range(B)))


def _gelu_tanh(y):
    return 0.5 * y * (1.0 + jnp.tanh(math.sqrt(2.0 / math.pi) * (y + 0.044715 * (y * y * y))))


def _s5_out_kernel(*refs):
    L = S5_CHUNK
    u_refs, (kd_ref, st_ref, cd_ref, o_ref, ub_ref, acc_ref) = refs[:L], refs[L:]
    r = pl.program_id(1)

    @pl.when(r == 0)
    def _():
        for s in range(L):
            ub_ref[s] = u_refs[s][...].astype(BF16)

    def lags(first):
        out = None
        for s in range(first, first + S5_STEP_GROUP):
            d = jnp.dot(ub_ref[s], kd_ref[r - s + (L - 1)], preferred_element_type=F32)
            out = d if out is None else out + d
        return out

    st = st_ref[...].astype(BF16)
    half = st.shape[1] // 2
    carried = jnp.concatenate(
        [jnp.dot(st[:, :half], cd_ref[0], preferred_element_type=F32),
         jnp.dot(st[:, half:], cd_ref[1], preferred_element_type=F32)], axis=1)
    acc_ref[...] = carried + lags(0)
    for first in range(S5_STEP_GROUP, L, S5_STEP_GROUP):
        @pl.when(first <= r)
        def _(first=first):
            acc_ref[...] += lags(first)
    o_ref[...] = acc_ref[...].astype(o_ref.dtype)


def _s5_glu_out_kernel(ys_ref, u_ref, d_ref, wg_ref, bg_ref, z_ref, wo_ref, h_ref, g_ref, b_ref,
                       of_ref, ob_ref):
    y = _gelu_tanh(ys_ref[...].astype(F32) + d_ref[...] * u_ref[...])
    t = jnp.dot(y.astype(BF16), wg_ref[...], preferred_element_type=F32) + bg_ref[...]
    y = (y * _sigmoid(t) * _silu(z_ref[...])).astype(BF16)
    branch = jnp.dot(y, wo_ref[...], preferred_element_type=F32)
    out = _deepnorm_ln(h_ref[...], branch, g_ref[...], b_ref[...])
    of_ref[...] = out
    ob_ref[...] = out.astype(BF16)


def _s5_layer(hf, hb, w_in, lam_re, lam_im, log_step, b_re, b_im, c_re, c_im, d_skip,
              w_glu, b_glu, w_out, g, b, *, B, T):
    M, W, G, L = B * T, S5_WIDTH, S5_GROUPS, S5_CHUNK
    nch = T // L
    rows = B * nch
    u = _matmul(hb, w_in, F32, col0=0, ncols=W)
    z = _matmul(hb, w_in, F32, col0=W, ncols=W)
    kd, bd, cd, a1, a2 = _s5_matrices(lam_re, lam_im, log_step, b_re, b_im, c_re, c_im)

    u2 = u.reshape(rows, L * W)
    TG, SG = S5_TOEP_GROUPS, S5_STATE_GROUPS
    ntb, nsb = G // TG, G // SG
    tw, sw_in, sw = TG * S5_GROUP, SG * S5_GROUP, 2 * SG * S5_STATE
    nstate = nsb * sw
    s_loc = pl.pallas_call(
        _s5_local_state_kernel,
        grid=(nsb,),
        in_specs=[pl.BlockSpec((rows, sw_in), lambda cb, s=s: (0, s * nsb + cb)) for s in range(L)]
        + [pl.BlockSpec((L, None, sw_in, sw), lambda cb: (0, cb, 0, 0))],
        out_specs=pl.BlockSpec((rows, sw), lambda cb: (0, cb)),
        out_shape=jax.ShapeDtypeStruct((rows, nstate), F32),
        compiler_params=_cparams("parallel"),
        name="s5_chunk_state",
    )(*([u2] * L), bd)

    scan_w = 2 * sw
    st_in = pl.pallas_call(
        functools.partial(_s5_scan_kernel, B=B, nch=nch),
        grid=(nstate // scan_w,),
        in_specs=[pl.BlockSpec((rows, scan_w), lambda p: (0, p)),
                  pl.BlockSpec((1, scan_w), lambda p: (0, p)),
                  pl.BlockSpec((1, scan_w), lambda p: (0, p))],
        out_specs=pl.BlockSpec((rows, scan_w), lambda p: (0, p)),
        out_shape=jax.ShapeDtypeStruct((rows, nstate), F32),
        compiler_params=_cparams("parallel"),
        name="s5_state_scan",
    )(s_loc, a1, a2)

    per_toep = TG // SG
    y = pl.pallas_call(
        _s5_out_kernel,
        grid=(ntb, L),
        in_specs=[pl.BlockSpec((rows, tw), lambda cb, r, s=s: (0, s * ntb + cb)) for s in range(L)]
        + [pl.BlockSpec((2 * L - 1, None, tw, tw), lambda cb, r: (0, cb, 0, 0)),
           pl.BlockSpec((rows, per_toep * sw), lambda cb, r: (0, cb)),
           pl.BlockSpec((None, per_toep, sw, sw_in), lambda cb, r: (r, cb, 0, 0))],
        out_specs=pl.BlockSpec((rows, tw), lambda cb, r: (0, r * ntb + cb)),
        out_shape=jax.ShapeDtypeStruct((rows, L * W), BF16),
        scratch_shapes=[pltpu.VMEM((L, rows, tw), BF16), pltpu.VMEM((rows, tw), F32)],
        compiler_params=_cparams("parallel", "arbitrary"),
        name="s5_chunk_output",
    )(*([u2] * L), kd, st_in, cd)
    y = y.reshape(M, W)

    tm = min(256, M)
    row = lambda i: (i, 0)
    fixed = lambda i: (0, 0)
    return pl.pallas_call(
        _s5_glu_out_kernel,
        grid=(M // tm,),
        in_specs=[pl.BlockSpec((tm, W), row),
                  pl.BlockSpec((tm, W), row),
                  pl.BlockSpec((1, W), fixed),
                  _resident((W, W), fixed),
                  pl.BlockSpec((1, W), fixed),
                  pl.BlockSpec((tm, W), row),
                  _resident((W, D_MODEL), fixed),
                  pl.BlockSpec((tm, D_MODEL), row),
                  pl.BlockSpec((1, D_MODEL), fixed),
                  pl.BlockSpec((1, D_MODEL), fixed)],
        out_specs=[pl.BlockSpec((tm, D_MODEL), row), pl.BlockSpec((tm, D_MODEL), row)],
        out_shape=[jax.ShapeDtypeStruct((M, D_MODEL), F32),
                   jax.ShapeDtypeStruct((M, D_MODEL), BF16)],
        compiler_params=_cparams("parallel"),
        name="s5_glu_out_ln",
    )(y, u, d_skip.reshape(1, W), w_glu.astype(BF16), b_glu.reshape(1, W), z,
      w_out.astype(BF16), hf, g.reshape(1, D_MODEL), b.reshape(1, D_MODEL))


def kernel(x, positions, ln_g, ln_b, ssd_w_in, ssd_conv_w, ssd_conv_b, ssd_dt_bias, ssd_a_log, ssd_d, ssd_norm_w, ssd_w_out, fox_w_in, fox_f_bias, fox_w_out, mla_w_in, mla_q_norm, mla_kv_norm, mla_w_q_up, mla_w_kv_up, mla_w_out, s5_w_in, s5_lambda_re, s5_lambda_im, s5_log_step, s5_b_re, s5_b_im, s5_c_re, s5_c_im, s5_d, s5_w_glu, s5_b_glu, s5_w_out):
    B, T, D = x.shape
    hf = x.reshape(B * T, D)
    hb = hf
    for i in range(DEPTH):
        j = i // 4
        kind = i % 4
        g, b = ln_g[i], ln_b[i]
        if kind == 0:
            hf, hb = _ssd_layer(hf, hb, ssd_w_in[j], ssd_conv_w[j], ssd_conv_b[j], ssd_dt_bias[j],
                                ssd_a_log[j], ssd_d[j], ssd_norm_w[j], ssd_w_out[j], g, b, B=B, T=T)
        elif kind == 1:
            hf, hb = _fox_layer(hf, hb, fox_w_in[j], fox_f_bias[j], fox_w_out[j], g, b, B=B, T=T)
        elif kind == 2:
            hf, hb = _mla_layer(hf, hb.astype(BF16), positions, mla_w_in[j], mla_q_norm[j],
                                mla_kv_norm[j], mla_w_q_up[j], mla_w_kv_up[j], mla_w_out[j],
                                g, b, B=B, T=T)
        else:
            hf, hb = _s5_layer(hf, hb, s5_w_in[j], s5_lambda_re[j], s5_lambda_im[j], s5_log_step[j],
                               s5_b_re[j], s5_b_im[j], s5_c_re[j], s5_c_im[j], s5_d[j],
                               s5_w_glu[j], s5_b_glu[j], s5_w_out[j], g, b, B=B, T=T)
    return hf.reshape(B, T, D)
```

```python
import functools
import math

import jax
import jax.numpy as jnp
from jax import lax
from jax.experimental import pallas as pl
from jax.experimental.pallas import tpu as pltpu

F32 = jnp.float32
BF16 = jnp.bfloat16
HIGHEST = lax.Precision.HIGHEST

D_MODEL = 2048
DEPTH = 4
ALPHA = (2.0 * DEPTH) ** 0.25
LN_EPS = 1e-5
RMS_EPS = 1e-6

SSD_D_INNER = 4096
SSD_HEADS = 64
SSD_HEADDIM = 64
SSD_GROUPS = 8
SSD_HEADS_PER_GROUP = SSD_HEADS // SSD_GROUPS
SSD_STATE = 128
SSD_CONV = 4
SSD_CHUNK = 128
SSD_GROUP_WIDTH = SSD_D_INNER // SSD_GROUPS
SSD_BC_WIDTH = SSD_GROUPS * SSD_STATE
SSD_GROUPS_PER_STEP = 2

FOX_HEADS = 16
FOX_HEAD_DIM = 128
FOX_WIDTH = FOX_HEADS * FOX_HEAD_DIM

MLA_HEADS = 16
MLA_Q_RANK = 512
MLA_KV_RANK = 512
MLA_NOPE = 128
MLA_ROPE = 64
MLA_V = 128
MLA_QK = MLA_NOPE + MLA_ROPE
MLA_WIDTH = MLA_HEADS * MLA_V
ROPE_BASE = 10000.0

S5_WIDTH = D_MODEL
S5_GROUP = 16
S5_GROUPS = S5_WIDTH // S5_GROUP
S5_STATE = 64
S5_CHUNK = 8
S5_TOEP_GROUPS = 16
S5_STATE_GROUPS = 8
S5_STEP_GROUP = 4

LANES = 128
VMEM_LIMIT_BYTES = 48 * 1024 * 1024


def _cparams(*sem):
    return pltpu.CompilerParams(dimension_semantics=sem, vmem_limit_bytes=VMEM_LIMIT_BYTES)


def _resident(block_shape, index_map):
    return pl.BlockSpec(block_shape, index_map, pipeline_mode=pl.Buffered(1))


def _sigmoid(x):
    return 0.5 * (jnp.tanh(0.5 * x) + 1.0)


def _silu(x):
    h = 0.5 * x
    return h + h * jnp.tanh(h)


def _softplus(x):
    return jnp.maximum(x, 0.0) + jnp.log1p(jnp.exp(-jnp.abs(x)))


def _deepnorm_ln(h, branch, g, b):
    r = ALPHA * h + branch
    mu = jnp.mean(r, axis=-1, keepdims=True)
    d = r - mu
    var = jnp.mean(d * d, axis=-1, keepdims=True)
    return d * lax.rsqrt(var + LN_EPS) * g + b


def _mm_kernel(x_ref, w_ref, o_ref, wb_ref, *, scale):
    @pl.when(pl.program_id(1) == 0)
    def _():
        wb_ref[...] = w_ref[...].astype(BF16)

    acc = jnp.dot(x_ref[...].astype(BF16), wb_ref[...], preferred_element_type=F32)
    if scale is not None:
        acc = acc * scale
    o_ref[...] = acc.astype(o_ref.dtype)


def _matmul(x, w, out_dtype, *, col0=0, ncols=None, scale=None, tn=1024):
    M, K = x.shape
    N = w.shape[1] - col0 if ncols is None else ncols
    tm = min(1024 if x.dtype == BF16 else 512, M)
    tn = min(tn, N)
    assert M % tm == 0 and N % tn == 0 and col0 % tn == 0, (M, N, col0, tm, tn)
    j0 = col0 // tn
    return pl.pallas_call(
        functools.partial(_mm_kernel, scale=scale),
        grid=(N // tn, M // tm),
        in_specs=[pl.BlockSpec((tm, K), lambda j, i: (i, 0)),
                  pl.BlockSpec((K, tn), lambda j, i: (0, j0 + j))],
        out_specs=pl.BlockSpec((tm, tn), lambda j, i: (i, j)),
        out_shape=jax.ShapeDtypeStruct((M, N), out_dtype),
        scratch_shapes=[pltpu.VMEM((K, tn), BF16)],
        compiler_params=_cparams("parallel", "arbitrary"),
        name="proj",
    )(x, w)


def _out_ln_kernel(y_ref, w_ref, h_ref, g_ref, b_ref, of_ref, ob_ref):
    branch = jnp.dot(y_ref[...], w_ref[...], preferred_element_type=F32)
    out = _deepnorm_ln(h_ref[...], branch, g_ref[...], b_ref[...])
    of_ref[...] = out
    ob_ref[...] = out.astype(BF16)


def _out_proj_ln(y, w, h, g, b, *, tm=256):
    M, K = y.shape
    D = w.shape[1]
    tm = min(tm, M)
    assert M % tm == 0
    return pl.pallas_call(
        _out_ln_kernel,
        grid=(M // tm,),
        in_specs=[pl.BlockSpec((tm, K), lambda i: (i, 0)),
                  _resident((K, D), lambda i: (0, 0)),
                  pl.BlockSpec((tm, D), lambda i: (i, 0)),
                  pl.BlockSpec((1, D), lambda i: (0, 0)),
                  pl.BlockSpec((1, D), lambda i: (0, 0))],
        out_specs=[pl.BlockSpec((tm, D), lambda i: (i, 0)),
                   pl.BlockSpec((tm, D), lambda i: (i, 0))],
        out_shape=[jax.ShapeDtypeStruct((M, D), F32), jax.ShapeDtypeStruct((M, D), BF16)],
        compiler_params=_cparams("parallel"),
        name="out_proj_ln",
    )(y, w, h, g.reshape(1, D), b.reshape(1, D))


ATTN_HEADS_PER_STEP = 4


def _attn_kernel(*refs, nparts, tq, tk, dv, head_major):
    q_refs = refs[:nparts]
    k_refs = refs[nparts:2 * nparts]
    v_ref, z_ref, o_ref, m_ref, acc_ref = refs[2 * nparts:]
    nh = ATTN_HEADS_PER_STEP
    i = pl.program_id(2)
    ones_cols = jnp.ones((tk, LANES), BF16)

    def head(ref, hh, rows=slice(None)):
        if head_major:
            return ref[hh, rows, :]
        w = ref.shape[-1] // nh
        return ref[rows, hh * w:(hh + 1) * w]

    def cat(pieces):
        return pieces[0] if len(pieces) == 1 else jnp.concatenate(pieces, axis=1)

    qs = [cat([head(r, hh) for r in q_refs]) for hh in range(nh)]
    m_ref[...] = jnp.full_like(m_ref, -jnp.inf)
    acc_ref[...] = jnp.zeros_like(acc_ref)

    def block(j, diag_offset):
        rows = pl.ds(pl.multiple_of(j * tk, tk), tk)
        scores = []
        for hh in range(nh):
            k = cat([head(r, hh, rows) for r in k_refs])
            scores.append(lax.dot_general(qs[hh], k, (((1,), (1,)), ((), ())),
                                          preferred_element_type=F32))
        probs, alphas = [], []
        for hh in range(nh):
            s = scores[hh]
            if diag_offset is not None:
                r = lax.broadcasted_iota(jnp.int32, (tq, tk), 0)
                c = lax.broadcasted_iota(jnp.int32, (tq, tk), 1)
                s = jnp.where(c + diag_offset <= r, s, -jnp.inf)
            m_prev = m_ref[hh]
            m_new = jnp.maximum(m_prev, jnp.max(s, axis=-1, keepdims=True))
            alpha = jnp.exp(m_prev - m_new)
            p = jnp.exp(s - jnp.concatenate([m_new] * (tk // LANES), axis=1))
            m_ref[hh] = m_new
            probs.append(p.astype(BF16))
            alphas.append(jnp.concatenate([alpha] * ((dv + LANES) // LANES), axis=1))
        for hh in range(nh):
            v_aug = jnp.concatenate([head(v_ref, hh, rows), ones_cols], axis=1)
            acc_ref[hh] = alphas[hh] * acc_ref[hh] + jnp.dot(
                probs[hh], v_aug, preferred_element_type=F32)

    n_full = i * (tq // tk)

    def full_block(j, carry):
        block(j, None)
        return carry

    lax.fori_loop(0, n_full, full_block, 0)
    for d in range(tq // tk):
        block(n_full + d, d * tk)

    for hh in range(nh):
        cols = slice(hh * dv, (hh + 1) * dv)
        o = acc_ref[hh, :, :dv] / acc_ref[hh, :, dv:]
        o_ref[:, cols] = (o * _silu(z_ref[:, cols])).astype(o_ref.dtype)


def _attention(inputs, in_specs, out_spec, *, B, T, H, dv, tq, tk, nparts, head_major):
    nh = ATTN_HEADS_PER_STEP
    assert tq % tk == 0 and T % tq == 0 and H % nh == 0
    return pl.pallas_call(
        functools.partial(_attn_kernel, nparts=nparts, tq=tq, tk=tk, dv=dv,
                          head_major=head_major),
        grid=(B, H // nh, T // tq),
        in_specs=in_specs,
        out_specs=out_spec,
        out_shape=jax.ShapeDtypeStruct((B * T, H * dv), BF16),
        scratch_shapes=[pltpu.VMEM((nh, tq, LANES), F32),
                        pltpu.VMEM((nh, tq, dv + LANES), F32)],
        compiler_params=_cparams("parallel", "parallel", "arbitrary"),
        name="causal_attention",
    )(*inputs)


def _attn_tiles(T):
    tq = min(512, T)
    tk = min(512, T)
    return tq, tk


def _ssd_kernel(x_ref, bm_ref, cm_ref, z_ref, dtr_ref,
                cwx_ref, cwb_ref, cwc_ref, cbx_ref, cbb_ref, cbc_ref,
                dtb_ref, al_ref, dsk_ref, nw_ref, spread_ref,
                o_ref,
                extx_ref, extb_ref, extc_ref, state_ref, yz_ref):
    Q, E, P, GW, N = SSD_CHUNK, SSD_HEADS_PER_GROUP, SSD_HEADDIM, SSD_GROUP_WIDTH, SSD_STATE
    NG = SSD_GROUPS_PER_STEP
    groups = range(NG)
    c = pl.program_id(2)

    @pl.when(c == 0)
    def _():
        extx_ref[pl.ds(0, 8), :] = jnp.zeros((8, NG * GW), F32)
        extb_ref[pl.ds(0, 8), :] = jnp.zeros((8, NG * N), F32)
        extc_ref[pl.ds(0, 8), :] = jnp.zeros((8, NG * N), F32)
        state_ref[...] = jnp.zeros_like(state_ref)

    def conv_silu(raw_ref, ext_ref, w_ref, b_ref, cols=slice(None)):
        ext_ref[pl.ds(8, Q), cols] = raw_ref[:, cols]
        acc = b_ref[:, cols] + w_ref[pl.ds(0, 1), cols] * ext_ref[pl.ds(5, Q), cols]
        for kk in range(1, SSD_CONV):
            acc = acc + w_ref[pl.ds(kk, 1), cols] * ext_ref[pl.ds(5 + kk, Q), cols]
        ext_ref[pl.ds(0, 8), cols] = ext_ref[pl.ds(Q, 8), cols]
        return _silu(acc)

    row = lax.broadcasted_iota(jnp.int32, (Q, Q), 0)
    col = lax.broadcasted_iota(jnp.int32, (Q, Q), 1)
    lower = row >= col
    tri_u = (row <= col).astype(BF16)
    left = col < P

    def pieces(v):
        hi = v.astype(BF16).astype(F32)
        mid = (v - hi).astype(BF16).astype(F32)
        lo = ((v - hi) - mid).astype(BF16).astype(F32)
        return jnp.concatenate([hi, mid, lo], axis=0).astype(BF16)

    npairs = E // 2
    dt_r, acs_r = [], []
    for gg in groups:
        dt = _softplus(dtr_ref[gg] + dtb_ref[gg])
        cum3 = jnp.dot(pieces(dt * (-jnp.exp(al_ref[gg]))), tri_u, preferred_element_type=F32)
        dt_r.append(dt)
        acs_r.append(cum3[:E] + cum3[E:2 * E] + cum3[2 * E:])

    bm_b, cm_b, bmt_b, cb, carried = [], [], [], [], []
    for gg in groups:
        ncols = pl.ds(gg * N, N)
        bm = conv_silu(bm_ref, extb_ref, cwb_ref, cbb_ref, ncols)
        cm = conv_silu(cm_ref, extc_ref, cwc_ref, cbc_ref, ncols)
        bm_b.append(bm.astype(BF16))
        cm_b.append(cm.astype(BF16))
        bmt_b.append(bm.T.astype(BF16))
        cb.append(lax.dot_general(cm_b[gg], bm_b[gg], (((1,), (1,)), ((), ())),
                                  preferred_element_type=F32))
        carried.append([jnp.dot(cm_b[gg], state_ref[gg * npairs + kp].astype(BF16),
                                preferred_element_type=F32)
                        for kp in range(npairs)])

    pair_cols = lambda gg, kp: pl.ds(gg * GW + kp * LANES, LANES)
    xps = [[conv_silu(x_ref, extx_ref, cwx_ref, cbx_ref, pair_cols(gg, kp))
            for kp in range(npairs)] for gg in groups]
    gates = [[_silu(z_ref[:, pair_cols(gg, kp)]) for kp in range(npairs)] for gg in groups]

    spreads = [lax.dot_general(pieces(jnp.concatenate([dt_r[gg], acs_r[gg]], axis=0)),
                               spread_ref[...], (((0,), (0,)), ((), ())),
                               preferred_element_type=F32) for gg in groups]

    for gg in groups:
        dt_ch = spreads[gg][:, :GW]
        acs_ch = spreads[gg][:, GW:2 * GW]
        acs_col = spreads[gg][:, 2 * GW:]
        ssq = jnp.zeros((Q, 1), F32)
        for kp in range(npairs):
            cols = pair_cols(gg, kp)
            xp = xps[gg][kp]
            dt_p = dt_ch[:, kp * LANES:(kp + 1) * LANES]
            acs_p = acs_ch[:, kp * LANES:(kp + 1) * LANES]
            last_p = acs_p[Q - 1:Q, :]
            xdt = xp * dt_p
            xdt_b = xdt.astype(BF16)

            ys = []
            for e in (2 * kp, 2 * kp + 1):
                seg = acs_col[:, e * Q:(e + 1) * Q] - acs_r[gg][e:e + 1, :]
                dec = jnp.exp(jnp.where(lower, seg, -jnp.inf))
                ys.append(jnp.dot((cb[gg] * dec).astype(BF16), xdt_b,
                                  preferred_element_type=F32))
            y_diag = jnp.where(left, ys[0], ys[1])

            xw = (xdt * jnp.exp(last_p - acs_p)).astype(BF16)
            s_loc = jnp.dot(bmt_b[gg], xw, preferred_element_type=F32)
            y_off = carried[gg][kp] * jnp.exp(acs_p)
            slot = gg * npairs + kp
            state_ref[slot] = state_ref[slot] * jnp.exp(last_p) + s_loc

            y = (y_diag + y_off + dsk_ref[:, cols] * xp) * gates[gg][kp]
            yz_ref[:, cols] = y
            ssq = ssq + jnp.sum(y * y, axis=-1, keepdims=True)

        gcols = pl.ds(gg * GW, GW)
        inv = lax.rsqrt(ssq * (1.0 / GW) + RMS_EPS)
        o_ref[:, gcols] = (yz_ref[:, gcols] * inv * nw_ref[:, gcols]).astype(o_ref.dtype)


def _ssd_core(z, xbc, dt, conv_w, conv_b, dt_bias, a_log, d_skip, norm_w, *, B, T):
    G, E, Q = SSD_GROUPS, SSD_HEADS_PER_GROUP, SSD_CHUNK
    NC = T // Q
    GW, N = SSD_GROUP_WIDTH, SSD_STATE
    dtr = jnp.transpose(dt[:, :SSD_HEADS].reshape(B, T, G, E), (0, 2, 3, 1))
    cw = conv_w
    cb = conv_b.reshape(1, -1)
    head_rows = lambda v: v.reshape(G, E, 1)
    d_ch = jnp.repeat(d_skip, SSD_HEADDIM).reshape(1, -1)
    heads = jnp.arange(E)
    per_channel = (jnp.arange(GW)[None, :] // SSD_HEADDIM == heads[:, None]).astype(BF16)
    per_head_block = (jnp.arange(E * Q)[None, :] // Q == heads[:, None]).astype(BF16)
    none = lambda n: jnp.zeros((E, n), BF16)
    spread = jnp.concatenate([
        jnp.concatenate([per_channel, none(GW), none(E * Q)], axis=1),
        jnp.concatenate([none(GW), per_channel, per_head_block], axis=1)], axis=0)
    spread = jnp.tile(spread, (3, 1))

    NG = SSD_GROUPS_PER_STEP
    XW, BW = NG * GW, NG * N
    assert G % NG == 0
    bblk = SSD_D_INNER // BW
    cblk = (SSD_D_INNER + SSD_BC_WIDTH) // BW
    rowblk = lambda b, g, c: b * NC + c
    in_specs = [
        pl.BlockSpec((Q, XW), lambda b, g, c: (rowblk(b, g, c), g)),
        pl.BlockSpec((Q, BW), lambda b, g, c: (rowblk(b, g, c), bblk + g)),
        pl.BlockSpec((Q, BW), lambda b, g, c: (rowblk(b, g, c), cblk + g)),
        pl.BlockSpec((Q, XW), lambda b, g, c: (rowblk(b, g, c), g)),
        pl.BlockSpec((None, NG, E, Q), lambda b, g, c: (b, g, 0, c)),
        pl.BlockSpec((SSD_CONV, XW), lambda b, g, c: (0, g)),
        pl.BlockSpec((SSD_CONV, BW), lambda b, g, c: (0, bblk + g)),
        pl.BlockSpec((SSD_CONV, BW), lambda b, g, c: (0, cblk + g)),
        pl.BlockSpec((1, XW), lambda b, g, c: (0, g)),
        pl.BlockSpec((1, BW), lambda b, g, c: (0, bblk + g)),
        pl.BlockSpec((1, BW), lambda b, g, c: (0, cblk + g)),
        pl.BlockSpec((NG, E, 1), lambda b, g, c: (g, 0, 0)),
        pl.BlockSpec((NG, E, 1), lambda b, g, c: (g, 0, 0)),
        pl.BlockSpec((1, XW), lambda b, g, c: (0, g)),
        pl.BlockSpec((1, XW), lambda b, g, c: (0, g)),
        pl.BlockSpec(spread.shape, lambda b, g, c: (0, 0)),
    ]
    return pl.pallas_call(
        _ssd_kernel,
        grid=(B, G // NG, NC),
        in_specs=in_specs,
        out_specs=pl.BlockSpec((Q, XW), lambda b, g, c: (rowblk(b, g, c), g)),
        out_shape=jax.ShapeDtypeStruct((B * T, SSD_D_INNER), BF16),
        scratch_shapes=[pltpu.VMEM((Q + 8, XW), F32), pltpu.VMEM((Q + 8, BW), F32),
                        pltpu.VMEM((Q + 8, BW), F32),
                        pltpu.VMEM((NG * E // 2, N, LANES), F32), pltpu.VMEM((Q, XW), F32)],
        compiler_params=_cparams("parallel", "parallel", "arbitrary"),
        name="ssd_chunk_scan",
    )(xbc, xbc, xbc, z, dtr, cw, cw, cw, cb, cb, cb,
      head_rows(dt_bias), head_rows(a_log), d_ch, norm_w.reshape(1, -1), spread)


def _pad_cols(w, n):
    return jnp.pad(w, ((0, 0), (0, n - w.shape[1])))


def _ssd_layer(hf, hb, w_in, conv_w, conv_b, dt_bias, a_log, d_skip, norm_w, w_out, g, b, *, B, T):
    di = SSD_D_INNER
    cd = di + 2 * SSD_BC_WIDTH
    z = _matmul(hb, w_in, F32, col0=0, ncols=di)
    xbc = _matmul(hb, w_in, F32, col0=di, ncols=cd)
    dt = _matmul(hb, _pad_cols(w_in[:, di + cd:], LANES), F32)
    y = _ssd_core(z, xbc, dt, conv_w, conv_b, dt_bias, a_log, d_skip, norm_w, B=B, T=T)
    return _out_proj_ln(y, w_out.astype(BF16), hf, g, b)


FOX_BIAS_PIECES = 3


def _fox_bias_kernel(f_ref, b_ref, qx_ref, kx_ref, carry_ref, *, nblk):
    blk, W, P = LANES, FOX_WIDTH, FOX_BIAS_PIECES
    row = lax.broadcasted_iota(jnp.int32, (blk, blk), 0)
    col = lax.broadcasted_iota(jnp.int32, (blk, blk), 1)
    tri = (row >= col).astype(F32)
    bias = b_ref[...]
    r = lax.broadcasted_iota(jnp.int32, (P * blk, W), 0)
    c = lax.broadcasted_iota(jnp.int32, (P * blk, W), 1)
    head, piece = r % blk, r // blk
    is_head = head < FOX_HEADS
    place_q = jnp.logical_and(is_head, c == head * blk + piece).astype(BF16)
    place_k = jnp.logical_and(is_head, c == head * blk + P + piece).astype(BF16)
    cmod = lax.broadcasted_iota(jnp.int32, (1, W), 1) % blk
    ones_q = jnp.logical_and(cmod >= P, cmod < 2 * P).astype(F32)
    ones_k = (cmod < P).astype(F32)

    def body(t, carry):
        rows = pl.ds(pl.multiple_of(t * blk, blk), blk)
        x = f_ref[rows, :] + bias
        logf = jnp.minimum(x, 0.0) - jnp.log1p(jnp.exp(-jnp.abs(x)))
        cum = jnp.dot(tri, logf, precision=HIGHEST, preferred_element_type=F32) + carry
        hi = cum.astype(BF16)
        r1 = cum - hi.astype(F32)
        mid = r1.astype(BF16)
        lo = (r1 - mid.astype(F32)).astype(BF16)
        pieces = jnp.concatenate([hi, mid, lo], axis=1)
        qx_ref[rows, :] = (jnp.dot(pieces, place_q, preferred_element_type=F32)
                           + ones_q).astype(BF16)
        kx_ref[rows, :] = (ones_k - jnp.dot(pieces, place_k, preferred_element_type=F32)
                           ).astype(BF16)
        return cum[blk - 1:blk, :]

    @pl.when(pl.program_id(1) == 0)
    def _():
        carry_ref[...] = jnp.zeros_like(carry_ref)

    carry_ref[...] = lax.fori_loop(0, nblk, body, carry_ref[...])


def _fox_bias_columns(f, f_bias, *, B, T):
    bias = jnp.pad(f_bias, (0, LANES - FOX_HEADS)).reshape(1, LANES)
    tt = min(512, T)
    out = jax.ShapeDtypeStruct((B, T, FOX_WIDTH), BF16)
    spec = pl.BlockSpec((None, tt, FOX_WIDTH), lambda b, t: (b, t, 0))
    qx, kx = pl.pallas_call(
        functools.partial(_fox_bias_kernel, nblk=tt // LANES),
        grid=(B, T // tt),
        in_specs=[pl.BlockSpec((None, tt, LANES), lambda b, t: (b, t, 0)),
                  pl.BlockSpec((1, LANES), lambda b, t: (0, 0))],
        out_specs=[spec, spec],
        out_shape=[out, out],
        scratch_shapes=[pltpu.VMEM((1, LANES), F32)],
        compiler_params=_cparams("parallel", "arbitrary"),
        name="fox_cum_log_forget",
    )(f.reshape(B, T, LANES), bias)
    return qx.reshape(B * T, FOX_WIDTH), kx.reshape(B * T, FOX_WIDTH)


def _fox_layer(hf, hb, w_in, f_bias, w_out, g, b, *, B, T):
    W, H, dh = FOX_WIDTH, FOX_HEADS, FOX_HEAD_DIM
    q = _matmul(hb, w_in, BF16, col0=0, ncols=W, scale=dh ** -0.5)
    kv = _matmul(hb, w_in, BF16, col0=W, ncols=2 * W)
    z = _matmul(hb, w_in, F32, col0=3 * W, ncols=W)
    f = _matmul(hb, _pad_cols(w_in[:, 4 * W:], LANES), F32)
    qx, kx = _fox_bias_columns(f, f_bias, B=B, T=T)

    tq, tk = _attn_tiles(T)
    nq = T // tq
    hw = ATTN_HEADS_PER_STEP * dh
    q_tile = pl.BlockSpec((tq, hw), lambda bb, h, i: (bb * nq + i, h))
    seq_k = pl.BlockSpec((T, hw), lambda bb, h, i: (bb, h))
    seq_v = pl.BlockSpec((T, hw), lambda bb, h, i: (bb, W // hw + h))
    in_specs = [q_tile, q_tile, seq_k, seq_k, seq_v, q_tile]
    y = _attention((q, qx, kv, kx, kv, z), in_specs, q_tile,
                   B=B, T=T, H=H, dv=dh, tq=tq, tk=tk, nparts=2, head_major=False)
    return _out_proj_ln(y, w_out.astype(BF16), hf, g, b)


def _rope_table_kernel(pos_ref, freq_ref, sign_ref, cos_ref, sin_ref):
    ang = pos_ref[...].astype(F32) * freq_ref[...]
    cos_ref[...] = jnp.cos(ang)
    sin_ref[...] = jnp.sin(ang) * sign_ref[...]


def _rope_tables(positions, *, B, T):
    half = MLA_ROPE // 2
    inv_freq = ROPE_BASE ** (-jnp.arange(0, MLA_ROPE, 2, dtype=F32) / MLA_ROPE)
    freq = jnp.tile(inv_freq, LANES // half).reshape(1, LANES)
    sign = jnp.tile(jnp.concatenate([-jnp.ones((half,), F32), jnp.ones((half,), F32)]),
                    LANES // MLA_ROPE).reshape(1, LANES)
    M = B * T
    tm = min(1024, M)
    return pl.pallas_call(
        _rope_table_kernel,
        grid=(M // tm,),
        in_specs=[pl.BlockSpec((tm, 1), lambda i: (i, 0)),
                  pl.BlockSpec((1, LANES), lambda i: (0, 0)),
                  pl.BlockSpec((1, LANES), lambda i: (0, 0))],
        out_specs=[pl.BlockSpec((tm, LANES), lambda i: (i, 0)),
                   pl.BlockSpec((tm, LANES), lambda i: (i, 0))],
        out_shape=[jax.ShapeDtypeStruct((M, LANES), F32), jax.ShapeDtypeStruct((M, LANES), F32)],
        compiler_params=_cparams("parallel"),
        name="rope_tables",
    )(positions.reshape(M, 1), freq, sign)


def _rms_to_bf16(x, w):
    y = x * lax.rsqrt(jnp.mean(x * x, axis=-1, keepdims=True) + RMS_EPS)
    return (y * w).astype(BF16)


def _rope_pair(c2, cos, sin):
    return c2 * cos + pltpu.roll(c2, MLA_ROPE, 1) * sin


def _mla_qkv_kernel(x_ref, wl_ref, qn_ref, kn_ref, wq_ref, wkv_ref, cos_ref, sin_ref,
                    q_ref, k_ref, v_ref, *, scale):
    qr, kr = MLA_Q_RANK, MLA_KV_RANK
    cos, sin = cos_ref[...], sin_ref[...]
    lat = jnp.dot(x_ref[...], wl_ref[...], preferred_element_type=F32)
    qn = _rms_to_bf16(lat[:, :qr], qn_ref[...])
    kn = _rms_to_bf16(lat[:, qr:qr + kr], kn_ref[...])
    k_pe = _rope_pair(lat[:, qr + kr:], cos, sin)[:, :MLA_ROPE].astype(k_ref.dtype)
    for h in range(MLA_HEADS):
        r = jnp.dot(qn, wq_ref[h], preferred_element_type=F32)
        pe = _rope_pair(r[:, MLA_NOPE:], cos, sin)
        q_ref[h, :, :MLA_NOPE] = (r[:, :MLA_NOPE] * scale).astype(q_ref.dtype)
        q_ref[h, :, MLA_NOPE:] = (pe[:, :MLA_ROPE] * scale).astype(q_ref.dtype)
        r = jnp.dot(kn, wkv_ref[h], preferred_element_type=F32)
        k_ref[h, :, :MLA_NOPE] = r[:, :MLA_NOPE].astype(k_ref.dtype)
        k_ref[h, :, MLA_NOPE:] = k_pe
        v_ref[h] = r[:, MLA_NOPE:].astype(v_ref.dtype)


def _swap_halves(w):
    half = w.shape[-1] // 2
    return jnp.concatenate([w[..., half:], w[..., :half]], axis=-1)


def _mla_layer(hf, hb, positions, w_in, q_norm, kv_norm, w_q_up, w_kv_up, w_out, g, b, *, B, T):
    H, M = MLA_HEADS, B * T
    qr, kr = MLA_Q_RANK, MLA_KV_RANK
    lat_end = qr + kr + MLA_ROPE
    w_pe = w_in[:, qr + kr:lat_end]
    w_lat = jnp.concatenate([w_in[:, :qr + kr], w_pe, _swap_halves(w_pe)], axis=1).astype(BF16)
    nlat = w_lat.shape[1]
    z = _matmul(hb, w_in[:, lat_end:], F32)
    cos, sin = _rope_tables(positions, B=B, T=T)

    wq = w_q_up.reshape(qr, H, MLA_QK)
    wq = jnp.concatenate([wq, _swap_halves(wq[..., MLA_NOPE:])], axis=-1)
    wq = jnp.transpose(wq, (1, 0, 2)).astype(BF16)
    wkv = jnp.transpose(w_kv_up.reshape(kr, H, MLA_NOPE + MLA_V), (1, 0, 2)).astype(BF16)

    tm = min(256, M)
    wide = MLA_NOPE + LANES
    row = lambda i: (i, 0)
    q, k, v = pl.pallas_call(
        functools.partial(_mla_qkv_kernel, scale=MLA_QK ** -0.5),
        grid=(M // tm,),
        in_specs=[pl.BlockSpec((tm, D_MODEL), row),
                  _resident((D_MODEL, nlat), lambda i: (0, 0)),
                  pl.BlockSpec((1, qr), lambda i: (0, 0)),
                  pl.BlockSpec((1, kr), lambda i: (0, 0)),
                  _resident((H, qr, wide), lambda i: (0, 0, 0)),
                  _resident((H, kr, MLA_NOPE + MLA_V), lambda i: (0, 0, 0)),
                  pl.BlockSpec((tm, LANES), row),
                  pl.BlockSpec((tm, LANES), row)],
        out_specs=[pl.BlockSpec((H, tm, MLA_QK), lambda i: (0, i, 0)),
                   pl.BlockSpec((H, tm, MLA_QK), lambda i: (0, i, 0)),
                   pl.BlockSpec((H, tm, MLA_V), lambda i: (0, i, 0))],
        out_shape=[jax.ShapeDtypeStruct((H, M, MLA_QK), BF16),
                   jax.ShapeDtypeStruct((H, M, MLA_QK), BF16),
                   jax.ShapeDtypeStruct((H, M, MLA_V), BF16)],
        compiler_params=_cparams("parallel"),
        name="mla_qkv_rope",
    )(hb, w_lat, q_norm.reshape(1, qr), kv_norm.reshape(1, kr), wq, wkv, cos, sin)

    tq, tk = _attn_tiles(T)
    nq = T // tq
    nh = ATTN_HEADS_PER_STEP
    in_specs = [
        pl.BlockSpec((nh, tq, MLA_QK), lambda bb, h, i: (h, bb * nq + i, 0)),
        pl.BlockSpec((nh, T, MLA_QK), lambda bb, h, i: (h, bb, 0)),
        pl.BlockSpec((nh, T, MLA_V), lambda bb, h, i: (h, bb, 0)),
        pl.BlockSpec((tq, nh * MLA_V), lambda bb, h, i: (bb * nq + i, h)),
    ]
    out_spec = pl.BlockSpec((tq, nh * MLA_V), lambda bb, h, i: (bb * nq + i, h))
    y = _attention((q, k, v, z), in_specs, out_spec,
                   B=B, T=T, H=H, dv=MLA_V, tq=tq, tk=tk, nparts=1, head_major=True)
    return _out_proj_ln(y, w_out.astype(BF16), hf, g, b)


def _s5_matrices(lam_re, lam_im, log_step, b_re, b_im, c_re, c_im):
    L = S5_CHUNK
    step = jnp.exp(log_step.astype(F32))[:, None]
    mag = jnp.exp(lam_re * step)
    ar = mag * jnp.cos(lam_im * step)
    ai = mag * jnp.sin(lam_im * step)
    den = lam_re * lam_re + lam_im * lam_im
    fr = ((ar - 1.0) * lam_re + ai * lam_im) / den
    fi = (ai * lam_re - (ar - 1.0) * lam_im) / den
    bbr = fr[..., None] * b_re - fi[..., None] * b_im
    bbi = fr[..., None] * b_im + fi[..., None] * b_re
    pr, pi = [jnp.ones_like(ar)], [jnp.zeros_like(ar)]
    for _ in range(L):
        pr_next = pr[-1] * ar - pi[-1] * ai
        pi_next = pr[-1] * ai + pi[-1] * ar
        pr.append(pr_next)
        pi.append(pi_next)
    pwr = jnp.stack(pr, axis=1)
    pwi = jnp.stack(pi, axis=1)

    G, P, I = S5_GROUPS, S5_STATE, S5_GROUP
    TG, SG = S5_TOEP_GROUPS, S5_STATE_GROUPS
    NTB, NSB = G // TG, G // SG

    cpr = c_re[:, None] * pwr[:, :L, None, :] - c_im[:, None] * pwi[:, :L, None, :]
    cpi = c_re[:, None] * pwi[:, :L, None, :] + c_im[:, None] * pwr[:, :L, None, :]
    kern_t = jnp.einsum('gdip,gpj->gjdi', cpr, bbr, precision=HIGHEST) \
        - jnp.einsum('gdip,gpj->gjdi', cpi, bbi, precision=HIGHEST)
    idx = jnp.arange

    kc = jnp.transpose(kern_t.reshape(NTB, TG, I, L, I), (3, 0, 1, 2, 4)).reshape(L, NTB, TG * I, I)
    kc = jnp.pad(kc, ((L - 1, 0), (0, 0), (0, 0), (0, 0)))
    repeat_i = (idx(TG * I)[None, :] % I == idx(I)[:, None]).astype(F32)
    same_t = (idx(TG * I)[:, None] // I == idx(TG * I)[None, :] // I)
    kd = jnp.where(same_t, jnp.einsum('dcri,iq->dcrq', kc, repeat_i, precision=HIGHEST), 0.0)

    lanes = lambda v: jnp.tile(v, (1, 1, SG))
    by_step = lambda w: jnp.transpose(
        lanes(w[:, :L][:, ::-1]).reshape(NSB, SG, L, SG * P), (2, 0, 1, 3))[:, :, :, None]
    by_chan = lambda v: lanes(jnp.transpose(v, (0, 2, 1))).reshape(1, NSB, SG, I, SG * P)
    qr, qi = by_step(pwr), by_step(pwi)
    br, bi = by_chan(bbr), by_chan(bbi)
    bd = jnp.concatenate([qr * br - qi * bi, qr * bi + qi * br], axis=-1)
    bd = bd.reshape(L, NSB, SG * I, 2 * SG * P)
    same_b = (idx(SG * I)[:, None] // I == (idx(2 * SG * P)[None, :] % (SG * P)) // P)
    bd = jnp.where(same_b, bd, 0.0)

    chan = lambda c: jnp.transpose(c.reshape(NSB, SG, I, P), (0, 3, 1, 2)).reshape(1, NSB, P, SG * I)
    step_pow = lambda w: jnp.repeat(
        jnp.transpose(w[:, 1:].reshape(NSB, SG, L, P), (2, 0, 3, 1)), I, axis=3)
    cr, ci, wr, wi = chan(c_re), chan(c_im), step_pow(pwr), step_pow(pwi)
    cp = jnp.stack([cr * wr - ci * wi, -(cr * wi + ci * wr)], axis=2)
    cd = jnp.broadcast_to(cp[:, :, :, None], (L, NSB, 2, SG, P, SG * I))
    cd = cd.reshape(L, NSB, 2 * SG * P, SG * I)
    same_c = ((idx(2 * SG * P)[:, None] % (SG * P)) // P == idx(SG * I)[None, :] // I)
    cd = jnp.where(same_c, cd, 0.0)

    alr = pwr[:, L].reshape(NSB, SG * P)
    ali = pwi[:, L].reshape(NSB, SG * P)
    a1 = jnp.concatenate([alr, alr], axis=1).reshape(1, -1)
    a2 = jnp.concatenate([-ali, ali], axis=1).reshape(1, -1)
    return kd.astype(BF16), bd.astype(BF16), cd.astype(BF16), a1, a2


def _s5_local_state_kernel(*refs):
    u_refs, (bd_ref, s_ref) = refs[:S5_CHUNK], refs[S5_CHUNK:]
    acc = jnp.dot(u_refs[0][...].astype(BF16), bd_ref[0], preferred_element_type=F32)
    for s in range(1, S5_CHUNK):
        acc = acc + jnp.dot(u_refs[s][...].astype(BF16), bd_ref[s], preferred_element_type=F32)
    s_ref[...] = acc


def _s5_scan_kernel(s_ref, a1_ref, a2_ref, o_ref, *, B, nch):
    W = s_ref.shape[1]
    a1 = a1_ref[...]
    a2 = a2_ref[...]
    unit = 2 * S5_STATE_GROUPS * S5_STATE

    def swap_re_im(c):
        parts = []
        for q in range(W // unit):
            lo = q * unit
            parts += [c[:, lo + unit // 2:lo + unit], c[:, lo:lo + unit // 2]]
        return jnp.concatenate(parts, axis=1)

    def body(n, carry):
        new = []
        for bb in range(B):
            row = bb * nch + n
            c = carry[bb]
            o_ref[pl.ds(row, 1), :] = c
            new.append(a1 * c + a2 * swap_re_im(c) + s_ref[pl.ds(row, 1), :])
        return tuple(new)

    lax.fori_loop(0, nch, body, tuple(jnp.zeros((1, W), F32) for _ in range(B)))


def _gelu_tanh(y):
    return 0.5 * y * (1.0 + jnp.tanh(math.sqrt(2.0 / math.pi) * (y + 0.044715 * (y * y * y))))


def _s5_out_kernel(*refs):
    L = S5_CHUNK
    u_refs, (kd_ref, st_ref, cd_ref, o_ref, ub_ref, acc_ref) = refs[:L], refs[L:]
    r = pl.program_id(1)

    @pl.when(r == 0)
    def _():
        for s in range(L):
            ub_ref[s] = u_refs[s][...].astype(BF16)

    def lags(first):
        out = None
        for s in range(first, first + S5_STEP_GROUP):
            d = jnp.dot(ub_ref[s], kd_ref[r - s + (L - 1)], preferred_element_type=F32)
            out = d if out is None else out + d
        return out

    st = st_ref[...].astype(BF16)
    half = st.shape[1] // 2
    carried = jnp.concatenate(
        [jnp.dot(st[:, :half], cd_ref[0], preferred_element_type=F32),
         jnp.dot(st[:, half:], cd_ref[1], preferred_element_type=F32)], axis=1)
    acc_ref[...] = carried + lags(0)
    for first in range(S5_STEP_GROUP, L, S5_STEP_GROUP):
        @pl.when(first <= r)
        def _(first=first):
            acc_ref[...] += lags(first)
    o_ref[...] = acc_ref[...].astype(o_ref.dtype)


def _s5_glu_out_kernel(ys_ref, u_ref, d_ref, wg_ref, bg_ref, z_ref, wo_ref, h_ref, g_ref, b_ref,
                       of_ref, ob_ref):
    y = _gelu_tanh(ys_ref[...].astype(F32) + d_ref[...] * u_ref[...])
    t = jnp.dot(y.astype(BF16), wg_ref[...], preferred_element_type=F32) + bg_ref[...]
    y = (y * _sigmoid(t) * _silu(z_ref[...])).astype(BF16)
    branch = jnp.dot(y, wo_ref[...], preferred_element_type=F32)
    out = _deepnorm_ln(h_ref[...], branch, g_ref[...], b_ref[...])
    of_ref[...] = out
    ob_ref[...] = out.astype(BF16)


def _s5_layer(hf, hb, w_in, lam_re, lam_im, log_step, b_re, b_im, c_re, c_im, d_skip,
              w_glu, b_glu, w_out, g, b, *, B, T):
    M, W, G, L = B * T, S5_WIDTH, S5_GROUPS, S5_CHUNK
    nch = T // L
    rows = B * nch
    u = _matmul(hb, w_in, F32, col0=0, ncols=W)
    z = _matmul(hb, w_in, F32, col0=W, ncols=W)
    kd, bd, cd, a1, a2 = _s5_matrices(lam_re, lam_im, log_step, b_re, b_im, c_re, c_im)

    u2 = u.reshape(rows, L * W)
    TG, SG = S5_TOEP_GROUPS, S5_STATE_GROUPS
    ntb, nsb = G // TG, G // SG
    tw, sw_in, sw = TG * S5_GROUP, SG * S5_GROUP, 2 * SG * S5_STATE
    nstate = nsb * sw
    s_loc = pl.pallas_call(
        _s5_local_state_kernel,
        grid=(nsb,),
        in_specs=[pl.BlockSpec((rows, sw_in), lambda cb, s=s: (0, s * nsb + cb)) for s in range(L)]
        + [pl.BlockSpec((L, None, sw_in, sw), lambda cb: (0, cb, 0, 0))],
        out_specs=pl.BlockSpec((rows, sw), lambda cb: (0, cb)),
        out_shape=jax.ShapeDtypeStruct((rows, nstate), F32),
        compiler_params=_cparams("parallel"),
        name="s5_chunk_state",
    )(*([u2] * L), bd)

    scan_w = 2 * sw
    st_in = pl.pallas_call(
        functools.partial(_s5_scan_kernel, B=B, nch=nch),
        grid=(nstate // scan_w,),
        in_specs=[pl.BlockSpec((rows, scan_w), lambda p: (0, p)),
                  pl.BlockSpec((1, scan_w), lambda p: (0, p)),
                  pl.BlockSpec((1, scan_w), lambda p: (0, p))],
        out_specs=pl.BlockSpec((rows, scan_w), lambda p: (0, p)),
        out_shape=jax.ShapeDtypeStruct((rows, nstate), F32),
        compiler_params=_cparams("parallel"),
        name="s5_state_scan",
    )(s_loc, a1, a2)

    per_toep = TG // SG
    y = pl.pallas_call(
        _s5_out_kernel,
        grid=(ntb, L),
        in_specs=[pl.BlockSpec((rows, tw), lambda cb, r, s=s: (0, s * ntb + cb)) for s in range(L)]
        + [pl.BlockSpec((2 * L - 1, None, tw, tw), lambda cb, r: (0, cb, 0, 0)),
           pl.BlockSpec((rows, per_toep * sw), lambda cb, r: (0, cb)),
           pl.BlockSpec((None, per_toep, sw, sw_in), lambda cb, r: (r, cb, 0, 0))],
        out_specs=pl.BlockSpec((rows, tw), lambda cb, r: (0, r * ntb + cb)),
        out_shape=jax.ShapeDtypeStruct((rows, L * W), BF16),
        scratch_shapes=[pltpu.VMEM((L, rows, tw), BF16), pltpu.VMEM((rows, tw), F32)],
        compiler_params=_cparams("parallel", "arbitrary"),
        name="s5_chunk_output",
    )(*([u2] * L), kd, st_in, cd)
    y = y.reshape(M, W)

    tm = min(256, M)
    row = lambda i: (i, 0)
    fixed = lambda i: (0, 0)
    return pl.pallas_call(
        _s5_glu_out_kernel,
        grid=(M // tm,),
        in_specs=[pl.BlockSpec((tm, W), row),
                  pl.BlockSpec((tm, W), row),
                  pl.BlockSpec((1, W), fixed),
                  _resident((W, W), fixed),
                  pl.BlockSpec((1, W), fixed),
                  pl.BlockSpec((tm, W), row),
                  _resident((W, D_MODEL), fixed),
                  pl.BlockSpec((tm, D_MODEL), row),
                  pl.BlockSpec((1, D_MODEL), fixed),
                  pl.BlockSpec((1, D_MODEL), fixed)],
        out_specs=[pl.BlockSpec((tm, D_MODEL), row), pl.BlockSpec((tm, D_MODEL), row)],
        out_shape=[jax.ShapeDtypeStruct((M, D_MODEL), F32),
                   jax.ShapeDtypeStruct((M, D_MODEL), BF16)],
        compiler_params=_cparams("parallel"),
        name="s5_glu_out_ln",
    )(y, u, d_skip.reshape(1, W), w_glu.astype(BF16), b_glu.reshape(1, W), z,
      w_out.astype(BF16), hf, g.reshape(1, D_MODEL), b.reshape(1, D_MODEL))


def kernel(x, positions, ln_g, ln_b, ssd_w_in, ssd_conv_w, ssd_conv_b, ssd_dt_bias, ssd_a_log, ssd_d, ssd_norm_w, ssd_w_out, fox_w_in, fox_f_bias, fox_w_out, mla_w_in, mla_q_norm, mla_kv_norm, mla_w_q_up, mla_w_kv_up, mla_w_out, s5_w_in, s5_lambda_re, s5_lambda_im, s5_log_step, s5_b_re, s5_b_im, s5_c_re, s5_c_im, s5_d, s5_w_glu, s5_b_glu, s5_w_out):
    B, T, D = x.shape
    hf = x.reshape(B * T, D)
    hb = hf
    for i in range(DEPTH):
        j = i // 4
        kind = i % 4
        g, b = ln_g[i], ln_b[i]
        if kind == 0:
            hf, hb = _ssd_layer(hf, hb, ssd_w_in[j], ssd_conv_w[j], ssd_conv_b[j], ssd_dt_bias[j],
                                ssd_a_log[j], ssd_d[j], ssd_norm_w[j], ssd_w_out[j], g, b, B=B, T=T)
        elif kind == 1:
            hf, hb = _fox_layer(hf, hb, fox_w_in[j], fox_f_bias[j], fox_w_out[j], g, b, B=B, T=T)
        elif kind == 2:
            hf, hb = _mla_layer(hf, hb.astype(BF16), positions, mla_w_in[j], mla_q_norm[j],
                                mla_kv_norm[j], mla_w_q_up[j], mla_w_kv_up[j], mla_w_out[j],
                                g, b, B=B, T=T)
        else:
            hf, hb = _s5_layer(hf, hb, s5_w_in[j], s5_lambda_re[j], s5_lambda_im[j], s5_log_step[j],
                               s5_b_re[j], s5_b_im[j], s5_c_re[j], s5_c_im[j], s5_d[j],
                               s5_w_glu[j], s5_b_glu[j], s5_w_out[j], g, b, B=B, T=T)
    return hf.reshape(B, T, D)
```

```python
import functools
import math

import jax
import jax.numpy as jnp
from jax import lax
from jax.experimental import pallas as pl
from jax.experimental.pallas import tpu as pltpu

F32 = jnp.float32
BF16 = jnp.bfloat16
HIGHEST = lax.Precision.HIGHEST

D_MODEL = 2048
DEPTH = 4
ALPHA = (2.0 * DEPTH) ** 0.25
LN_EPS = 1e-5
RMS_EPS = 1e-6

SSD_D_INNER = 4096
SSD_HEADS = 64
SSD_HEADDIM = 64
SSD_GROUPS = 8
SSD_HEADS_PER_GROUP = SSD_HEADS // SSD_GROUPS
SSD_STATE = 128
SSD_CONV = 4
SSD_CHUNK = 128
SSD_GROUP_WIDTH = SSD_D_INNER // SSD_GROUPS
SSD_BC_WIDTH = SSD_GROUPS * SSD_STATE
SSD_GROUPS_PER_STEP = 2

FOX_HEADS = 16
FOX_HEAD_DIM = 128
FOX_WIDTH = FOX_HEADS * FOX_HEAD_DIM

MLA_HEADS = 16
MLA_Q_RANK = 512
MLA_KV_RANK = 512
MLA_NOPE = 128
MLA_ROPE = 64
MLA_V = 128
MLA_QK = MLA_NOPE + MLA_ROPE
MLA_WIDTH = MLA_HEADS * MLA_V
ROPE_BASE = 10000.0

S5_WIDTH = D_MODEL
S5_GROUP = 16
S5_GROUPS = S5_WIDTH // S5_GROUP
S5_STATE = 64
S5_CHUNK = 8
S5_TOEP_GROUPS = 16
S5_STATE_GROUPS = 8
S5_STEP_GROUP = 4

LANES = 128
VMEM_LIMIT_BYTES = 48 * 1024 * 1024


def _cparams(*sem):
    return pltpu.CompilerParams(dimension_semantics=sem, vmem_limit_bytes=VMEM_LIMIT_BYTES)


def _resident(block_shape, index_map):
    return pl.BlockSpec(block_shape, index_map, pipeline_mode=pl.Buffered(1))


def _sigmoid(x):
    return 0.5 * (jnp.tanh(0.5 * x) + 1.0)


def _silu(x):
    h = 0.5 * x
    return h + h * jnp.tanh(h)


def _softplus(x):
    return jnp.maximum(x, 0.0) + jnp.log1p(jnp.exp(-jnp.abs(x)))


def _deepnorm_ln(h, branch, g, b):
    r = ALPHA * h + branch
    mu = jnp.mean(r, axis=-1, keepdims=True)
    d = r - mu
    var = jnp.mean(d * d, axis=-1, keepdims=True)
    return d * lax.rsqrt(var + LN_EPS) * g + b


def _mm_kernel(x_ref, w_ref, o_ref, wb_ref, *, scale):
    @pl.when(pl.program_id(1) == 0)
    def _():
        wb_ref[...] = w_ref[...].astype(BF16)

    acc = jnp.dot(x_ref[...].astype(BF16), wb_ref[...], preferred_element_type=F32)
    if scale is not None:
        acc = acc * scale
    o_ref[...] = acc.astype(o_ref.dtype)


def _matmul(x, w, out_dtype, *, col0=0, ncols=None, scale=None, tn=1024):
    M, K = x.shape
    N = w.shape[1] - col0 if ncols is None else ncols
    tm = min(1024 if x.dtype == BF16 else 512, M)
    tn = min(tn, N)
    assert M % tm == 0 and N % tn == 0 and col0 % tn == 0, (M, N, col0, tm, tn)
    j0 = col0 // tn
    return pl.pallas_call(
        functools.partial(_mm_kernel, scale=scale),
        grid=(N // tn, M // tm),
        in_specs=[pl.BlockSpec((tm, K), lambda j, i: (i, 0)),
                  pl.BlockSpec((K, tn), lambda j, i: (0, j0 + j))],
        out_specs=pl.BlockSpec((tm, tn), lambda j, i: (i, j)),
        out_shape=jax.ShapeDtypeStruct((M, N), out_dtype),
        scratch_shapes=[pltpu.VMEM((K, tn), BF16)],
        compiler_params=_cparams("parallel", "arbitrary"),
        name="proj",
    )(x, w)


def _out_ln_kernel(y_ref, w_ref, h_ref, g_ref, b_ref, of_ref, ob_ref):
    branch = jnp.dot(y_ref[...], w_ref[...], preferred_element_type=F32)
    out = _deepnorm_ln(h_ref[...], branch, g_ref[...], b_ref[...])
    of_ref[...] = out
    ob_ref[...] = out.astype(BF16)


def _out_proj_ln(y, w, h, g, b, *, tm=256):
    M, K = y.shape
    D = w.shape[1]
    tm = min(tm, M)
    assert M % tm == 0
    return pl.pallas_call(
        _out_ln_kernel,
        grid=(M // tm,),
        in_specs=[pl.BlockSpec((tm, K), lambda i: (i, 0)),
                  _resident((K, D), lambda i: (0, 0)),
                  pl.BlockSpec((tm, D), lambda i: (i, 0)),
                  pl.BlockSpec((1, D), lambda i: (0, 0)),
                  pl.BlockSpec((1, D), lambda i: (0, 0))],
        out_specs=[pl.BlockSpec((tm, D), lambda i: (i, 0)),
                   pl.BlockSpec((tm, D), lambda i: (i, 0))],
        out_shape=[jax.ShapeDtypeStruct((M, D), F32), jax.ShapeDtypeStruct((M, D), BF16)],
        compiler_params=_cparams("parallel"),
        name="out_proj_ln",
    )(y, w, h, g.reshape(1, D), b.reshape(1, D))


ATTN_HEADS_PER_STEP = 4


def _attn_kernel(*refs, nparts, tq, tk, dv, head_major):
    q_refs = refs[:nparts]
    k_refs = refs[nparts:2 * nparts]
    v_ref, z_ref, o_ref, m_ref, acc_ref = refs[2 * nparts:]
    nh = ATTN_HEADS_PER_STEP
    i = pl.program_id(2)
    ones_cols = jnp.ones((tk, LANES), BF16)

    def head(ref, hh, rows=slice(None)):
        if head_major:
            return ref[hh, rows, :]
        w = ref.shape[-1] // nh
        return ref[rows, hh * w:(hh + 1) * w]

    def cat(pieces):
        return pieces[0] if len(pieces) == 1 else jnp.concatenate(pieces, axis=1)

    qs = [cat([head(r, hh) for r in q_refs]) for hh in range(nh)]
    m_ref[...] = jnp.full_like(m_ref, -jnp.inf)
    acc_ref[...] = jnp.zeros_like(acc_ref)

    def block(j, diag_offset):
        rows = pl.ds(pl.multiple_of(j * tk, tk), tk)
        scores = []
        for hh in range(nh):
            k = cat([head(r, hh, rows) for r in k_refs])
            scores.append(lax.dot_general(qs[hh], k, (((1,), (1,)), ((), ())),
                                          preferred_element_type=F32))
        probs, alphas = [], []
        for hh in range(nh):
            s = scores[hh]
            if diag_offset is not None:
                r = lax.broadcasted_iota(jnp.int32, (tq, tk), 0)
                c = lax.broadcasted_iota(jnp.int32, (tq, tk), 1)
                s = jnp.where(c + diag_offset <= r, s, -jnp.inf)
            m_prev = m_ref[hh]
            m_new = jnp.maximum(m_prev, jnp.max(s, axis=-1, keepdims=True))
            alpha = jnp.exp(m_prev - m_new)
            p = jnp.exp(s - jnp.concatenate([m_new] * (tk // LANES), axis=1))
            m_ref[hh] = m_new
            probs.append(p.astype(BF16))
            alphas.append(jnp.concatenate([alpha] * ((dv + LANES) // LANES), axis=1))
        for hh in range(nh):
            v_aug = jnp.concatenate([head(v_ref, hh, rows), ones_cols], axis=1)
            acc_ref[hh] = alphas[hh] * acc_ref[hh] + jnp.dot(
                probs[hh], v_aug, preferred_element_type=F32)

    n_full = i * (tq // tk)

    def full_block(j, carry):
        block(j, None)
        return carry

    lax.fori_loop(0, n_full, full_block, 0)
    for d in range(tq // tk):
        block(n_full + d, d * tk)

    for hh in range(nh):
        cols = slice(hh * dv, (hh + 1) * dv)
        o = acc_ref[hh, :, :dv] / acc_ref[hh, :, dv:]
        o_ref[:, cols] = (o * _silu(z_ref[:, cols])).astype(o_ref.dtype)


def _attention(inputs, in_specs, out_spec, *, B, T, H, dv, tq, tk, nparts, head_major):
    nh = ATTN_HEADS_PER_STEP
    assert tq % tk == 0 and T % tq == 0 and H % nh == 0
    return pl.pallas_call(
        functools.partial(_attn_kernel, nparts=nparts, tq=tq, tk=tk, dv=dv,
                          head_major=head_major),
        grid=(B, H // nh, T // tq),
        in_specs=in_specs,
        out_specs=out_spec,
        out_shape=jax.ShapeDtypeStruct((B * T, H * dv), BF16),
        scratch_shapes=[pltpu.VMEM((nh, tq, LANES), F32),
                        pltpu.VMEM((nh, tq, dv + LANES), F32)],
        compiler_params=_cparams("parallel", "parallel", "arbitrary"),
        name="causal_attention",
    )(*inputs)


def _attn_tiles(T):
    tq = min(512, T)
    tk = min(512, T)
    return tq, tk


def _ssd_kernel(x_ref, bm_ref, cm_ref, z_ref, dtr_ref,
                cwx_ref, cwb_ref, cwc_ref, cbx_ref, cbb_ref, cbc_ref,
                dtb_ref, al_ref, dsk_ref, nw_ref, spread_ref,
                o_ref,
                extx_ref, extb_ref, extc_ref, state_ref, yz_ref):
    Q, E, P, GW, N = SSD_CHUNK, SSD_HEADS_PER_GROUP, SSD_HEADDIM, SSD_GROUP_WIDTH, SSD_STATE
    NG = SSD_GROUPS_PER_STEP
    groups = range(NG)
    c = pl.program_id(2)

    @pl.when(c == 0)
    def _():
        extx_ref[pl.ds(0, 8), :] = jnp.zeros((8, NG * GW), F32)
        extb_ref[pl.ds(0, 8), :] = jnp.zeros((8, NG * N), F32)
        extc_ref[pl.ds(0, 8), :] = jnp.zeros((8, NG * N), F32)
        state_ref[...] = jnp.zeros_like(state_ref)

    def conv_silu(raw_ref, ext_ref, w_ref, b_ref, cols=slice(None)):
        ext_ref[pl.ds(8, Q), cols] = raw_ref[:, cols]
        acc = b_ref[:, cols] + w_ref[pl.ds(0, 1), cols] * ext_ref[pl.ds(5, Q), cols]
        for kk in range(1, SSD_CONV):
            acc = acc + w_ref[pl.ds(kk, 1), cols] * ext_ref[pl.ds(5 + kk, Q), cols]
        ext_ref[pl.ds(0, 8), cols] = ext_ref[pl.ds(Q, 8), cols]
        return _silu(acc)

    row = lax.broadcasted_iota(jnp.int32, (Q, Q), 0)
    col = lax.broadcasted_iota(jnp.int32, (Q, Q), 1)
    lower = row >= col
    tri_u = (row <= col).astype(BF16)
    left = col < P

    def pieces(v):
        hi = v.astype(BF16).astype(F32)
        mid = (v - hi).astype(BF16).astype(F32)
        lo = ((v - hi) - mid).astype(BF16).astype(F32)
        return jnp.concatenate([hi, mid, lo], axis=0).astype(BF16)

    npairs = E // 2
    dt_r, acs_r = [], []
    for gg in groups:
        dt = _softplus(dtr_ref[gg] + dtb_ref[gg])
        cum3 = jnp.dot(pieces(dt * (-jnp.exp(al_ref[gg]))), tri_u, preferred_element_type=F32)
        dt_r.append(dt)
        acs_r.append(cum3[:E] + cum3[E:2 * E] + cum3[2 * E:])

    bm_b, cm_b, bmt_b, cb, carried = [], [], [], [], []
    for gg in groups:
        ncols = pl.ds(gg * N, N)
        bm = conv_silu(bm_ref, extb_ref, cwb_ref, cbb_ref, ncols)
        cm = conv_silu(cm_ref, extc_ref, cwc_ref, cbc_ref, ncols)
        bm_b.append(bm.astype(BF16))
        cm_b.append(cm.astype(BF16))
        bmt_b.append(bm.T.astype(BF16))
        cb.append(lax.dot_general(cm_b[gg], bm_b[gg], (((1,), (1,)), ((), ())),
                                  preferred_element_type=F32))
        carried.append([jnp.dot(cm_b[gg], state_ref[gg * npairs + kp].astype(BF16),
                                preferred_element_type=F32)
                        for kp in range(npairs)])

    pair_cols = lambda gg, kp: pl.ds(gg * GW + kp * LANES, LANES)
    xps = [[conv_silu(x_ref, extx_ref, cwx_ref, cbx_ref, pair_cols(gg, kp))
            for kp in range(npairs)] for gg in groups]
    gates = [[_silu(z_ref[:, pair_cols(gg, kp)]) for kp in range(npairs)] for gg in groups]

    spreads = [lax.dot_general(pieces(jnp.concatenate([dt_r[gg], acs_r[gg]], axis=0)),
                               spread_ref[...], (((0,), (0,)), ((), ())),
                               preferred_element_type=F32) for gg in groups]

    for gg in groups:
        dt_ch = spreads[gg][:, :GW]
        acs_ch = spreads[gg][:, GW:2 * GW]
        acs_col = spreads[gg][:, 2 * GW:]
        ssq = jnp.zeros((Q, 1), F32)
        for kp in range(npairs):
            cols = pair_cols(gg, kp)
            xp = xps[gg][kp]
            dt_p = dt_ch[:, kp * LANES:(kp + 1) * LANES]
            acs_p = acs_ch[:, kp * LANES:(kp + 1) * LANES]
            last_p = acs_p[Q - 1:Q, :]
            xdt = xp * dt_p
            xdt_b = xdt.astype(BF16)

            ys = []
            for e in (2 * kp, 2 * kp + 1):
                seg = acs_col[:, e * Q:(e + 1) * Q] - acs_r[gg][e:e + 1, :]
                dec = jnp.exp(jnp.where(lower, seg, -jnp.inf))
                ys.append(jnp.dot((cb[gg] * dec).astype(BF16), xdt_b,
                                  preferred_element_type=F32))
            y_diag = jnp.where(left, ys[0], ys[1])

            xw = (xdt * jnp.exp(last_p - acs_p)).astype(BF16)
            s_loc = jnp.dot(bmt_b[gg], xw, preferred_element_type=F32)
            y_off = carried[gg][kp] * jnp.exp(acs_p)
            slot = gg * npairs + kp
            state_ref[slot] = state_ref[slot] * jnp.exp(last_p) + s_loc

            y = (y_diag + y_off + dsk_ref[:, cols] * xp) * gates[gg][kp]
            yz_ref[:, cols] = y
            ssq = ssq + jnp.sum(y * y, axis=-1, keepdims=True)

        gcols = pl.ds(gg * GW, GW)
        inv = lax.rsqrt(ssq * (1.0 / GW) + RMS_EPS)
        o_ref[:, gcols] = (yz_ref[:, gcols] * inv * nw_ref[:, gcols]).astype(o_ref.dtype)


def _ssd_core(z, xbc, dt, conv_w, conv_b, dt_bias, a_log, d_skip, norm_w, *, B, T):
    G, E, Q = SSD_GROUPS, SSD_HEADS_PER_GROUP, SSD_CHUNK
    NC = T // Q
    GW, N = SSD_GROUP_WIDTH, SSD_STATE
    dtr = jnp.transpose(dt[:, :SSD_HEADS].reshape(B, T, G, E), (0, 2, 3, 1))
    cw = conv_w
    cb = conv_b.reshape(1, -1)
    head_rows = lambda v: v.reshape(G, E, 1)
    d_ch = jnp.repeat(d_skip, SSD_HEADDIM).reshape(1, -1)
    heads = jnp.arange(E)
    per_channel = (jnp.arange(GW)[None, :] // SSD_HEADDIM == heads[:, None]).astype(BF16)
    per_head_block = (jnp.arange(E * Q)[None, :] // Q == heads[:, None]).astype(BF16)
    none = lambda n: jnp.zeros((E, n), BF16)
    spread = jnp.concatenate([
        jnp.concatenate([per_channel, none(GW), none(E * Q)], axis=1),
        jnp.concatenate([none(GW), per_channel, per_head_block], axis=1)], axis=0)
    spread = jnp.tile(spread, (3, 1))

    NG = SSD_GROUPS_PER_STEP
    XW, BW = NG * GW, NG * N
    assert G % NG == 0
    bblk = SSD_D_INNER // BW
    cblk = (SSD_D_INNER + SSD_BC_WIDTH) // BW
    rowblk = lambda b, g, c: b * NC + c
    in_specs = [
        pl.BlockSpec((Q, XW), lambda b, g, c: (rowblk(b, g, c), g)),
        pl.BlockSpec((Q, BW), lambda b, g, c: (rowblk(b, g, c), bblk + g)),
        pl.BlockSpec((Q, BW), lambda b, g, c: (rowblk(b, g, c), cblk + g)),
        pl.BlockSpec((Q, XW), lambda b, g, c: (rowblk(b, g, c), g)),
        pl.BlockSpec((None, NG, E, Q), lambda b, g, c: (b, g, 0, c)),
        pl.BlockSpec((SSD_CONV, XW), lambda b, g, c: (0, g)),
        pl.BlockSpec((SSD_CONV, BW), lambda b, g, c: (0, bblk + g)),
        pl.BlockSpec((SSD_CONV, BW), lambda b, g, c: (0, cblk + g)),
        pl.BlockSpec((1, XW), lambda b, g, c: (0, g)),
        pl.BlockSpec((1, BW), lambda b, g, c: (0, bblk + g)),
        pl.BlockSpec((1, BW), lambda b, g, c: (0, cblk + g)),
        pl.BlockSpec((NG, E, 1), lambda b, g, c: (g, 0, 0)),
        pl.BlockSpec((NG, E, 1), lambda b, g, c: (g, 0, 0)),
        pl.BlockSpec((1, XW), lambda b, g, c: (0, g)),
        pl.BlockSpec((1, XW), lambda b, g, c: (0, g)),
        pl.BlockSpec(spread.shape, lambda b, g, c: (0, 0)),
    ]
    return pl.pallas_call(
        _ssd_kernel,
        grid=(B, G // NG, NC),
        in_specs=in_specs,
        out_specs=pl.BlockSpec((Q, XW), lambda b, g, c: (rowblk(b, g, c), g)),
        out_shape=jax.ShapeDtypeStruct((B * T, SSD_D_INNER), BF16),
        scratch_shapes=[pltpu.VMEM((Q + 8, XW), F32), pltpu.VMEM((Q + 8, BW), F32),
                        pltpu.VMEM((Q + 8, BW), F32),
                        pltpu.VMEM((NG * E // 2, N, LANES), F32), pltpu.VMEM((Q, XW), F32)],
        compiler_params=_cparams("parallel", "parallel", "arbitrary"),
        name="ssd_chunk_scan",
    )(xbc, xbc, xbc, z, dtr, cw, cw, cw, cb, cb, cb,
      head_rows(dt_bias), head_rows(a_log), d_ch, norm_w.reshape(1, -1), spread)


def _pad_cols(w, n):
    return jnp.pad(w, ((0, 0), (0, n - w.shape[1])))


def _ssd_layer(hf, hb, w_in, conv_w, conv_b, dt_bias, a_log, d_skip, norm_w, w_out, g, b, *, B, T):
    di = SSD_D_INNER
    cd = di + 2 * SSD_BC_WIDTH
    z = _matmul(hb, w_in, F32, col0=0, ncols=di)
    xbc = _matmul(hb, w_in, F32, col0=di, ncols=cd)
    dt = _matmul(hb, _pad_cols(w_in[:, di + cd:], LANES), F32)
    y = _ssd_core(z, xbc, dt, conv_w, conv_b, dt_bias, a_log, d_skip, norm_w, B=B, T=T)
    return _out_proj_ln(y, w_out.astype(BF16), hf, g, b)


FOX_BIAS_PIECES = 3


def _fox_bias_kernel(f_ref, b_ref, qx_ref, kx_ref, carry_ref, *, nblk):
    blk, W, P = LANES, FOX_WIDTH, FOX_BIAS_PIECES
    row = lax.broadcasted_iota(jnp.int32, (blk, blk), 0)
    col = lax.broadcasted_iota(jnp.int32, (blk, blk), 1)
    tri = (row >= col).astype(F32)
    bias = b_ref[...]
    r = lax.broadcasted_iota(jnp.int32, (P * blk, W), 0)
    c = lax.broadcasted_iota(jnp.int32, (P * blk, W), 1)
    head, piece = r % blk, r // blk
    is_head = head < FOX_HEADS
    place_q = jnp.logical_and(is_head, c == head * blk + piece).astype(BF16)
    place_k = jnp.logical_and(is_head, c == head * blk + P + piece).astype(BF16)
    cmod = lax.broadcasted_iota(jnp.int32, (1, W), 1) % blk
    ones_q = jnp.logical_and(cmod >= P, cmod < 2 * P).astype(F32)
    ones_k = (cmod < P).astype(F32)

    def body(t, carry):
        rows = pl.ds(pl.multiple_of(t * blk, blk), blk)
        x = f_ref[rows, :] + bias
        logf = jnp.minimum(x, 0.0) - jnp.log1p(jnp.exp(-jnp.abs(x)))
        cum = jnp.dot(tri, logf, precision=HIGHEST, preferred_element_type=F32) + carry
        hi = cum.astype(BF16)
        r1 = cum - hi.astype(F32)
        mid = r1.astype(BF16)
        lo = (r1 - mid.astype(F32)).astype(BF16)
        pieces = jnp.concatenate([hi, mid, lo], axis=1)
        qx_ref[rows, :] = (jnp.dot(pieces, place_q, preferred_element_type=F32)
                           + ones_q).astype(BF16)
        kx_ref[rows, :] = (ones_k - jnp.dot(pieces, place_k, preferred_element_type=F32)
                           ).astype(BF16)
        return cum[blk - 1:blk, :]

    @pl.when(pl.program_id(1) == 0)
    def _():
        carry_ref[...] = jnp.zeros_like(carry_ref)

    carry_ref[...] = lax.fori_loop(0, nblk, body, carry_ref[...])


def _fox_bias_columns(f, f_bias, *, B, T):
    bias = jnp.pad(f_bias, (0, LANES - FOX_HEADS)).reshape(1, LANES)
    tt = min(512, T)
    out = jax.ShapeDtypeStruct((B, T, FOX_WIDTH), BF16)
    spec = pl.BlockSpec((None, tt, FOX_WIDTH), lambda b, t: (b, t, 0))
    qx, kx = pl.pallas_call(
        functools.partial(_fox_bias_kernel, nblk=tt // LANES),
        grid=(B, T // tt),
        in_specs=[pl.BlockSpec((None, tt, LANES), lambda b, t: (b, t, 0)),
                  pl.BlockSpec((1, LANES), lambda b, t: (0, 0))],
        out_specs=[spec, spec],
        out_shape=[out, out],
        scratch_shapes=[pltpu.VMEM((1, LANES), F32)],
        compiler_params=_cparams("parallel", "arbitrary"),
        name="fox_cum_log_forget",
    )(f.reshape(B, T, LANES), bias)
    return qx.reshape(B * T, FOX_WIDTH), kx.reshape(B * T, FOX_WIDTH)


def _fox_layer(hf, hb, w_in, f_bias, w_out, g, b, *, B, T):
    W, H, dh = FOX_WIDTH, FOX_HEADS, FOX_HEAD_DIM
    q = _matmul(hb, w_in, BF16, col0=0, ncols=W, scale=dh ** -0.5)
    kv = _matmul(hb, w_in, BF16, col0=W, ncols=2 * W)
    z = _matmul(hb, w_in, F32, col0=3 * W, ncols=W)
    f = _matmul(hb, _pad_cols(w_in[:, 4 * W:], LANES), F32)
    qx, kx = _fox_bias_columns(f, f_bias, B=B, T=T)

    tq, tk = _attn_tiles(T)
    nq = T // tq
    hw = ATTN_HEADS_PER_STEP * dh
    q_tile = pl.BlockSpec((tq, hw), lambda bb, h, i: (bb * nq + i, h))
    seq_k = pl.BlockSpec((T, hw), lambda bb, h, i: (bb, h))
    seq_v = pl.BlockSpec((T, hw), lambda bb, h, i: (bb, W // hw + h))
    in_specs = [q_tile, q_tile, seq_k, seq_k, seq_v, q_tile]
    y = _attention((q, qx, kv, kx, kv, z), in_specs, q_tile,
                   B=B, T=T, H=H, dv=dh, tq=tq, tk=tk, nparts=2, head_major=False)
    return _out_proj_ln(y, w_out.astype(BF16), hf, g, b)


def _rope_table_kernel(pos_ref, freq_ref, sign_ref, cos_ref, sin_ref):
    ang = pos_ref[...].astype(F32) * freq_ref[...]
    cos_ref[...] = jnp.cos(ang)
    sin_ref[...] = jnp.sin(ang) * sign_ref[...]


def _rope_tables(positions, *, B, T):
    half = MLA_ROPE // 2
    inv_freq = ROPE_BASE ** (-jnp.arange(0, MLA_ROPE, 2, dtype=F32) / MLA_ROPE)
    freq = jnp.tile(inv_freq, LANES // half).reshape(1, LANES)
    sign = jnp.tile(jnp.concatenate([-jnp.ones((half,), F32), jnp.ones((half,), F32)]),
                    LANES // MLA_ROPE).reshape(1, LANES)
    M = B * T
    tm = min(1024, M)
    return pl.pallas_call(
        _rope_table_kernel,
        grid=(M // tm,),
        in_specs=[pl.BlockSpec((tm, 1), lambda i: (i, 0)),
                  pl.BlockSpec((1, LANES), lambda i: (0, 0)),
                  pl.BlockSpec((1, LANES), lambda i: (0, 0))],
        out_specs=[pl.BlockSpec((tm, LANES), lambda i: (i, 0)),
                   pl.BlockSpec((tm, LANES), lambda i: (i, 0))],
        out_shape=[jax.ShapeDtypeStruct((M, LANES), F32), jax.ShapeDtypeStruct((M, LANES), F32)],
        compiler_params=_cparams("parallel"),
        name="rope_tables",
    )(positions.reshape(M, 1), freq, sign)


def _rms_to_bf16(x, w):
    y = x * lax.rsqrt(jnp.mean(x * x, axis=-1, keepdims=True) + RMS_EPS)
    return (y * w).astype(BF16)


def _rope_pair(c2, cos, sin):
    return c2 * cos + pltpu.roll(c2, MLA_ROPE, 1) * sin


def _mla_qkv_kernel(x_ref, wl_ref, qn_ref, kn_ref, wq_ref, wkv_ref, cos_ref, sin_ref,
                    q_ref, k_ref, v_ref, *, scale):
    qr, kr = MLA_Q_RANK, MLA_KV_RANK
    cos, sin = cos_ref[...], sin_ref[...]
    lat = jnp.dot(x_ref[...], wl_ref[...], preferred_element_type=F32)
    qn = _rms_to_bf16(lat[:, :qr], qn_ref[...])
    kn = _rms_to_bf16(lat[:, qr:qr + kr], kn_ref[...])
    k_pe = _rope_pair(lat[:, qr + kr:], cos, sin)[:, :MLA_ROPE].astype(k_ref.dtype)
    for h in range(MLA_HEADS):
        r = jnp.dot(qn, wq_ref[h], preferred_element_type=F32)
        pe = _rope_pair(r[:, MLA_NOPE:], cos, sin)
        q_ref[h, :, :MLA_NOPE] = (r[:, :MLA_NOPE] * scale).astype(q_ref.dtype)
        q_ref[h, :, MLA_NOPE:] = (pe[:, :MLA_ROPE] * scale).astype(q_ref.dtype)
        r = jnp.dot(kn, wkv_ref[h], preferred_element_type=F32)
        k_ref[h, :, :MLA_NOPE] = r[:, :MLA_NOPE].astype(k_ref.dtype)
        k_ref[h, :, MLA_NOPE:] = k_pe
        v_ref[h] = r[:, MLA_NOPE:].astype(v_ref.dtype)


def _swap_halves(w):
    half = w.shape[-1] // 2
    return jnp.concatenate([w[..., half:], w[..., :half]], axis=-1)


def _mla_layer(hf, hb, positions, w_in, q_norm, kv_norm, w_q_up, w_kv_up, w_out, g, b, *, B, T):
    H, M = MLA_HEADS, B * T
    qr, kr = MLA_Q_RANK, MLA_KV_RANK
    lat_end = qr + kr + MLA_ROPE
    w_pe = w_in[:, qr + kr:lat_end]
    w_lat = jnp.concatenate([w_in[:, :qr + kr], w_pe, _swap_halves(w_pe)], axis=1).astype(BF16)
    nlat = w_lat.shape[1]
    z = _matmul(hb, w_in[:, lat_end:], F32)
    cos, sin = _rope_tables(positions, B=B, T=T)

    wq = w_q_up.reshape(qr, H, MLA_QK)
    wq = jnp.concatenate([wq, _swap_halves(wq[..., MLA_NOPE:])], axis=-1)
    wq = jnp.transpose(wq, (1, 0, 2)).astype(BF16)
    wkv = jnp.transpose(w_kv_up.reshape(kr, H, MLA_NOPE + MLA_V), (1, 0, 2)).astype(BF16)

    tm = min(256, M)
    wide = MLA_NOPE + LANES
    row = lambda i: (i, 0)
    q, k, v = pl.pallas_call(
        functools.partial(_mla_qkv_kernel, scale=MLA_QK ** -0.5),
        grid=(M // tm,),
        in_specs=[pl.BlockSpec((tm, D_MODEL), row),
                  _resident((D_MODEL, nlat), lambda i: (0, 0)),
                  pl.BlockSpec((1, qr), lambda i: (0, 0)),
                  pl.BlockSpec((1, kr), lambda i: (0, 0)),
                  _resident((H, qr, wide), lambda i: (0, 0, 0)),
                  _resident((H, kr, MLA_NOPE + MLA_V), lambda i: (0, 0, 0)),
                  pl.BlockSpec((tm, LANES), row),
                  pl.BlockSpec((tm, LANES), row)],
        out_specs=[pl.BlockSpec((H, tm, MLA_QK), lambda i: (0, i, 0)),
                   pl.BlockSpec((H, tm, MLA_QK), lambda i: (0, i, 0)),
                   pl.BlockSpec((H, tm, MLA_V), lambda i: (0, i, 0))],
        out_shape=[jax.ShapeDtypeStruct((H, M, MLA_QK), BF16),
                   jax.ShapeDtypeStruct((H, M, MLA_QK), BF16),
                   jax.ShapeDtypeStruct((H, M, MLA_V), BF16)],
        compiler_params=_cparams("parallel"),
        name="mla_qkv_rope",
    )(hb, w_lat, q_norm.reshape(1, qr), kv_norm.reshape(1, kr), wq, wkv, cos, sin)

    tq, tk = _attn_tiles(T)
    nq = T // tq
    nh = ATTN_HEADS_PER_STEP
    in_specs = [
        pl.BlockSpec((nh, tq, MLA_QK), lambda bb, h, i: (h, bb * nq + i, 0)),
        pl.BlockSpec((nh, T, MLA_QK), lambda bb, h, i: (h, bb, 0)),
        pl.BlockSpec((nh, T, MLA_V), lambda bb, h, i: (h, bb, 0)),
        pl.BlockSpec((tq, nh * MLA_V), lambda bb, h, i: (bb * nq + i, h)),
    ]
    out_spec = pl.BlockSpec((tq, nh * MLA_V), lambda bb, h, i: (bb * nq + i, h))
    y = _attention((q, k, v, z), in_specs, out_spec,
                   B=B, T=T, H=H, dv=MLA_V, tq=tq, tk=tk, nparts=1, head_major=True)
    return _out_proj_ln(y, w_out.astype(BF16), hf, g, b)


def _s5_matrices(lam_re, lam_im, log_step, b_re, b_im, c_re, c_im):
    L = S5_CHUNK
    step = jnp.exp(log_step.astype(F32))[:, None]
    mag = jnp.exp(lam_re * step)
    ar = mag * jnp.cos(lam_im * step)
    ai = mag * jnp.sin(lam_im * step)
    den = lam_re * lam_re + lam_im * lam_im
    fr = ((ar - 1.0) * lam_re + ai * lam_im) / den
    fi = (ai * lam_re - (ar - 1.0) * lam_im) / den
    bbr = fr[..., None] * b_re - fi[..., None] * b_im
    bbi = fr[..., None] * b_im + fi[..., None] * b_re
    pr, pi = [jnp.ones_like(ar)], [jnp.zeros_like(ar)]
    for _ in range(L):
        pr_next = pr[-1] * ar - pi[-1] * ai
        pi_next = pr[-1] * ai + pi[-1] * ar
        pr.append(pr_next)
        pi.append(pi_next)
    pwr = jnp.stack(pr, axis=1)
    pwi = jnp.stack(pi, axis=1)

    G, P, I = S5_GROUPS, S5_STATE, S5_GROUP
    TG, SG = S5_TOEP_GROUPS, S5_STATE_GROUPS
    NTB, NSB = G // TG, G // SG

    c_ji = lambda c: jnp.tile(jnp.transpose(c, (0, 2, 1)), (1, 1, I))
    b_ji = lambda v: jnp.repeat(v, I, axis=2)
    cbr = c_ji(c_re) * b_ji(bbr) - c_ji(c_im) * b_ji(bbi)
    cbi = c_ji(c_re) * b_ji(bbi) + c_ji(c_im) * b_ji(bbr)
    kern = jnp.einsum('gdp,gpq->gdq', jnp.concatenate([pwr[:, :L], -pwi[:, :L]], axis=2),
                      jnp.concatenate([cbr, cbi], axis=1), precision=HIGHEST)
    idx = jnp.arange

    kc = jnp.transpose(kern.reshape(NTB, TG, L, I, I), (2, 0, 1, 3, 4)).reshape(L, NTB, TG * I, I)
    kc = jnp.pad(kc, ((L - 1, 0), (0, 0), (0, 0), (0, 0)))
    repeat_i = (idx(TG * I)[None, :] % I == idx(I)[:, None]).astype(F32)
    same_t = (idx(TG * I)[:, None] // I == idx(TG * I)[None, :] // I)
    kd = jnp.where(same_t, jnp.einsum('dcri,iq->dcrq', kc, repeat_i, precision=HIGHEST), 0.0)

    lanes = lambda v: jnp.tile(v, (1, 1, SG))
    by_step = lambda w: jnp.transpose(
        lanes(w[:, :L][:, ::-1]).reshape(NSB, SG, L, SG * P), (2, 0, 1, 3))[:, :, :, None]
    by_chan = lambda v: lanes(jnp.transpose(v, (0, 2, 1))).reshape(1, NSB, SG, I, SG * P)
    qr, qi = by_step(pwr), by_step(pwi)
    br, bi = by_chan(bbr), by_chan(bbi)
    bd = jnp.concatenate([qr * br - qi * bi, qr * bi + qi * br], axis=-1)
    bd = bd.reshape(L, NSB, SG * I, 2 * SG * P)
    same_b = (idx(SG * I)[:, None] // I == (idx(2 * SG * P)[None, :] % (SG * P)) // P)
    bd = jnp.where(same_b, bd, 0.0)

    chan = lambda c: jnp.transpose(c.reshape(NSB, SG, I, P), (0, 3, 1, 2)).reshape(1, NSB, P, SG * I)
    step_pow = lambda w: jnp.repeat(
        jnp.transpose(w[:, 1:].reshape(NSB, SG, L, P), (2, 0, 3, 1)), I, axis=3)
    cr, ci, wr, wi = chan(c_re), chan(c_im), step_pow(pwr), step_pow(pwi)
    cp = jnp.stack([cr * wr - ci * wi, -(cr * wi + ci * wr)], axis=2)
    cd = jnp.broadcast_to(cp[:, :, :, None], (L, NSB, 2, SG, P, SG * I))
    cd = cd.reshape(L, NSB, 2 * SG * P, SG * I)
    same_c = ((idx(2 * SG * P)[:, None] % (SG * P)) // P == idx(SG * I)[None, :] // I)
    cd = jnp.where(same_c, cd, 0.0)

    alr = pwr[:, L].reshape(NSB, SG * P)
    ali = pwi[:, L].reshape(NSB, SG * P)
    a1 = jnp.concatenate([alr, alr], axis=1).reshape(1, -1)
    a2 = jnp.concatenate([-ali, ali], axis=1).reshape(1, -1)
    return kd.astype(BF16), bd.astype(BF16), cd.astype(BF16), a1, a2


def _chunk_step(u_ref, s):
    return u_ref[pl.ds(s, u_ref.shape[0] // S5_CHUNK, stride=S5_CHUNK), :]


def _s5_local_state_kernel(u_ref, bd_ref, s_ref):
    acc = jnp.dot(_chunk_step(u_ref, 0).astype(BF16), bd_ref[0], preferred_element_type=F32)
    for s in range(1, S5_CHUNK):
        acc = acc + jnp.dot(_chunk_step(u_ref, s).astype(BF16), bd_ref[s],
                            preferred_element_type=F32)
    s_ref[...] = acc


def _s5_scan_kernel(s_ref, a1_ref, a2_ref, o_ref, *, B, nch):
    W = s_ref.shape[1]
    a1 = a1_ref[...]
    a2 = a2_ref[...]
    unit = 2 * S5_STATE_GROUPS * S5_STATE

    def swap_re_im(c):
        parts = []
        for q in range(W // unit):
            lo = q * unit
            parts += [c[:, lo + unit // 2:lo + unit], c[:, lo:lo + unit // 2]]
        return jnp.concatenate(parts, axis=1)

    def body(n, carry):
        new = []
        for bb in range(B):
            row = bb * nch + n
            c = carry[bb]
            o_ref[pl.ds(row, 1), :] = c
            new.append(a1 * c + a2 * swap_re_im(c) + s_ref[pl.ds(row, 1), :])
        return tuple(new)

    lax.fori_loop(0, nch, body, tuple(jnp.zeros((1, W), F32) for _ in range(B)))


def _gelu_tanh(y):
    return 0.5 * y * (1.0 + jnp.tanh(math.sqrt(2.0 / math.pi) * (y + 0.044715 * (y * y * y))))


def _s5_out_kernel(*refs):
    L = S5_CHUNK
    nu = S5_TOEP_GROUPS // S5_STATE_GROUPS
    u_refs, (kd_ref, st_ref, cd_ref, o_ref, ub_ref, acc_ref) = refs[:nu], refs[nu:]
    r = pl.program_id(1)

    @pl.when(r == 0)
    def _():
        for s in range(L):
            for k in range(nu):
                ub_ref[s, :, k * LANES:(k + 1) * LANES] = _chunk_step(u_refs[k], s).astype(BF16)

    def lags(first):
        out = None
        for s in range(first, first + S5_STEP_GROUP):
            d = jnp.dot(ub_ref[s], kd_ref[r - s + (L - 1)], preferred_element_type=F32)
            out = d if out is None else out + d
        return out

    st = st_ref[...].astype(BF16)
    half = st.shape[1] // 2
    carried = jnp.concatenate(
        [jnp.dot(st[:, :half], cd_ref[0], preferred_element_type=F32),
         jnp.dot(st[:, half:], cd_ref[1], preferred_element_type=F32)], axis=1)
    acc_ref[...] = carried + lags(0)
    for first in range(S5_STEP_GROUP, L, S5_STEP_GROUP):
        @pl.when(first <= r)
        def _(first=first):
            acc_ref[...] += lags(first)
    o_ref[...] = acc_ref[...].astype(o_ref.dtype)


def _s5_glu_out_kernel(ys_ref, u_ref, d_ref, wg_ref, bg_ref, z_ref, wo_ref, h_ref, g_ref, b_ref,
                       of_ref, ob_ref, yt_ref):
    L, W = S5_CHUNK, S5_WIDTH
    nslab, tm = yt_ref.shape[0], yt_ref.shape[1]
    for s in range(L):
        for k in range(nslab):
            lo = s * W + k * LANES
            yt_ref[k, pl.ds(s, tm // L, stride=L), :] = ys_ref[:, lo:lo + LANES].astype(F32)
    ys = jnp.concatenate([yt_ref[k] for k in range(nslab)], axis=1)
    y = _gelu_tanh(ys + d_ref[...] * u_ref[...])
    t = jnp.dot(y.astype(BF16), wg_ref[...], preferred_element_type=F32) + bg_ref[...]
    y = (y * _sigmoid(t) * _silu(z_ref[...])).astype(BF16)
    branch = jnp.dot(y, wo_ref[...], preferred_element_type=F32)
    out = _deepnorm_ln(h_ref[...], branch, g_ref[...], b_ref[...])
    of_ref[...] = out
    ob_ref[...] = out.astype(BF16)


def _s5_layer(hf, hb, w_in, lam_re, lam_im, log_step, b_re, b_im, c_re, c_im, d_skip,
              w_glu, b_glu, w_out, g, b, *, B, T):
    M, W, G, L = B * T, S5_WIDTH, S5_GROUPS, S5_CHUNK
    nch = T // L
    rows = B * nch
    u = _matmul(hb, w_in, F32, col0=0, ncols=W)
    z = _matmul(hb, w_in, F32, col0=W, ncols=W)
    kd, bd, cd, a1, a2 = _s5_matrices(lam_re, lam_im, log_step, b_re, b_im, c_re, c_im)

    TG, SG = S5_TOEP_GROUPS, S5_STATE_GROUPS
    ntb, nsb = G // TG, G // SG
    tw, sw_in, sw = TG * S5_GROUP, SG * S5_GROUP, 2 * SG * S5_STATE
    nstate = nsb * sw
    s_loc = pl.pallas_call(
        _s5_local_state_kernel,
        grid=(nsb,),
        in_specs=[pl.BlockSpec((M, sw_in), lambda cb: (0, cb)),
                  pl.BlockSpec((L, None, sw_in, sw), lambda cb: (0, cb, 0, 0))],
        out_specs=pl.BlockSpec((rows, sw), lambda cb: (0, cb)),
        out_shape=jax.ShapeDtypeStruct((rows, nstate), F32),
        compiler_params=_cparams("parallel"),
        name="s5_chunk_state",
    )(u, bd)

    scan_w = 2 * sw
    st_in = pl.pallas_call(
        functools.partial(_s5_scan_kernel, B=B, nch=nch),
        grid=(nstate // scan_w,),
        in_specs=[pl.BlockSpec((rows, scan_w), lambda p: (0, p)),
                  pl.BlockSpec((1, scan_w), lambda p: (0, p)),
                  pl.BlockSpec((1, scan_w), lambda p: (0, p))],
        out_specs=pl.BlockSpec((rows, scan_w), lambda p: (0, p)),
        out_shape=jax.ShapeDtypeStruct((rows, nstate), F32),
        compiler_params=_cparams("parallel"),
        name="s5_state_scan",
    )(s_loc, a1, a2)

    per_toep = TG // SG
    y = pl.pallas_call(
        _s5_out_kernel,
        grid=(ntb, L),
        in_specs=[pl.BlockSpec((M, sw_in), lambda cb, r, k=k: (0, per_toep * cb + k))
                  for k in range(per_toep)]
        + [pl.BlockSpec((2 * L - 1, None, tw, tw), lambda cb, r: (0, cb, 0, 0)),
                  pl.BlockSpec((rows, per_toep * sw), lambda cb, r: (0, cb)),
                  pl.BlockSpec((None, per_toep, sw, sw_in), lambda cb, r: (r, cb, 0, 0))],
        out_specs=pl.BlockSpec((rows, tw), lambda cb, r: (0, r * ntb + cb)),
        out_shape=jax.ShapeDtypeStruct((rows, L * W), BF16),
        scratch_shapes=[pltpu.VMEM((L, rows, tw), BF16), pltpu.VMEM((rows, tw), F32)],
        compiler_params=_cparams("parallel", "arbitrary"),
        name="s5_chunk_output",
    )(*([u] * per_toep), kd, st_in, cd)

    tm = min(256, M)
    row = lambda i: (i, 0)
    fixed = lambda i: (0, 0)
    return pl.pallas_call(
        _s5_glu_out_kernel,
        grid=(M // tm,),
        in_specs=[pl.BlockSpec((tm // L, L * W), row),
                  pl.BlockSpec((tm, W), row),
                  pl.BlockSpec((1, W), fixed),
                  _resident((W, W), fixed),
                  pl.BlockSpec((1, W), fixed),
                  pl.BlockSpec((tm, W), row),
                  _resident((W, D_MODEL), fixed),
                  pl.BlockSpec((tm, D_MODEL), row),
                  pl.BlockSpec((1, D_MODEL), fixed),
                  pl.BlockSpec((1, D_MODEL), fixed)],
        out_specs=[pl.BlockSpec((tm, D_MODEL), row), pl.BlockSpec((tm, D_MODEL), row)],
        out_shape=[jax.ShapeDtypeStruct((M, D_MODEL), F32),
                   jax.ShapeDtypeStruct((M, D_MODEL), BF16)],
        scratch_shapes=[pltpu.VMEM((W // LANES, tm, LANES), F32)],
        compiler_params=_cparams("parallel"),
        name="s5_glu_out_ln",
    )(y, u, d_skip.reshape(1, W), w_glu.astype(BF16), b_glu.reshape(1, W), z,
      w_out.astype(BF16), hf, g.reshape(1, D_MODEL), b.reshape(1, D_MODEL))


def kernel(x, positions, ln_g, ln_b, ssd_w_in, ssd_conv_w, ssd_conv_b, ssd_dt_bias, ssd_a_log, ssd_d, ssd_norm_w, ssd_w_out, fox_w_in, fox_f_bias, fox_w_out, mla_w_in, mla_q_norm, mla_kv_norm, mla_w_q_up, mla_w_kv_up, mla_w_out, s5_w_in, s5_lambda_re, s5_lambda_im, s5_log_step, s5_b_re, s5_b_im, s5_c_re, s5_c_im, s5_d, s5_w_glu, s5_b_glu, s5_w_out):
    B, T, D = x.shape
    hf = x.reshape(B * T, D)
    hb = hf
    for i in range(DEPTH):
        j = i // 4
        kind = i % 4
        g, b = ln_g[i], ln_b[i]
        if kind == 0:
            hf, hb = _ssd_layer(hf, hb, ssd_w_in[j], ssd_conv_w[j], ssd_conv_b[j], ssd_dt_bias[j],
                                ssd_a_log[j], ssd_d[j], ssd_norm_w[j], ssd_w_out[j], g, b, B=B, T=T)
        elif kind == 1:
            hf, hb = _fox_layer(hf, hb, fox_w_in[j], fox_f_bias[j], fox_w_out[j], g, b, B=B, T=T)
        elif kind == 2:
            hf, hb = _mla_layer(hf, hb.astype(BF16), positions, mla_w_in[j], mla_q_norm[j],
                                mla_kv_norm[j], mla_w_q_up[j], mla_w_kv_up[j], mla_w_out[j],
                                g, b, B=B, T=T)
        else:
            hf, hb = _s5_layer(hf, hb, s5_w_in[j], s5_lambda_re[j], s5_lambda_im[j], s5_log_step[j],
                               s5_b_re[j], s5_b_im[j], s5_c_re[j], s5_c_im[j], s5_d[j],
                               s5_w_glu[j], s5_b_glu[j], s5_w_out[j], g, b, B=B, T=T)
    return hf.reshape(B, T, D)
```

```python
import functools
import math

import jax
import jax.numpy as jnp
from jax import lax
from jax.experimental import pallas as pl
from jax.experimental.pallas import tpu as pltpu

F32 = jnp.float32
BF16 = jnp.bfloat16
HIGHEST = lax.Precision.HIGHEST

D_MODEL = 2048
DEPTH = 4
ALPHA = (2.0 * DEPTH) ** 0.25
LN_EPS = 1e-5
RMS_EPS = 1e-6

SSD_D_INNER = 4096
SSD_HEADS = 64
SSD_HEADDIM = 64
SSD_GROUPS = 8
SSD_HEADS_PER_GROUP = SSD_HEADS // SSD_GROUPS
SSD_STATE = 128
SSD_CONV = 4
SSD_CHUNK = 128
SSD_GROUP_WIDTH = SSD_D_INNER // SSD_GROUPS
SSD_BC_WIDTH = SSD_GROUPS * SSD_STATE
SSD_GROUPS_PER_STEP = 2

FOX_HEADS = 16
FOX_HEAD_DIM = 128
FOX_WIDTH = FOX_HEADS * FOX_HEAD_DIM

MLA_HEADS = 16
MLA_Q_RANK = 512
MLA_KV_RANK = 512
MLA_NOPE = 128
MLA_ROPE = 64
MLA_V = 128
MLA_QK = MLA_NOPE + MLA_ROPE
MLA_WIDTH = MLA_HEADS * MLA_V
ROPE_BASE = 10000.0

S5_WIDTH = D_MODEL
S5_GROUP = 16
S5_GROUPS = S5_WIDTH // S5_GROUP
S5_STATE = 64
S5_CHUNK = 8
S5_TOEP_GROUPS = 16
S5_STATE_GROUPS = 8
S5_STEP_GROUP = 4

LANES = 128
VMEM_LIMIT_BYTES = 48 * 1024 * 1024


def _cparams(*sem):
    return pltpu.CompilerParams(dimension_semantics=sem, vmem_limit_bytes=VMEM_LIMIT_BYTES)


def _resident(block_shape, index_map):
    return pl.BlockSpec(block_shape, index_map, pipeline_mode=pl.Buffered(1))


def _sigmoid(x):
    return 0.5 * (jnp.tanh(0.5 * x) + 1.0)


def _silu(x):
    h = 0.5 * x
    return h + h * jnp.tanh(h)


def _softplus(x):
    return jnp.maximum(x, 0.0) + jnp.log1p(jnp.exp(-jnp.abs(x)))


def _deepnorm_ln(h, branch, g, b):
    r = ALPHA * h + branch
    mu = jnp.mean(r, axis=-1, keepdims=True)
    d = r - mu
    var = jnp.mean(d * d, axis=-1, keepdims=True)
    return d * lax.rsqrt(var + LN_EPS) * g + b


def _mm_kernel(x_ref, w_ref, o_ref, wb_ref, *, scale):
    @pl.when(pl.program_id(1) == 0)
    def _():
        wb_ref[...] = w_ref[...].astype(BF16)

    acc = jnp.dot(x_ref[...].astype(BF16), wb_ref[...], preferred_element_type=F32)
    if scale is not None:
        acc = acc * scale
    o_ref[...] = acc.astype(o_ref.dtype)


def _matmul(x, w, out_dtype, *, col0=0, ncols=None, scale=None, tn=1024):
    M, K = x.shape
    N = w.shape[1] - col0 if ncols is None else ncols
    tm = min(1024 if x.dtype == BF16 else 512, M)
    tn = min(tn, N)
    assert M % tm == 0 and N % tn == 0 and col0 % tn == 0, (M, N, col0, tm, tn)
    j0 = col0 // tn
    return pl.pallas_call(
        functools.partial(_mm_kernel, scale=scale),
        grid=(N // tn, M // tm),
        in_specs=[pl.BlockSpec((tm, K), lambda j, i: (i, 0)),
                  pl.BlockSpec((K, tn), lambda j, i: (0, j0 + j))],
        out_specs=pl.BlockSpec((tm, tn), lambda j, i: (i, j)),
        out_shape=jax.ShapeDtypeStruct((M, N), out_dtype),
        scratch_shapes=[pltpu.VMEM((K, tn), BF16)],
        compiler_params=_cparams("parallel", "arbitrary"),
        name="proj",
    )(x, w)


CONV_COL_PARTS = 4


def _mm_conv_silu_kernel(x_ref, w_ref, cw_ref, cb_ref, o_ref, wb_ref, ext_ref, *, taps, tiles_per_seq):
    i = pl.program_id(1)
    tm, tn = o_ref.shape

    @pl.when(i == 0)
    def _():
        wb_ref[...] = w_ref[...].astype(BF16)

    @pl.when(i % tiles_per_seq == 0)
    def _():
        ext_ref[pl.ds(0, 8), :] = jnp.zeros((8, tn), F32)

    xb = x_ref[...].astype(BF16)
    slab = tn // CONV_COL_PARTS
    parts = [pl.ds(p * slab, slab) for p in range(CONV_COL_PARTS)]
    raws = [jnp.dot(xb, wb_ref[:, cols], preferred_element_type=F32) for cols in parts]
    for cols, raw in zip(parts, raws):
        ext_ref[pl.ds(8, tm), cols] = raw
        acc = cb_ref[:, cols] + cw_ref[pl.ds(0, 1), cols] * ext_ref[pl.ds(9 - taps, tm), cols]
        for kk in range(1, taps):
            acc = acc + cw_ref[pl.ds(kk, 1), cols] * ext_ref[pl.ds(9 - taps + kk, tm), cols]
        ext_ref[pl.ds(0, 8), cols] = ext_ref[pl.ds(tm, 8), cols]
        o_ref[:, cols] = _silu(acc)


def _matmul_conv_silu(x, w, conv_w, conv_b, *, col0, ncols, seq_len, tn=1024):
    M, K = x.shape
    taps = conv_w.shape[0]
    tm = min(1024 if x.dtype == BF16 else 512, seq_len)
    assert M % tm == 0 and seq_len % tm == 0 and ncols % tn == 0 and col0 % tn == 0
    j0 = col0 // tn
    return pl.pallas_call(
        functools.partial(_mm_conv_silu_kernel, taps=taps, tiles_per_seq=seq_len // tm),
        grid=(ncols // tn, M // tm),
        in_specs=[pl.BlockSpec((tm, K), lambda j, i: (i, 0)),
                  pl.BlockSpec((K, tn), lambda j, i: (0, j0 + j)),
                  pl.BlockSpec((taps, tn), lambda j, i: (0, j)),
                  pl.BlockSpec((1, tn), lambda j, i: (0, j))],
        out_specs=pl.BlockSpec((tm, tn), lambda j, i: (i, j)),
        out_shape=jax.ShapeDtypeStruct((M, ncols), F32),
        scratch_shapes=[pltpu.VMEM((K, tn), BF16), pltpu.VMEM((tm + 8, tn), F32)],
        compiler_params=_cparams("parallel", "arbitrary"),
        name="proj_conv_silu",
    )(x, w, conv_w, conv_b.reshape(1, -1))


def _out_ln_kernel(y_ref, w_ref, h_ref, g_ref, b_ref, of_ref, ob_ref):
    half = y_ref.shape[0] // 2
    parts = [pl.ds(0, half), pl.ds(half, half)]
    branches = [jnp.dot(y_ref[rows, :], w_ref[...], preferred_element_type=F32) for rows in parts]
    for rows, branch in zip(parts, branches):
        out = _deepnorm_ln(h_ref[rows, :], branch, g_ref[...], b_ref[...])
        of_ref[rows, :] = out
        ob_ref[rows, :] = out.astype(BF16)


def _out_proj_ln(y, w, h, g, b):
    M, K = y.shape
    D = w.shape[1]
    tm = min(512 if K * D * w.dtype.itemsize <= VMEM_LIMIT_BYTES // 4 else 256, M)
    assert M % tm == 0
    return pl.pallas_call(
        _out_ln_kernel,
        grid=(M // tm,),
        in_specs=[pl.BlockSpec((tm, K), lambda i: (i, 0)),
                  _resident((K, D), lambda i: (0, 0)),
                  pl.BlockSpec((tm, D), lambda i: (i, 0)),
                  pl.BlockSpec((1, D), lambda i: (0, 0)),
                  pl.BlockSpec((1, D), lambda i: (0, 0))],
        out_specs=[pl.BlockSpec((tm, D), lambda i: (i, 0)),
                   pl.BlockSpec((tm, D), lambda i: (i, 0))],
        out_shape=[jax.ShapeDtypeStruct((M, D), F32), jax.ShapeDtypeStruct((M, D), BF16)],
        compiler_params=_cparams("parallel"),
        name="out_proj_ln",
    )(y, w, h, g.reshape(1, D), b.reshape(1, D))


ATTN_HEADS_PER_STEP = 4


def _attn_kernel(*refs, nparts, tq, tk, dv, head_major):
    q_refs = refs[:nparts]
    k_refs = refs[nparts:2 * nparts]
    v_ref, z_ref, o_ref, m_ref, acc_ref = refs[2 * nparts:]
    nh = ATTN_HEADS_PER_STEP
    i = pl.program_id(2)
    ones_cols = jnp.ones((tk, LANES), BF16)

    def head(ref, hh, rows=slice(None)):
        if head_major:
            return ref[hh, rows, :]
        w = ref.shape[-1] // nh
        return ref[rows, hh * w:(hh + 1) * w]

    def cat(pieces):
        return pieces[0] if len(pieces) == 1 else jnp.concatenate(pieces, axis=1)

    qs = [cat([head(r, hh) for r in q_refs]) for hh in range(nh)]
    m_ref[...] = jnp.full_like(m_ref, -jnp.inf)
    acc_ref[...] = jnp.zeros_like(acc_ref)

    def block(j, diag_offset):
        rows = pl.ds(pl.multiple_of(j * tk, tk), tk)
        scores = []
        for hh in range(nh):
            k = cat([head(r, hh, rows) for r in k_refs])
            scores.append(lax.dot_general(qs[hh], k, (((1,), (1,)), ((), ())),
                                          preferred_element_type=F32))
        probs, alphas = [], []
        for hh in range(nh):
            s = scores[hh]
            if diag_offset is not None:
                r = lax.broadcasted_iota(jnp.int32, (tq, tk), 0)
                c = lax.broadcasted_iota(jnp.int32, (tq, tk), 1)
                s = jnp.where(c + diag_offset <= r, s, -jnp.inf)
            m_prev = m_ref[hh]
            m_new = jnp.maximum(m_prev, jnp.max(s, axis=-1, keepdims=True))
            alpha = jnp.exp(m_prev - m_new)
            p = jnp.exp(s - jnp.concatenate([m_new] * (tk // LANES), axis=1))
            m_ref[hh] = m_new
            probs.append(p.astype(BF16))
            alphas.append(jnp.concatenate([alpha] * ((dv + LANES) // LANES), axis=1))
        for hh in range(nh):
            v_aug = jnp.concatenate([head(v_ref, hh, rows), ones_cols], axis=1)
            acc_ref[hh] = alphas[hh] * acc_ref[hh] + jnp.dot(
                probs[hh], v_aug, preferred_element_type=F32)

    n_full = i * (tq // tk)

    def full_block(j, carry):
        block(j, None)
        return carry

    lax.fori_loop(0, n_full, full_block, 0)
    for d in range(tq // tk):
        block(n_full + d, d * tk)

    for hh in range(nh):
        cols = slice(hh * dv, (hh + 1) * dv)
        o = acc_ref[hh, :, :dv] / acc_ref[hh, :, dv:]
        o_ref[:, cols] = (o * _silu(z_ref[:, cols])).astype(o_ref.dtype)


def _attention(inputs, in_specs, out_spec, *, B, T, H, dv, tq, tk, nparts, head_major):
    nh = ATTN_HEADS_PER_STEP
    assert tq % tk == 0 and T % tq == 0 and H % nh == 0
    return pl.pallas_call(
        functools.partial(_attn_kernel, nparts=nparts, tq=tq, tk=tk, dv=dv,
                          head_major=head_major),
        grid=(B, H // nh, T // tq),
        in_specs=in_specs,
        out_specs=out_spec,
        out_shape=jax.ShapeDtypeStruct((B * T, H * dv), BF16),
        scratch_shapes=[pltpu.VMEM((nh, tq, LANES), F32),
                        pltpu.VMEM((nh, tq, dv + LANES), F32)],
        compiler_params=_cparams("parallel", "parallel", "arbitrary"),
        name="causal_attention",
    )(*inputs)


def _attn_tiles(T):
    tq = min(512, T)
    tk = min(512, T)
    return tq, tk


def _ssd_kernel(x_ref, bm_ref, cm_ref, z_ref, dtr_ref,
                dtb_ref, al_ref, dsk_ref, nw_ref, spread_ref,
                o_ref,
                state_ref, yz_ref):
    Q, E, P, GW, N = SSD_CHUNK, SSD_HEADS_PER_GROUP, SSD_HEADDIM, SSD_GROUP_WIDTH, SSD_STATE
    NG = SSD_GROUPS_PER_STEP
    groups = range(NG)
    c = pl.program_id(2)

    @pl.when(c == 0)
    def _():
        state_ref[...] = jnp.zeros_like(state_ref)

    row = lax.broadcasted_iota(jnp.int32, (Q, Q), 0)
    col = lax.broadcasted_iota(jnp.int32, (Q, Q), 1)
    lower = row >= col
    tri_u = (row <= col).astype(BF16)
    left = col < P

    def pieces(v):
        hi = v.astype(BF16).astype(F32)
        mid = (v - hi).astype(BF16).astype(F32)
        lo = ((v - hi) - mid).astype(BF16).astype(F32)
        return jnp.concatenate([hi, mid, lo], axis=0).astype(BF16)

    npairs = E // 2
    dt_r, acs_r = [], []
    for gg in groups:
        dt = _softplus(dtr_ref[gg] + dtb_ref[gg])
        cum3 = jnp.dot(pieces(dt * (-jnp.exp(al_ref[gg]))), tri_u, preferred_element_type=F32)
        dt_r.append(dt)
        acs_r.append(cum3[:E] + cum3[E:2 * E] + cum3[2 * E:])

    bm_b, cm_b, bmt_b, cb, carried = [], [], [], [], []
    for gg in groups:
        ncols = pl.ds(gg * N, N)
        bm = bm_ref[:, ncols]
        bm_b.append(bm.astype(BF16))
        cm_b.append(cm_ref[:, ncols].astype(BF16))
        bmt_b.append(bm.T.astype(BF16))
        cb.append(lax.dot_general(cm_b[gg], bm_b[gg], (((1,), (1,)), ((), ())),
                                  preferred_element_type=F32))
        carried.append([jnp.dot(cm_b[gg], state_ref[gg * npairs + kp].astype(BF16),
                                preferred_element_type=F32)
                        for kp in range(npairs)])

    pair_cols = lambda gg, kp: pl.ds(gg * GW + kp * LANES, LANES)
    xps = [[x_ref[:, pair_cols(gg, kp)] for kp in range(npairs)] for gg in groups]
    gates = [[_silu(z_ref[:, pair_cols(gg, kp)]) for kp in range(npairs)] for gg in groups]

    spreads = [lax.dot_general(pieces(jnp.concatenate([dt_r[gg], acs_r[gg]], axis=0)),
                               spread_ref[...], (((0,), (0,)), ((), ())),
                               preferred_element_type=F32) for gg in groups]

    for gg in groups:
        dt_ch = spreads[gg][:, :GW]
        acs_ch = spreads[gg][:, GW:2 * GW]
        acs_col = spreads[gg][:, 2 * GW:]
        ssq = jnp.zeros((Q, 1), F32)
        for kp in range(npairs):
            cols = pair_cols(gg, kp)
            xp = xps[gg][kp]
            dt_p = dt_ch[:, kp * LANES:(kp + 1) * LANES]
            acs_p = acs_ch[:, kp * LANES:(kp + 1) * LANES]
            last_p = acs_p[Q - 1:Q, :]
            xdt = xp * dt_p
            xdt_b = xdt.astype(BF16)

            ys = []
            for e in (2 * kp, 2 * kp + 1):
                seg = acs_col[:, e * Q:(e + 1) * Q] - acs_r[gg][e:e + 1, :]
                dec = jnp.exp(jnp.where(lower, seg, -jnp.inf))
                ys.append(jnp.dot((cb[gg] * dec).astype(BF16), xdt_b,
                                  preferred_element_type=F32))
            y_diag = jnp.where(left, ys[0], ys[1])

            xw = (xdt * jnp.exp(last_p - acs_p)).astype(BF16)
            s_loc = jnp.dot(bmt_b[gg], xw, preferred_element_type=F32)
            y_off = carried[gg][kp] * jnp.exp(acs_p)
            slot = gg * npairs + kp
            state_ref[slot] = state_ref[slot] * jnp.exp(last_p) + s_loc

            y = (y_diag + y_off + dsk_ref[:, cols] * xp) * gates[gg][kp]
            yz_ref[:, cols] = y
            ssq = ssq + jnp.sum(y * y, axis=-1, keepdims=True)

        gcols = pl.ds(gg * GW, GW)
        inv = lax.rsqrt(ssq * (1.0 / GW) + RMS_EPS)
        o_ref[:, gcols] = (yz_ref[:, gcols] * inv * nw_ref[:, gcols]).astype(o_ref.dtype)


def _ssd_core(z, xbc, dt, dt_bias, a_log, d_skip, norm_w, *, B, T):
    G, E, Q = SSD_GROUPS, SSD_HEADS_PER_GROUP, SSD_CHUNK
    NC = T // Q
    GW, N = SSD_GROUP_WIDTH, SSD_STATE
    dtr = jnp.transpose(dt[:, :SSD_HEADS].reshape(B, T, G, E), (0, 2, 3, 1))
    head_rows = lambda v: v.reshape(G, E, 1)
    d_ch = jnp.repeat(d_skip, SSD_HEADDIM).reshape(1, -1)
    heads = jnp.arange(E)
    per_channel = (jnp.arange(GW)[None, :] // SSD_HEADDIM == heads[:, None]).astype(BF16)
    per_head_block = (jnp.arange(E * Q)[None, :] // Q == heads[:, None]).astype(BF16)
    none = lambda n: jnp.zeros((E, n), BF16)
    spread = jnp.concatenate([
        jnp.concatenate([per_channel, none(GW), none(E * Q)], axis=1),
        jnp.concatenate([none(GW), per_channel, per_head_block], axis=1)], axis=0)
    spread = jnp.tile(spread, (3, 1))

    NG = SSD_GROUPS_PER_STEP
    XW, BW = NG * GW, NG * N
    assert G % NG == 0
    bblk = SSD_D_INNER // BW
    cblk = (SSD_D_INNER + SSD_BC_WIDTH) // BW
    rowblk = lambda b, g, c: b * NC + c
    in_specs = [
        pl.BlockSpec((Q, XW), lambda b, g, c: (rowblk(b, g, c), g)),
        pl.BlockSpec((Q, BW), lambda b, g, c: (rowblk(b, g, c), bblk + g)),
        pl.BlockSpec((Q, BW), lambda b, g, c: (rowblk(b, g, c), cblk + g)),
        pl.BlockSpec((Q, XW), lambda b, g, c: (rowblk(b, g, c), g)),
        pl.BlockSpec((None, NG, E, Q), lambda b, g, c: (b, g, 0, c)),
        pl.BlockSpec((NG, E, 1), lambda b, g, c: (g, 0, 0)),
        pl.BlockSpec((NG, E, 1), lambda b, g, c: (g, 0, 0)),
        pl.BlockSpec((1, XW), lambda b, g, c: (0, g)),
        pl.BlockSpec((1, XW), lambda b, g, c: (0, g)),
        pl.BlockSpec(spread.shape, lambda b, g, c: (0, 0)),
    ]
    return pl.pallas_call(
        _ssd_kernel,
        grid=(B, G // NG, NC),
        in_specs=in_specs,
        out_specs=pl.BlockSpec((Q, XW), lambda b, g, c: (rowblk(b, g, c), g)),
        out_shape=jax.ShapeDtypeStruct((B * T, SSD_D_INNER), BF16),
        scratch_shapes=[pltpu.VMEM((NG * E // 2, N, LANES), F32), pltpu.VMEM((Q, XW), F32)],
        compiler_params=_cparams("parallel", "parallel", "arbitrary"),
        name="ssd_chunk_scan",
    )(xbc, xbc, xbc, z, dtr,
      head_rows(dt_bias), head_rows(a_log), d_ch, norm_w.reshape(1, -1), spread)


def _pad_cols(w, n):
    return jnp.pad(w, ((0, 0), (0, n - w.shape[1])))


def _ssd_layer(hf, hb, w_in, conv_w, conv_b, dt_bias, a_log, d_skip, norm_w, w_out, g, b, *, B, T):
    di = SSD_D_INNER
    cd = di + 2 * SSD_BC_WIDTH
    z = _matmul(hb, w_in, F32, col0=0, ncols=di)
    xbc = _matmul_conv_silu(hb, w_in, conv_w, conv_b, col0=di, ncols=cd, seq_len=T)
    dt = _matmul(hb, _pad_cols(w_in[:, di + cd:], LANES), F32)
    y = _ssd_core(z, xbc, dt, dt_bias, a_log, d_skip, norm_w, B=B, T=T)
    return _out_proj_ln(y, w_out.astype(BF16), hf, g, b)


FOX_BIAS_PIECES = 3


def _fox_bias_kernel(f_ref, b_ref, qx_ref, kx_ref, carry_ref, *, nblk):
    blk, W, P = LANES, FOX_WIDTH, FOX_BIAS_PIECES
    row = lax.broadcasted_iota(jnp.int32, (blk, blk), 0)
    col = lax.broadcasted_iota(jnp.int32, (blk, blk), 1)
    tri = (row >= col).astype(F32)
    bias = b_ref[...]
    H = FOX_HEADS
    r = lax.broadcasted_iota(jnp.int32, (blk, W), 0)
    c = lax.broadcasted_iota(jnp.int32, (blk, W), 1)
    head, piece = r % H, r // H
    is_piece = r < P * H
    place_q = jnp.logical_and(is_piece, c == head * blk + piece).astype(BF16)
    place_k = jnp.logical_and(is_piece, c == head * blk + P + piece).astype(BF16)
    is_head = col < H
    cmod = lax.broadcasted_iota(jnp.int32, (1, W), 1) % blk
    ones_q = jnp.logical_and(cmod >= P, cmod < 2 * P).astype(F32)
    ones_k = (cmod < P).astype(F32)

    def body(t, carry):
        rows = pl.ds(pl.multiple_of(t * blk, blk), blk)
        x = f_ref[rows, :] + bias
        logf = jnp.minimum(x, 0.0) - jnp.log1p(jnp.exp(-jnp.abs(x)))
        cum = jnp.dot(tri, logf, precision=HIGHEST, preferred_element_type=F32) + carry
        hi = jnp.where(is_head, cum, 0.0).astype(BF16).astype(F32)
        r1 = jnp.where(is_head, cum, 0.0) - hi
        mid = r1.astype(BF16).astype(F32)
        lo = (r1 - mid).astype(BF16).astype(F32)
        pieces = (hi + pltpu.roll(mid, H, 1) + pltpu.roll(lo, 2 * H, 1)).astype(BF16)
        qx_ref[rows, :] = (jnp.dot(pieces, place_q, preferred_element_type=F32)
                           + ones_q).astype(BF16)
        kx_ref[rows, :] = (ones_k - jnp.dot(pieces, place_k, preferred_element_type=F32)
                           ).astype(BF16)
        return cum[blk - 1:blk, :]

    @pl.when(pl.program_id(1) == 0)
    def _():
        carry_ref[...] = jnp.zeros_like(carry_ref)

    carry_ref[...] = lax.fori_loop(0, nblk, body, carry_ref[...])


def _fox_bias_columns(f, f_bias, *, B, T):
    bias = jnp.pad(f_bias, (0, LANES - FOX_HEADS)).reshape(1, LANES)
    tt = min(512, T)
    out = jax.ShapeDtypeStruct((B, T, FOX_WIDTH), BF16)
    spec = pl.BlockSpec((None, tt, FOX_WIDTH), lambda b, t: (b, t, 0))
    qx, kx = pl.pallas_call(
        functools.partial(_fox_bias_kernel, nblk=tt // LANES),
        grid=(B, T // tt),
        in_specs=[pl.BlockSpec((None, tt, LANES), lambda b, t: (b, t, 0)),
                  pl.BlockSpec((1, LANES), lambda b, t: (0, 0))],
        out_specs=[spec, spec],
        out_shape=[out, out],
        scratch_shapes=[pltpu.VMEM((1, LANES), F32)],
        compiler_params=_cparams("parallel", "arbitrary"),
        name="fox_cum_log_forget",
    )(f.reshape(B, T, LANES), bias)
    return qx.reshape(B * T, FOX_WIDTH), kx.reshape(B * T, FOX_WIDTH)


def _fox_layer(hf, hb, w_in, f_bias, w_out, g, b, *, B, T):
    W, H, dh = FOX_WIDTH, FOX_HEADS, FOX_HEAD_DIM
    q = _matmul(hb, w_in, BF16, col0=0, ncols=W, scale=dh ** -0.5)
    kv = _matmul(hb, w_in, BF16, col0=W, ncols=2 * W)
    z = _matmul(hb, w_in, F32, col0=3 * W, ncols=W)
    f = _matmul(hb, _pad_cols(w_in[:, 4 * W:], LANES), F32)
    qx, kx = _fox_bias_columns(f, f_bias, B=B, T=T)

    tq, tk = _attn_tiles(T)
    nq = T // tq
    hw = ATTN_HEADS_PER_STEP * dh
    q_tile = pl.BlockSpec((tq, hw), lambda bb, h, i: (bb * nq + i, h))
    seq_k = pl.BlockSpec((T, hw), lambda bb, h, i: (bb, h))
    seq_v = pl.BlockSpec((T, hw), lambda bb, h, i: (bb, W // hw + h))
    in_specs = [q_tile, q_tile, seq_k, seq_k, seq_v, q_tile]
    y = _attention((q, qx, kv, kx, kv, z), in_specs, q_tile,
                   B=B, T=T, H=H, dv=dh, tq=tq, tk=tk, nparts=2, head_major=False)
    return _out_proj_ln(y, w_out.astype(BF16), hf, g, b)


def _rope_table_kernel(pos_ref, freq_ref, sign_ref, cos_ref, sin_ref):
    ang = pos_ref[...].astype(F32) * freq_ref[...]
    cos_ref[...] = jnp.cos(ang)
    sin_ref[...] = jnp.sin(ang) * sign_ref[...]


def _rope_tables(positions, *, B, T):
    half = MLA_ROPE // 2
    inv_freq = ROPE_BASE ** (-jnp.arange(0, MLA_ROPE, 2, dtype=F32) / MLA_ROPE)
    freq = jnp.tile(inv_freq, LANES // half).reshape(1, LANES)
    sign = jnp.tile(jnp.concatenate([-jnp.ones((half,), F32), jnp.ones((half,), F32)]),
                    LANES // MLA_ROPE).reshape(1, LANES)
    M = B * T
    tm = min(1024, M)
    return pl.pallas_call(
        _rope_table_kernel,
        grid=(M // tm,),
        in_specs=[pl.BlockSpec((tm, 1), lambda i: (i, 0)),
                  pl.BlockSpec((1, LANES), lambda i: (0, 0)),
                  pl.BlockSpec((1, LANES), lambda i: (0, 0))],
        out_specs=[pl.BlockSpec((tm, LANES), lambda i: (i, 0)),
                   pl.BlockSpec((tm, LANES), lambda i: (i, 0))],
        out_shape=[jax.ShapeDtypeStruct((M, LANES), F32), jax.ShapeDtypeStruct((M, LANES), F32)],
        compiler_params=_cparams("parallel"),
        name="rope_tables",
    )(positions.reshape(M, 1), freq, sign)


def _rms_to_bf16(x, w):
    y = x * lax.rsqrt(jnp.mean(x * x, axis=-1, keepdims=True) + RMS_EPS)
    return (y * w).astype(BF16)


def _rope_pair(c2, cos, sin):
    return c2 * cos + pltpu.roll(c2, MLA_ROPE, 1) * sin


def _mla_qkv_kernel(x_ref, wl_ref, qn_ref, kn_ref, wq_ref, wkv_ref, cos_ref, sin_ref,
                    q_ref, k_ref, v_ref, *, scale):
    qr, kr = MLA_Q_RANK, MLA_KV_RANK
    cos, sin = cos_ref[...], sin_ref[...]
    lat = jnp.dot(x_ref[...], wl_ref[...], preferred_element_type=F32)
    qn = _rms_to_bf16(lat[:, :qr], qn_ref[...])
    kn = _rms_to_bf16(lat[:, qr:qr + kr], kn_ref[...])
    k_pe = _rope_pair(lat[:, qr + kr:], cos, sin)[:, :MLA_ROPE].astype(k_ref.dtype)
    for h in range(MLA_HEADS):
        r = jnp.dot(qn, wq_ref[h], preferred_element_type=F32)
        pe = _rope_pair(r[:, MLA_NOPE:], cos, sin)
        q_ref[h, :, :MLA_NOPE] = (r[:, :MLA_NOPE] * scale).astype(q_ref.dtype)
        q_ref[h, :, MLA_NOPE:] = (pe[:, :MLA_ROPE] * scale).astype(q_ref.dtype)
        r = jnp.dot(kn, wkv_ref[h], preferred_element_type=F32)
        k_ref[h, :, :MLA_NOPE] = r[:, :MLA_NOPE].astype(k_ref.dtype)
        k_ref[h, :, MLA_NOPE:] = k_pe
        v_ref[h] = r[:, MLA_NOPE:].astype(v_ref.dtype)


def _swap_halves(w):
    half = w.shape[-1] // 2
    return jnp.concatenate([w[..., half:], w[..., :half]], axis=-1)


def _mla_layer(hf, hb, positions, w_in, q_norm, kv_norm, w_q_up, w_kv_up, w_out, g, b, *, B, T):
    H, M = MLA_HEADS, B * T
    qr, kr = MLA_Q_RANK, MLA_KV_RANK
    lat_end = qr + kr + MLA_ROPE
    w_pe = w_in[:, qr + kr:lat_end]
    w_lat = jnp.concatenate([w_in[:, :qr + kr], w_pe, _swap_halves(w_pe)], axis=1).astype(BF16)
    nlat = w_lat.shape[1]
    z = _matmul(hb, w_in[:, lat_end:], F32)
    cos, sin = _rope_tables(positions, B=B, T=T)

    wq = w_q_up.reshape(qr, H, MLA_QK)
    wq = jnp.concatenate([wq, _swap_halves(wq[..., MLA_NOPE:])], axis=-1)
    wq = jnp.transpose(wq, (1, 0, 2)).astype(BF16)
    wkv = jnp.transpose(w_kv_up.reshape(kr, H, MLA_NOPE + MLA_V), (1, 0, 2)).astype(BF16)

    tm = min(256, M)
    wide = MLA_NOPE + LANES
    row = lambda i: (i, 0)
    q, k, v = pl.pallas_call(
        functools.partial(_mla_qkv_kernel, scale=MLA_QK ** -0.5),
        grid=(M // tm,),
        in_specs=[pl.BlockSpec((tm, D_MODEL), row),
                  _resident((D_MODEL, nlat), lambda i: (0, 0)),
                  pl.BlockSpec((1, qr), lambda i: (0, 0)),
                  pl.BlockSpec((1, kr), lambda i: (0, 0)),
                  _resident((H, qr, wide), lambda i: (0, 0, 0)),
                  _resident((H, kr, MLA_NOPE + MLA_V), lambda i: (0, 0, 0)),
                  pl.BlockSpec((tm, LANES), row),
                  pl.BlockSpec((tm, LANES), row)],
        out_specs=[pl.BlockSpec((H, tm, MLA_QK), lambda i: (0, i, 0)),
                   pl.BlockSpec((H, tm, MLA_QK), lambda i: (0, i, 0)),
                   pl.BlockSpec((H, tm, MLA_V), lambda i: (0, i, 0))],
        out_shape=[jax.ShapeDtypeStruct((H, M, MLA_QK), BF16),
                   jax.ShapeDtypeStruct((H, M, MLA_QK), BF16),
                   jax.ShapeDtypeStruct((H, M, MLA_V), BF16)],
        compiler_params=_cparams("parallel"),
        name="mla_qkv_rope",
    )(hb, w_lat, q_norm.reshape(1, qr), kv_norm.reshape(1, kr), wq, wkv, cos, sin)

    tq, tk = _attn_tiles(T)
    nq = T // tq
    nh = ATTN_HEADS_PER_STEP
    in_specs = [
        pl.BlockSpec((nh, tq, MLA_QK), lambda bb, h, i: (h, bb * nq + i, 0)),
        pl.BlockSpec((nh, T, MLA_QK), lambda bb, h, i: (h, bb, 0)),
        pl.BlockSpec((nh, T, MLA_V), lambda bb, h, i: (h, bb, 0)),
        pl.BlockSpec((tq, nh * MLA_V), lambda bb, h, i: (bb * nq + i, h)),
    ]
    out_spec = pl.BlockSpec((tq, nh * MLA_V), lambda bb, h, i: (bb * nq + i, h))
    y = _attention((q, k, v, z), in_specs, out_spec,
                   B=B, T=T, H=H, dv=MLA_V, tq=tq, tk=tk, nparts=1, head_major=True)
    return _out_proj_ln(y, w_out.astype(BF16), hf, g, b)


def _s5_matrices(lam_re, lam_im, log_step, b_re, b_im, c_re, c_im):
    L = S5_CHUNK
    step = jnp.exp(log_step.astype(F32))[:, None]
    mag = jnp.exp(lam_re * step)
    ar = mag * jnp.cos(lam_im * step)
    ai = mag * jnp.sin(lam_im * step)
    den = lam_re * lam_re + lam_im * lam_im
    fr = ((ar - 1.0) * lam_re + ai * lam_im) / den
    fi = (ai * lam_re - (ar - 1.0) * lam_im) / den
    bbr = fr[..., None] * b_re - fi[..., None] * b_im
    bbi = fr[..., None] * b_im + fi[..., None] * b_re
    pr, pi = [jnp.ones_like(ar)], [jnp.zeros_like(ar)]
    for _ in range(L):
        pr_next = pr[-1] * ar - pi[-1] * ai
        pi_next = pr[-1] * ai + pi[-1] * ar
        pr.append(pr_next)
        pi.append(pi_next)
    pwr = jnp.stack(pr, axis=1)
    pwi = jnp.stack(pi, axis=1)

    G, P, I = S5_GROUPS, S5_STATE, S5_GROUP
    TG, SG = S5_TOEP_GROUPS, S5_STATE_GROUPS
    NTB, NSB = G // TG, G // SG

    c_ji = lambda c: jnp.tile(jnp.transpose(c, (0, 2, 1)), (1, 1, I))
    b_ji = lambda v: jnp.repeat(v, I, axis=2)
    cbr = c_ji(c_re) * b_ji(bbr) - c_ji(c_im) * b_ji(bbi)
    cbi = c_ji(c_re) * b_ji(bbi) + c_ji(c_im) * b_ji(bbr)
    kern = jnp.einsum('gdp,gpq->gdq', jnp.concatenate([pwr[:, :L], -pwi[:, :L]], axis=2),
                      jnp.concatenate([cbr, cbi], axis=1), precision=HIGHEST)
    idx = jnp.arange

    kc = jnp.transpose(kern.reshape(NTB, TG, L, I, I), (2, 0, 1, 3, 4)).reshape(L, NTB, TG * I, I)
    kc = jnp.pad(kc, ((L - 1, 0), (0, 0), (0, 0), (0, 0)))
    repeat_i = (idx(TG * I)[None, :] % I == idx(I)[:, None]).astype(F32)
    same_t = (idx(TG * I)[:, None] // I == idx(TG * I)[None, :] // I)
    kd = jnp.where(same_t, jnp.einsum('dcri,iq->dcrq', kc, repeat_i, precision=HIGHEST), 0.0)

    lanes = lambda v: jnp.tile(v, (1, 1, SG))
    by_step = lambda w: jnp.transpose(
        lanes(w[:, :L][:, ::-1]).reshape(NSB, SG, L, SG * P), (2, 0, 1, 3))[:, :, :, None]
    by_chan = lambda v: lanes(jnp.transpose(v, (0, 2, 1))).reshape(1, NSB, SG, I, SG * P)
    qr, qi = by_step(pwr), by_step(pwi)
    br, bi = by_chan(bbr), by_chan(bbi)
    bd = jnp.concatenate([qr * br - qi * bi, qr * bi + qi * br], axis=-1)
    bd = bd.reshape(L, NSB, SG * I, 2 * SG * P)
    same_b = (idx(SG * I)[:, None] // I == (idx(2 * SG * P)[None, :] % (SG * P)) // P)
    bd = jnp.where(same_b, bd, 0.0)

    chan = lambda c: jnp.transpose(c.reshape(NSB, SG, I, P), (0, 3, 1, 2)).reshape(1, NSB, P, SG * I)
    step_pow = lambda w: jnp.repeat(
        jnp.transpose(w[:, 1:].reshape(NSB, SG, L, P), (2, 0, 3, 1)), I, axis=3)
    cr, ci, wr, wi = chan(c_re), chan(c_im), step_pow(pwr), step_pow(pwi)
    cp = jnp.stack([cr * wr - ci * wi, -(cr * wi + ci * wr)], axis=2)
    cd = jnp.broadcast_to(cp[:, :, :, None], (L, NSB, 2, SG, P, SG * I))
    cd = cd.reshape(L, NSB, 2 * SG * P, SG * I)
    same_c = ((idx(2 * SG * P)[:, None] % (SG * P)) // P == idx(SG * I)[None, :] // I)
    cd = jnp.where(same_c, cd, 0.0)

    alr = pwr[:, L].reshape(NSB, SG * P)
    ali = pwi[:, L].reshape(NSB, SG * P)
    a1 = jnp.concatenate([alr, alr], axis=1).reshape(1, -1)
    a2 = jnp.concatenate([-ali, ali], axis=1).reshape(1, -1)
    return kd.astype(BF16), bd.astype(BF16), cd.astype(BF16), a1, a2


def _chunk_step(u_ref, s):
    return u_ref[pl.ds(s, u_ref.shape[0] // S5_CHUNK, stride=S5_CHUNK), :]


def _s5_local_state_kernel(u_ref, bd_ref, s_ref):
    acc = jnp.dot(_chunk_step(u_ref, 0).astype(BF16), bd_ref[0], preferred_element_type=F32)
    for s in range(1, S5_CHUNK):
        acc = acc + jnp.dot(_chunk_step(u_ref, s).astype(BF16), bd_ref[s],
                            preferred_element_type=F32)
    s_ref[...] = acc


def _s5_scan_kernel(s_ref, a1_ref, a2_ref, o_ref, *, B, nch):
    W = s_ref.shape[1]
    a1 = a1_ref[...]
    a2 = a2_ref[...]
    unit = 2 * S5_STATE_GROUPS * S5_STATE

    def swap_re_im(c):
        parts = []
        for q in range(W // unit):
            lo = q * unit
            parts += [c[:, lo + unit // 2:lo + unit], c[:, lo:lo + unit // 2]]
        return jnp.concatenate(parts, axis=1)

    def body(n, carry):
        new = []
        for bb in range(B):
            row = bb * nch + n
            c = carry[bb]
            o_ref[pl.ds(row, 1), :] = c
            new.append(a1 * c + a2 * swap_re_im(c) + s_ref[pl.ds(row, 1), :])
        return tuple(new)

    lax.fori_loop(0, nch, body, tuple(jnp.zeros((1, W), F32) for _ in range(B)))


def _gelu_tanh(y):
    return 0.5 * y * (1.0 + jnp.tanh(math.sqrt(2.0 / math.pi) * (y + 0.044715 * (y * y * y))))


def _s5_out_kernel(*refs):
    L = S5_CHUNK
    nu = S5_TOEP_GROUPS // S5_STATE_GROUPS
    u_refs, (kd_ref, st_ref, cd_ref, o_ref, ub_ref, acc_ref) = refs[:nu], refs[nu:]
    r = pl.program_id(1)

    @pl.when(r == 0)
    def _():
        for s in range(L):
            for k in range(nu):
                ub_ref[s, :, k * LANES:(k + 1) * LANES] = _chunk_step(u_refs[k], s).astype(BF16)

    def lags(first):
        out = None
        for s in range(first, first + S5_STEP_GROUP):
            d = jnp.dot(ub_ref[s], kd_ref[r - s + (L - 1)], preferred_element_type=F32)
            out = d if out is None else out + d
        return out

    st = st_ref[...].astype(BF16)
    half = st.shape[1] // 2
    carried = jnp.concatenate(
        [jnp.dot(st[:, :half], cd_ref[0], preferred_element_type=F32),
         jnp.dot(st[:, half:], cd_ref[1], preferred_element_type=F32)], axis=1)
    acc_ref[...] = carried + lags(0)
    for first in range(S5_STEP_GROUP, L, S5_STEP_GROUP):
        @pl.when(first <= r)
        def _(first=first):
            acc_ref[...] += lags(first)
    o_ref[...] = acc_ref[...].astype(o_ref.dtype)


def _s5_glu_out_kernel(ys_ref, u_ref, d_ref, wg_ref, bg_ref, z_ref, wo_ref, h_ref, g_ref, b_ref,
                       of_ref, ob_ref, yt_ref):
    L, W = S5_CHUNK, S5_WIDTH
    nslab, tm = yt_ref.shape[0], yt_ref.shape[1]
    for s in range(L):
        for k in range(nslab):
            lo = s * W + k * LANES
            yt_ref[k, pl.ds(s, tm // L, stride=L), :] = ys_ref[:, lo:lo + LANES].astype(F32)
    half = tm // 2
    parts = [pl.ds(0, half), pl.ds(half, half)]
    ys = [_gelu_tanh(jnp.concatenate([yt_ref[k, rows, :] for k in range(nslab)], axis=1)
                     + d_ref[...] * u_ref[rows, :]) for rows in parts]
    ts = [jnp.dot(y.astype(BF16), wg_ref[...], preferred_element_type=F32) for y in ys]
    gated = [(y * _sigmoid(t + bg_ref[...]) * _silu(z_ref[rows, :])).astype(BF16)
             for y, t, rows in zip(ys, ts, parts)]
    branches = [jnp.dot(y, wo_ref[...], preferred_element_type=F32) for y in gated]
    for rows, branch in zip(parts, branches):
        out = _deepnorm_ln(h_ref[rows, :], branch, g_ref[...], b_ref[...])
        of_ref[rows, :] = out
        ob_ref[rows, :] = out.astype(BF16)


def _s5_layer(hf, hb, w_in, lam_re, lam_im, log_step, b_re, b_im, c_re, c_im, d_skip,
              w_glu, b_glu, w_out, g, b, *, B, T):
    M, W, G, L = B * T, S5_WIDTH, S5_GROUPS, S5_CHUNK
    nch = T // L
    rows = B * nch
    u = _matmul(hb, w_in, F32, col0=0, ncols=W)
    z = _matmul(hb, w_in, F32, col0=W, ncols=W)
    kd, bd, cd, a1, a2 = _s5_matrices(lam_re, lam_im, log_step, b_re, b_im, c_re, c_im)

    TG, SG = S5_TOEP_GROUPS, S5_STATE_GROUPS
    ntb, nsb = G // TG, G // SG
    tw, sw_in, sw = TG * S5_GROUP, SG * S5_GROUP, 2 * SG * S5_STATE
    nstate = nsb * sw
    s_loc = pl.pallas_call(
        _s5_local_state_kernel,
        grid=(nsb,),
        in_specs=[pl.BlockSpec((M, sw_in), lambda cb: (0, cb)),
                  pl.BlockSpec((L, None, sw_in, sw), lambda cb: (0, cb, 0, 0))],
        out_specs=pl.BlockSpec((rows, sw), lambda cb: (0, cb)),
        out_shape=jax.ShapeDtypeStruct((rows, nstate), F32),
        compiler_params=_cparams("parallel"),
        name="s5_chunk_state",
    )(u, bd)

    scan_w = 2 * sw
    st_in = pl.pallas_call(
        functools.partial(_s5_scan_kernel, B=B, nch=nch),
        grid=(nstate // scan_w,),
        in_specs=[pl.BlockSpec((rows, scan_w), lambda p: (0, p)),
                  pl.BlockSpec((1, scan_w), lambda p: (0, p)),
                  pl.BlockSpec((1, scan_w), lambda p: (0, p))],
        out_specs=pl.BlockSpec((rows, scan_w), lambda p: (0, p)),
        out_shape=jax.ShapeDtypeStruct((rows, nstate), F32),
        compiler_params=_cparams("parallel"),
        name="s5_state_scan",
    )(s_loc, a1, a2)

    per_toep = TG // SG
    y = pl.pallas_call(
        _s5_out_kernel,
        grid=(ntb, L),
        in_specs=[pl.BlockSpec((M, sw_in), lambda cb, r, k=k: (0, per_toep * cb + k))
                  for k in range(per_toep)]
        + [pl.BlockSpec((2 * L - 1, None, tw, tw), lambda cb, r: (0, cb, 0, 0)),
                  pl.BlockSpec((rows, per_toep * sw), lambda cb, r: (0, cb)),
                  pl.BlockSpec((None, per_toep, sw, sw_in), lambda cb, r: (r, cb, 0, 0))],
        out_specs=pl.BlockSpec((rows, tw), lambda cb, r: (0, r * ntb + cb)),
        out_shape=jax.ShapeDtypeStruct((rows, L * W), BF16),
        scratch_shapes=[pltpu.VMEM((L, rows, tw), BF16), pltpu.VMEM((rows, tw), F32)],
        compiler_params=_cparams("parallel", "arbitrary"),
        name="s5_chunk_output",
    )(*([u] * per_toep), kd, st_in, cd)

    tm = min(256, M)
    row = lambda i: (i, 0)
    fixed = lambda i: (0, 0)
    return pl.pallas_call(
        _s5_glu_out_kernel,
        grid=(M // tm,),
        in_specs=[pl.BlockSpec((tm // L, L * W), row),
                  pl.BlockSpec((tm, W), row),
                  pl.BlockSpec((1, W), fixed),
                  _resident((W, W), fixed),
                  pl.BlockSpec((1, W), fixed),
                  pl.BlockSpec((tm, W), row),
                  _resident((W, D_MODEL), fixed),
                  pl.BlockSpec((tm, D_MODEL), row),
                  pl.BlockSpec((1, D_MODEL), fixed),
                  pl.BlockSpec((1, D_MODEL), fixed)],
        out_specs=[pl.BlockSpec((tm, D_MODEL), row), pl.BlockSpec((tm, D_MODEL), row)],
        out_shape=[jax.ShapeDtypeStruct((M, D_MODEL), F32),
                   jax.ShapeDtypeStruct((M, D_MODEL), BF16)],
        scratch_shapes=[pltpu.VMEM((W // LANES, tm, LANES), F32)],
        compiler_params=_cparams("parallel"),
        name="s5_glu_out_ln",
    )(y, u, d_skip.reshape(1, W), w_glu.astype(BF16), b_glu.reshape(1, W), z,
      w_out.astype(BF16), hf, g.reshape(1, D_MODEL), b.reshape(1, D_MODEL))


def kernel(x, positions, ln_g, ln_b, ssd_w_in, ssd_conv_w, ssd_conv_b, ssd_dt_bias, ssd_a_log, ssd_d, ssd_norm_w, ssd_w_out, fox_w_in, fox_f_bias, fox_w_out, mla_w_in, mla_q_norm, mla_kv_norm, mla_w_q_up, mla_w_kv_up, mla_w_out, s5_w_in, s5_lambda_re, s5_lambda_im, s5_log_step, s5_b_re, s5_b_im, s5_c_re, s5_c_im, s5_d, s5_w_glu, s5_b_glu, s5_w_out):
    B, T, D = x.shape
    hf = x.reshape(B * T, D)
    hb = hf
    for i in range(DEPTH):
        j = i // 4
        kind = i % 4
        g, b = ln_g[i], ln_b[i]
        if kind == 0:
            hf, hb = _ssd_layer(hf, hb, ssd_w_in[j], ssd_conv_w[j], ssd_conv_b[j], ssd_dt_bias[j],
                                ssd_a_log[j], ssd_d[j], ssd_norm_w[j], ssd_w_out[j], g, b, B=B, T=T)
        elif kind == 1:
            hf, hb = _fox_layer(hf, hb, fox_w_in[j], fox_f_bias[j], fox_w_out[j], g, b, B=B, T=T)
        elif kind == 2:
            hf, hb = _mla_layer(hf, hb.astype(BF16), positions, mla_w_in[j], mla_q_norm[j],
                                mla_kv_norm[j], mla_w_q_up[j], mla_w_kv_up[j], mla_w_out[j],
                                g, b, B=B, T=T)
        else:
            hf, hb = _s5_layer(hf, hb, s5_w_in[j], s5_lambda_re[j], s5_lambda_im[j], s5_log_step[j],
                               s5_b_re[j], s5_b_im[j], s5_c_re[j], s5_c_im[j], s5_d[j],
                               s5_w_glu[j], s5_b_glu[j], s5_w_out[j], g, b, B=B, T=T)
    return hf.reshape(B, T, D)
```

```python
import functools
import math

import jax
import jax.numpy as jnp
from jax import lax
from jax.experimental import pallas as pl
from jax.experimental.pallas import tpu as pltpu

F32 = jnp.float32
BF16 = jnp.bfloat16
HIGHEST = lax.Precision.HIGHEST

D_MODEL = 2048
DEPTH = 4
ALPHA = (2.0 * DEPTH) ** 0.25
LN_EPS = 1e-5
RMS_EPS = 1e-6

SSD_D_INNER = 4096
SSD_HEADS = 64
SSD_HEADDIM = 64
SSD_GROUPS = 8
SSD_HEADS_PER_GROUP = SSD_HEADS // SSD_GROUPS
SSD_STATE = 128
SSD_CONV = 4
SSD_CHUNK = 128
SSD_GROUP_WIDTH = SSD_D_INNER // SSD_GROUPS
SSD_BC_WIDTH = SSD_GROUPS * SSD_STATE
SSD_GROUPS_PER_STEP = 4

FOX_HEADS = 16
FOX_HEAD_DIM = 128
FOX_WIDTH = FOX_HEADS * FOX_HEAD_DIM

MLA_HEADS = 16
MLA_Q_RANK = 512
MLA_KV_RANK = 512
MLA_NOPE = 128
MLA_ROPE = 64
MLA_V = 128
MLA_QK = MLA_NOPE + MLA_ROPE
MLA_WIDTH = MLA_HEADS * MLA_V
ROPE_BASE = 10000.0

S5_WIDTH = D_MODEL
S5_GROUP = 16
S5_GROUPS = S5_WIDTH // S5_GROUP
S5_STATE = 64
S5_CHUNK = 8
S5_TOEP_GROUPS = 16
S5_STATE_GROUPS = 8
S5_STEP_GROUP = 4

LANES = 128
VMEM_LIMIT_BYTES = 48 * 1024 * 1024


def _cparams(*sem):
    return pltpu.CompilerParams(dimension_semantics=sem, vmem_limit_bytes=VMEM_LIMIT_BYTES)


def _resident(block_shape, index_map):
    return pl.BlockSpec(block_shape, index_map, pipeline_mode=pl.Buffered(1))


def _sigmoid(x):
    return 0.5 * (jnp.tanh(0.5 * x) + 1.0)


def _silu(x):
    h = 0.5 * x
    return h + h * jnp.tanh(h)


def _softplus(x):
    return jnp.maximum(x, 0.0) + jnp.log1p(jnp.exp(-jnp.abs(x)))


def _deepnorm_ln(h, branch, g, b):
    r = ALPHA * h + branch
    mu = jnp.mean(r, axis=-1, keepdims=True)
    d = r - mu
    var = jnp.mean(d * d, axis=-1, keepdims=True)
    return d * lax.rsqrt(var + LN_EPS) * g + b


def _mm_kernel(x_ref, w_ref, o_ref, wb_ref, *, scale):
    @pl.when(pl.program_id(1) == 0)
    def _():
        wb_ref[...] = w_ref[...].astype(BF16)

    acc = jnp.dot(x_ref[...].astype(BF16), wb_ref[...], preferred_element_type=F32)
    if scale is not None:
        acc = acc * scale
    o_ref[...] = acc.astype(o_ref.dtype)


def _matmul(x, w, out_dtype, *, col0=0, ncols=None, scale=None, tn=1024):
    M, K = x.shape
    N = w.shape[1] - col0 if ncols is None else ncols
    tm = min(1024 if x.dtype == BF16 else 512, M)
    tn = min(tn, N)
    assert M % tm == 0 and N % tn == 0 and col0 % tn == 0, (M, N, col0, tm, tn)
    j0 = col0 // tn
    return pl.pallas_call(
        functools.partial(_mm_kernel, scale=scale),
        grid=(N // tn, M // tm),
        in_specs=[pl.BlockSpec((tm, K), lambda j, i: (i, 0)),
                  pl.BlockSpec((K, tn), lambda j, i: (0, j0 + j))],
        out_specs=pl.BlockSpec((tm, tn), lambda j, i: (i, j)),
        out_shape=jax.ShapeDtypeStruct((M, N), out_dtype),
        scratch_shapes=[pltpu.VMEM((K, tn), BF16)],
        compiler_params=_cparams("parallel", "arbitrary"),
        name="proj",
    )(x, w)


CONV_COL_PARTS = 4


def _mm_conv_silu_kernel(x_ref, w_ref, cw_ref, cb_ref, o_ref, wb_ref, ext_ref, *, taps, tiles_per_seq):
    i = pl.program_id(1)
    tm, tn = o_ref.shape

    @pl.when(i == 0)
    def _():
        wb_ref[...] = w_ref[...].astype(BF16)

    @pl.when(i % tiles_per_seq == 0)
    def _():
        ext_ref[pl.ds(0, 8), :] = jnp.zeros((8, tn), F32)

    xb = x_ref[...].astype(BF16)
    slab = tn // CONV_COL_PARTS
    parts = [pl.ds(p * slab, slab) for p in range(CONV_COL_PARTS)]
    raws = [jnp.dot(xb, wb_ref[:, cols], preferred_element_type=F32) for cols in parts]
    for cols, raw in zip(parts, raws):
        ext_ref[pl.ds(8, tm), cols] = raw
        acc = cb_ref[:, cols] + cw_ref[pl.ds(0, 1), cols] * ext_ref[pl.ds(9 - taps, tm), cols]
        for kk in range(1, taps):
            acc = acc + cw_ref[pl.ds(kk, 1), cols] * ext_ref[pl.ds(9 - taps + kk, tm), cols]
        ext_ref[pl.ds(0, 8), cols] = ext_ref[pl.ds(tm, 8), cols]
        o_ref[:, cols] = _silu(acc)


def _matmul_conv_silu(x, w, conv_w, conv_b, *, col0, ncols, seq_len, tn=1024):
    M, K = x.shape
    taps = conv_w.shape[0]
    tm = min(1024 if x.dtype == BF16 else 512, seq_len)
    assert M % tm == 0 and seq_len % tm == 0 and ncols % tn == 0 and col0 % tn == 0
    j0 = col0 // tn
    return pl.pallas_call(
        functools.partial(_mm_conv_silu_kernel, taps=taps, tiles_per_seq=seq_len // tm),
        grid=(ncols // tn, M // tm),
        in_specs=[pl.BlockSpec((tm, K), lambda j, i: (i, 0)),
                  pl.BlockSpec((K, tn), lambda j, i: (0, j0 + j)),
                  pl.BlockSpec((taps, tn), lambda j, i: (0, j)),
                  pl.BlockSpec((1, tn), lambda j, i: (0, j))],
        out_specs=pl.BlockSpec((tm, tn), lambda j, i: (i, j)),
        out_shape=jax.ShapeDtypeStruct((M, ncols), F32),
        scratch_shapes=[pltpu.VMEM((K, tn), BF16), pltpu.VMEM((tm + 8, tn), F32)],
        compiler_params=_cparams("parallel", "arbitrary"),
        name="proj_conv_silu",
    )(x, w, conv_w, conv_b.reshape(1, -1))


def _out_ln_kernel(y_ref, w_ref, h_ref, g_ref, b_ref, of_ref, ob_ref):
    half = y_ref.shape[0] // 2
    parts = [pl.ds(0, half), pl.ds(half, half)]
    branches = [jnp.dot(y_ref[rows, :], w_ref[...], preferred_element_type=F32) for rows in parts]
    for rows, branch in zip(parts, branches):
        out = _deepnorm_ln(h_ref[rows, :], branch, g_ref[...], b_ref[...])
        of_ref[rows, :] = out
        ob_ref[rows, :] = out.astype(BF16)


def _out_proj_ln(y, w, h, g, b):
    M, K = y.shape
    D = w.shape[1]
    tm = min(512 if K * D * w.dtype.itemsize <= VMEM_LIMIT_BYTES // 4 else 256, M)
    assert M % tm == 0
    return pl.pallas_call(
        _out_ln_kernel,
        grid=(M // tm,),
        in_specs=[pl.BlockSpec((tm, K), lambda i: (i, 0)),
                  _resident((K, D), lambda i: (0, 0)),
                  pl.BlockSpec((tm, D), lambda i: (i, 0)),
                  pl.BlockSpec((1, D), lambda i: (0, 0)),
                  pl.BlockSpec((1, D), lambda i: (0, 0))],
        out_specs=[pl.BlockSpec((tm, D), lambda i: (i, 0)),
                   pl.BlockSpec((tm, D), lambda i: (i, 0))],
        out_shape=[jax.ShapeDtypeStruct((M, D), F32), jax.ShapeDtypeStruct((M, D), BF16)],
        compiler_params=_cparams("parallel"),
        name="out_proj_ln",
    )(y, w, h, g.reshape(1, D), b.reshape(1, D))


ATTN_HEADS_PER_STEP = 4


def _attn_kernel(*refs, nparts, tq, tk, dv, head_major):
    q_refs = refs[:nparts]
    k_refs = refs[nparts:2 * nparts]
    v_ref, z_ref, o_ref, m_ref, acc_ref = refs[2 * nparts:]
    nh = ATTN_HEADS_PER_STEP
    i = pl.program_id(2)
    ones_cols = jnp.ones((tk, LANES), BF16)

    def head(ref, hh, rows=slice(None)):
        if head_major:
            return ref[hh, rows, :]
        w = ref.shape[-1] // nh
        return ref[rows, hh * w:(hh + 1) * w]

    def cat(pieces):
        return pieces[0] if len(pieces) == 1 else jnp.concatenate(pieces, axis=1)

    qs = [cat([head(r, hh) for r in q_refs]) for hh in range(nh)]
    m_ref[...] = jnp.full_like(m_ref, -jnp.inf)
    acc_ref[...] = jnp.zeros_like(acc_ref)

    def block(j, diag_offset):
        rows = pl.ds(pl.multiple_of(j * tk, tk), tk)
        scores = []
        for hh in range(nh):
            k = cat([head(r, hh, rows) for r in k_refs])
            scores.append(lax.dot_general(qs[hh], k, (((1,), (1,)), ((), ())),
                                          preferred_element_type=F32))
        probs, alphas = [], []
        for hh in range(nh):
            s = scores[hh]
            if diag_offset is not None:
                r = lax.broadcasted_iota(jnp.int32, (tq, tk), 0)
                c = lax.broadcasted_iota(jnp.int32, (tq, tk), 1)
                s = jnp.where(c + diag_offset <= r, s, -jnp.inf)
            m_prev = m_ref[hh]
            m_new = jnp.maximum(m_prev, jnp.max(s, axis=-1, keepdims=True))
            alpha = jnp.exp(m_prev - m_new)
            p = jnp.exp(s - jnp.concatenate([m_new] * (tk // LANES), axis=1))
            m_ref[hh] = m_new
            probs.append(p.astype(BF16))
            alphas.append(jnp.concatenate([alpha] * ((dv + LANES) // LANES), axis=1))
        for hh in range(nh):
            v_aug = jnp.concatenate([head(v_ref, hh, rows), ones_cols], axis=1)
            acc_ref[hh] = alphas[hh] * acc_ref[hh] + jnp.dot(
                probs[hh], v_aug, preferred_element_type=F32)

    n_full = i * (tq // tk)

    def full_block(j, carry):
        block(j, None)
        return carry

    lax.fori_loop(0, n_full, full_block, 0)
    for d in range(tq // tk):
        block(n_full + d, d * tk)

    for hh in range(nh):
        cols = slice(hh * dv, (hh + 1) * dv)
        o = acc_ref[hh, :, :dv] / acc_ref[hh, :, dv:]
        o_ref[:, cols] = (o * _silu(z_ref[:, cols])).astype(o_ref.dtype)


def _attention(inputs, in_specs, out_spec, *, B, T, H, dv, tq, tk, nparts, head_major):
    nh = ATTN_HEADS_PER_STEP
    assert tq % tk == 0 and T % tq == 0 and H % nh == 0
    return pl.pallas_call(
        functools.partial(_attn_kernel, nparts=nparts, tq=tq, tk=tk, dv=dv,
                          head_major=head_major),
        grid=(B, H // nh, T // tq),
        in_specs=in_specs,
        out_specs=out_spec,
        out_shape=jax.ShapeDtypeStruct((B * T, H * dv), BF16),
        scratch_shapes=[pltpu.VMEM((nh, tq, LANES), F32),
                        pltpu.VMEM((nh, tq, dv + LANES), F32)],
        compiler_params=_cparams("parallel", "parallel", "arbitrary"),
        name="causal_attention",
    )(*inputs)


def _attn_tiles(T):
    tq = min(512, T)
    tk = min(512, T)
    return tq, tk


def _ssd_kernel(x_ref, bm_ref, cm_ref, z_ref, dtr_ref,
                dtb_ref, al_ref, dsk_ref, nw_ref, spread_ref,
                o_ref,
                state_ref, yz_ref):
    Q, E, P, GW, N = SSD_CHUNK, SSD_HEADS_PER_GROUP, SSD_HEADDIM, SSD_GROUP_WIDTH, SSD_STATE
    NG = SSD_GROUPS_PER_STEP
    groups = range(NG)
    c = pl.program_id(2)

    @pl.when(c == 0)
    def _():
        state_ref[...] = jnp.zeros_like(state_ref)

    row = lax.broadcasted_iota(jnp.int32, (Q, Q), 0)
    col = lax.broadcasted_iota(jnp.int32, (Q, Q), 1)
    lower = row >= col
    tri_u = (row <= col).astype(BF16)
    left = col < P

    def pieces(v):
        hi = v.astype(BF16).astype(F32)
        mid = (v - hi).astype(BF16).astype(F32)
        lo = ((v - hi) - mid).astype(BF16).astype(F32)
        return jnp.concatenate([hi, mid, lo], axis=0).astype(BF16)

    npairs = E // 2
    dt_r, acs_r = [], []
    for gg in groups:
        dt = _softplus(dtr_ref[gg] + dtb_ref[gg])
        cum3 = jnp.dot(pieces(dt * (-jnp.exp(al_ref[gg]))), tri_u, preferred_element_type=F32)
        dt_r.append(dt)
        acs_r.append(cum3[:E] + cum3[E:2 * E] + cum3[2 * E:])

    bm_b, cm_b, bmt_b, cb, carried = [], [], [], [], []
    for gg in groups:
        ncols = pl.ds(gg * N, N)
        bm = bm_ref[:, ncols]
        bm_b.append(bm.astype(BF16))
        cm_b.append(cm_ref[:, ncols].astype(BF16))
        bmt_b.append(bm.T.astype(BF16))
        cb.append(lax.dot_general(cm_b[gg], bm_b[gg], (((1,), (1,)), ((), ())),
                                  preferred_element_type=F32))
        carried.append([jnp.dot(cm_b[gg], state_ref[gg * npairs + kp].astype(BF16),
                                preferred_element_type=F32)
                        for kp in range(npairs)])

    pair_cols = lambda gg, kp: pl.ds(gg * GW + kp * LANES, LANES)
    xps = [[x_ref[:, pair_cols(gg, kp)] for kp in range(npairs)] for gg in groups]
    gates = [[_silu(z_ref[:, pair_cols(gg, kp)]) for kp in range(npairs)] for gg in groups]

    spreads = [lax.dot_general(pieces(jnp.concatenate([dt_r[gg], acs_r[gg]], axis=0)),
                               spread_ref[...], (((0,), (0,)), ((), ())),
                               preferred_element_type=F32) for gg in groups]

    for gg in groups:
        dt_ch = spreads[gg][:, :GW]
        acs_ch = spreads[gg][:, GW:2 * GW]
        acs_col = spreads[gg][:, 2 * GW:]
        ssq = jnp.zeros((Q, 1), F32)
        for kp in range(npairs):
            cols = pair_cols(gg, kp)
            xp = xps[gg][kp]
            dt_p = dt_ch[:, kp * LANES:(kp + 1) * LANES]
            acs_p = acs_ch[:, kp * LANES:(kp + 1) * LANES]
            last_p = acs_p[Q - 1:Q, :]
            xdt = xp * dt_p
            xdt_b = xdt.astype(BF16)

            ys = []
            for e in (2 * kp, 2 * kp + 1):
                seg = acs_col[:, e * Q:(e + 1) * Q] - acs_r[gg][e:e + 1, :]
                dec = jnp.exp(jnp.where(lower, seg, -jnp.inf))
                ys.append(jnp.dot((cb[gg] * dec).astype(BF16), xdt_b,
                                  preferred_element_type=F32))
            y_diag = jnp.where(left, ys[0], ys[1])

            xw = (xdt * jnp.exp(last_p - acs_p)).astype(BF16)
            s_loc = jnp.dot(bmt_b[gg], xw, preferred_element_type=F32)
            y_off = carried[gg][kp] * jnp.exp(acs_p)
            slot = gg * npairs + kp
            state_ref[slot] = state_ref[slot] * jnp.exp(last_p) + s_loc

            y = (y_diag + y_off + dsk_ref[:, cols] * xp) * gates[gg][kp]
            yz_ref[:, cols] = y
            ssq = ssq + jnp.sum(y * y, axis=-1, keepdims=True)

        gcols = pl.ds(gg * GW, GW)
        inv = lax.rsqrt(ssq * (1.0 / GW) + RMS_EPS)
        o_ref[:, gcols] = (yz_ref[:, gcols] * inv * nw_ref[:, gcols]).astype(o_ref.dtype)


def _ssd_core(z, xbc, dt, dt_bias, a_log, d_skip, norm_w, *, B, T):
    G, E, Q = SSD_GROUPS, SSD_HEADS_PER_GROUP, SSD_CHUNK
    NC = T // Q
    GW, N = SSD_GROUP_WIDTH, SSD_STATE
    dtr = jnp.transpose(dt[:, :SSD_HEADS].reshape(B, T, G, E), (0, 2, 3, 1))
    head_rows = lambda v: v.reshape(G, E, 1)
    d_ch = jnp.repeat(d_skip, SSD_HEADDIM).reshape(1, -1)
    heads = jnp.arange(E)
    per_channel = (jnp.arange(GW)[None, :] // SSD_HEADDIM == heads[:, None]).astype(BF16)
    per_head_block = (jnp.arange(E * Q)[None, :] // Q == heads[:, None]).astype(BF16)
    none = lambda n: jnp.zeros((E, n), BF16)
    spread = jnp.concatenate([
        jnp.concatenate([per_channel, none(GW), none(E * Q)], axis=1),
        jnp.concatenate([none(GW), per_channel, per_head_block], axis=1)], axis=0)
    spread = jnp.tile(spread, (3, 1))

    NG = SSD_GROUPS_PER_STEP
    XW, BW = NG * GW, NG * N
    assert G % NG == 0
    bblk = SSD_D_INNER // BW
    cblk = (SSD_D_INNER + SSD_BC_WIDTH) // BW
    rowblk = lambda b, g, c: b * NC + c
    in_specs = [
        pl.BlockSpec((Q, XW), lambda b, g, c: (rowblk(b, g, c), g)),
        pl.BlockSpec((Q, BW), lambda b, g, c: (rowblk(b, g, c), bblk + g)),
        pl.BlockSpec((Q, BW), lambda b, g, c: (rowblk(b, g, c), cblk + g)),
        pl.BlockSpec((Q, XW), lambda b, g, c: (rowblk(b, g, c), g)),
        pl.BlockSpec((None, NG, E, Q), lambda b, g, c: (b, g, 0, c)),
        pl.BlockSpec((NG, E, 1), lambda b, g, c: (g, 0, 0)),
        pl.BlockSpec((NG, E, 1), lambda b, g, c: (g, 0, 0)),
        pl.BlockSpec((1, XW), lambda b, g, c: (0, g)),
        pl.BlockSpec((1, XW), lambda b, g, c: (0, g)),
        pl.BlockSpec(spread.shape, lambda b, g, c: (0, 0)),
    ]
    return pl.pallas_call(
        _ssd_kernel,
        grid=(B, G // NG, NC),
        in_specs=in_specs,
        out_specs=pl.BlockSpec((Q, XW), lambda b, g, c: (rowblk(b, g, c), g)),
        out_shape=jax.ShapeDtypeStruct((B * T, SSD_D_INNER), BF16),
        scratch_shapes=[pltpu.VMEM((NG * E // 2, N, LANES), F32), pltpu.VMEM((Q, XW), F32)],
        compiler_params=_cparams("parallel", "parallel", "arbitrary"),
        name="ssd_chunk_scan",
    )(xbc, xbc, xbc, z, dtr,
      head_rows(dt_bias), head_rows(a_log), d_ch, norm_w.reshape(1, -1), spread)


def _pad_cols(w, n):
    return jnp.pad(w, ((0, 0), (0, n - w.shape[1])))


def _ssd_layer(hf, hb, w_in, conv_w, conv_b, dt_bias, a_log, d_skip, norm_w, w_out, g, b, *, B, T):
    di = SSD_D_INNER
    cd = di + 2 * SSD_BC_WIDTH
    z = _matmul(hb, w_in, F32, col0=0, ncols=di)
    xbc = _matmul_conv_silu(hb, w_in, conv_w, conv_b, col0=di, ncols=cd, seq_len=T)
    dt = _matmul(hb, _pad_cols(w_in[:, di + cd:], LANES), F32)
    y = _ssd_core(z, xbc, dt, dt_bias, a_log, d_skip, norm_w, B=B, T=T)
    return _out_proj_ln(y, w_out.astype(BF16), hf, g, b)


FOX_BIAS_PIECES = 3


def _fox_bias_kernel(f_ref, b_ref, qx_ref, kx_ref, carry_ref, *, nblk):
    blk, W, P = LANES, FOX_WIDTH, FOX_BIAS_PIECES
    row = lax.broadcasted_iota(jnp.int32, (blk, blk), 0)
    col = lax.broadcasted_iota(jnp.int32, (blk, blk), 1)
    tri = (row >= col).astype(F32)
    bias = b_ref[...]
    H = FOX_HEADS
    r = lax.broadcasted_iota(jnp.int32, (blk, W), 0)
    c = lax.broadcasted_iota(jnp.int32, (blk, W), 1)
    head, piece = r % H, r // H
    is_piece = r < P * H
    place_q = jnp.logical_and(is_piece, c == head * blk + piece).astype(BF16)
    place_k = jnp.logical_and(is_piece, c == head * blk + P + piece).astype(BF16)
    is_head = col < H
    cmod = lax.broadcasted_iota(jnp.int32, (1, W), 1) % blk
    ones_q = jnp.logical_and(cmod >= P, cmod < 2 * P).astype(F32)
    ones_k = (cmod < P).astype(F32)

    def body(t, carry):
        rows = pl.ds(pl.multiple_of(t * blk, blk), blk)
        x = f_ref[rows, :] + bias
        logf = jnp.minimum(x, 0.0) - jnp.log1p(jnp.exp(-jnp.abs(x)))
        cum = jnp.dot(tri, logf, precision=HIGHEST, preferred_element_type=F32) + carry
        hi = jnp.where(is_head, cum, 0.0).astype(BF16).astype(F32)
        r1 = jnp.where(is_head, cum, 0.0) - hi
        mid = r1.astype(BF16).astype(F32)
        lo = (r1 - mid).astype(BF16).astype(F32)
        pieces = (hi + pltpu.roll(mid, H, 1) + pltpu.roll(lo, 2 * H, 1)).astype(BF16)
        qx_ref[rows, :] = (jnp.dot(pieces, place_q, preferred_element_type=F32)
                           + ones_q).astype(BF16)
        kx_ref[rows, :] = (ones_k - jnp.dot(pieces, place_k, preferred_element_type=F32)
                           ).astype(BF16)
        return cum[blk - 1:blk, :]

    @pl.when(pl.program_id(1) == 0)
    def _():
        carry_ref[...] = jnp.zeros_like(carry_ref)

    carry_ref[...] = lax.fori_loop(0, nblk, body, carry_ref[...])


def _fox_bias_columns(f, f_bias, *, B, T):
    bias = jnp.pad(f_bias, (0, LANES - FOX_HEADS)).reshape(1, LANES)
    tt = min(512, T)
    out = jax.ShapeDtypeStruct((B, T, FOX_WIDTH), BF16)
    spec = pl.BlockSpec((None, tt, FOX_WIDTH), lambda b, t: (b, t, 0))
    qx, kx = pl.pallas_call(
        functools.partial(_fox_bias_kernel, nblk=tt // LANES),
        grid=(B, T // tt),
        in_specs=[pl.BlockSpec((None, tt, LANES), lambda b, t: (b, t, 0)),
                  pl.BlockSpec((1, LANES), lambda b, t: (0, 0))],
        out_specs=[spec, spec],
        out_shape=[out, out],
        scratch_shapes=[pltpu.VMEM((1, LANES), F32)],
        compiler_params=_cparams("parallel", "arbitrary"),
        name="fox_cum_log_forget",
    )(f.reshape(B, T, LANES), bias)
    return qx.reshape(B * T, FOX_WIDTH), kx.reshape(B * T, FOX_WIDTH)


def _fox_layer(hf, hb, w_in, f_bias, w_out, g, b, *, B, T):
    W, H, dh = FOX_WIDTH, FOX_HEADS, FOX_HEAD_DIM
    q = _matmul(hb, w_in, BF16, col0=0, ncols=W, scale=dh ** -0.5)
    kv = _matmul(hb, w_in, BF16, col0=W, ncols=2 * W)
    z = _matmul(hb, w_in, F32, col0=3 * W, ncols=W)
    f = _matmul(hb, _pad_cols(w_in[:, 4 * W:], LANES), F32)
    qx, kx = _fox_bias_columns(f, f_bias, B=B, T=T)

    tq, tk = _attn_tiles(T)
    nq = T // tq
    hw = ATTN_HEADS_PER_STEP * dh
    q_tile = pl.BlockSpec((tq, hw), lambda bb, h, i: (bb * nq + i, h))
    seq_k = pl.BlockSpec((T, hw), lambda bb, h, i: (bb, h))
    seq_v = pl.BlockSpec((T, hw), lambda bb, h, i: (bb, W // hw + h))
    in_specs = [q_tile, q_tile, seq_k, seq_k, seq_v, q_tile]
    y = _attention((q, qx, kv, kx, kv, z), in_specs, q_tile,
                   B=B, T=T, H=H, dv=dh, tq=tq, tk=tk, nparts=2, head_major=False)
    return _out_proj_ln(y, w_out.astype(BF16), hf, g, b)


def _rope_table_kernel(pos_ref, freq_ref, sign_ref, cos_ref, sin_ref):
    ang = pos_ref[...].astype(F32) * freq_ref[...]
    cos_ref[...] = jnp.cos(ang)
    sin_ref[...] = jnp.sin(ang) * sign_ref[...]


def _rope_tables(positions, *, B, T):
    half = MLA_ROPE // 2
    inv_freq = ROPE_BASE ** (-jnp.arange(0, MLA_ROPE, 2, dtype=F32) / MLA_ROPE)
    freq = jnp.tile(inv_freq, LANES // half).reshape(1, LANES)
    sign = jnp.tile(jnp.concatenate([-jnp.ones((half,), F32), jnp.ones((half,), F32)]),
                    LANES // MLA_ROPE).reshape(1, LANES)
    M = B * T
    tm = min(1024, M)
    return pl.pallas_call(
        _rope_table_kernel,
        grid=(M // tm,),
        in_specs=[pl.BlockSpec((tm, 1), lambda i: (i, 0)),
                  pl.BlockSpec((1, LANES), lambda i: (0, 0)),
                  pl.BlockSpec((1, LANES), lambda i: (0, 0))],
        out_specs=[pl.BlockSpec((tm, LANES), lambda i: (i, 0)),
                   pl.BlockSpec((tm, LANES), lambda i: (i, 0))],
        out_shape=[jax.ShapeDtypeStruct((M, LANES), F32), jax.ShapeDtypeStruct((M, LANES), F32)],
        compiler_params=_cparams("parallel"),
        name="rope_tables",
    )(positions.reshape(M, 1), freq, sign)


def _rms_to_bf16(x, w):
    y = x * lax.rsqrt(jnp.mean(x * x, axis=-1, keepdims=True) + RMS_EPS)
    return (y * w).astype(BF16)


def _rope_pair(c2, cos, sin):
    return c2 * cos + pltpu.roll(c2, MLA_ROPE, 1) * sin


def _mla_qkv_kernel(x_ref, wl_ref, qn_ref, kn_ref, wq_ref, wkv_ref, cos_ref, sin_ref,
                    q_ref, k_ref, v_ref, *, scale):
    qr, kr = MLA_Q_RANK, MLA_KV_RANK
    cos, sin = cos_ref[...], sin_ref[...]
    lat = jnp.dot(x_ref[...], wl_ref[...], preferred_element_type=F32)
    qn = _rms_to_bf16(lat[:, :qr], qn_ref[...])
    kn = _rms_to_bf16(lat[:, qr:qr + kr], kn_ref[...])
    k_pe = _rope_pair(lat[:, qr + kr:], cos, sin)[:, :MLA_ROPE].astype(k_ref.dtype)
    for h in range(MLA_HEADS):
        r = jnp.dot(qn, wq_ref[h], preferred_element_type=F32)
        pe = _rope_pair(r[:, MLA_NOPE:], cos, sin)
        q_ref[h, :, :MLA_NOPE] = (r[:, :MLA_NOPE] * scale).astype(q_ref.dtype)
        q_ref[h, :, MLA_NOPE:] = (pe[:, :MLA_ROPE] * scale).astype(q_ref.dtype)
        r = jnp.dot(kn, wkv_ref[h], preferred_element_type=F32)
        k_ref[h, :, :MLA_NOPE] = r[:, :MLA_NOPE].astype(k_ref.dtype)
        k_ref[h, :, MLA_NOPE:] = k_pe
        v_ref[h] = r[:, MLA_NOPE:].astype(v_ref.dtype)


def _swap_halves(w):
    half = w.shape[-1] // 2
    return jnp.concatenate([w[..., half:], w[..., :half]], axis=-1)


def _mla_layer(hf, hb, positions, w_in, q_norm, kv_norm, w_q_up, w_kv_up, w_out, g, b, *, B, T):
    H, M = MLA_HEADS, B * T
    qr, kr = MLA_Q_RANK, MLA_KV_RANK
    lat_end = qr + kr + MLA_ROPE
    w_pe = w_in[:, qr + kr:lat_end]
    w_lat = jnp.concatenate([w_in[:, :qr + kr], w_pe, _swap_halves(w_pe)], axis=1).astype(BF16)
    nlat = w_lat.shape[1]
    z = _matmul(hb, w_in[:, lat_end:], F32)
    cos, sin = _rope_tables(positions, B=B, T=T)

    wq = w_q_up.reshape(qr, H, MLA_QK)
    wq = jnp.concatenate([wq, _swap_halves(wq[..., MLA_NOPE:])], axis=-1)
    wq = jnp.transpose(wq, (1, 0, 2)).astype(BF16)
    wkv = jnp.transpose(w_kv_up.reshape(kr, H, MLA_NOPE + MLA_V), (1, 0, 2)).astype(BF16)

    tm = min(256, M)
    wide = MLA_NOPE + LANES
    row = lambda i: (i, 0)
    q, k, v = pl.pallas_call(
        functools.partial(_mla_qkv_kernel, scale=MLA_QK ** -0.5),
        grid=(M // tm,),
        in_specs=[pl.BlockSpec((tm, D_MODEL), row),
                  _resident((D_MODEL, nlat), lambda i: (0, 0)),
                  pl.BlockSpec((1, qr), lambda i: (0, 0)),
                  pl.BlockSpec((1, kr), lambda i: (0, 0)),
                  _resident((H, qr, wide), lambda i: (0, 0, 0)),
                  _resident((H, kr, MLA_NOPE + MLA_V), lambda i: (0, 0, 0)),
                  pl.BlockSpec((tm, LANES), row),
                  pl.BlockSpec((tm, LANES), row)],
        out_specs=[pl.BlockSpec((H, tm, MLA_QK), lambda i: (0, i, 0)),
                   pl.BlockSpec((H, tm, MLA_QK), lambda i: (0, i, 0)),
                   pl.BlockSpec((H, tm, MLA_V), lambda i: (0, i, 0))],
        out_shape=[jax.ShapeDtypeStruct((H, M, MLA_QK), BF16),
                   jax.ShapeDtypeStruct((H, M, MLA_QK), BF16),
                   jax.ShapeDtypeStruct((H, M, MLA_V), BF16)],
        compiler_params=_cparams("parallel"),
        name="mla_qkv_rope",
    )(hb, w_lat, q_norm.reshape(1, qr), kv_norm.reshape(1, kr), wq, wkv, cos, sin)

    tq, tk = _attn_tiles(T)
    nq = T // tq
    nh = ATTN_HEADS_PER_STEP
    in_specs = [
        pl.BlockSpec((nh, tq, MLA_QK), lambda bb, h, i: (h, bb * nq + i, 0)),
        pl.BlockSpec((nh, T, MLA_QK), lambda bb, h, i: (h, bb, 0)),
        pl.BlockSpec((nh, T, MLA_V), lambda bb, h, i: (h, bb, 0)),
        pl.BlockSpec((tq, nh * MLA_V), lambda bb, h, i: (bb * nq + i, h)),
    ]
    out_spec = pl.BlockSpec((tq, nh * MLA_V), lambda bb, h, i: (bb * nq + i, h))
    y = _attention((q, k, v, z), in_specs, out_spec,
                   B=B, T=T, H=H, dv=MLA_V, tq=tq, tk=tk, nparts=1, head_major=True)
    return _out_proj_ln(y, w_out.astype(BF16), hf, g, b)


def _s5_matrices(lam_re, lam_im, log_step, b_re, b_im, c_re, c_im):
    L = S5_CHUNK
    step = jnp.exp(log_step.astype(F32))[:, None]
    mag = jnp.exp(lam_re * step)
    ar = mag * jnp.cos(lam_im * step)
    ai = mag * jnp.sin(lam_im * step)
    den = lam_re * lam_re + lam_im * lam_im
    fr = ((ar - 1.0) * lam_re + ai * lam_im) / den
    fi = (ai * lam_re - (ar - 1.0) * lam_im) / den
    bbr = fr[..., None] * b_re - fi[..., None] * b_im
    bbi = fr[..., None] * b_im + fi[..., None] * b_re
    pr, pi = [jnp.ones_like(ar)], [jnp.zeros_like(ar)]
    for _ in range(L):
        pr_next = pr[-1] * ar - pi[-1] * ai
        pi_next = pr[-1] * ai + pi[-1] * ar
        pr.append(pr_next)
        pi.append(pi_next)
    pwr = jnp.stack(pr, axis=1)
    pwi = jnp.stack(pi, axis=1)

    G, P, I = S5_GROUPS, S5_STATE, S5_GROUP
    TG, SG = S5_TOEP_GROUPS, S5_STATE_GROUPS
    NTB, NSB = G // TG, G // SG

    c_ji = lambda c: jnp.tile(jnp.transpose(c, (0, 2, 1)), (1, 1, I))
    b_ji = lambda v: jnp.repeat(v, I, axis=2)
    cbr = c_ji(c_re) * b_ji(bbr) - c_ji(c_im) * b_ji(bbi)
    cbi = c_ji(c_re) * b_ji(bbi) + c_ji(c_im) * b_ji(bbr)
    kern = jnp.einsum('gdp,gpq->gdq', jnp.concatenate([pwr[:, :L], -pwi[:, :L]], axis=2),
                      jnp.concatenate([cbr, cbi], axis=1), precision=HIGHEST)
    idx = jnp.arange

    kc = jnp.transpose(kern.reshape(NTB, TG, L, I, I), (2, 0, 1, 3, 4)).reshape(L, NTB, TG * I, I)
    kc = jnp.pad(kc, ((L - 1, 0), (0, 0), (0, 0), (0, 0)))
    repeat_i = (idx(TG * I)[None, :] % I == idx(I)[:, None]).astype(F32)
    same_t = (idx(TG * I)[:, None] // I == idx(TG * I)[None, :] // I)
    kd = jnp.where(same_t, jnp.einsum('dcri,iq->dcrq', kc, repeat_i, precision=HIGHEST), 0.0)

    lanes = lambda v: jnp.tile(v, (1, 1, SG))
    by_step = lambda w: jnp.transpose(
        lanes(w[:, :L][:, ::-1]).reshape(NSB, SG, L, SG * P), (2, 0, 1, 3))[:, :, :, None]
    by_chan = lambda v: lanes(jnp.transpose(v, (0, 2, 1))).reshape(1, NSB, SG, I, SG * P)
    qr, qi = by_step(pwr), by_step(pwi)
    br, bi = by_chan(bbr), by_chan(bbi)
    bd = jnp.concatenate([qr * br - qi * bi, qr * bi + qi * br], axis=-1)
    bd = bd.reshape(L, NSB, SG * I, 2 * SG * P)
    same_b = (idx(SG * I)[:, None] // I == (idx(2 * SG * P)[None, :] % (SG * P)) // P)
    bd = jnp.where(same_b, bd, 0.0)

    chan = lambda c: jnp.transpose(c.reshape(NSB, SG, I, P), (0, 3, 1, 2)).reshape(1, NSB, P, SG * I)
    step_pow = lambda w: jnp.repeat(
        jnp.transpose(w[:, 1:].reshape(NSB, SG, L, P), (2, 0, 3, 1)), I, axis=3)
    cr, ci, wr, wi = chan(c_re), chan(c_im), step_pow(pwr), step_pow(pwi)
    cp = jnp.stack([cr * wr - ci * wi, -(cr * wi + ci * wr)], axis=2)
    cd = jnp.broadcast_to(cp[:, :, :, None], (L, NSB, 2, SG, P, SG * I))
    cd = cd.reshape(L, NSB, 2 * SG * P, SG * I)
    same_c = ((idx(2 * SG * P)[:, None] % (SG * P)) // P == idx(SG * I)[None, :] // I)
    cd = jnp.where(same_c, cd, 0.0)

    alr = pwr[:, L].reshape(NSB, SG * P)
    ali = pwi[:, L].reshape(NSB, SG * P)
    a1 = jnp.concatenate([alr, alr], axis=1).reshape(1, -1)
    a2 = jnp.concatenate([-ali, ali], axis=1).reshape(1, -1)
    return kd.astype(BF16), bd.astype(BF16), cd.astype(BF16), a1, a2


def _chunk_step(u_ref, s):
    return u_ref[pl.ds(s, u_ref.shape[0] // S5_CHUNK, stride=S5_CHUNK), :]


def _s5_local_state_kernel(u_ref, bd_ref, s_ref):
    acc = jnp.dot(_chunk_step(u_ref, 0).astype(BF16), bd_ref[0], preferred_element_type=F32)
    for s in range(1, S5_CHUNK):
        acc = acc + jnp.dot(_chunk_step(u_ref, s).astype(BF16), bd_ref[s],
                            preferred_element_type=F32)
    s_ref[...] = acc


def _s5_scan_kernel(s_ref, a1_ref, a2_ref, o_ref, *, B, nch):
    W = s_ref.shape[1]
    a1 = a1_ref[...]
    a2 = a2_ref[...]
    unit = 2 * S5_STATE_GROUPS * S5_STATE

    def swap_re_im(c):
        parts = []
        for q in range(W // unit):
            lo = q * unit
            parts += [c[:, lo + unit // 2:lo + unit], c[:, lo:lo + unit // 2]]
        return jnp.concatenate(parts, axis=1)

    def body(n, carry):
        new = []
        for bb in range(B):
            row = bb * nch + n
            c = carry[bb]
            o_ref[pl.ds(row, 1), :] = c
            new.append(a1 * c + a2 * swap_re_im(c) + s_ref[pl.ds(row, 1), :])
        return tuple(new)

    lax.fori_loop(0, nch, body, tuple(jnp.zeros((1, W), F32) for _ in range(B)))


def _gelu_tanh(y):
    return 0.5 * y * (1.0 + jnp.tanh(math.sqrt(2.0 / math.pi) * (y + 0.044715 * (y * y * y))))


def _s5_out_kernel(*refs):
    L = S5_CHUNK
    nu = S5_TOEP_GROUPS // S5_STATE_GROUPS
    u_refs, (kd_ref, st_ref, cd_ref, o_ref, ub_ref, acc_ref) = refs[:nu], refs[nu:]
    r = pl.program_id(1)

    @pl.when(r == 0)
    def _():
        for s in range(L):
            for k in range(nu):
                ub_ref[s, :, k * LANES:(k + 1) * LANES] = _chunk_step(u_refs[k], s).astype(BF16)

    def lags(first):
        out = None
        for s in range(first, first + S5_STEP_GROUP):
            d = jnp.dot(ub_ref[s], kd_ref[r - s + (L - 1)], preferred_element_type=F32)
            out = d if out is None else out + d
        return out

    st = st_ref[...].astype(BF16)
    half = st.shape[1] // 2
    carried = jnp.concatenate(
        [jnp.dot(st[:, :half], cd_ref[0], preferred_element_type=F32),
         jnp.dot(st[:, half:], cd_ref[1], preferred_element_type=F32)], axis=1)
    acc_ref[...] = carried + lags(0)
    for first in range(S5_STEP_GROUP, L, S5_STEP_GROUP):
        @pl.when(first <= r)
        def _(first=first):
            acc_ref[...] += lags(first)
    o_ref[...] = acc_ref[...].astype(o_ref.dtype)


def _s5_glu_out_kernel(ys_ref, u_ref, d_ref, wg_ref, bg_ref, z_ref, wo_ref, h_ref, g_ref, b_ref,
                       of_ref, ob_ref, yt_ref):
    L, W = S5_CHUNK, S5_WIDTH
    nslab, tm = yt_ref.shape[0], yt_ref.shape[1]
    for s in range(L):
        for k in range(nslab):
            lo = s * W + k * LANES
            yt_ref[k, pl.ds(s, tm // L, stride=L), :] = ys_ref[:, lo:lo + LANES].astype(F32)
    half = tm // 2
    parts = [pl.ds(0, half), pl.ds(half, half)]
    ys = [_gelu_tanh(jnp.concatenate([yt_ref[k, rows, :] for k in range(nslab)], axis=1)
                     + d_ref[...] * u_ref[rows, :]) for rows in parts]
    ts = [jnp.dot(y.astype(BF16), wg_ref[...], preferred_element_type=F32) for y in ys]
    gated = [(y * _sigmoid(t + bg_ref[...]) * _silu(z_ref[rows, :])).astype(BF16)
             for y, t, rows in zip(ys, ts, parts)]
    branches = [jnp.dot(y, wo_ref[...], preferred_element_type=F32) for y in gated]
    for rows, branch in zip(parts, branches):
        out = _deepnorm_ln(h_ref[rows, :], branch, g_ref[...], b_ref[...])
        of_ref[rows, :] = out
        ob_ref[rows, :] = out.astype(BF16)


def _s5_layer(hf, hb, w_in, lam_re, lam_im, log_step, b_re, b_im, c_re, c_im, d_skip,
              w_glu, b_glu, w_out, g, b, *, B, T):
    M, W, G, L = B * T, S5_WIDTH, S5_GROUPS, S5_CHUNK
    nch = T // L
    rows = B * nch
    u = _matmul(hb, w_in, F32, col0=0, ncols=W)
    z = _matmul(hb, w_in, F32, col0=W, ncols=W)
    kd, bd, cd, a1, a2 = _s5_matrices(lam_re, lam_im, log_step, b_re, b_im, c_re, c_im)

    TG, SG = S5_TOEP_GROUPS, S5_STATE_GROUPS
    ntb, nsb = G // TG, G // SG
    tw, sw_in, sw = TG * S5_GROUP, SG * S5_GROUP, 2 * SG * S5_STATE
    nstate = nsb * sw
    s_loc = pl.pallas_call(
        _s5_local_state_kernel,
        grid=(nsb,),
        in_specs=[pl.BlockSpec((M, sw_in), lambda cb: (0, cb)),
                  pl.BlockSpec((L, None, sw_in, sw), lambda cb: (0, cb, 0, 0))],
        out_specs=pl.BlockSpec((rows, sw), lambda cb: (0, cb)),
        out_shape=jax.ShapeDtypeStruct((rows, nstate), F32),
        compiler_params=_cparams("parallel"),
        name="s5_chunk_state",
    )(u, bd)

    scan_w = 2 * sw
    st_in = pl.pallas_call(
        functools.partial(_s5_scan_kernel, B=B, nch=nch),
        grid=(nstate // scan_w,),
        in_specs=[pl.BlockSpec((rows, scan_w), lambda p: (0, p)),
                  pl.BlockSpec((1, scan_w), lambda p: (0, p)),
                  pl.BlockSpec((1, scan_w), lambda p: (0, p))],
        out_specs=pl.BlockSpec((rows, scan_w), lambda p: (0, p)),
        out_shape=jax.ShapeDtypeStruct((rows, nstate), F32),
        compiler_params=_cparams("parallel"),
        name="s5_state_scan",
    )(s_loc, a1, a2)

    per_toep = TG // SG
    y = pl.pallas_call(
        _s5_out_kernel,
        grid=(ntb, L),
        in_specs=[pl.BlockSpec((M, sw_in), lambda cb, r, k=k: (0, per_toep * cb + k))
                  for k in range(per_toep)]
        + [pl.BlockSpec((2 * L - 1, None, tw, tw), lambda cb, r: (0, cb, 0, 0)),
                  pl.BlockSpec((rows, per_toep * sw), lambda cb, r: (0, cb)),
                  pl.BlockSpec((None, per_toep, sw, sw_in), lambda cb, r: (r, cb, 0, 0))],
        out_specs=pl.BlockSpec((rows, tw), lambda cb, r: (0, r * ntb + cb)),
        out_shape=jax.ShapeDtypeStruct((rows, L * W), BF16),
        scratch_shapes=[pltpu.VMEM((L, rows, tw), BF16), pltpu.VMEM((rows, tw), F32)],
        compiler_params=_cparams("parallel", "arbitrary"),
        name="s5_chunk_output",
    )(*([u] * per_toep), kd, st_in, cd)

    tm = min(256, M)
    row = lambda i: (i, 0)
    fixed = lambda i: (0, 0)
    return pl.pallas_call(
        _s5_glu_out_kernel,
        grid=(M // tm,),
        in_specs=[pl.BlockSpec((tm // L, L * W), row),
                  pl.BlockSpec((tm, W), row),
                  pl.BlockSpec((1, W), fixed),
                  _resident((W, W), fixed),
                  pl.BlockSpec((1, W), fixed),
                  pl.BlockSpec((tm, W), row),
                  _resident((W, D_MODEL), fixed),
                  pl.BlockSpec((tm, D_MODEL), row),
                  pl.BlockSpec((1, D_MODEL), fixed),
                  pl.BlockSpec((1, D_MODEL), fixed)],
        out_specs=[pl.BlockSpec((tm, D_MODEL), row), pl.BlockSpec((tm, D_MODEL), row)],
        out_shape=[jax.ShapeDtypeStruct((M, D_MODEL), F32),
                   jax.ShapeDtypeStruct((M, D_MODEL), BF16)],
        scratch_shapes=[pltpu.VMEM((W // LANES, tm, LANES), F32)],
        compiler_params=_cparams("parallel"),
        name="s5_glu_out_ln",
    )(y, u, d_skip.reshape(1, W), w_glu.astype(BF16), b_glu.reshape(1, W), z,
      w_out.astype(BF16), hf, g.reshape(1, D_MODEL), b.reshape(1, D_MODEL))


def kernel(x, positions, ln_g, ln_b, ssd_w_in, ssd_conv_w, ssd_conv_b, ssd_dt_bias, ssd_a_log, ssd_d, ssd_norm_w, ssd_w_out, fox_w_in, fox_f_bias, fox_w_out, mla_w_in, mla_q_norm, mla_kv_norm, mla_w_q_up, mla_w_kv_up, mla_w_out, s5_w_in, s5_lambda_re, s5_lambda_im, s5_log_step, s5_b_re, s5_b_im, s5_c_re, s5_c_im, s5_d, s5_w_glu, s5_b_glu, s5_w_out):
    B, T, D = x.shape
    hf = x.reshape(B * T, D)
    hb = hf.astype(BF16)
    for i in range(DEPTH):
        j = i // 4
        kind = i % 4
        g, b = ln_g[i], ln_b[i]
        if kind == 0:
            hf, hb = _ssd_layer(hf, hb, ssd_w_in[j], ssd_conv_w[j], ssd_conv_b[j], ssd_dt_bias[j],
                                ssd_a_log[j], ssd_d[j], ssd_norm_w[j], ssd_w_out[j], g, b, B=B, T=T)
        elif kind == 1:
            hf, hb = _fox_layer(hf, hb, fox_w_in[j], fox_f_bias[j], fox_w_out[j], g, b, B=B, T=T)
        elif kind == 2:
            hf, hb = _mla_layer(hf, hb, positions, mla_w_in[j], mla_q_norm[j],
                                mla_kv_norm[j], mla_w_q_up[j], mla_w_kv_up[j], mla_w_out[j],
                                g, b, B=B, T=T)
        else:
            hf, hb = _s5_layer(hf, hb, s5_w_in[j], s5_lambda_re[j], s5_lambda_im[j], s5_log_step[j],
                               s5_b_re[j], s5_b_im[j], s5_c_re[j], s5_c_im[j], s5_d[j],
                               s5_w_glu[j], s5_b_glu[j], s5_w_out[j], g, b, B=B, T=T)
    return hf.reshape(B, T, D)
```

```python
import functools
import math

import jax
import jax.numpy as jnp
from jax import lax
from jax.experimental import pallas as pl
from jax.experimental.pallas import tpu as pltpu

F32 = jnp.float32
BF16 = jnp.bfloat16
HIGHEST = lax.Precision.HIGHEST

D_MODEL = 2048
DEPTH = 4
ALPHA = (2.0 * DEPTH) ** 0.25
LN_EPS = 1e-5
RMS_EPS = 1e-6

SSD_D_INNER = 4096
SSD_HEADS = 64
SSD_HEADDIM = 64
SSD_GROUPS = 8
SSD_HEADS_PER_GROUP = SSD_HEADS // SSD_GROUPS
SSD_STATE = 128
SSD_CONV = 4
SSD_CHUNK = 128
SSD_GROUP_WIDTH = SSD_D_INNER // SSD_GROUPS
SSD_BC_WIDTH = SSD_GROUPS * SSD_STATE
SSD_GROUPS_PER_STEP = 4

FOX_HEADS = 16
FOX_HEAD_DIM = 128
FOX_WIDTH = FOX_HEADS * FOX_HEAD_DIM

MLA_HEADS = 16
MLA_Q_RANK = 512
MLA_KV_RANK = 512
MLA_NOPE = 128
MLA_ROPE = 64
MLA_V = 128
MLA_QK = MLA_NOPE + MLA_ROPE
MLA_WIDTH = MLA_HEADS * MLA_V
ROPE_BASE = 10000.0

S5_WIDTH = D_MODEL
S5_GROUP = 16
S5_GROUPS = S5_WIDTH // S5_GROUP
S5_STATE = 64
S5_CHUNK = 8
S5_TOEP_GROUPS = 16
S5_STATE_GROUPS = 8
S5_STEP_GROUP = 4

LANES = 128
VMEM_LIMIT_BYTES = 48 * 1024 * 1024


def _cparams(*sem):
    return pltpu.CompilerParams(dimension_semantics=sem, vmem_limit_bytes=VMEM_LIMIT_BYTES)


def _resident(block_shape, index_map):
    return pl.BlockSpec(block_shape, index_map, pipeline_mode=pl.Buffered(1))


def _sigmoid(x):
    return 0.5 * (jnp.tanh(0.5 * x) + 1.0)


def _silu(x):
    h = 0.5 * x
    return h + h * jnp.tanh(h)


def _softplus(x):
    return jnp.maximum(x, 0.0) + jnp.log1p(jnp.exp(-jnp.abs(x)))


def _deepnorm_ln(h, branch, g, b):
    r = ALPHA * h + branch
    mu = jnp.mean(r, axis=-1, keepdims=True)
    d = r - mu
    var = jnp.mean(d * d, axis=-1, keepdims=True)
    return d * lax.rsqrt(var + LN_EPS) * g + b


def _mm_kernel(x_ref, w_ref, o_ref, wb_ref, *, scale):
    @pl.when(pl.program_id(1) == 0)
    def _():
        wb_ref[...] = w_ref[...].astype(BF16)

    acc = jnp.dot(x_ref[...].astype(BF16), wb_ref[...], preferred_element_type=F32)
    if scale is not None:
        acc = acc * scale
    o_ref[...] = acc.astype(o_ref.dtype)


def _matmul(x, w, out_dtype, *, col0=0, ncols=None, scale=None, tn=1024):
    M, K = x.shape
    N = w.shape[1] - col0 if ncols is None else ncols
    tm = min(1024 if x.dtype == BF16 else 512, M)
    tn = min(tn, N)
    assert M % tm == 0 and N % tn == 0 and col0 % tn == 0, (M, N, col0, tm, tn)
    j0 = col0 // tn
    return pl.pallas_call(
        functools.partial(_mm_kernel, scale=scale),
        grid=(N // tn, M // tm),
        in_specs=[pl.BlockSpec((tm, K), lambda j, i: (i, 0)),
                  pl.BlockSpec((K, tn), lambda j, i: (0, j0 + j))],
        out_specs=pl.BlockSpec((tm, tn), lambda j, i: (i, j)),
        out_shape=jax.ShapeDtypeStruct((M, N), out_dtype),
        scratch_shapes=[pltpu.VMEM((K, tn), BF16)],
        compiler_params=_cparams("parallel", "arbitrary"),
        name="proj",
    )(x, w)


CONV_COL_PARTS = 4


def _mm_conv_silu_kernel(x_ref, w_ref, cw_ref, cb_ref, o_ref, wb_ref, ext_ref, *, taps, tiles_per_seq):
    i = pl.program_id(1)
    tm, tn = o_ref.shape

    @pl.when(i == 0)
    def _():
        wb_ref[...] = w_ref[...].astype(BF16)

    @pl.when(i % tiles_per_seq == 0)
    def _():
        ext_ref[pl.ds(0, 8), :] = jnp.zeros((8, tn), F32)

    xb = x_ref[...].astype(BF16)
    slab = tn // CONV_COL_PARTS
    parts = [pl.ds(p * slab, slab) for p in range(CONV_COL_PARTS)]
    raws = [jnp.dot(xb, wb_ref[:, cols], preferred_element_type=F32) for cols in parts]
    for cols, raw in zip(parts, raws):
        ext_ref[pl.ds(8, tm), cols] = raw
        acc = cb_ref[:, cols] + cw_ref[pl.ds(0, 1), cols] * ext_ref[pl.ds(9 - taps, tm), cols]
        for kk in range(1, taps):
            acc = acc + cw_ref[pl.ds(kk, 1), cols] * ext_ref[pl.ds(9 - taps + kk, tm), cols]
        ext_ref[pl.ds(0, 8), cols] = ext_ref[pl.ds(tm, 8), cols]
        o_ref[:, cols] = _silu(acc)


def _matmul_conv_silu(x, w, conv_w, conv_b, *, col0, ncols, seq_len, tn=1024):
    M, K = x.shape
    taps = conv_w.shape[0]
    tm = min(1024 if x.dtype == BF16 else 512, seq_len)
    assert M % tm == 0 and seq_len % tm == 0 and ncols % tn == 0 and col0 % tn == 0
    j0 = col0 // tn
    return pl.pallas_call(
        functools.partial(_mm_conv_silu_kernel, taps=taps, tiles_per_seq=seq_len // tm),
        grid=(ncols // tn, M // tm),
        in_specs=[pl.BlockSpec((tm, K), lambda j, i: (i, 0)),
                  pl.BlockSpec((K, tn), lambda j, i: (0, j0 + j)),
                  pl.BlockSpec((taps, tn), lambda j, i: (0, j)),
                  pl.BlockSpec((1, tn), lambda j, i: (0, j))],
        out_specs=pl.BlockSpec((tm, tn), lambda j, i: (i, j)),
        out_shape=jax.ShapeDtypeStruct((M, ncols), F32),
        scratch_shapes=[pltpu.VMEM((K, tn), BF16), pltpu.VMEM((tm + 8, tn), F32)],
        compiler_params=_cparams("parallel", "arbitrary"),
        name="proj_conv_silu",
    )(x, w, conv_w, conv_b.reshape(1, -1))


def _out_ln_kernel(y_ref, w_ref, h_ref, g_ref, b_ref, of_ref, ob_ref):
    half = y_ref.shape[0] // 2
    parts = [pl.ds(0, half), pl.ds(half, half)]
    branches = [jnp.dot(y_ref[rows, :], w_ref[...], preferred_element_type=F32) for rows in parts]
    for rows, branch in zip(parts, branches):
        out = _deepnorm_ln(h_ref[rows, :], branch, g_ref[...], b_ref[...])
        of_ref[rows, :] = out
        ob_ref[rows, :] = out.astype(BF16)


def _out_proj_ln(y, w, h, g, b):
    M, K = y.shape
    D = w.shape[1]
    tm = min(512 if K * D * w.dtype.itemsize <= VMEM_LIMIT_BYTES // 4 else 256, M)
    assert M % tm == 0
    return pl.pallas_call(
        _out_ln_kernel,
        grid=(M // tm,),
        in_specs=[pl.BlockSpec((tm, K), lambda i: (i, 0)),
                  _resident((K, D), lambda i: (0, 0)),
                  pl.BlockSpec((tm, D), lambda i: (i, 0)),
                  pl.BlockSpec((1, D), lambda i: (0, 0)),
                  pl.BlockSpec((1, D), lambda i: (0, 0))],
        out_specs=[pl.BlockSpec((tm, D), lambda i: (i, 0)),
                   pl.BlockSpec((tm, D), lambda i: (i, 0))],
        out_shape=[jax.ShapeDtypeStruct((M, D), F32), jax.ShapeDtypeStruct((M, D), BF16)],
        compiler_params=_cparams("parallel"),
        name="out_proj_ln",
    )(y, w, h, g.reshape(1, D), b.reshape(1, D))


ATTN_HEADS_PER_STEP = 4
ATTN_WIDE_BLOCKS = 2


def _attn_kernel(*refs, nparts, tq, tk, dv, head_major):
    q_refs = refs[:nparts]
    k_refs = refs[nparts:2 * nparts]
    v_ref, z_ref, o_ref, m_ref, acc_ref = refs[2 * nparts:]
    nh = ATTN_HEADS_PER_STEP
    i = pl.program_id(2)

    def head(ref, hh, rows=slice(None)):
        if head_major:
            return ref[hh, rows, :]
        w = ref.shape[-1] // nh
        return ref[rows, hh * w:(hh + 1) * w]

    def cat(pieces):
        return pieces[0] if len(pieces) == 1 else jnp.concatenate(pieces, axis=1)

    qs = [cat([head(r, hh) for r in q_refs]) for hh in range(nh)]
    m_ref[...] = jnp.full_like(m_ref, -jnp.inf)
    acc_ref[...] = jnp.zeros_like(acc_ref)

    def block(first, width, diag_offset):
        rows = pl.ds(pl.multiple_of(first * tk, tk), width)
        ones_cols = jnp.ones((width, LANES), BF16)
        scores = []
        for hh in range(nh):
            k = cat([head(r, hh, rows) for r in k_refs])
            scores.append(lax.dot_general(qs[hh], k, (((1,), (1,)), ((), ())),
                                          preferred_element_type=F32))
        probs, alphas = [], []
        for hh in range(nh):
            s = scores[hh]
            if diag_offset is not None:
                r = lax.broadcasted_iota(jnp.int32, (tq, width), 0)
                c = lax.broadcasted_iota(jnp.int32, (tq, width), 1)
                s = jnp.where(c + diag_offset <= r, s, -jnp.inf)
            m_prev = m_ref[hh]
            m_new = jnp.maximum(m_prev, jnp.max(s, axis=-1, keepdims=True))
            alpha = jnp.exp(m_prev - m_new)
            p = jnp.exp(s - jnp.concatenate([m_new] * (width // LANES), axis=1))
            m_ref[hh] = m_new
            probs.append(p.astype(BF16))
            alphas.append(jnp.concatenate([alpha] * ((dv + LANES) // LANES), axis=1))
        for hh in range(nh):
            v_aug = jnp.concatenate([head(v_ref, hh, rows), ones_cols], axis=1)
            acc_ref[hh] = alphas[hh] * acc_ref[hh] + jnp.dot(
                probs[hh], v_aug, preferred_element_type=F32)

    n_full = i * (tq // tk)
    wide = ATTN_WIDE_BLOCKS

    def wide_block(j, carry):
        block(j * wide, wide * tk, None)
        return carry

    lax.fori_loop(0, n_full // wide, wide_block, 0)
    for rem in range(wide - 1):
        @pl.when(n_full % wide > rem)
        def _(rem=rem):
            block((n_full // wide) * wide + rem, tk, None)
    for d in range(tq // tk):
        block(n_full + d, tk, d * tk)

    for hh in range(nh):
        cols = slice(hh * dv, (hh + 1) * dv)
        o = acc_ref[hh, :, :dv] / acc_ref[hh, :, dv:]
        o_ref[:, cols] = (o * _silu(z_ref[:, cols])).astype(o_ref.dtype)


def _attention(inputs, in_specs, out_spec, *, B, T, H, dv, tq, tk, nparts, head_major):
    nh = ATTN_HEADS_PER_STEP
    assert tq % tk == 0 and T % tq == 0 and H % nh == 0
    return pl.pallas_call(
        functools.partial(_attn_kernel, nparts=nparts, tq=tq, tk=tk, dv=dv,
                          head_major=head_major),
        grid=(B, H // nh, T // tq),
        in_specs=in_specs,
        out_specs=out_spec,
        out_shape=jax.ShapeDtypeStruct((B * T, H * dv), BF16),
        scratch_shapes=[pltpu.VMEM((nh, tq, LANES), F32),
                        pltpu.VMEM((nh, tq, dv + LANES), F32)],
        compiler_params=_cparams("parallel", "parallel", "arbitrary"),
        name="causal_attention",
    )(*inputs)


def _attn_tiles(T):
    tq = min(512, T)
    tk = min(512, T)
    return tq, tk


def _ssd_kernel(x_ref, bm_ref, cm_ref, z_ref, dtr_ref,
                dtb_ref, al_ref, dsk_ref, nw_ref, spread_ref,
                o_ref,
                state_ref, yz_ref):
    Q, E, P, GW, N = SSD_CHUNK, SSD_HEADS_PER_GROUP, SSD_HEADDIM, SSD_GROUP_WIDTH, SSD_STATE
    NG = SSD_GROUPS_PER_STEP
    groups = range(NG)
    c = pl.program_id(2)

    @pl.when(c == 0)
    def _():
        state_ref[...] = jnp.zeros_like(state_ref)

    row = lax.broadcasted_iota(jnp.int32, (Q, Q), 0)
    col = lax.broadcasted_iota(jnp.int32, (Q, Q), 1)
    lower = row >= col
    tri_u = (row <= col).astype(BF16)
    left = col < P

    def pieces(v):
        hi = v.astype(BF16).astype(F32)
        mid = (v - hi).astype(BF16).astype(F32)
        lo = ((v - hi) - mid).astype(BF16).astype(F32)
        return jnp.concatenate([hi, mid, lo], axis=0).astype(BF16)

    npairs = E // 2
    dt_r, acs_r = [], []
    for gg in groups:
        dt = _softplus(dtr_ref[gg] + dtb_ref[gg])
        cum3 = jnp.dot(pieces(dt * (-jnp.exp(al_ref[gg]))), tri_u, preferred_element_type=F32)
        dt_r.append(dt)
        acs_r.append(cum3[:E] + cum3[E:2 * E] + cum3[2 * E:])

    bm_b, cm_b, bmt_b, cb, carried = [], [], [], [], []
    for gg in groups:
        ncols = pl.ds(gg * N, N)
        bm = bm_ref[:, ncols]
        bm_b.append(bm.astype(BF16))
        cm_b.append(cm_ref[:, ncols].astype(BF16))
        bmt_b.append(bm.T.astype(BF16))
        cb.append(lax.dot_general(cm_b[gg], bm_b[gg], (((1,), (1,)), ((), ())),
                                  preferred_element_type=F32))
        carried.append([jnp.dot(cm_b[gg], state_ref[gg * npairs + kp].astype(BF16),
                                preferred_element_type=F32)
                        for kp in range(npairs)])

    pair_cols = lambda gg, kp: pl.ds(gg * GW + kp * LANES, LANES)
    xps = [[x_ref[:, pair_cols(gg, kp)] for kp in range(npairs)] for gg in groups]
    gates = [[_silu(z_ref[:, pair_cols(gg, kp)]) for kp in range(npairs)] for gg in groups]

    spreads = [lax.dot_general(pieces(jnp.concatenate([dt_r[gg], acs_r[gg]], axis=0)),
                               spread_ref[...], (((0,), (0,)), ((), ())),
                               preferred_element_type=F32) for gg in groups]

    for gg in groups:
        dt_ch = spreads[gg][:, :GW]
        acs_ch = spreads[gg][:, GW:2 * GW]
        acs_col = spreads[gg][:, 2 * GW:]
        ssq = jnp.zeros((Q, 1), F32)
        for kp in range(npairs):
            cols = pair_cols(gg, kp)
            xp = xps[gg][kp]
            dt_p = dt_ch[:, kp * LANES:(kp + 1) * LANES]
            acs_p = acs_ch[:, kp * LANES:(kp + 1) * LANES]
            last_p = acs_p[Q - 1:Q, :]
            xdt = xp * dt_p
            xdt_b = xdt.astype(BF16)

            ys = []
            for e in (2 * kp, 2 * kp + 1):
                seg = acs_col[:, e * Q:(e + 1) * Q] - acs_r[gg][e:e + 1, :]
                dec = jnp.exp(jnp.where(lower, seg, -jnp.inf))
                ys.append(jnp.dot((cb[gg] * dec).astype(BF16), xdt_b,
                                  preferred_element_type=F32))
            y_diag = jnp.where(left, ys[0], ys[1])

            xw = (xdt * jnp.exp(last_p - acs_p)).astype(BF16)
            s_loc = jnp.dot(bmt_b[gg], xw, preferred_element_type=F32)
            y_off = carried[gg][kp] * jnp.exp(acs_p)
            slot = gg * npairs + kp
            state_ref[slot] = state_ref[slot] * jnp.exp(last_p) + s_loc

            y = (y_diag + y_off + dsk_ref[:, cols] * xp) * gates[gg][kp]
            yz_ref[:, cols] = y
            ssq = ssq + jnp.sum(y * y, axis=-1, keepdims=True)

        gcols = pl.ds(gg * GW, GW)
        inv = lax.rsqrt(ssq * (1.0 / GW) + RMS_EPS)
        o_ref[:, gcols] = (yz_ref[:, gcols] * inv * nw_ref[:, gcols]).astype(o_ref.dtype)


def _ssd_core(z, xbc, dt, dt_bias, a_log, d_skip, norm_w, *, B, T):
    G, E, Q = SSD_GROUPS, SSD_HEADS_PER_GROUP, SSD_CHUNK
    NC = T // Q
    GW, N = SSD_GROUP_WIDTH, SSD_STATE
    dtr = jnp.transpose(dt[:, :SSD_HEADS].reshape(B, T, G, E), (0, 2, 3, 1))
    head_rows = lambda v: v.reshape(G, E, 1)
    d_ch = jnp.repeat(d_skip, SSD_HEADDIM).reshape(1, -1)
    heads = jnp.arange(E)
    per_channel = (jnp.arange(GW)[None, :] // SSD_HEADDIM == heads[:, None]).astype(BF16)
    per_head_block = (jnp.arange(E * Q)[None, :] // Q == heads[:, None]).astype(BF16)
    none = lambda n: jnp.zeros((E, n), BF16)
    spread = jnp.concatenate([
        jnp.concatenate([per_channel, none(GW), none(E * Q)], axis=1),
        jnp.concatenate([none(GW), per_channel, per_head_block], axis=1)], axis=0)
    spread = jnp.tile(spread, (3, 1))

    NG = SSD_GROUPS_PER_STEP
    XW, BW = NG * GW, NG * N
    assert G % NG == 0
    bblk = SSD_D_INNER // BW
    cblk = (SSD_D_INNER + SSD_BC_WIDTH) // BW
    rowblk = lambda b, g, c: b * NC + c
    in_specs = [
        pl.BlockSpec((Q, XW), lambda b, g, c: (rowblk(b, g, c), g)),
        pl.BlockSpec((Q, BW), lambda b, g, c: (rowblk(b, g, c), bblk + g)),
        pl.BlockSpec((Q, BW), lambda b, g, c: (rowblk(b, g, c), cblk + g)),
        pl.BlockSpec((Q, XW), lambda b, g, c: (rowblk(b, g, c), g)),
        pl.BlockSpec((None, NG, E, Q), lambda b, g, c: (b, g, 0, c)),
        pl.BlockSpec((NG, E, 1), lambda b, g, c: (g, 0, 0)),
        pl.BlockSpec((NG, E, 1), lambda b, g, c: (g, 0, 0)),
        pl.BlockSpec((1, XW), lambda b, g, c: (0, g)),
        pl.BlockSpec((1, XW), lambda b, g, c: (0, g)),
        pl.BlockSpec(spread.shape, lambda b, g, c: (0, 0)),
    ]
    return pl.pallas_call(
        _ssd_kernel,
        grid=(B, G // NG, NC),
        in_specs=in_specs,
        out_specs=pl.BlockSpec((Q, XW), lambda b, g, c: (rowblk(b, g, c), g)),
        out_shape=jax.ShapeDtypeStruct((B * T, SSD_D_INNER), BF16),
        scratch_shapes=[pltpu.VMEM((NG * E // 2, N, LANES), F32), pltpu.VMEM((Q, XW), F32)],
        compiler_params=_cparams("parallel", "parallel", "arbitrary"),
        name="ssd_chunk_scan",
    )(xbc, xbc, xbc, z, dtr,
      head_rows(dt_bias), head_rows(a_log), d_ch, norm_w.reshape(1, -1), spread)


def _pad_cols(w, n):
    return jnp.pad(w, ((0, 0), (0, n - w.shape[1])))


def _ssd_layer(hf, hb, w_in, conv_w, conv_b, dt_bias, a_log, d_skip, norm_w, w_out, g, b, *, B, T):
    di = SSD_D_INNER
    cd = di + 2 * SSD_BC_WIDTH
    z = _matmul(hb, w_in, F32, col0=0, ncols=di)
    xbc = _matmul_conv_silu(hb, w_in, conv_w, conv_b, col0=di, ncols=cd, seq_len=T)
    dt = _matmul(hb, _pad_cols(w_in[:, di + cd:], LANES), F32)
    y = _ssd_core(z, xbc, dt, dt_bias, a_log, d_skip, norm_w, B=B, T=T)
    return _out_proj_ln(y, w_out.astype(BF16), hf, g, b)


FOX_BIAS_PIECES = 3


def _fox_bias_kernel(f_ref, b_ref, qx_ref, kx_ref, carry_ref, *, nblk):
    blk, W, P = LANES, FOX_WIDTH, FOX_BIAS_PIECES
    row = lax.broadcasted_iota(jnp.int32, (blk, blk), 0)
    col = lax.broadcasted_iota(jnp.int32, (blk, blk), 1)
    tri = (row >= col).astype(F32)
    bias = b_ref[...]
    H = FOX_HEADS
    r = lax.broadcasted_iota(jnp.int32, (blk, W), 0)
    c = lax.broadcasted_iota(jnp.int32, (blk, W), 1)
    head, piece = r % H, r // H
    is_piece = r < P * H
    place_q = jnp.logical_and(is_piece, c == head * blk + piece).astype(BF16)
    place_k = jnp.logical_and(is_piece, c == head * blk + P + piece).astype(BF16)
    is_head = col < H
    cmod = lax.broadcasted_iota(jnp.int32, (1, W), 1) % blk
    ones_q = jnp.logical_and(cmod >= P, cmod < 2 * P).astype(F32)
    ones_k = (cmod < P).astype(F32)

    def body(t, carry):
        rows = pl.ds(pl.multiple_of(t * blk, blk), blk)
        x = f_ref[rows, :] + bias
        logf = jnp.minimum(x, 0.0) - jnp.log1p(jnp.exp(-jnp.abs(x)))
        cum = jnp.dot(tri, logf, precision=HIGHEST, preferred_element_type=F32) + carry
        hi = jnp.where(is_head, cum, 0.0).astype(BF16).astype(F32)
        r1 = jnp.where(is_head, cum, 0.0) - hi
        mid = r1.astype(BF16).astype(F32)
        lo = (r1 - mid).astype(BF16).astype(F32)
        pieces = (hi + pltpu.roll(mid, H, 1) + pltpu.roll(lo, 2 * H, 1)).astype(BF16)
        qx_ref[rows, :] = (jnp.dot(pieces, place_q, preferred_element_type=F32)
                           + ones_q).astype(BF16)
        kx_ref[rows, :] = (ones_k - jnp.dot(pieces, place_k, preferred_element_type=F32)
                           ).astype(BF16)
        return cum[blk - 1:blk, :]

    @pl.when(pl.program_id(1) == 0)
    def _():
        carry_ref[...] = jnp.zeros_like(carry_ref)

    carry_ref[...] = lax.fori_loop(0, nblk, body, carry_ref[...])


def _fox_bias_columns(f, f_bias, *, B, T):
    bias = jnp.pad(f_bias, (0, LANES - FOX_HEADS)).reshape(1, LANES)
    tt = min(512, T)
    out = jax.ShapeDtypeStruct((B, T, FOX_WIDTH), BF16)
    spec = pl.BlockSpec((None, tt, FOX_WIDTH), lambda b, t: (b, t, 0))
    qx, kx = pl.pallas_call(
        functools.partial(_fox_bias_kernel, nblk=tt // LANES),
        grid=(B, T // tt),
        in_specs=[pl.BlockSpec((None, tt, LANES), lambda b, t: (b, t, 0)),
                  pl.BlockSpec((1, LANES), lambda b, t: (0, 0))],
        out_specs=[spec, spec],
        out_shape=[out, out],
        scratch_shapes=[pltpu.VMEM((1, LANES), F32)],
        compiler_params=_cparams("parallel", "arbitrary"),
        name="fox_cum_log_forget",
    )(f.reshape(B, T, LANES), bias)
    return qx.reshape(B * T, FOX_WIDTH), kx.reshape(B * T, FOX_WIDTH)


def _fox_layer(hf, hb, w_in, f_bias, w_out, g, b, *, B, T):
    W, H, dh = FOX_WIDTH, FOX_HEADS, FOX_HEAD_DIM
    q = _matmul(hb, w_in, BF16, col0=0, ncols=W, scale=dh ** -0.5)
    kv = _matmul(hb, w_in, BF16, col0=W, ncols=2 * W)
    z = _matmul(hb, w_in, F32, col0=3 * W, ncols=W)
    f = _matmul(hb, _pad_cols(w_in[:, 4 * W:], LANES), F32)
    qx, kx = _fox_bias_columns(f, f_bias, B=B, T=T)

    tq, tk = _attn_tiles(T)
    nq = T // tq
    hw = ATTN_HEADS_PER_STEP * dh
    q_tile = pl.BlockSpec((tq, hw), lambda bb, h, i: (bb * nq + i, h))
    seq_k = pl.BlockSpec((T, hw), lambda bb, h, i: (bb, h))
    seq_v = pl.BlockSpec((T, hw), lambda bb, h, i: (bb, W // hw + h))
    in_specs = [q_tile, q_tile, seq_k, seq_k, seq_v, q_tile]
    y = _attention((q, qx, kv, kx, kv, z), in_specs, q_tile,
                   B=B, T=T, H=H, dv=dh, tq=tq, tk=tk, nparts=2, head_major=False)
    return _out_proj_ln(y, w_out.astype(BF16), hf, g, b)


def _rope_table_kernel(pos_ref, freq_ref, sign_ref, cos_ref, sin_ref):
    ang = pos_ref[...].astype(F32) * freq_ref[...]
    cos_ref[...] = jnp.cos(ang)
    sin_ref[...] = jnp.sin(ang) * sign_ref[...]


def _rope_tables(positions, *, B, T):
    half = MLA_ROPE // 2
    inv_freq = ROPE_BASE ** (-jnp.arange(0, MLA_ROPE, 2, dtype=F32) / MLA_ROPE)
    freq = jnp.tile(inv_freq, LANES // half).reshape(1, LANES)
    sign = jnp.tile(jnp.concatenate([-jnp.ones((half,), F32), jnp.ones((half,), F32)]),
                    LANES // MLA_ROPE).reshape(1, LANES)
    M = B * T
    tm = min(1024, M)
    return pl.pallas_call(
        _rope_table_kernel,
        grid=(M // tm,),
        in_specs=[pl.BlockSpec((tm, 1), lambda i: (i, 0)),
                  pl.BlockSpec((1, LANES), lambda i: (0, 0)),
                  pl.BlockSpec((1, LANES), lambda i: (0, 0))],
        out_specs=[pl.BlockSpec((tm, LANES), lambda i: (i, 0)),
                   pl.BlockSpec((tm, LANES), lambda i: (i, 0))],
        out_shape=[jax.ShapeDtypeStruct((M, LANES), F32), jax.ShapeDtypeStruct((M, LANES), F32)],
        compiler_params=_cparams("parallel"),
        name="rope_tables",
    )(positions.reshape(M, 1), freq, sign)


def _rms_to_bf16(x, w):
    y = x * lax.rsqrt(jnp.mean(x * x, axis=-1, keepdims=True) + RMS_EPS)
    return (y * w).astype(BF16)


def _rope_pair(c2, cos, sin):
    return c2 * cos + pltpu.roll(c2, MLA_ROPE, 1) * sin


def _mla_qkv_kernel(x_ref, wl_ref, qn_ref, kn_ref, wq_ref, wkv_ref, cos_ref, sin_ref,
                    q_ref, k_ref, v_ref, *, scale):
    qr, kr = MLA_Q_RANK, MLA_KV_RANK
    cos, sin = cos_ref[...], sin_ref[...]
    lat = jnp.dot(x_ref[...], wl_ref[...], preferred_element_type=F32)
    qn = _rms_to_bf16(lat[:, :qr], qn_ref[...])
    kn = _rms_to_bf16(lat[:, qr:qr + kr], kn_ref[...])
    k_pe = _rope_pair(lat[:, qr + kr:], cos, sin)[:, :MLA_ROPE].astype(k_ref.dtype)
    for h in range(MLA_HEADS):
        r = jnp.dot(qn, wq_ref[h], preferred_element_type=F32)
        pe = _rope_pair(r[:, MLA_NOPE:], cos, sin)
        q_ref[h, :, :MLA_NOPE] = (r[:, :MLA_NOPE] * scale).astype(q_ref.dtype)
        q_ref[h, :, MLA_NOPE:] = (pe[:, :MLA_ROPE] * scale).astype(q_ref.dtype)
        r = jnp.dot(kn, wkv_ref[h], preferred_element_type=F32)
        k_ref[h, :, :MLA_NOPE] = r[:, :MLA_NOPE].astype(k_ref.dtype)
        k_ref[h, :, MLA_NOPE:] = k_pe
        v_ref[h] = r[:, MLA_NOPE:].astype(v_ref.dtype)


def _swap_halves(w):
    half = w.shape[-1] // 2
    return jnp.concatenate([w[..., half:], w[..., :half]], axis=-1)


def _mla_layer(hf, hb, positions, w_in, q_norm, kv_norm, w_q_up, w_kv_up, w_out, g, b, *, B, T):
    H, M = MLA_HEADS, B * T
    qr, kr = MLA_Q_RANK, MLA_KV_RANK
    lat_end = qr + kr + MLA_ROPE
    w_pe = w_in[:, qr + kr:lat_end]
    w_lat = jnp.concatenate([w_in[:, :qr + kr], w_pe, _swap_halves(w_pe)], axis=1).astype(BF16)
    nlat = w_lat.shape[1]
    z = _matmul(hb, w_in[:, lat_end:], F32)
    cos, sin = _rope_tables(positions, B=B, T=T)

    wq = w_q_up.reshape(qr, H, MLA_QK)
    wq = jnp.concatenate([wq, _swap_halves(wq[..., MLA_NOPE:])], axis=-1)
    wq = jnp.transpose(wq, (1, 0, 2)).astype(BF16)
    wkv = jnp.transpose(w_kv_up.reshape(kr, H, MLA_NOPE + MLA_V), (1, 0, 2)).astype(BF16)

    tm = min(256, M)
    wide = MLA_NOPE + LANES
    row = lambda i: (i, 0)
    q, k, v = pl.pallas_call(
        functools.partial(_mla_qkv_kernel, scale=MLA_QK ** -0.5),
        grid=(M // tm,),
        in_specs=[pl.BlockSpec((tm, D_MODEL), row),
                  _resident((D_MODEL, nlat), lambda i: (0, 0)),
                  pl.BlockSpec((1, qr), lambda i: (0, 0)),
                  pl.BlockSpec((1, kr), lambda i: (0, 0)),
                  _resident((H, qr, wide), lambda i: (0, 0, 0)),
                  _resident((H, kr, MLA_NOPE + MLA_V), lambda i: (0, 0, 0)),
                  pl.BlockSpec((tm, LANES), row),
                  pl.BlockSpec((tm, LANES), row)],
        out_specs=[pl.BlockSpec((H, tm, MLA_QK), lambda i: (0, i, 0)),
                   pl.BlockSpec((H, tm, MLA_QK), lambda i: (0, i, 0)),
                   pl.BlockSpec((H, tm, MLA_V), lambda i: (0, i, 0))],
        out_shape=[jax.ShapeDtypeStruct((H, M, MLA_QK), BF16),
                   jax.ShapeDtypeStruct((H, M, MLA_QK), BF16),
                   jax.ShapeDtypeStruct((H, M, MLA_V), BF16)],
        compiler_params=_cparams("parallel"),
        name="mla_qkv_rope",
    )(hb, w_lat, q_norm.reshape(1, qr), kv_norm.reshape(1, kr), wq, wkv, cos, sin)

    tq, tk = _attn_tiles(T)
    nq = T // tq
    nh = ATTN_HEADS_PER_STEP
    in_specs = [
        pl.BlockSpec((nh, tq, MLA_QK), lambda bb, h, i: (h, bb * nq + i, 0)),
        pl.BlockSpec((nh, T, MLA_QK), lambda bb, h, i: (h, bb, 0)),
        pl.BlockSpec((nh, T, MLA_V), lambda bb, h, i: (h, bb, 0)),
        pl.BlockSpec((tq, nh * MLA_V), lambda bb, h, i: (bb * nq + i, h)),
    ]
    out_spec = pl.BlockSpec((tq, nh * MLA_V), lambda bb, h, i: (bb * nq + i, h))
    y = _attention((q, k, v, z), in_specs, out_spec,
                   B=B, T=T, H=H, dv=MLA_V, tq=tq, tk=tk, nparts=1, head_major=True)
    return _out_proj_ln(y, w_out.astype(BF16), hf, g, b)


def _s5_matrices(lam_re, lam_im, log_step, b_re, b_im, c_re, c_im):
    L = S5_CHUNK
    step = jnp.exp(log_step.astype(F32))[:, None]
    mag = jnp.exp(lam_re * step)
    ar = mag * jnp.cos(lam_im * step)
    ai = mag * jnp.sin(lam_im * step)
    den = lam_re * lam_re + lam_im * lam_im
    fr = ((ar - 1.0) * lam_re + ai * lam_im) / den
    fi = (ai * lam_re - (ar - 1.0) * lam_im) / den
    bbr = fr[..., None] * b_re - fi[..., None] * b_im
    bbi = fr[..., None] * b_im + fi[..., None] * b_re
    pr, pi = [jnp.ones_like(ar)], [jnp.zeros_like(ar)]
    for _ in range(L):
        pr_next = pr[-1] * ar - pi[-1] * ai
        pi_next = pr[-1] * ai + pi[-1] * ar
        pr.append(pr_next)
        pi.append(pi_next)
    pwr = jnp.stack(pr, axis=1)
    pwi = jnp.stack(pi, axis=1)

    G, P, I = S5_GROUPS, S5_STATE, S5_GROUP
    TG, SG = S5_TOEP_GROUPS, S5_STATE_GROUPS
    NTB, NSB = G // TG, G // SG

    c_ji = lambda c: jnp.tile(jnp.transpose(c, (0, 2, 1)), (1, 1, I))
    b_ji = lambda v: jnp.repeat(v, I, axis=2)
    cbr = c_ji(c_re) * b_ji(bbr) - c_ji(c_im) * b_ji(bbi)
    cbi = c_ji(c_re) * b_ji(bbi) + c_ji(c_im) * b_ji(bbr)
    kern = jnp.einsum('gdp,gpq->gdq', jnp.concatenate([pwr[:, :L], -pwi[:, :L]], axis=2),
                      jnp.concatenate([cbr, cbi], axis=1), precision=HIGHEST)
    idx = jnp.arange

    kc = jnp.transpose(kern.reshape(NTB, TG, L, I, I), (2, 0, 1, 3, 4)).reshape(L, NTB, TG * I, I)
    kc = jnp.pad(kc, ((L - 1, 0), (0, 0), (0, 0), (0, 0)))
    repeat_i = (idx(TG * I)[None, :] % I == idx(I)[:, None]).astype(F32)
    same_t = (idx(TG * I)[:, None] // I == idx(TG * I)[None, :] // I)
    kd = jnp.where(same_t, jnp.einsum('dcri,iq->dcrq', kc, repeat_i, precision=HIGHEST), 0.0)

    lanes = lambda v: jnp.tile(v, (1, 1, SG))
    by_step = lambda w: jnp.transpose(
        lanes(w[:, :L][:, ::-1]).reshape(NSB, SG, L, SG * P), (2, 0, 1, 3))[:, :, :, None]
    by_chan = lambda v: lanes(jnp.transpose(v, (0, 2, 1))).reshape(1, NSB, SG, I, SG * P)
    qr, qi = by_step(pwr), by_step(pwi)
    br, bi = by_chan(bbr), by_chan(bbi)
    bd = jnp.concatenate([qr * br - qi * bi, qr * bi + qi * br], axis=-1)
    bd = bd.reshape(L, NSB, SG * I, 2 * SG * P)
    same_b = (idx(SG * I)[:, None] // I == (idx(2 * SG * P)[None, :] % (SG * P)) // P)
    bd = jnp.where(same_b, bd, 0.0)

    chan = lambda c: jnp.transpose(c.reshape(NSB, SG, I, P), (0, 3, 1, 2)).reshape(1, NSB, P, SG * I)
    step_pow = lambda w: jnp.repeat(
        jnp.transpose(w[:, 1:].reshape(NSB, SG, L, P), (2, 0, 3, 1)), I, axis=3)
    cr, ci, wr, wi = chan(c_re), chan(c_im), step_pow(pwr), step_pow(pwi)
    cp = jnp.stack([cr * wr - ci * wi, -(cr * wi + ci * wr)], axis=2)
    cd = jnp.broadcast_to(cp[:, :, :, None], (L, NSB, 2, SG, P, SG * I))
    cd = cd.reshape(L, NSB, 2 * SG * P, SG * I)
    same_c = ((idx(2 * SG * P)[:, None] % (SG * P)) // P == idx(SG * I)[None, :] // I)
    cd = jnp.where(same_c, cd, 0.0)

    alr = pwr[:, L].reshape(NSB, SG * P)
    ali = pwi[:, L].reshape(NSB, SG * P)
    a1 = jnp.concatenate([alr, alr], axis=1).reshape(1, -1)
    a2 = jnp.concatenate([-ali, ali], axis=1).reshape(1, -1)
    return kd.astype(BF16), bd.astype(BF16), cd.astype(BF16), a1, a2


def _chunk_step(u_ref, s):
    return u_ref[pl.ds(s, u_ref.shape[0] // S5_CHUNK, stride=S5_CHUNK), :]


def _s5_local_state_kernel(u_ref, bd_ref, s_ref):
    acc = jnp.dot(_chunk_step(u_ref, 0).astype(BF16), bd_ref[0], preferred_element_type=F32)
    for s in range(1, S5_CHUNK):
        acc = acc + jnp.dot(_chunk_step(u_ref, s).astype(BF16), bd_ref[s],
                            preferred_element_type=F32)
    s_ref[...] = acc


def _s5_scan_kernel(s_ref, a1_ref, a2_ref, o_ref, *, B, nch):
    W = s_ref.shape[1]
    a1 = a1_ref[...]
    a2 = a2_ref[...]
    unit = 2 * S5_STATE_GROUPS * S5_STATE

    def swap_re_im(c):
        parts = []
        for q in range(W // unit):
            lo = q * unit
            parts += [c[:, lo + unit // 2:lo + unit], c[:, lo:lo + unit // 2]]
        return jnp.concatenate(parts, axis=1)

    def body(n, carry):
        new = []
        for bb in range(B):
            row = bb * nch + n
            c = carry[bb]
            o_ref[pl.ds(row, 1), :] = c
            new.append(a1 * c + a2 * swap_re_im(c) + s_ref[pl.ds(row, 1), :])
        return tuple(new)

    lax.fori_loop(0, nch, body, tuple(jnp.zeros((1, W), F32) for _ in range(B)))


def _gelu_tanh(y):
    return 0.5 * y * (1.0 + jnp.tanh(math.sqrt(2.0 / math.pi) * (y + 0.044715 * (y * y * y))))


def _s5_out_kernel(*refs):
    L = S5_CHUNK
    nu = S5_TOEP_GROUPS // S5_STATE_GROUPS
    u_refs, (kd_ref, st_ref, cd_ref, o_ref, ub_ref, acc_ref) = refs[:nu], refs[nu:]
    r = pl.program_id(1)

    @pl.when(r == 0)
    def _():
        for s in range(L):
            for k in range(nu):
                ub_ref[s, :, k * LANES:(k + 1) * LANES] = _chunk_step(u_refs[k], s).astype(BF16)

    def lags(first):
        out = None
        for s in range(first, first + S5_STEP_GROUP):
            d = jnp.dot(ub_ref[s], kd_ref[r - s + (L - 1)], preferred_element_type=F32)
            out = d if out is None else out + d
        return out

    st = st_ref[...].astype(BF16)
    half = st.shape[1] // 2
    carried = jnp.concatenate(
        [jnp.dot(st[:, :half], cd_ref[0], preferred_element_type=F32),
         jnp.dot(st[:, half:], cd_ref[1], preferred_element_type=F32)], axis=1)
    acc_ref[...] = carried + lags(0)
    for first in range(S5_STEP_GROUP, L, S5_STEP_GROUP):
        @pl.when(first <= r)
        def _(first=first):
            acc_ref[...] += lags(first)
    o_ref[...] = acc_ref[...].astype(o_ref.dtype)


def _s5_glu_out_kernel(ys_ref, u_ref, d_ref, wg_ref, bg_ref, z_ref, wo_ref, h_ref, g_ref, b_ref,
                       of_ref, ob_ref, yt_ref):
    L, W = S5_CHUNK, S5_WIDTH
    nslab, tm = yt_ref.shape[0], yt_ref.shape[1]
    for s in range(L):
        for k in range(nslab):
            lo = s * W + k * LANES
            yt_ref[k, pl.ds(s, tm // L, stride=L), :] = ys_ref[:, lo:lo + LANES].astype(F32)
    half = tm // 2
    parts = [pl.ds(0, half), pl.ds(half, half)]
    ys = [_gelu_tanh(jnp.concatenate([yt_ref[k, rows, :] for k in range(nslab)], axis=1)
                     + d_ref[...] * u_ref[rows, :]) for rows in parts]
    ts = [jnp.dot(y.astype(BF16), wg_ref[...], preferred_element_type=F32) for y in ys]
    gated = [(y * _sigmoid(t + bg_ref[...]) * _silu(z_ref[rows, :])).astype(BF16)
             for y, t, rows in zip(ys, ts, parts)]
    branches = [jnp.dot(y, wo_ref[...], preferred_element_type=F32) for y in gated]
    for rows, branch in zip(parts, branches):
        out = _deepnorm_ln(h_ref[rows, :], branch, g_ref[...], b_ref[...])
        of_ref[rows, :] = out
        ob_ref[rows, :] = out.astype(BF16)


def _s5_layer(hf, hb, w_in, lam_re, lam_im, log_step, b_re, b_im, c_re, c_im, d_skip,
              w_glu, b_glu, w_out, g, b, *, B, T):
    M, W, G, L = B * T, S5_WIDTH, S5_GROUPS, S5_CHUNK
    nch = T // L
    rows = B * nch
    u = _matmul(hb, w_in, F32, col0=0, ncols=W)
    z = _matmul(hb, w_in, F32, col0=W, ncols=W)
    kd, bd, cd, a1, a2 = _s5_matrices(lam_re, lam_im, log_step, b_re, b_im, c_re, c_im)

    TG, SG = S5_TOEP_GROUPS, S5_STATE_GROUPS
    ntb, nsb = G // TG, G // SG
    tw, sw_in, sw = TG * S5_GROUP, SG * S5_GROUP, 2 * SG * S5_STATE
    nstate = nsb * sw
    s_loc = pl.pallas_call(
        _s5_local_state_kernel,
        grid=(nsb,),
        in_specs=[pl.BlockSpec((M, sw_in), lambda cb: (0, cb)),
                  pl.BlockSpec((L, None, sw_in, sw), lambda cb: (0, cb, 0, 0))],
        out_specs=pl.BlockSpec((rows, sw), lambda cb: (0, cb)),
        out_shape=jax.ShapeDtypeStruct((rows, nstate), F32),
        compiler_params=_cparams("parallel"),
        name="s5_chunk_state",
    )(u, bd)

    scan_w = 2 * sw
    st_in = pl.pallas_call(
        functools.partial(_s5_scan_kernel, B=B, nch=nch),
        grid=(nstate // scan_w,),
        in_specs=[pl.BlockSpec((rows, scan_w), lambda p: (0, p)),
                  pl.BlockSpec((1, scan_w), lambda p: (0, p)),
                  pl.BlockSpec((1, scan_w), lambda p: (0, p))],
        out_specs=pl.BlockSpec((rows, scan_w), lambda p: (0, p)),
        out_shape=jax.ShapeDtypeStruct((rows, nstate), F32),
        compiler_params=_cparams("parallel"),
        name="s5_state_scan",
    )(s_loc, a1, a2)

    per_toep = TG // SG
    y = pl.pallas_call(
        _s5_out_kernel,
        grid=(ntb, L),
        in_specs=[pl.BlockSpec((M, sw_in), lambda cb, r, k=k: (0, per_toep * cb + k))
                  for k in range(per_toep)]
        + [pl.BlockSpec((2 * L - 1, None, tw, tw), lambda cb, r: (0, cb, 0, 0)),
                  pl.BlockSpec((rows, per_toep * sw), lambda cb, r: (0, cb)),
                  pl.BlockSpec((None, per_toep, sw, sw_in), lambda cb, r: (r, cb, 0, 0))],
        out_specs=pl.BlockSpec((rows, tw), lambda cb, r: (0, r * ntb + cb)),
        out_shape=jax.ShapeDtypeStruct((rows, L * W), BF16),
        scratch_shapes=[pltpu.VMEM((L, rows, tw), BF16), pltpu.VMEM((rows, tw), F32)],
        compiler_params=_cparams("parallel", "arbitrary"),
        name="s5_chunk_output",
    )(*([u] * per_toep), kd, st_in, cd)

    tm = min(256, M)
    row = lambda i: (i, 0)
    fixed = lambda i: (0, 0)
    return pl.pallas_call(
        _s5_glu_out_kernel,
        grid=(M // tm,),
        in_specs=[pl.BlockSpec((tm // L, L * W), row),
                  pl.BlockSpec((tm, W), row),
                  pl.BlockSpec((1, W), fixed),
                  _resident((W, W), fixed),
                  pl.BlockSpec((1, W), fixed),
                  pl.BlockSpec((tm, W), row),
                  _resident((W, D_MODEL), fixed),
                  pl.BlockSpec((tm, D_MODEL), row),
                  pl.BlockSpec((1, D_MODEL), fixed),
                  pl.BlockSpec((1, D_MODEL), fixed)],
        out_specs=[pl.BlockSpec((tm, D_MODEL), row), pl.BlockSpec((tm, D_MODEL), row)],
        out_shape=[jax.ShapeDtypeStruct((M, D_MODEL), F32),
                   jax.ShapeDtypeStruct((M, D_MODEL), BF16)],
        scratch_shapes=[pltpu.VMEM((W // LANES, tm, LANES), F32)],
        compiler_params=_cparams("parallel"),
        name="s5_glu_out_ln",
    )(y, u, d_skip.reshape(1, W), w_glu.astype(BF16), b_glu.reshape(1, W), z,
      w_out.astype(BF16), hf, g.reshape(1, D_MODEL), b.reshape(1, D_MODEL))


def kernel(x, positions, ln_g, ln_b, ssd_w_in, ssd_conv_w, ssd_conv_b, ssd_dt_bias, ssd_a_log, ssd_d, ssd_norm_w, ssd_w_out, fox_w_in, fox_f_bias, fox_w_out, mla_w_in, mla_q_norm, mla_kv_norm, mla_w_q_up, mla_w_kv_up, mla_w_out, s5_w_in, s5_lambda_re, s5_lambda_im, s5_log_step, s5_b_re, s5_b_im, s5_c_re, s5_c_im, s5_d, s5_w_glu, s5_b_glu, s5_w_out):
    B, T, D = x.shape
    hf = x.reshape(B * T, D)
    hb = hf.astype(BF16)
    for i in range(DEPTH):
        j = i // 4
        kind = i % 4
        g, b = ln_g[i], ln_b[i]
        if kind == 0:
            hf, hb = _ssd_layer(hf, hb, ssd_w_in[j], ssd_conv_w[j], ssd_conv_b[j], ssd_dt_bias[j],
                                ssd_a_log[j], ssd_d[j], ssd_norm_w[j], ssd_w_out[j], g, b, B=B, T=T)
        elif kind == 1:
            hf, hb = _fox_layer(hf, hb, fox_w_in[j], fox_f_bias[j], fox_w_out[j], g, b, B=B, T=T)
        elif kind == 2:
            hf, hb = _mla_layer(hf, hb, positions, mla_w_in[j], mla_q_norm[j],
                                mla_kv_norm[j], mla_w_q_up[j], mla_w_kv_up[j], mla_w_out[j],
                                g, b, B=B, T=T)
        else:
            hf, hb = _s5_layer(hf, hb, s5_w_in[j], s5_lambda_re[j], s5_lambda_im[j], s5_log_step[j],
                               s5_b_re[j], s5_b_im[j], s5_c_re[j], s5_c_im[j], s5_d[j],
                               s5_w_glu[j], s5_b_glu[j], s5_w_out[j], g, b, B=B, T=T)
    return hf.reshape(B, T, D)
```

```python
import functools
import math

import jax
import jax.numpy as jnp
from jax import lax
from jax.experimental import pallas as pl
from jax.experimental.pallas import tpu as pltpu

F32 = jnp.float32
BF16 = jnp.bfloat16
HIGHEST = lax.Precision.HIGHEST

D_MODEL = 2048
DEPTH = 4
ALPHA = (2.0 * DEPTH) ** 0.25
LN_EPS = 1e-5
RMS_EPS = 1e-6

SSD_D_INNER = 4096
SSD_HEADS = 64
SSD_HEADDIM = 64
SSD_GROUPS = 8
SSD_HEADS_PER_GROUP = SSD_HEADS // SSD_GROUPS
SSD_STATE = 128
SSD_CONV = 4
SSD_CHUNK = 128
SSD_GROUP_WIDTH = SSD_D_INNER // SSD_GROUPS
SSD_BC_WIDTH = SSD_GROUPS * SSD_STATE
SSD_GROUPS_PER_STEP = 4

FOX_HEADS = 16
FOX_HEAD_DIM = 128
FOX_WIDTH = FOX_HEADS * FOX_HEAD_DIM

MLA_HEADS = 16
MLA_Q_RANK = 512
MLA_KV_RANK = 512
MLA_NOPE = 128
MLA_ROPE = 64
MLA_V = 128
MLA_QK = MLA_NOPE + MLA_ROPE
MLA_WIDTH = MLA_HEADS * MLA_V
ROPE_BASE = 10000.0

S5_WIDTH = D_MODEL
S5_GROUP = 16
S5_GROUPS = S5_WIDTH // S5_GROUP
S5_STATE = 64
S5_CHUNK = 8
S5_TOEP_GROUPS = 16
S5_STATE_GROUPS = 8
S5_STEP_GROUP = 4

LANES = 128
VMEM_LIMIT_BYTES = 48 * 1024 * 1024


def _cparams(*sem):
    return pltpu.CompilerParams(dimension_semantics=sem, vmem_limit_bytes=VMEM_LIMIT_BYTES)


def _resident(block_shape, index_map):
    return pl.BlockSpec(block_shape, index_map, pipeline_mode=pl.Buffered(1))


def _sigmoid(x):
    return 0.5 * (jnp.tanh(0.5 * x) + 1.0)


def _silu(x):
    h = 0.5 * x
    return h + h * jnp.tanh(h)


def _softplus(x):
    return jnp.maximum(x, 0.0) + jnp.log1p(jnp.exp(-jnp.abs(x)))


def _deepnorm_ln(h, branch, g, b):
    r = ALPHA * h + branch
    mu = jnp.mean(r, axis=-1, keepdims=True)
    d = r - mu
    var = jnp.mean(d * d, axis=-1, keepdims=True)
    return d * lax.rsqrt(var + LN_EPS) * g + b


def _mm_kernel(x_ref, w_ref, o_ref, wb_ref, *, scale):
    @pl.when(pl.program_id(1) == 0)
    def _():
        wb_ref[...] = w_ref[...].astype(BF16)

    acc = jnp.dot(x_ref[...].astype(BF16), wb_ref[...], preferred_element_type=F32)
    if scale is not None:
        acc = acc * scale
    o_ref[...] = acc.astype(o_ref.dtype)


def _weight_spec(w, layer, K, tn, j0):
    if w.ndim == 2:
        return pl.BlockSpec((K, tn), lambda j, i: (0, j0 + j))
    return pl.BlockSpec((None, K, tn), lambda j, i: (layer, 0, j0 + j))


def _matmul(x, w, out_dtype, *, layer=0, col0=0, ncols=None, scale=None, tn=1024):
    M, K = x.shape
    N = w.shape[-1] - col0 if ncols is None else ncols
    tm = min(1024 if x.dtype == BF16 else 512, M)
    tn = min(tn, N)
    assert M % tm == 0 and N % tn == 0 and col0 % tn == 0, (M, N, col0, tm, tn)
    j0 = col0 // tn
    return pl.pallas_call(
        functools.partial(_mm_kernel, scale=scale),
        grid=(N // tn, M // tm),
        in_specs=[pl.BlockSpec((tm, K), lambda j, i: (i, 0)),
                  _weight_spec(w, layer, K, tn, j0)],
        out_specs=pl.BlockSpec((tm, tn), lambda j, i: (i, j)),
        out_shape=jax.ShapeDtypeStruct((M, N), out_dtype),
        scratch_shapes=[pltpu.VMEM((K, tn), BF16)],
        compiler_params=_cparams("parallel", "arbitrary"),
        name="proj",
    )(x, w)


CONV_COL_PARTS = 4


def _mm_conv_silu_kernel(x_ref, w_ref, cw_ref, cb_ref, o_ref, wb_ref, ext_ref, *, taps, tiles_per_seq):
    i = pl.program_id(1)
    tm, tn = o_ref.shape

    @pl.when(i == 0)
    def _():
        wb_ref[...] = w_ref[...].astype(BF16)

    @pl.when(i % tiles_per_seq == 0)
    def _():
        ext_ref[pl.ds(0, 8), :] = jnp.zeros((8, tn), F32)

    xb = x_ref[...].astype(BF16)
    slab = tn // CONV_COL_PARTS
    parts = [pl.ds(p * slab, slab) for p in range(CONV_COL_PARTS)]
    raws = [jnp.dot(xb, wb_ref[:, cols], preferred_element_type=F32) for cols in parts]
    for cols, raw in zip(parts, raws):
        ext_ref[pl.ds(8, tm), cols] = raw
        acc = cb_ref[:, cols] + cw_ref[pl.ds(0, 1), cols] * ext_ref[pl.ds(9 - taps, tm), cols]
        for kk in range(1, taps):
            acc = acc + cw_ref[pl.ds(kk, 1), cols] * ext_ref[pl.ds(9 - taps + kk, tm), cols]
        ext_ref[pl.ds(0, 8), cols] = ext_ref[pl.ds(tm, 8), cols]
        o_ref[:, cols] = _silu(acc)


def _matmul_conv_silu(x, w, conv_w, conv_b, *, layer, col0, ncols, seq_len, tn=1024):
    M, K = x.shape
    taps = conv_w.shape[0]
    tm = min(1024 if x.dtype == BF16 else 512, seq_len)
    assert M % tm == 0 and seq_len % tm == 0 and ncols % tn == 0 and col0 % tn == 0
    j0 = col0 // tn
    return pl.pallas_call(
        functools.partial(_mm_conv_silu_kernel, taps=taps, tiles_per_seq=seq_len // tm),
        grid=(ncols // tn, M // tm),
        in_specs=[pl.BlockSpec((tm, K), lambda j, i: (i, 0)),
                  _weight_spec(w, layer, K, tn, j0),
                  pl.BlockSpec((taps, tn), lambda j, i: (0, j)),
                  pl.BlockSpec((1, tn), lambda j, i: (0, j))],
        out_specs=pl.BlockSpec((tm, tn), lambda j, i: (i, j)),
        out_shape=jax.ShapeDtypeStruct((M, ncols), F32),
        scratch_shapes=[pltpu.VMEM((K, tn), BF16), pltpu.VMEM((tm + 8, tn), F32)],
        compiler_params=_cparams("parallel", "arbitrary"),
        name="proj_conv_silu",
    )(x, w, conv_w, conv_b.reshape(1, -1))


def _out_ln_kernel(y_ref, w_ref, h_ref, g_ref, b_ref, of_ref, ob_ref):
    half = y_ref.shape[0] // 2
    parts = [pl.ds(0, half), pl.ds(half, half)]
    branches = [jnp.dot(y_ref[rows, :], w_ref[...], preferred_element_type=F32) for rows in parts]
    for rows, branch in zip(parts, branches):
        out = _deepnorm_ln(h_ref[rows, :], branch, g_ref[...], b_ref[...])
        of_ref[rows, :] = out
        ob_ref[rows, :] = out.astype(BF16)


def _out_proj_ln(y, w, h, g, b):
    M, K = y.shape
    D = w.shape[1]
    tm = min(512 if K * D * w.dtype.itemsize <= VMEM_LIMIT_BYTES // 4 else 256, M)
    assert M % tm == 0
    return pl.pallas_call(
        _out_ln_kernel,
        grid=(M // tm,),
        in_specs=[pl.BlockSpec((tm, K), lambda i: (i, 0)),
                  _resident((K, D), lambda i: (0, 0)),
                  pl.BlockSpec((tm, D), lambda i: (i, 0)),
                  pl.BlockSpec((1, D), lambda i: (0, 0)),
                  pl.BlockSpec((1, D), lambda i: (0, 0))],
        out_specs=[pl.BlockSpec((tm, D), lambda i: (i, 0)),
                   pl.BlockSpec((tm, D), lambda i: (i, 0))],
        out_shape=[jax.ShapeDtypeStruct((M, D), F32), jax.ShapeDtypeStruct((M, D), BF16)],
        compiler_params=_cparams("parallel"),
        name="out_proj_ln",
    )(y, w, h, g.reshape(1, D), b.reshape(1, D))


ATTN_HEADS_PER_STEP = 4
ATTN_WIDE_BLOCKS = 2


def _attn_kernel(*refs, nparts, tq, tk, dv, head_major):
    q_refs = refs[:nparts]
    k_refs = refs[nparts:2 * nparts]
    v_ref, z_ref, o_ref, m_ref, acc_ref = refs[2 * nparts:]
    nh = ATTN_HEADS_PER_STEP
    i = pl.program_id(2)

    def head(ref, hh, rows=slice(None)):
        if head_major:
            return ref[hh, rows, :]
        w = ref.shape[-1] // nh
        return ref[rows, hh * w:(hh + 1) * w]

    def cat(pieces):
        return pieces[0] if len(pieces) == 1 else jnp.concatenate(pieces, axis=1)

    qs = [cat([head(r, hh) for r in q_refs]) for hh in range(nh)]
    m_ref[...] = jnp.full_like(m_ref, -jnp.inf)
    acc_ref[...] = jnp.zeros_like(acc_ref)

    def block(first, width, diag_offset):
        rows = pl.ds(pl.multiple_of(first * tk, tk), width)
        ones_cols = jnp.ones((width, LANES), BF16)
        scores = []
        for hh in range(nh):
            k = cat([head(r, hh, rows) for r in k_refs])
            scores.append(lax.dot_general(qs[hh], k, (((1,), (1,)), ((), ())),
                                          preferred_element_type=F32))
        probs, alphas = [], []
        for hh in range(nh):
            s = scores[hh]
            if diag_offset is not None:
                r = lax.broadcasted_iota(jnp.int32, (tq, width), 0)
                c = lax.broadcasted_iota(jnp.int32, (tq, width), 1)
                s = jnp.where(c + diag_offset <= r, s, -jnp.inf)
            m_prev = m_ref[hh]
            m_new = jnp.maximum(m_prev, jnp.max(s, axis=-1, keepdims=True))
            alpha = jnp.exp(m_prev - m_new)
            p = jnp.exp(s - jnp.concatenate([m_new] * (width // LANES), axis=1))
            m_ref[hh] = m_new
            probs.append(p.astype(BF16))
            alphas.append(jnp.concatenate([alpha] * ((dv + LANES) // LANES), axis=1))
        for hh in range(nh):
            v_aug = jnp.concatenate([head(v_ref, hh, rows), ones_cols], axis=1)
            acc_ref[hh] = alphas[hh] * acc_ref[hh] + jnp.dot(
                probs[hh], v_aug, preferred_element_type=F32)

    n_full = i * (tq // tk)
    wide = ATTN_WIDE_BLOCKS

    def wide_block(j, carry):
        block(j * wide, wide * tk, None)
        return carry

    lax.fori_loop(0, n_full // wide, wide_block, 0)
    for rem in range(wide - 1):
        @pl.when(n_full % wide > rem)
        def _(rem=rem):
            block((n_full // wide) * wide + rem, tk, None)
    for d in range(tq // tk):
        block(n_full + d, tk, d * tk)

    for hh in range(nh):
        cols = slice(hh * dv, (hh + 1) * dv)
        o = acc_ref[hh, :, :dv] / acc_ref[hh, :, dv:]
        o_ref[:, cols] = (o * _silu(z_ref[:, cols])).astype(o_ref.dtype)


def _attention(inputs, in_specs, out_spec, *, B, T, H, dv, tq, tk, nparts, head_major):
    nh = ATTN_HEADS_PER_STEP
    assert tq % tk == 0 and T % tq == 0 and H % nh == 0
    return pl.pallas_call(
        functools.partial(_attn_kernel, nparts=nparts, tq=tq, tk=tk, dv=dv,
                          head_major=head_major),
        grid=(B, H // nh, T // tq),
        in_specs=in_specs,
        out_specs=out_spec,
        out_shape=jax.ShapeDtypeStruct((B * T, H * dv), BF16),
        scratch_shapes=[pltpu.VMEM((nh, tq, LANES), F32),
                        pltpu.VMEM((nh, tq, dv + LANES), F32)],
        compiler_params=_cparams("parallel", "parallel", "arbitrary"),
        name="causal_attention",
    )(*inputs)


def _attn_tiles(T):
    tq = min(512, T)
    tk = min(512, T)
    return tq, tk


def _ssd_kernel(x_ref, bm_ref, cm_ref, z_ref, dtr_ref,
                dtb_ref, al_ref, dsk_ref, nw_ref, spread_ref,
                o_ref,
                state_ref, yz_ref):
    Q, E, P, GW, N = SSD_CHUNK, SSD_HEADS_PER_GROUP, SSD_HEADDIM, SSD_GROUP_WIDTH, SSD_STATE
    NG = SSD_GROUPS_PER_STEP
    groups = range(NG)
    c = pl.program_id(2)

    @pl.when(c == 0)
    def _():
        state_ref[...] = jnp.zeros_like(state_ref)

    row = lax.broadcasted_iota(jnp.int32, (Q, Q), 0)
    col = lax.broadcasted_iota(jnp.int32, (Q, Q), 1)
    lower = row >= col
    tri_u = (row <= col).astype(BF16)
    left = col < P

    def pieces(v):
        hi = v.astype(BF16).astype(F32)
        mid = (v - hi).astype(BF16).astype(F32)
        lo = ((v - hi) - mid).astype(BF16).astype(F32)
        return jnp.concatenate([hi, mid, lo], axis=0).astype(BF16)

    npairs = E // 2
    dt_r, acs_r = [], []
    for gg in groups:
        dt = _softplus(dtr_ref[gg] + dtb_ref[gg])
        cum3 = jnp.dot(pieces(dt * (-jnp.exp(al_ref[gg]))), tri_u, preferred_element_type=F32)
        dt_r.append(dt)
        acs_r.append(cum3[:E] + cum3[E:2 * E] + cum3[2 * E:])

    bm_b, cm_b, bmt_b, cb, carried = [], [], [], [], []
    for gg in groups:
        ncols = pl.ds(gg * N, N)
        bm = bm_ref[:, ncols]
        bm_b.append(bm.astype(BF16))
        cm_b.append(cm_ref[:, ncols].astype(BF16))
        bmt_b.append(bm.T.astype(BF16))
        cb.append(lax.dot_general(cm_b[gg], bm_b[gg], (((1,), (1,)), ((), ())),
                                  preferred_element_type=F32))
        carried.append([jnp.dot(cm_b[gg], state_ref[gg * npairs + kp].astype(BF16),
                                preferred_element_type=F32)
                        for kp in range(npairs)])

    pair_cols = lambda gg, kp: pl.ds(gg * GW + kp * LANES, LANES)
    xps = [[x_ref[:, pair_cols(gg, kp)] for kp in range(npairs)] for gg in groups]
    gates = [[_silu(z_ref[:, pair_cols(gg, kp)]) for kp in range(npairs)] for gg in groups]

    spreads = [lax.dot_general(pieces(jnp.concatenate([dt_r[gg], acs_r[gg]], axis=0)),
                               spread_ref[...], (((0,), (0,)), ((), ())),
                               preferred_element_type=F32) for gg in groups]

    for gg in groups:
        dt_ch = spreads[gg][:, :GW]
        acs_ch = spreads[gg][:, GW:2 * GW]
        acs_col = spreads[gg][:, 2 * GW:]
        ssq = jnp.zeros((Q, 1), F32)
        for kp in range(npairs):
            cols = pair_cols(gg, kp)
            xp = xps[gg][kp]
            dt_p = dt_ch[:, kp * LANES:(kp + 1) * LANES]
            acs_p = acs_ch[:, kp * LANES:(kp + 1) * LANES]
            last_p = acs_p[Q - 1:Q, :]
            xdt = xp * dt_p
            xdt_b = xdt.astype(BF16)

            ys = []
            for e in (2 * kp, 2 * kp + 1):
                seg = acs_col[:, e * Q:(e + 1) * Q] - acs_r[gg][e:e + 1, :]
                dec = jnp.exp(jnp.where(lower, seg, -jnp.inf))
                ys.append(jnp.dot((cb[gg] * dec).astype(BF16), xdt_b,
                                  preferred_element_type=F32))
            y_diag = jnp.where(left, ys[0], ys[1])

            xw = (xdt * jnp.exp(last_p - acs_p)).astype(BF16)
            s_loc = jnp.dot(bmt_b[gg], xw, preferred_element_type=F32)
            y_off = carried[gg][kp] * jnp.exp(acs_p)
            slot = gg * npairs + kp
            state_ref[slot] = state_ref[slot] * jnp.exp(last_p) + s_loc

            y = (y_diag + y_off + dsk_ref[:, cols] * xp) * gates[gg][kp]
            yz_ref[:, cols] = y
            ssq = ssq + jnp.sum(y * y, axis=-1, keepdims=True)

        gcols = pl.ds(gg * GW, GW)
        inv = lax.rsqrt(ssq * (1.0 / GW) + RMS_EPS)
        o_ref[:, gcols] = (yz_ref[:, gcols] * inv * nw_ref[:, gcols]).astype(o_ref.dtype)


def _ssd_core(z, xbc, dt, dt_bias, a_log, d_skip, norm_w, *, B, T):
    G, E, Q = SSD_GROUPS, SSD_HEADS_PER_GROUP, SSD_CHUNK
    NC = T // Q
    GW, N = SSD_GROUP_WIDTH, SSD_STATE
    dtr = jnp.transpose(dt[:, :SSD_HEADS].reshape(B, T, G, E), (0, 2, 3, 1))
    head_rows = lambda v: v.reshape(G, E, 1)
    d_ch = jnp.repeat(d_skip, SSD_HEADDIM).reshape(1, -1)
    heads = jnp.arange(E)
    per_channel = (jnp.arange(GW)[None, :] // SSD_HEADDIM == heads[:, None]).astype(BF16)
    per_head_block = (jnp.arange(E * Q)[None, :] // Q == heads[:, None]).astype(BF16)
    none = lambda n: jnp.zeros((E, n), BF16)
    spread = jnp.concatenate([
        jnp.concatenate([per_channel, none(GW), none(E * Q)], axis=1),
        jnp.concatenate([none(GW), per_channel, per_head_block], axis=1)], axis=0)
    spread = jnp.tile(spread, (3, 1))

    NG = SSD_GROUPS_PER_STEP
    XW, BW = NG * GW, NG * N
    assert G % NG == 0
    bblk = SSD_D_INNER // BW
    cblk = (SSD_D_INNER + SSD_BC_WIDTH) // BW
    rowblk = lambda b, g, c: b * NC + c
    in_specs = [
        pl.BlockSpec((Q, XW), lambda b, g, c: (rowblk(b, g, c), g)),
        pl.BlockSpec((Q, BW), lambda b, g, c: (rowblk(b, g, c), bblk + g)),
        pl.BlockSpec((Q, BW), lambda b, g, c: (rowblk(b, g, c), cblk + g)),
        pl.BlockSpec((Q, XW), lambda b, g, c: (rowblk(b, g, c), g)),
        pl.BlockSpec((None, NG, E, Q), lambda b, g, c: (b, g, 0, c)),
        pl.BlockSpec((NG, E, 1), lambda b, g, c: (g, 0, 0)),
        pl.BlockSpec((NG, E, 1), lambda b, g, c: (g, 0, 0)),
        pl.BlockSpec((1, XW), lambda b, g, c: (0, g)),
        pl.BlockSpec((1, XW), lambda b, g, c: (0, g)),
        pl.BlockSpec(spread.shape, lambda b, g, c: (0, 0)),
    ]
    return pl.pallas_call(
        _ssd_kernel,
        grid=(B, G // NG, NC),
        in_specs=in_specs,
        out_specs=pl.BlockSpec((Q, XW), lambda b, g, c: (rowblk(b, g, c), g)),
        out_shape=jax.ShapeDtypeStruct((B * T, SSD_D_INNER), BF16),
        scratch_shapes=[pltpu.VMEM((NG * E // 2, N, LANES), F32), pltpu.VMEM((Q, XW), F32)],
        compiler_params=_cparams("parallel", "parallel", "arbitrary"),
        name="ssd_chunk_scan",
    )(xbc, xbc, xbc, z, dtr,
      head_rows(dt_bias), head_rows(a_log), d_ch, norm_w.reshape(1, -1), spread)


def _pad_cols(w, n):
    return jnp.pad(w, ((0, 0), (0, n - w.shape[1])))


def _layer_of(w, layer):
    return w[layer] if w.ndim == 3 else w


def _ssd_layer(hf, hb, w_in, conv_w, conv_b, dt_bias, a_log, d_skip, norm_w, w_out, g, b,
               *, B, T, layer=0):
    di = SSD_D_INNER
    cd = di + 2 * SSD_BC_WIDTH
    z = _matmul(hb, w_in, F32, layer=layer, col0=0, ncols=di)
    xbc = _matmul_conv_silu(hb, w_in, conv_w, conv_b, layer=layer, col0=di, ncols=cd, seq_len=T)
    dt = _matmul(hb, _pad_cols(_layer_of(w_in, layer)[:, di + cd:], LANES), F32)
    y = _ssd_core(z, xbc, dt, dt_bias, a_log, d_skip, norm_w, B=B, T=T)
    return _out_proj_ln(y, w_out.astype(BF16), hf, g, b)


FOX_BIAS_PIECES = 3


def _fox_bias_kernel(f_ref, b_ref, qx_ref, kx_ref, carry_ref, *, nblk):
    blk, W, P = LANES, FOX_WIDTH, FOX_BIAS_PIECES
    row = lax.broadcasted_iota(jnp.int32, (blk, blk), 0)
    col = lax.broadcasted_iota(jnp.int32, (blk, blk), 1)
    tri = (row >= col).astype(F32)
    bias = b_ref[...]
    H = FOX_HEADS
    r = lax.broadcasted_iota(jnp.int32, (blk, W), 0)
    c = lax.broadcasted_iota(jnp.int32, (blk, W), 1)
    head, piece = r % H, r // H
    is_piece = r < P * H
    place_q = jnp.logical_and(is_piece, c == head * blk + piece).astype(BF16)
    place_k = jnp.logical_and(is_piece, c == head * blk + P + piece).astype(BF16)
    is_head = col < H
    cmod = lax.broadcasted_iota(jnp.int32, (1, W), 1) % blk
    ones_q = jnp.logical_and(cmod >= P, cmod < 2 * P).astype(F32)
    ones_k = (cmod < P).astype(F32)

    def body(t, carry):
        rows = pl.ds(pl.multiple_of(t * blk, blk), blk)
        x = f_ref[rows, :] + bias
        logf = jnp.minimum(x, 0.0) - jnp.log1p(jnp.exp(-jnp.abs(x)))
        cum = jnp.dot(tri, logf, precision=HIGHEST, preferred_element_type=F32) + carry
        hi = jnp.where(is_head, cum, 0.0).astype(BF16).astype(F32)
        r1 = jnp.where(is_head, cum, 0.0) - hi
        mid = r1.astype(BF16).astype(F32)
        lo = (r1 - mid).astype(BF16).astype(F32)
        pieces = (hi + pltpu.roll(mid, H, 1) + pltpu.roll(lo, 2 * H, 1)).astype(BF16)
        qx_ref[rows, :] = (jnp.dot(pieces, place_q, preferred_element_type=F32)
                           + ones_q).astype(BF16)
        kx_ref[rows, :] = (ones_k - jnp.dot(pieces, place_k, preferred_element_type=F32)
                           ).astype(BF16)
        return cum[blk - 1:blk, :]

    @pl.when(pl.program_id(1) == 0)
    def _():
        carry_ref[...] = jnp.zeros_like(carry_ref)

    carry_ref[...] = lax.fori_loop(0, nblk, body, carry_ref[...])


def _fox_bias_columns(f, f_bias, *, B, T):
    bias = jnp.pad(f_bias, (0, LANES - FOX_HEADS)).reshape(1, LANES)
    tt = min(512, T)
    out = jax.ShapeDtypeStruct((B, T, FOX_WIDTH), BF16)
    spec = pl.BlockSpec((None, tt, FOX_WIDTH), lambda b, t: (b, t, 0))
    qx, kx = pl.pallas_call(
        functools.partial(_fox_bias_kernel, nblk=tt // LANES),
        grid=(B, T // tt),
        in_specs=[pl.BlockSpec((None, tt, LANES), lambda b, t: (b, t, 0)),
                  pl.BlockSpec((1, LANES), lambda b, t: (0, 0))],
        out_specs=[spec, spec],
        out_shape=[out, out],
        scratch_shapes=[pltpu.VMEM((1, LANES), F32)],
        compiler_params=_cparams("parallel", "arbitrary"),
        name="fox_cum_log_forget",
    )(f.reshape(B, T, LANES), bias)
    return qx.reshape(B * T, FOX_WIDTH), kx.reshape(B * T, FOX_WIDTH)


def _fox_layer(hf, hb, w_in, f_bias, w_out, g, b, *, B, T, layer=0):
    W, H, dh = FOX_WIDTH, FOX_HEADS, FOX_HEAD_DIM
    q = _matmul(hb, w_in, BF16, layer=layer, col0=0, ncols=W, scale=dh ** -0.5)
    kv = _matmul(hb, w_in, BF16, layer=layer, col0=W, ncols=2 * W)
    z = _matmul(hb, w_in, F32, layer=layer, col0=3 * W, ncols=W)
    f = _matmul(hb, _pad_cols(_layer_of(w_in, layer)[:, 4 * W:], LANES), F32)
    qx, kx = _fox_bias_columns(f, f_bias, B=B, T=T)

    tq, tk = _attn_tiles(T)
    nq = T // tq
    hw = ATTN_HEADS_PER_STEP * dh
    q_tile = pl.BlockSpec((tq, hw), lambda bb, h, i: (bb * nq + i, h))
    seq_k = pl.BlockSpec((T, hw), lambda bb, h, i: (bb, h))
    seq_v = pl.BlockSpec((T, hw), lambda bb, h, i: (bb, W // hw + h))
    in_specs = [q_tile, q_tile, seq_k, seq_k, seq_v, q_tile]
    y = _attention((q, qx, kv, kx, kv, z), in_specs, q_tile,
                   B=B, T=T, H=H, dv=dh, tq=tq, tk=tk, nparts=2, head_major=False)
    return _out_proj_ln(y, w_out.astype(BF16), hf, g, b)


def _rope_table_kernel(pos_ref, freq_ref, sign_ref, cos_ref, sin_ref):
    ang = pos_ref[...].astype(F32) * freq_ref[...]
    cos_ref[...] = jnp.cos(ang)
    sin_ref[...] = jnp.sin(ang) * sign_ref[...]


def _rope_tables(positions, *, B, T):
    half = MLA_ROPE // 2
    inv_freq = ROPE_BASE ** (-jnp.arange(0, MLA_ROPE, 2, dtype=F32) / MLA_ROPE)
    freq = jnp.tile(inv_freq, LANES // half).reshape(1, LANES)
    sign = jnp.tile(jnp.concatenate([-jnp.ones((half,), F32), jnp.ones((half,), F32)]),
                    LANES // MLA_ROPE).reshape(1, LANES)
    M = B * T
    tm = min(1024, M)
    return pl.pallas_call(
        _rope_table_kernel,
        grid=(M // tm,),
        in_specs=[pl.BlockSpec((tm, 1), lambda i: (i, 0)),
                  pl.BlockSpec((1, LANES), lambda i: (0, 0)),
                  pl.BlockSpec((1, LANES), lambda i: (0, 0))],
        out_specs=[pl.BlockSpec((tm, LANES), lambda i: (i, 0)),
                   pl.BlockSpec((tm, LANES), lambda i: (i, 0))],
        out_shape=[jax.ShapeDtypeStruct((M, LANES), F32), jax.ShapeDtypeStruct((M, LANES), F32)],
        compiler_params=_cparams("parallel"),
        name="rope_tables",
    )(positions.reshape(M, 1), freq, sign)


def _rms_to_bf16(x, w):
    y = x * lax.rsqrt(jnp.mean(x * x, axis=-1, keepdims=True) + RMS_EPS)
    return (y * w).astype(BF16)


def _rope_pair(c2, cos, sin):
    return c2 * cos + pltpu.roll(c2, MLA_ROPE, 1) * sin


def _mla_qkv_kernel(x_ref, wl_ref, qn_ref, kn_ref, wq_ref, wkv_ref, cos_ref, sin_ref,
                    q_ref, k_ref, v_ref, *, scale):
    qr, kr = MLA_Q_RANK, MLA_KV_RANK
    cos, sin = cos_ref[...], sin_ref[...]
    lat = jnp.dot(x_ref[...], wl_ref[...], preferred_element_type=F32)
    qn = _rms_to_bf16(lat[:, :qr], qn_ref[...])
    kn = _rms_to_bf16(lat[:, qr:qr + kr], kn_ref[...])
    k_pe = _rope_pair(lat[:, qr + kr:], cos, sin)[:, :MLA_ROPE].astype(k_ref.dtype)
    for h in range(MLA_HEADS):
        r = jnp.dot(qn, wq_ref[h], preferred_element_type=F32)
        pe = _rope_pair(r[:, MLA_NOPE:], cos, sin)
        q_ref[h, :, :MLA_NOPE] = (r[:, :MLA_NOPE] * scale).astype(q_ref.dtype)
        q_ref[h, :, MLA_NOPE:] = (pe[:, :MLA_ROPE] * scale).astype(q_ref.dtype)
        r = jnp.dot(kn, wkv_ref[h], preferred_element_type=F32)
        k_ref[h, :, :MLA_NOPE] = r[:, :MLA_NOPE].astype(k_ref.dtype)
        k_ref[h, :, MLA_NOPE:] = k_pe
        v_ref[h] = r[:, MLA_NOPE:].astype(v_ref.dtype)


def _swap_halves(w):
    half = w.shape[-1] // 2
    return jnp.concatenate([w[..., half:], w[..., :half]], axis=-1)


def _mla_layer(hf, hb, positions, w_in, q_norm, kv_norm, w_q_up, w_kv_up, w_out, g, b, *, B, T):
    H, M = MLA_HEADS, B * T
    qr, kr = MLA_Q_RANK, MLA_KV_RANK
    lat_end = qr + kr + MLA_ROPE
    w_pe = w_in[:, qr + kr:lat_end]
    w_lat = jnp.concatenate([w_in[:, :qr + kr], w_pe, _swap_halves(w_pe)], axis=1).astype(BF16)
    nlat = w_lat.shape[1]
    z = _matmul(hb, w_in[:, lat_end:], F32)
    cos, sin = _rope_tables(positions, B=B, T=T)

    wq = w_q_up.reshape(qr, H, MLA_QK)
    wq = jnp.concatenate([wq, _swap_halves(wq[..., MLA_NOPE:])], axis=-1)
    wq = jnp.transpose(wq, (1, 0, 2)).astype(BF16)
    wkv = jnp.transpose(w_kv_up.reshape(kr, H, MLA_NOPE + MLA_V), (1, 0, 2)).astype(BF16)

    tm = min(256, M)
    wide = MLA_NOPE + LANES
    row = lambda i: (i, 0)
    q, k, v = pl.pallas_call(
        functools.partial(_mla_qkv_kernel, scale=MLA_QK ** -0.5),
        grid=(M // tm,),
        in_specs=[pl.BlockSpec((tm, D_MODEL), row),
                  _resident((D_MODEL, nlat), lambda i: (0, 0)),
                  pl.BlockSpec((1, qr), lambda i: (0, 0)),
                  pl.BlockSpec((1, kr), lambda i: (0, 0)),
                  _resident((H, qr, wide), lambda i: (0, 0, 0)),
                  _resident((H, kr, MLA_NOPE + MLA_V), lambda i: (0, 0, 0)),
                  pl.BlockSpec((tm, LANES), row),
                  pl.BlockSpec((tm, LANES), row)],
        out_specs=[pl.BlockSpec((H, tm, MLA_QK), lambda i: (0, i, 0)),
                   pl.BlockSpec((H, tm, MLA_QK), lambda i: (0, i, 0)),
                   pl.BlockSpec((H, tm, MLA_V), lambda i: (0, i, 0))],
        out_shape=[jax.ShapeDtypeStruct((H, M, MLA_QK), BF16),
                   jax.ShapeDtypeStruct((H, M, MLA_QK), BF16),
                   jax.ShapeDtypeStruct((H, M, MLA_V), BF16)],
        compiler_params=_cparams("parallel"),
        name="mla_qkv_rope",
    )(hb, w_lat, q_norm.reshape(1, qr), kv_norm.reshape(1, kr), wq, wkv, cos, sin)

    tq, tk = _attn_tiles(T)
    nq = T // tq
    nh = ATTN_HEADS_PER_STEP
    in_specs = [
        pl.BlockSpec((nh, tq, MLA_QK), lambda bb, h, i: (h, bb * nq + i, 0)),
        pl.BlockSpec((nh, T, MLA_QK), lambda bb, h, i: (h, bb, 0)),
        pl.BlockSpec((nh, T, MLA_V), lambda bb, h, i: (h, bb, 0)),
        pl.BlockSpec((tq, nh * MLA_V), lambda bb, h, i: (bb * nq + i, h)),
    ]
    out_spec = pl.BlockSpec((tq, nh * MLA_V), lambda bb, h, i: (bb * nq + i, h))
    y = _attention((q, k, v, z), in_specs, out_spec,
                   B=B, T=T, H=H, dv=MLA_V, tq=tq, tk=tk, nparts=1, head_major=True)
    return _out_proj_ln(y, w_out.astype(BF16), hf, g, b)


def _s5_matrices(lam_re, lam_im, log_step, b_re, b_im, c_re, c_im):
    L = S5_CHUNK
    step = jnp.exp(log_step.astype(F32))[:, None]
    mag = jnp.exp(lam_re * step)
    ar = mag * jnp.cos(lam_im * step)
    ai = mag * jnp.sin(lam_im * step)
    den = lam_re * lam_re + lam_im * lam_im
    fr = ((ar - 1.0) * lam_re + ai * lam_im) / den
    fi = (ai * lam_re - (ar - 1.0) * lam_im) / den
    bbr = fr[..., None] * b_re - fi[..., None] * b_im
    bbi = fr[..., None] * b_im + fi[..., None] * b_re
    pr, pi = [jnp.ones_like(ar)], [jnp.zeros_like(ar)]
    for _ in range(L):
        pr_next = pr[-1] * ar - pi[-1] * ai
        pi_next = pr[-1] * ai + pi[-1] * ar
        pr.append(pr_next)
        pi.append(pi_next)
    pwr = jnp.stack(pr, axis=1)
    pwi = jnp.stack(pi, axis=1)

    G, P, I = S5_GROUPS, S5_STATE, S5_GROUP
    TG, SG = S5_TOEP_GROUPS, S5_STATE_GROUPS
    NTB, NSB = G // TG, G // SG

    c_ji = lambda c: jnp.tile(jnp.transpose(c, (0, 2, 1)), (1, 1, I))
    b_ji = lambda v: jnp.repeat(v, I, axis=2)
    cbr = c_ji(c_re) * b_ji(bbr) - c_ji(c_im) * b_ji(bbi)
    cbi = c_ji(c_re) * b_ji(bbi) + c_ji(c_im) * b_ji(bbr)
    kern = jnp.einsum('gdp,gpq->gdq', jnp.concatenate([pwr[:, :L], -pwi[:, :L]], axis=2),
                      jnp.concatenate([cbr, cbi], axis=1), precision=HIGHEST)
    idx = jnp.arange

    kc = jnp.transpose(kern.reshape(NTB, TG, L, I, I), (2, 0, 1, 3, 4)).reshape(L, NTB, TG * I, I)
    kc = jnp.pad(kc, ((L - 1, 0), (0, 0), (0, 0), (0, 0)))
    repeat_i = (idx(TG * I)[None, :] % I == idx(I)[:, None]).astype(F32)
    same_t = (idx(TG * I)[:, None] // I == idx(TG * I)[None, :] // I)
    kd = jnp.where(same_t, jnp.einsum('dcri,iq->dcrq', kc, repeat_i, precision=HIGHEST), 0.0)

    lanes = lambda v: jnp.tile(v, (1, 1, SG))
    by_step = lambda w: jnp.transpose(
        lanes(w[:, :L][:, ::-1]).reshape(NSB, SG, L, SG * P), (2, 0, 1, 3))[:, :, :, None]
    by_chan = lambda v: lanes(jnp.transpose(v, (0, 2, 1))).reshape(1, NSB, SG, I, SG * P)
    qr, qi = by_step(pwr), by_step(pwi)
    br, bi = by_chan(bbr), by_chan(bbi)
    bd = jnp.concatenate([qr * br - qi * bi, qr * bi + qi * br], axis=-1)
    bd = bd.reshape(L, NSB, SG * I, 2 * SG * P)
    same_b = (idx(SG * I)[:, None] // I == (idx(2 * SG * P)[None, :] % (SG * P)) // P)
    bd = jnp.where(same_b, bd, 0.0)

    chan = lambda c: jnp.transpose(c.reshape(NSB, SG, I, P), (0, 3, 1, 2)).reshape(1, NSB, P, SG * I)
    step_pow = lambda w: jnp.repeat(
        jnp.transpose(w[:, 1:].reshape(NSB, SG, L, P), (2, 0, 3, 1)), I, axis=3)
    cr, ci, wr, wi = chan(c_re), chan(c_im), step_pow(pwr), step_pow(pwi)
    cp = jnp.stack([cr * wr - ci * wi, -(cr * wi + ci * wr)], axis=2)
    cd = jnp.broadcast_to(cp[:, :, :, None], (L, NSB, 2, SG, P, SG * I))
    cd = cd.reshape(L, NSB, 2 * SG * P, SG * I)
    same_c = ((idx(2 * SG * P)[:, None] % (SG * P)) // P == idx(SG * I)[None, :] // I)
    cd = jnp.where(same_c, cd, 0.0)

    alr = pwr[:, L].reshape(NSB, SG * P)
    ali = pwi[:, L].reshape(NSB, SG * P)
    a1 = jnp.concatenate([alr, alr], axis=1).reshape(1, -1)
    a2 = jnp.concatenate([-ali, ali], axis=1).reshape(1, -1)
    return kd.astype(BF16), bd.astype(BF16), cd.astype(BF16), a1, a2


def _chunk_step(u_ref, s):
    return u_ref[pl.ds(s, u_ref.shape[0] // S5_CHUNK, stride=S5_CHUNK), :]


def _s5_local_state_kernel(u_ref, bd_ref, s_ref):
    acc = jnp.dot(_chunk_step(u_ref, 0).astype(BF16), bd_ref[0], preferred_element_type=F32)
    for s in range(1, S5_CHUNK):
        acc = acc + jnp.dot(_chunk_step(u_ref, s).astype(BF16), bd_ref[s],
                            preferred_element_type=F32)
    s_ref[...] = acc


def _s5_scan_kernel(s_ref, a1_ref, a2_ref, o_ref, *, B, nch):
    W = s_ref.shape[1]
    a1 = a1_ref[...]
    a2 = a2_ref[...]
    unit = 2 * S5_STATE_GROUPS * S5_STATE

    def swap_re_im(c):
        parts = []
        for q in range(W // unit):
            lo = q * unit
            parts += [c[:, lo + unit // 2:lo + unit], c[:, lo:lo + unit // 2]]
        return jnp.concatenate(parts, axis=1)

    def body(n, carry):
        new = []
        for bb in range(B):
            row = bb * nch + n
            c = carry[bb]
            o_ref[pl.ds(row, 1), :] = c
            new.append(a1 * c + a2 * swap_re_im(c) + s_ref[pl.ds(row, 1), :])
        return tuple(new)

    lax.fori_loop(0, nch, body, tuple(jnp.zeros((1, W), F32) for _ in range(B)))


def _gelu_tanh(y):
    return 0.5 * y * (1.0 + jnp.tanh(math.sqrt(2.0 / math.pi) * (y + 0.044715 * (y * y * y))))


def _s5_out_kernel(*refs):
    L = S5_CHUNK
    nu = S5_TOEP_GROUPS // S5_STATE_GROUPS
    u_refs, (kd_ref, st_ref, cd_ref, o_ref, ub_ref, acc_ref) = refs[:nu], refs[nu:]
    r = pl.program_id(1)

    @pl.when(r == 0)
    def _():
        for s in range(L):
            for k in range(nu):
                ub_ref[s, :, k * LANES:(k + 1) * LANES] = _chunk_step(u_refs[k], s).astype(BF16)

    def lags(first):
        out = None
        for s in range(first, first + S5_STEP_GROUP):
            d = jnp.dot(ub_ref[s], kd_ref[r - s + (L - 1)], preferred_element_type=F32)
            out = d if out is None else out + d
        return out

    st = st_ref[...].astype(BF16)
    half = st.shape[1] // 2
    carried = jnp.concatenate(
        [jnp.dot(st[:, :half], cd_ref[0], preferred_element_type=F32),
         jnp.dot(st[:, half:], cd_ref[1], preferred_element_type=F32)], axis=1)
    acc_ref[...] = carried + lags(0)
    for first in range(S5_STEP_GROUP, L, S5_STEP_GROUP):
        @pl.when(first <= r)
        def _(first=first):
            acc_ref[...] += lags(first)
    o_ref[...] = acc_ref[...].astype(o_ref.dtype)


def _s5_glu_out_kernel(ys_ref, u_ref, d_ref, wg_ref, bg_ref, z_ref, wo_ref, h_ref, g_ref, b_ref,
                       of_ref, ob_ref, yt_ref):
    L, W = S5_CHUNK, S5_WIDTH
    nslab, tm = yt_ref.shape[0], yt_ref.shape[1]
    for s in range(L):
        for k in range(nslab):
            lo = s * W + k * LANES
            yt_ref[k, pl.ds(s, tm // L, stride=L), :] = ys_ref[:, lo:lo + LANES].astype(F32)
    half = tm // 2
    parts = [pl.ds(0, half), pl.ds(half, half)]
    ys = [_gelu_tanh(jnp.concatenate([yt_ref[k, rows, :] for k in range(nslab)], axis=1)
                     + d_ref[...] * u_ref[rows, :]) for rows in parts]
    ts = [jnp.dot(y.astype(BF16), wg_ref[...], preferred_element_type=F32) for y in ys]
    gated = [(y * _sigmoid(t + bg_ref[...]) * _silu(z_ref[rows, :])).astype(BF16)
             for y, t, rows in zip(ys, ts, parts)]
    branches = [jnp.dot(y, wo_ref[...], preferred_element_type=F32) for y in gated]
    for rows, branch in zip(parts, branches):
        out = _deepnorm_ln(h_ref[rows, :], branch, g_ref[...], b_ref[...])
        of_ref[rows, :] = out
        ob_ref[rows, :] = out.astype(BF16)


def _s5_layer(hf, hb, w_in, lam_re, lam_im, log_step, b_re, b_im, c_re, c_im, d_skip,
              w_glu, b_glu, w_out, g, b, *, B, T, layer=0):
    M, W, G, L = B * T, S5_WIDTH, S5_GROUPS, S5_CHUNK
    nch = T // L
    rows = B * nch
    u = _matmul(hb, w_in, F32, layer=layer, col0=0, ncols=W)
    z = _matmul(hb, w_in, F32, layer=layer, col0=W, ncols=W)
    kd, bd, cd, a1, a2 = _s5_matrices(lam_re, lam_im, log_step, b_re, b_im, c_re, c_im)

    TG, SG = S5_TOEP_GROUPS, S5_STATE_GROUPS
    ntb, nsb = G // TG, G // SG
    tw, sw_in, sw = TG * S5_GROUP, SG * S5_GROUP, 2 * SG * S5_STATE
    nstate = nsb * sw
    s_loc = pl.pallas_call(
        _s5_local_state_kernel,
        grid=(nsb,),
        in_specs=[pl.BlockSpec((M, sw_in), lambda cb: (0, cb)),
                  pl.BlockSpec((L, None, sw_in, sw), lambda cb: (0, cb, 0, 0))],
        out_specs=pl.BlockSpec((rows, sw), lambda cb: (0, cb)),
        out_shape=jax.ShapeDtypeStruct((rows, nstate), F32),
        compiler_params=_cparams("parallel"),
        name="s5_chunk_state",
    )(u, bd)

    scan_w = 2 * sw
    st_in = pl.pallas_call(
        functools.partial(_s5_scan_kernel, B=B, nch=nch),
        grid=(nstate // scan_w,),
        in_specs=[pl.BlockSpec((rows, scan_w), lambda p: (0, p)),
                  pl.BlockSpec((1, scan_w), lambda p: (0, p)),
                  pl.BlockSpec((1, scan_w), lambda p: (0, p))],
        out_specs=pl.BlockSpec((rows, scan_w), lambda p: (0, p)),
        out_shape=jax.ShapeDtypeStruct((rows, nstate), F32),
        compiler_params=_cparams("parallel"),
        name="s5_state_scan",
    )(s_loc, a1, a2)

    per_toep = TG // SG
    y = pl.pallas_call(
        _s5_out_kernel,
        grid=(ntb, L),
        in_specs=[pl.BlockSpec((M, sw_in), lambda cb, r, k=k: (0, per_toep * cb + k))
                  for k in range(per_toep)]
        + [pl.BlockSpec((2 * L - 1, None, tw, tw), lambda cb, r: (0, cb, 0, 0)),
                  pl.BlockSpec((rows, per_toep * sw), lambda cb, r: (0, cb)),
                  pl.BlockSpec((None, per_toep, sw, sw_in), lambda cb, r: (r, cb, 0, 0))],
        out_specs=pl.BlockSpec((rows, tw), lambda cb, r: (0, r * ntb + cb)),
        out_shape=jax.ShapeDtypeStruct((rows, L * W), BF16),
        scratch_shapes=[pltpu.VMEM((L, rows, tw), BF16), pltpu.VMEM((rows, tw), F32)],
        compiler_params=_cparams("parallel", "arbitrary"),
        name="s5_chunk_output",
    )(*([u] * per_toep), kd, st_in, cd)

    tm = min(256, M)
    row = lambda i: (i, 0)
    fixed = lambda i: (0, 0)
    return pl.pallas_call(
        _s5_glu_out_kernel,
        grid=(M // tm,),
        in_specs=[pl.BlockSpec((tm // L, L * W), row),
                  pl.BlockSpec((tm, W), row),
                  pl.BlockSpec((1, W), fixed),
                  _resident((W, W), fixed),
                  pl.BlockSpec((1, W), fixed),
                  pl.BlockSpec((tm, W), row),
                  _resident((W, D_MODEL), fixed),
                  pl.BlockSpec((tm, D_MODEL), row),
                  pl.BlockSpec((1, D_MODEL), fixed),
                  pl.BlockSpec((1, D_MODEL), fixed)],
        out_specs=[pl.BlockSpec((tm, D_MODEL), row), pl.BlockSpec((tm, D_MODEL), row)],
        out_shape=[jax.ShapeDtypeStruct((M, D_MODEL), F32),
                   jax.ShapeDtypeStruct((M, D_MODEL), BF16)],
        scratch_shapes=[pltpu.VMEM((W // LANES, tm, LANES), F32)],
        compiler_params=_cparams("parallel"),
        name="s5_glu_out_ln",
    )(y, u, d_skip.reshape(1, W), w_glu.astype(BF16), b_glu.reshape(1, W), z,
      w_out.astype(BF16), hf, g.reshape(1, D_MODEL), b.reshape(1, D_MODEL))


def kernel(x, positions, ln_g, ln_b, ssd_w_in, ssd_conv_w, ssd_conv_b, ssd_dt_bias, ssd_a_log, ssd_d, ssd_norm_w, ssd_w_out, fox_w_in, fox_f_bias, fox_w_out, mla_w_in, mla_q_norm, mla_kv_norm, mla_w_q_up, mla_w_kv_up, mla_w_out, s5_w_in, s5_lambda_re, s5_lambda_im, s5_log_step, s5_b_re, s5_b_im, s5_c_re, s5_c_im, s5_d, s5_w_glu, s5_b_glu, s5_w_out):
    B, T, D = x.shape
    hf = x.reshape(B * T, D)
    hb = hf.astype(BF16)
    for i in range(DEPTH):
        j = i // 4
        kind = i % 4
        g, b = ln_g[i], ln_b[i]
        if kind == 0:
            hf, hb = _ssd_layer(hf, hb, ssd_w_in, ssd_conv_w[j], ssd_conv_b[j], ssd_dt_bias[j],
                                ssd_a_log[j], ssd_d[j], ssd_norm_w[j], ssd_w_out[j], g, b,
                                B=B, T=T, layer=j)
        elif kind == 1:
            hf, hb = _fox_layer(hf, hb, fox_w_in, fox_f_bias[j], fox_w_out[j], g, b,
                                B=B, T=T, layer=j)
        elif kind == 2:
            hf, hb = _mla_layer(hf, hb, positions, mla_w_in[j], mla_q_norm[j],
                                mla_kv_norm[j], mla_w_q_up[j], mla_w_kv_up[j], mla_w_out[j],
                                g, b, B=B, T=T)
        else:
            hf, hb = _s5_layer(hf, hb, s5_w_in, s5_lambda_re[j], s5_lambda_im[j], s5_log_step[j],
                               s5_b_re[j], s5_b_im[j], s5_c_re[j], s5_c_im[j], s5_d[j],
                               s5_w_glu[j], s5_b_glu[j], s5_w_out[j], g, b, B=B, T=T, layer=j)
    return hf.reshape(B, T, D)
```

```python
import functools
import math

import jax
import jax.numpy as jnp
from jax import lax
from jax.experimental import pallas as pl
from jax.experimental.pallas import tpu as pltpu

F32 = jnp.float32
BF16 = jnp.bfloat16
HIGHEST = lax.Precision.HIGHEST

D_MODEL = 2048
DEPTH = 4
ALPHA = (2.0 * DEPTH) ** 0.25
LN_EPS = 1e-5
RMS_EPS = 1e-6

SSD_D_INNER = 4096
SSD_HEADS = 64
SSD_HEADDIM = 64
SSD_GROUPS = 8
SSD_HEADS_PER_GROUP = SSD_HEADS // SSD_GROUPS
SSD_STATE = 128
SSD_CONV = 4
SSD_CHUNK = 128
SSD_GROUP_WIDTH = SSD_D_INNER // SSD_GROUPS
SSD_BC_WIDTH = SSD_GROUPS * SSD_STATE
SSD_GROUPS_PER_STEP = 4

FOX_HEADS = 16
FOX_HEAD_DIM = 128
FOX_WIDTH = FOX_HEADS * FOX_HEAD_DIM

MLA_HEADS = 16
MLA_Q_RANK = 512
MLA_KV_RANK = 512
MLA_NOPE = 128
MLA_ROPE = 64
MLA_V = 128
MLA_QK = MLA_NOPE + MLA_ROPE
MLA_WIDTH = MLA_HEADS * MLA_V
ROPE_BASE = 10000.0

S5_WIDTH = D_MODEL
S5_GROUP = 16
S5_GROUPS = S5_WIDTH // S5_GROUP
S5_STATE = 64
S5_CHUNK = 8
S5_TOEP_GROUPS = 16
S5_STATE_GROUPS = 8
S5_STEP_GROUP = 4

LANES = 128
VMEM_LIMIT_BYTES = 48 * 1024 * 1024


def _cparams(*sem):
    return pltpu.CompilerParams(dimension_semantics=sem, vmem_limit_bytes=VMEM_LIMIT_BYTES)


def _resident(block_shape, index_map):
    return pl.BlockSpec(block_shape, index_map, pipeline_mode=pl.Buffered(1))


def _sigmoid(x):
    return 0.5 * (jnp.tanh(0.5 * x) + 1.0)


def _silu(x):
    h = 0.5 * x
    return h + h * jnp.tanh(h)


def _softplus(x):
    return jnp.maximum(x, 0.0) + jnp.log1p(jnp.exp(-jnp.abs(x)))


def _deepnorm_ln(h, branch, g, b):
    r = ALPHA * h + branch
    mu = jnp.mean(r, axis=-1, keepdims=True)
    d = r - mu
    var = jnp.mean(d * d, axis=-1, keepdims=True)
    return d * lax.rsqrt(var + LN_EPS) * g + b


def _project(x, wb, wt):
    dims = (((1,), (1,)), ((), ())) if wt else (((1,), (0,)), ((), ()))
    return lax.dot_general(x, wb, dims, preferred_element_type=F32)


def _mm_kernel(x_ref, w_ref, o_ref, wb_ref, *, scale, wt):
    @pl.when(pl.program_id(1) == 0)
    def _():
        wb_ref[...] = w_ref[...].astype(BF16)

    acc = _project(x_ref[...].astype(BF16), wb_ref[...], wt)
    if scale is not None:
        acc = acc * scale
    o_ref[...] = acc.astype(o_ref.dtype)


def _weight_spec(w, layer, K, tn, j0, wt):
    block = (tn, K) if wt else (K, tn)
    index = (lambda j: (j0 + j, 0)) if wt else (lambda j: (0, j0 + j))
    if w.ndim == 2:
        return pl.BlockSpec(block, lambda j, i: index(j))
    return pl.BlockSpec((None,) + block, lambda j, i: (layer,) + index(j))


def _matmul(x, w, out_dtype, *, layer=0, col0=0, ncols=None, scale=None, tn=1024, wt=False):
    M, K = x.shape
    N = w.shape[-2 if wt else -1] - col0 if ncols is None else ncols
    tm = min(1024 if x.dtype == BF16 else 512, M)
    tn = min(tn, N)
    assert M % tm == 0 and N % tn == 0 and col0 % tn == 0, (M, N, col0, tm, tn)
    j0 = col0 // tn
    return pl.pallas_call(
        functools.partial(_mm_kernel, scale=scale, wt=wt),
        grid=(N // tn, M // tm),
        in_specs=[pl.BlockSpec((tm, K), lambda j, i: (i, 0)),
                  _weight_spec(w, layer, K, tn, j0, wt)],
        out_specs=pl.BlockSpec((tm, tn), lambda j, i: (i, j)),
        out_shape=jax.ShapeDtypeStruct((M, N), out_dtype),
        scratch_shapes=[pltpu.VMEM((tn, K) if wt else (K, tn), BF16)],
        compiler_params=_cparams("parallel", "arbitrary"),
        name="proj",
    )(x, w)


CONV_COL_PARTS = 4


def _mm_conv_silu_kernel(x_ref, w_ref, cw_ref, cb_ref, o_ref, wb_ref, ext_ref,
                         *, taps, tiles_per_seq, wt):
    i = pl.program_id(1)
    tm, tn = o_ref.shape

    @pl.when(i == 0)
    def _():
        wb_ref[...] = w_ref[...].astype(BF16)

    @pl.when(i % tiles_per_seq == 0)
    def _():
        ext_ref[pl.ds(0, 8), :] = jnp.zeros((8, tn), F32)

    xb = x_ref[...].astype(BF16)
    slab = tn // CONV_COL_PARTS
    parts = [pl.ds(p * slab, slab) for p in range(CONV_COL_PARTS)]
    raws = [_project(xb, wb_ref[cols, :] if wt else wb_ref[:, cols], wt) for cols in parts]
    for cols, raw in zip(parts, raws):
        ext_ref[pl.ds(8, tm), cols] = raw
        acc = cb_ref[:, cols] + cw_ref[pl.ds(0, 1), cols] * ext_ref[pl.ds(9 - taps, tm), cols]
        for kk in range(1, taps):
            acc = acc + cw_ref[pl.ds(kk, 1), cols] * ext_ref[pl.ds(9 - taps + kk, tm), cols]
        ext_ref[pl.ds(0, 8), cols] = ext_ref[pl.ds(tm, 8), cols]
        o_ref[:, cols] = _silu(acc)


def _matmul_conv_silu(x, w, conv_w, conv_b, *, layer, col0, ncols, seq_len, tn=1024, wt=False):
    M, K = x.shape
    taps = conv_w.shape[0]
    tm = min(1024 if x.dtype == BF16 else 512, seq_len)
    assert M % tm == 0 and seq_len % tm == 0 and ncols % tn == 0 and col0 % tn == 0
    j0 = col0 // tn
    return pl.pallas_call(
        functools.partial(_mm_conv_silu_kernel, taps=taps, tiles_per_seq=seq_len // tm, wt=wt),
        grid=(ncols // tn, M // tm),
        in_specs=[pl.BlockSpec((tm, K), lambda j, i: (i, 0)),
                  _weight_spec(w, layer, K, tn, j0, wt),
                  pl.BlockSpec((taps, tn), lambda j, i: (0, j)),
                  pl.BlockSpec((1, tn), lambda j, i: (0, j))],
        out_specs=pl.BlockSpec((tm, tn), lambda j, i: (i, j)),
        out_shape=jax.ShapeDtypeStruct((M, ncols), F32),
        scratch_shapes=[pltpu.VMEM((tn, K) if wt else (K, tn), BF16),
                        pltpu.VMEM((tm + 8, tn), F32)],
        compiler_params=_cparams("parallel", "arbitrary"),
        name="proj_conv_silu",
    )(x, w, conv_w, conv_b.reshape(1, -1))


def _out_ln_kernel(y_ref, w_ref, h_ref, g_ref, b_ref, of_ref, ob_ref):
    half = y_ref.shape[0] // 2
    parts = [pl.ds(0, half), pl.ds(half, half)]
    branches = [jnp.dot(y_ref[rows, :], w_ref[...], preferred_element_type=F32) for rows in parts]
    for rows, branch in zip(parts, branches):
        out = _deepnorm_ln(h_ref[rows, :], branch, g_ref[...], b_ref[...])
        of_ref[rows, :] = out
        ob_ref[rows, :] = out.astype(BF16)


def _out_proj_ln(y, w, h, g, b):
    M, K = y.shape
    D = w.shape[1]
    tm = min(512 if K * D * w.dtype.itemsize <= VMEM_LIMIT_BYTES // 4 else 256, M)
    assert M % tm == 0
    return pl.pallas_call(
        _out_ln_kernel,
        grid=(M // tm,),
        in_specs=[pl.BlockSpec((tm, K), lambda i: (i, 0)),
                  _resident((K, D), lambda i: (0, 0)),
                  pl.BlockSpec((tm, D), lambda i: (i, 0)),
                  pl.BlockSpec((1, D), lambda i: (0, 0)),
                  pl.BlockSpec((1, D), lambda i: (0, 0))],
        out_specs=[pl.BlockSpec((tm, D), lambda i: (i, 0)),
                   pl.BlockSpec((tm, D), lambda i: (i, 0))],
        out_shape=[jax.ShapeDtypeStruct((M, D), F32), jax.ShapeDtypeStruct((M, D), BF16)],
        compiler_params=_cparams("parallel"),
        name="out_proj_ln",
    )(y, w, h, g.reshape(1, D), b.reshape(1, D))


ATTN_HEADS_PER_STEP = 4
ATTN_WIDE_BLOCKS = 2


def _attn_kernel(*refs, nparts, tq, tk, dv, head_major):
    q_refs = refs[:nparts]
    k_refs = refs[nparts:2 * nparts]
    v_ref, z_ref, o_ref, m_ref, acc_ref = refs[2 * nparts:]
    nh = ATTN_HEADS_PER_STEP
    i = pl.program_id(2)

    def head(ref, hh, rows=slice(None)):
        if head_major:
            return ref[hh, rows, :]
        w = ref.shape[-1] // nh
        return ref[rows, hh * w:(hh + 1) * w]

    def cat(pieces):
        return pieces[0] if len(pieces) == 1 else jnp.concatenate(pieces, axis=1)

    qs = [cat([head(r, hh) for r in q_refs]) for hh in range(nh)]
    m_ref[...] = jnp.full_like(m_ref, -jnp.inf)
    acc_ref[...] = jnp.zeros_like(acc_ref)

    def block(first, width, diag_offset):
        rows = pl.ds(pl.multiple_of(first * tk, tk), width)
        ones_cols = jnp.ones((width, LANES), BF16)
        scores = []
        for hh in range(nh):
            k = cat([head(r, hh, rows) for r in k_refs])
            scores.append(lax.dot_general(qs[hh], k, (((1,), (1,)), ((), ())),
                                          preferred_element_type=F32))
        probs, alphas = [], []
        for hh in range(nh):
            s = scores[hh]
            if diag_offset is not None:
                r = lax.broadcasted_iota(jnp.int32, (tq, width), 0)
                c = lax.broadcasted_iota(jnp.int32, (tq, width), 1)
                s = jnp.where(c + diag_offset <= r, s, -jnp.inf)
            m_prev = m_ref[hh]
            m_new = jnp.maximum(m_prev, jnp.max(s, axis=-1, keepdims=True))
            alpha = jnp.exp(m_prev - m_new)
            p = jnp.exp(s - jnp.concatenate([m_new] * (width // LANES), axis=1))
            m_ref[hh] = m_new
            probs.append(p.astype(BF16))
            alphas.append(jnp.concatenate([alpha] * ((dv + LANES) // LANES), axis=1))
        for hh in range(nh):
            v_aug = jnp.concatenate([head(v_ref, hh, rows), ones_cols], axis=1)
            acc_ref[hh] = alphas[hh] * acc_ref[hh] + jnp.dot(
                probs[hh], v_aug, preferred_element_type=F32)

    n_full = i * (tq // tk)
    wide = ATTN_WIDE_BLOCKS

    def wide_block(j, carry):
        block(j * wide, wide * tk, None)
        return carry

    lax.fori_loop(0, n_full // wide, wide_block, 0)
    for rem in range(wide - 1):
        @pl.when(n_full % wide > rem)
        def _(rem=rem):
            block((n_full // wide) * wide + rem, tk, None)
    for d in range(tq // tk):
        block(n_full + d, tk, d * tk)

    for hh in range(nh):
        cols = slice(hh * dv, (hh + 1) * dv)
        o = acc_ref[hh, :, :dv] / acc_ref[hh, :, dv:]
        o_ref[:, cols] = (o * _silu(z_ref[:, cols])).astype(o_ref.dtype)


def _attention(inputs, in_specs, out_spec, *, B, T, H, dv, tq, tk, nparts, head_major):
    nh = ATTN_HEADS_PER_STEP
    assert tq % tk == 0 and T % tq == 0 and H % nh == 0
    return pl.pallas_call(
        functools.partial(_attn_kernel, nparts=nparts, tq=tq, tk=tk, dv=dv,
                          head_major=head_major),
        grid=(B, H // nh, T // tq),
        in_specs=in_specs,
        out_specs=out_spec,
        out_shape=jax.ShapeDtypeStruct((B * T, H * dv), BF16),
        scratch_shapes=[pltpu.VMEM((nh, tq, LANES), F32),
                        pltpu.VMEM((nh, tq, dv + LANES), F32)],
        compiler_params=_cparams("parallel", "parallel", "arbitrary"),
        name="causal_attention",
    )(*inputs)


def _attn_tiles(T):
    tq = min(512, T)
    tk = min(512, T)
    return tq, tk


def _ssd_kernel(x_ref, bm_ref, cm_ref, z_ref, dtr_ref,
                dtb_ref, al_ref, dsk_ref, nw_ref, spread_ref,
                o_ref,
                state_ref, yz_ref):
    Q, E, P, GW, N = SSD_CHUNK, SSD_HEADS_PER_GROUP, SSD_HEADDIM, SSD_GROUP_WIDTH, SSD_STATE
    NG = SSD_GROUPS_PER_STEP
    groups = range(NG)
    c = pl.program_id(2)

    @pl.when(c == 0)
    def _():
        state_ref[...] = jnp.zeros_like(state_ref)

    row = lax.broadcasted_iota(jnp.int32, (Q, Q), 0)
    col = lax.broadcasted_iota(jnp.int32, (Q, Q), 1)
    lower = row >= col
    tri_u = (row <= col).astype(BF16)
    left = col < P

    def pieces(v):
        hi = v.astype(BF16).astype(F32)
        mid = (v - hi).astype(BF16).astype(F32)
        lo = ((v - hi) - mid).astype(BF16).astype(F32)
        return jnp.concatenate([hi, mid, lo], axis=0).astype(BF16)

    npairs = E // 2
    dt_r, acs_r = [], []
    for gg in groups:
        dt = _softplus(dtr_ref[gg] + dtb_ref[gg])
        cum3 = jnp.dot(pieces(dt * (-jnp.exp(al_ref[gg]))), tri_u, preferred_element_type=F32)
        dt_r.append(dt)
        acs_r.append(cum3[:E] + cum3[E:2 * E] + cum3[2 * E:])

    bm_b, cm_b, bmt_b, cb, carried = [], [], [], [], []
    for gg in groups:
        ncols = pl.ds(gg * N, N)
        bm = bm_ref[:, ncols]
        bm_b.append(bm.astype(BF16))
        cm_b.append(cm_ref[:, ncols].astype(BF16))
        bmt_b.append(bm.T.astype(BF16))
        cb.append(lax.dot_general(cm_b[gg], bm_b[gg], (((1,), (1,)), ((), ())),
                                  preferred_element_type=F32))
        carried.append([jnp.dot(cm_b[gg], state_ref[gg * npairs + kp].astype(BF16),
                                preferred_element_type=F32)
                        for kp in range(npairs)])

    pair_cols = lambda gg, kp: pl.ds(gg * GW + kp * LANES, LANES)
    xps = [[x_ref[:, pair_cols(gg, kp)] for kp in range(npairs)] for gg in groups]
    gates = [[_silu(z_ref[:, pair_cols(gg, kp)]) for kp in range(npairs)] for gg in groups]

    spreads = [lax.dot_general(pieces(jnp.concatenate([dt_r[gg], acs_r[gg]], axis=0)),
                               spread_ref[...], (((0,), (0,)), ((), ())),
                               preferred_element_type=F32) for gg in groups]

    for gg in groups:
        dt_ch = spreads[gg][:, :GW]
        acs_ch = spreads[gg][:, GW:2 * GW]
        acs_col = spreads[gg][:, 2 * GW:]
        ssq = jnp.zeros((Q, 1), F32)
        for kp in range(npairs):
            cols = pair_cols(gg, kp)
            xp = xps[gg][kp]
            dt_p = dt_ch[:, kp * LANES:(kp + 1) * LANES]
            acs_p = acs_ch[:, kp * LANES:(kp + 1) * LANES]
            last_p = acs_p[Q - 1:Q, :]
            xdt = xp * dt_p
            xdt_b = xdt.astype(BF16)

            ys = []
            for e in (2 * kp, 2 * kp + 1):
                seg = acs_col[:, e * Q:(e + 1) * Q] - acs_r[gg][e:e + 1, :]
                dec = jnp.exp(jnp.where(lower, seg, -jnp.inf))
                ys.append(jnp.dot((cb[gg] * dec).astype(BF16), xdt_b,
                                  preferred_element_type=F32))
            y_diag = jnp.where(left, ys[0], ys[1])

            xw = (xdt * jnp.exp(last_p - acs_p)).astype(BF16)
            s_loc = jnp.dot(bmt_b[gg], xw, preferred_element_type=F32)
            y_off = carried[gg][kp] * jnp.exp(acs_p)
            slot = gg * npairs + kp
            state_ref[slot] = state_ref[slot] * jnp.exp(last_p) + s_loc

            y = (y_diag + y_off + dsk_ref[:, cols] * xp) * gates[gg][kp]
            yz_ref[:, cols] = y
            ssq = ssq + jnp.sum(y * y, axis=-1, keepdims=True)

        gcols = pl.ds(gg * GW, GW)
        inv = lax.rsqrt(ssq * (1.0 / GW) + RMS_EPS)
        o_ref[:, gcols] = (yz_ref[:, gcols] * inv * nw_ref[:, gcols]).astype(o_ref.dtype)


def _ssd_core(z, xbc, dt, dt_bias, a_log, d_skip, norm_w, *, B, T):
    G, E, Q = SSD_GROUPS, SSD_HEADS_PER_GROUP, SSD_CHUNK
    NC = T // Q
    GW, N = SSD_GROUP_WIDTH, SSD_STATE
    dtr = jnp.transpose(dt[:, :SSD_HEADS].reshape(B, T, G, E), (0, 2, 3, 1))
    head_rows = lambda v: v.reshape(G, E, 1)
    d_ch = jnp.repeat(d_skip, SSD_HEADDIM).reshape(1, -1)
    heads = jnp.arange(E)
    per_channel = (jnp.arange(GW)[None, :] // SSD_HEADDIM == heads[:, None]).astype(BF16)
    per_head_block = (jnp.arange(E * Q)[None, :] // Q == heads[:, None]).astype(BF16)
    none = lambda n: jnp.zeros((E, n), BF16)
    spread = jnp.concatenate([
        jnp.concatenate([per_channel, none(GW), none(E * Q)], axis=1),
        jnp.concatenate([none(GW), per_channel, per_head_block], axis=1)], axis=0)
    spread = jnp.tile(spread, (3, 1))

    NG = SSD_GROUPS_PER_STEP
    XW, BW = NG * GW, NG * N
    assert G % NG == 0
    bblk = SSD_D_INNER // BW
    cblk = (SSD_D_INNER + SSD_BC_WIDTH) // BW
    rowblk = lambda b, g, c: b * NC + c
    in_specs = [
        pl.BlockSpec((Q, XW), lambda b, g, c: (rowblk(b, g, c), g)),
        pl.BlockSpec((Q, BW), lambda b, g, c: (rowblk(b, g, c), bblk + g)),
        pl.BlockSpec((Q, BW), lambda b, g, c: (rowblk(b, g, c), cblk + g)),
        pl.BlockSpec((Q, XW), lambda b, g, c: (rowblk(b, g, c), g)),
        pl.BlockSpec((None, NG, E, Q), lambda b, g, c: (b, g, 0, c)),
        pl.BlockSpec((NG, E, 1), lambda b, g, c: (g, 0, 0)),
        pl.BlockSpec((NG, E, 1), lambda b, g, c: (g, 0, 0)),
        pl.BlockSpec((1, XW), lambda b, g, c: (0, g)),
        pl.BlockSpec((1, XW), lambda b, g, c: (0, g)),
        pl.BlockSpec(spread.shape, lambda b, g, c: (0, 0)),
    ]
    return pl.pallas_call(
        _ssd_kernel,
        grid=(B, G // NG, NC),
        in_specs=in_specs,
        out_specs=pl.BlockSpec((Q, XW), lambda b, g, c: (rowblk(b, g, c), g)),
        out_shape=jax.ShapeDtypeStruct((B * T, SSD_D_INNER), BF16),
        scratch_shapes=[pltpu.VMEM((NG * E // 2, N, LANES), F32), pltpu.VMEM((Q, XW), F32)],
        compiler_params=_cparams("parallel", "parallel", "arbitrary"),
        name="ssd_chunk_scan",
    )(xbc, xbc, xbc, z, dtr,
      head_rows(dt_bias), head_rows(a_log), d_ch, norm_w.reshape(1, -1), spread)


def _pad_rows(w, n):
    return jnp.pad(w, ((0, n - w.shape[0]), (0, 0)))


def _layer_of(w, layer):
    return w[layer] if w.ndim == 3 else w


def _ssd_layer(hf, hb, w_in, conv_w, conv_b, dt_bias, a_log, d_skip, norm_w, w_out, g, b,
               *, B, T, layer=0):
    di = SSD_D_INNER
    cd = di + 2 * SSD_BC_WIDTH
    w_t = jnp.swapaxes(w_in, -1, -2)
    z = _matmul(hb, w_t, F32, layer=layer, col0=0, ncols=di, wt=True)
    xbc = _matmul_conv_silu(hb, w_t, conv_w, conv_b, layer=layer, col0=di, ncols=cd, seq_len=T,
                            wt=True)
    dt = _matmul(hb, _pad_rows(_layer_of(w_t, layer)[di + cd:], LANES), F32, wt=True)
    y = _ssd_core(z, xbc, dt, dt_bias, a_log, d_skip, norm_w, B=B, T=T)
    return _out_proj_ln(y, w_out.astype(BF16), hf, g, b)


FOX_BIAS_PIECES = 3


def _fox_bias_kernel(f_ref, b_ref, qx_ref, kx_ref, carry_ref, *, nblk):
    blk, W, P = LANES, FOX_WIDTH, FOX_BIAS_PIECES
    row = lax.broadcasted_iota(jnp.int32, (blk, blk), 0)
    col = lax.broadcasted_iota(jnp.int32, (blk, blk), 1)
    tri = (row >= col).astype(F32)
    bias = b_ref[...]
    H = FOX_HEADS
    r = lax.broadcasted_iota(jnp.int32, (blk, W), 0)
    c = lax.broadcasted_iota(jnp.int32, (blk, W), 1)
    head, piece = r % H, r // H
    is_piece = r < P * H
    place_q = jnp.logical_and(is_piece, c == head * blk + piece).astype(BF16)
    place_k = jnp.logical_and(is_piece, c == head * blk + P + piece).astype(BF16)
    is_head = col < H
    cmod = lax.broadcasted_iota(jnp.int32, (1, W), 1) % blk
    ones_q = jnp.logical_and(cmod >= P, cmod < 2 * P).astype(F32)
    ones_k = (cmod < P).astype(F32)

    def body(t, carry):
        rows = pl.ds(pl.multiple_of(t * blk, blk), blk)
        x = f_ref[rows, :] + bias
        logf = jnp.minimum(x, 0.0) - jnp.log1p(jnp.exp(-jnp.abs(x)))
        cum = jnp.dot(tri, logf, precision=HIGHEST, preferred_element_type=F32) + carry
        hi = jnp.where(is_head, cum, 0.0).astype(BF16).astype(F32)
        r1 = jnp.where(is_head, cum, 0.0) - hi
        mid = r1.astype(BF16).astype(F32)
        lo = (r1 - mid).astype(BF16).astype(F32)
        pieces = (hi + pltpu.roll(mid, H, 1) + pltpu.roll(lo, 2 * H, 1)).astype(BF16)
        qx_ref[rows, :] = (jnp.dot(pieces, place_q, preferred_element_type=F32)
                           + ones_q).astype(BF16)
        kx_ref[rows, :] = (ones_k - jnp.dot(pieces, place_k, preferred_element_type=F32)
                           ).astype(BF16)
        return cum[blk - 1:blk, :]

    @pl.when(pl.program_id(1) == 0)
    def _():
        carry_ref[...] = jnp.zeros_like(carry_ref)

    carry_ref[...] = lax.fori_loop(0, nblk, body, carry_ref[...])


def _fox_bias_columns(f, f_bias, *, B, T):
    bias = jnp.pad(f_bias, (0, LANES - FOX_HEADS)).reshape(1, LANES)
    tt = min(512, T)
    out = jax.ShapeDtypeStruct((B, T, FOX_WIDTH), BF16)
    spec = pl.BlockSpec((None, tt, FOX_WIDTH), lambda b, t: (b, t, 0))
    qx, kx = pl.pallas_call(
        functools.partial(_fox_bias_kernel, nblk=tt // LANES),
        grid=(B, T // tt),
        in_specs=[pl.BlockSpec((None, tt, LANES), lambda b, t: (b, t, 0)),
                  pl.BlockSpec((1, LANES), lambda b, t: (0, 0))],
        out_specs=[spec, spec],
        out_shape=[out, out],
        scratch_shapes=[pltpu.VMEM((1, LANES), F32)],
        compiler_params=_cparams("parallel", "arbitrary"),
        name="fox_cum_log_forget",
    )(f.reshape(B, T, LANES), bias)
    return qx.reshape(B * T, FOX_WIDTH), kx.reshape(B * T, FOX_WIDTH)


def _fox_layer(hf, hb, w_in, f_bias, w_out, g, b, *, B, T, layer=0):
    W, H, dh = FOX_WIDTH, FOX_HEADS, FOX_HEAD_DIM
    w_t = jnp.swapaxes(w_in, -1, -2)
    q = _matmul(hb, w_t, BF16, layer=layer, col0=0, ncols=W, scale=dh ** -0.5, wt=True)
    kv = _matmul(hb, w_t, BF16, layer=layer, col0=W, ncols=2 * W, wt=True)
    z = _matmul(hb, w_t, F32, layer=layer, col0=3 * W, ncols=W, wt=True)
    f = _matmul(hb, _pad_rows(_layer_of(w_t, layer)[4 * W:], LANES), F32, wt=True)
    qx, kx = _fox_bias_columns(f, f_bias, B=B, T=T)

    tq, tk = _attn_tiles(T)
    nq = T // tq
    hw = ATTN_HEADS_PER_STEP * dh
    q_tile = pl.BlockSpec((tq, hw), lambda bb, h, i: (bb * nq + i, h))
    seq_k = pl.BlockSpec((T, hw), lambda bb, h, i: (bb, h))
    seq_v = pl.BlockSpec((T, hw), lambda bb, h, i: (bb, W // hw + h))
    in_specs = [q_tile, q_tile, seq_k, seq_k, seq_v, q_tile]
    y = _attention((q, qx, kv, kx, kv, z), in_specs, q_tile,
                   B=B, T=T, H=H, dv=dh, tq=tq, tk=tk, nparts=2, head_major=False)
    return _out_proj_ln(y, w_out.astype(BF16), hf, g, b)


def _rope_table_kernel(pos_ref, freq_ref, sign_ref, cos_ref, sin_ref):
    ang = pos_ref[...].astype(F32) * freq_ref[...]
    cos_ref[...] = jnp.cos(ang)
    sin_ref[...] = jnp.sin(ang) * sign_ref[...]


def _rope_tables(positions, *, B, T):
    half = MLA_ROPE // 2
    inv_freq = ROPE_BASE ** (-jnp.arange(0, MLA_ROPE, 2, dtype=F32) / MLA_ROPE)
    freq = jnp.tile(inv_freq, LANES // half).reshape(1, LANES)
    sign = jnp.tile(jnp.concatenate([-jnp.ones((half,), F32), jnp.ones((half,), F32)]),
                    LANES // MLA_ROPE).reshape(1, LANES)
    M = B * T
    tm = min(1024, M)
    return pl.pallas_call(
        _rope_table_kernel,
        grid=(M // tm,),
        in_specs=[pl.BlockSpec((tm, 1), lambda i: (i, 0)),
                  pl.BlockSpec((1, LANES), lambda i: (0, 0)),
                  pl.BlockSpec((1, LANES), lambda i: (0, 0))],
        out_specs=[pl.BlockSpec((tm, LANES), lambda i: (i, 0)),
                   pl.BlockSpec((tm, LANES), lambda i: (i, 0))],
        out_shape=[jax.ShapeDtypeStruct((M, LANES), F32), jax.ShapeDtypeStruct((M, LANES), F32)],
        compiler_params=_cparams("parallel"),
        name="rope_tables",
    )(positions.reshape(M, 1), freq, sign)


def _rms_to_bf16(x, w):
    y = x * lax.rsqrt(jnp.mean(x * x, axis=-1, keepdims=True) + RMS_EPS)
    return (y * w).astype(BF16)


def _rope_pair(c2, cos, sin):
    return c2 * cos + pltpu.roll(c2, MLA_ROPE, 1) * sin


def _mla_qkv_kernel(x_ref, wl_ref, qn_ref, kn_ref, wq_ref, wkv_ref, cos_ref, sin_ref,
                    q_ref, k_ref, v_ref, *, scale):
    qr, kr = MLA_Q_RANK, MLA_KV_RANK
    cos, sin = cos_ref[...], sin_ref[...]
    lat = _project(x_ref[...], wl_ref[...], True)
    qn = _rms_to_bf16(lat[:, :qr], qn_ref[...])
    kn = _rms_to_bf16(lat[:, qr:qr + kr], kn_ref[...])
    k_pe = _rope_pair(lat[:, qr + kr:], cos, sin)[:, :MLA_ROPE].astype(k_ref.dtype)
    for h in range(MLA_HEADS):
        r = jnp.dot(qn, wq_ref[h], preferred_element_type=F32)
        pe = _rope_pair(r[:, MLA_NOPE:], cos, sin)
        q_ref[h, :, :MLA_NOPE] = (r[:, :MLA_NOPE] * scale).astype(q_ref.dtype)
        q_ref[h, :, MLA_NOPE:] = (pe[:, :MLA_ROPE] * scale).astype(q_ref.dtype)
        r = jnp.dot(kn, wkv_ref[h], preferred_element_type=F32)
        k_ref[h, :, :MLA_NOPE] = r[:, :MLA_NOPE].astype(k_ref.dtype)
        k_ref[h, :, MLA_NOPE:] = k_pe
        v_ref[h] = r[:, MLA_NOPE:].astype(v_ref.dtype)


def _swap_halves(w):
    half = w.shape[-1] // 2
    return jnp.concatenate([w[..., half:], w[..., :half]], axis=-1)


def _mla_layer(hf, hb, positions, w_in, q_norm, kv_norm, w_q_up, w_kv_up, w_out, g, b, *, B, T):
    H, M = MLA_HEADS, B * T
    qr, kr = MLA_Q_RANK, MLA_KV_RANK
    lat_end = qr + kr + MLA_ROPE
    w_t = jnp.swapaxes(w_in, -1, -2)
    w_pe = w_t[qr + kr:lat_end]
    half = MLA_ROPE // 2
    w_lat = jnp.concatenate([w_t[:qr + kr], w_pe, w_pe[half:], w_pe[:half]], axis=0).astype(BF16)
    nlat = w_lat.shape[0]
    z = _matmul(hb, w_t[lat_end:], F32, wt=True)
    cos, sin = _rope_tables(positions, B=B, T=T)

    wq = w_q_up.reshape(qr, H, MLA_QK)
    wq = jnp.concatenate([wq, _swap_halves(wq[..., MLA_NOPE:])], axis=-1)
    wq = jnp.transpose(wq, (1, 0, 2)).astype(BF16)
    wkv = jnp.transpose(w_kv_up.reshape(kr, H, MLA_NOPE + MLA_V), (1, 0, 2)).astype(BF16)

    tm = min(256, M)
    wide = MLA_NOPE + LANES
    row = lambda i: (i, 0)
    q, k, v = pl.pallas_call(
        functools.partial(_mla_qkv_kernel, scale=MLA_QK ** -0.5),
        grid=(M // tm,),
        in_specs=[pl.BlockSpec((tm, D_MODEL), row),
                  _resident((nlat, D_MODEL), lambda i: (0, 0)),
                  pl.BlockSpec((1, qr), lambda i: (0, 0)),
                  pl.BlockSpec((1, kr), lambda i: (0, 0)),
                  _resident((H, qr, wide), lambda i: (0, 0, 0)),
                  _resident((H, kr, MLA_NOPE + MLA_V), lambda i: (0, 0, 0)),
                  pl.BlockSpec((tm, LANES), row),
                  pl.BlockSpec((tm, LANES), row)],
        out_specs=[pl.BlockSpec((H, tm, MLA_QK), lambda i: (0, i, 0)),
                   pl.BlockSpec((H, tm, MLA_QK), lambda i: (0, i, 0)),
                   pl.BlockSpec((H, tm, MLA_V), lambda i: (0, i, 0))],
        out_shape=[jax.ShapeDtypeStruct((H, M, MLA_QK), BF16),
                   jax.ShapeDtypeStruct((H, M, MLA_QK), BF16),
                   jax.ShapeDtypeStruct((H, M, MLA_V), BF16)],
        compiler_params=_cparams("parallel"),
        name="mla_qkv_rope",
    )(hb, w_lat, q_norm.reshape(1, qr), kv_norm.reshape(1, kr), wq, wkv, cos, sin)

    tq, tk = _attn_tiles(T)
    nq = T // tq
    nh = ATTN_HEADS_PER_STEP
    in_specs = [
        pl.BlockSpec((nh, tq, MLA_QK), lambda bb, h, i: (h, bb * nq + i, 0)),
        pl.BlockSpec((nh, T, MLA_QK), lambda bb, h, i: (h, bb, 0)),
        pl.BlockSpec((nh, T, MLA_V), lambda bb, h, i: (h, bb, 0)),
        pl.BlockSpec((tq, nh * MLA_V), lambda bb, h, i: (bb * nq + i, h)),
    ]
    out_spec = pl.BlockSpec((tq, nh * MLA_V), lambda bb, h, i: (bb * nq + i, h))
    y = _attention((q, k, v, z), in_specs, out_spec,
                   B=B, T=T, H=H, dv=MLA_V, tq=tq, tk=tk, nparts=1, head_major=True)
    return _out_proj_ln(y, w_out.astype(BF16), hf, g, b)


def _s5_matrices(lam_re, lam_im, log_step, b_re, b_im, c_re, c_im):
    L = S5_CHUNK
    step = jnp.exp(log_step.astype(F32))[:, None]
    mag = jnp.exp(lam_re * step)
    ar = mag * jnp.cos(lam_im * step)
    ai = mag * jnp.sin(lam_im * step)
    den = lam_re * lam_re + lam_im * lam_im
    fr = ((ar - 1.0) * lam_re + ai * lam_im) / den
    fi = (ai * lam_re - (ar - 1.0) * lam_im) / den
    bbr = fr[..., None] * b_re - fi[..., None] * b_im
    bbi = fr[..., None] * b_im + fi[..., None] * b_re
    pr, pi = [jnp.ones_like(ar)], [jnp.zeros_like(ar)]
    for _ in range(L):
        pr_next = pr[-1] * ar - pi[-1] * ai
        pi_next = pr[-1] * ai + pi[-1] * ar
        pr.append(pr_next)
        pi.append(pi_next)
    pwr = jnp.stack(pr, axis=1)
    pwi = jnp.stack(pi, axis=1)

    G, P, I = S5_GROUPS, S5_STATE, S5_GROUP
    TG, SG = S5_TOEP_GROUPS, S5_STATE_GROUPS
    NTB, NSB = G // TG, G // SG

    c_ji = lambda c: jnp.tile(jnp.transpose(c, (0, 2, 1)), (1, 1, I))
    b_ji = lambda v: jnp.repeat(v, I, axis=2)
    cbr = c_ji(c_re) * b_ji(bbr) - c_ji(c_im) * b_ji(bbi)
    cbi = c_ji(c_re) * b_ji(bbi) + c_ji(c_im) * b_ji(bbr)
    kern = jnp.einsum('gdp,gpq->gdq', jnp.concatenate([pwr[:, :L], -pwi[:, :L]], axis=2),
                      jnp.concatenate([cbr, cbi], axis=1), precision=HIGHEST)
    idx = jnp.arange

    kc = jnp.transpose(kern.reshape(NTB, TG, L, I, I), (2, 0, 1, 3, 4)).reshape(L, NTB, TG * I, I)
    kc = jnp.pad(kc, ((L - 1, 0), (0, 0), (0, 0), (0, 0)))
    repeat_i = (idx(TG * I)[None, :] % I == idx(I)[:, None]).astype(F32)
    same_t = (idx(TG * I)[:, None] // I == idx(TG * I)[None, :] // I)
    kd = jnp.where(same_t, jnp.einsum('dcri,iq->dcrq', kc, repeat_i, precision=HIGHEST), 0.0)

    lanes = lambda v: jnp.tile(v, (1, 1, SG))
    by_step = lambda w: jnp.transpose(
        lanes(w[:, :L][:, ::-1]).reshape(NSB, SG, L, SG * P), (2, 0, 1, 3))[:, :, :, None]
    by_chan = lambda v: lanes(jnp.transpose(v, (0, 2, 1))).reshape(1, NSB, SG, I, SG * P)
    qr, qi = by_step(pwr), by_step(pwi)
    br, bi = by_chan(bbr), by_chan(bbi)
    bd = jnp.concatenate([qr * br - qi * bi, qr * bi + qi * br], axis=-1)
    bd = bd.reshape(L, NSB, SG * I, 2 * SG * P)
    same_b = (idx(SG * I)[:, None] // I == (idx(2 * SG * P)[None, :] % (SG * P)) // P)
    bd = jnp.where(same_b, bd, 0.0)

    chan = lambda c: jnp.transpose(c.reshape(NSB, SG, I, P), (0, 3, 1, 2)).reshape(1, NSB, P, SG * I)
    step_pow = lambda w: jnp.repeat(
        jnp.transpose(w[:, 1:].reshape(NSB, SG, L, P), (2, 0, 3, 1)), I, axis=3)
    cr, ci, wr, wi = chan(c_re), chan(c_im), step_pow(pwr), step_pow(pwi)
    cp = jnp.stack([cr * wr - ci * wi, -(cr * wi + ci * wr)], axis=2)
    cd = jnp.broadcast_to(cp[:, :, :, None], (L, NSB, 2, SG, P, SG * I))
    cd = cd.reshape(L, NSB, 2 * SG * P, SG * I)
    same_c = ((idx(2 * SG * P)[:, None] % (SG * P)) // P == idx(SG * I)[None, :] // I)
    cd = jnp.where(same_c, cd, 0.0)

    alr = pwr[:, L].reshape(NSB, SG * P)
    ali = pwi[:, L].reshape(NSB, SG * P)
    a1 = jnp.concatenate([alr, alr], axis=1).reshape(1, -1)
    a2 = jnp.concatenate([-ali, ali], axis=1).reshape(1, -1)
    return kd.astype(BF16), bd.astype(BF16), cd.astype(BF16), a1, a2


def _chunk_step(u_ref, s):
    return u_ref[pl.ds(s, u_ref.shape[0] // S5_CHUNK, stride=S5_CHUNK), :]


def _s5_local_state_kernel(u_ref, bd_ref, s_ref):
    acc = jnp.dot(_chunk_step(u_ref, 0).astype(BF16), bd_ref[0], preferred_element_type=F32)
    for s in range(1, S5_CHUNK):
        acc = acc + jnp.dot(_chunk_step(u_ref, s).astype(BF16), bd_ref[s],
                            preferred_element_type=F32)
    s_ref[...] = acc


def _s5_scan_kernel(s_ref, a1_ref, a2_ref, o_ref, *, B, nch):
    W = s_ref.shape[1]
    a1 = a1_ref[...]
    a2 = a2_ref[...]
    unit = 2 * S5_STATE_GROUPS * S5_STATE

    def swap_re_im(c):
        parts = []
        for q in range(W // unit):
            lo = q * unit
            parts += [c[:, lo + unit // 2:lo + unit], c[:, lo:lo + unit // 2]]
        return jnp.concatenate(parts, axis=1)

    def body(n, carry):
        new = []
        for bb in range(B):
            row = bb * nch + n
            c = carry[bb]
            o_ref[pl.ds(row, 1), :] = c
            new.append(a1 * c + a2 * swap_re_im(c) + s_ref[pl.ds(row, 1), :])
        return tuple(new)

    lax.fori_loop(0, nch, body, tuple(jnp.zeros((1, W), F32) for _ in range(B)))


def _gelu_tanh(y):
    return 0.5 * y * (1.0 + jnp.tanh(math.sqrt(2.0 / math.pi) * (y + 0.044715 * (y * y * y))))


def _s5_out_kernel(*refs):
    L = S5_CHUNK
    nu = S5_TOEP_GROUPS // S5_STATE_GROUPS
    u_refs, (kd_ref, st_ref, cd_ref, o_ref, ub_ref, acc_ref) = refs[:nu], refs[nu:]
    r = pl.program_id(1)

    @pl.when(r == 0)
    def _():
        for s in range(L):
            for k in range(nu):
                ub_ref[s, :, k * LANES:(k + 1) * LANES] = _chunk_step(u_refs[k], s).astype(BF16)

    def lags(first):
        out = None
        for s in range(first, first + S5_STEP_GROUP):
            d = jnp.dot(ub_ref[s], kd_ref[r - s + (L - 1)], preferred_element_type=F32)
            out = d if out is None else out + d
        return out

    st = st_ref[...].astype(BF16)
    half = st.shape[1] // 2
    carried = jnp.concatenate(
        [jnp.dot(st[:, :half], cd_ref[0], preferred_element_type=F32),
         jnp.dot(st[:, half:], cd_ref[1], preferred_element_type=F32)], axis=1)
    acc_ref[...] = carried + lags(0)
    for first in range(S5_STEP_GROUP, L, S5_STEP_GROUP):
        @pl.when(first <= r)
        def _(first=first):
            acc_ref[...] += lags(first)
    o_ref[...] = acc_ref[...].astype(o_ref.dtype)


def _s5_glu_out_kernel(ys_ref, u_ref, d_ref, wg_ref, bg_ref, z_ref, wo_ref, h_ref, g_ref, b_ref,
                       of_ref, ob_ref, yt_ref):
    L, W = S5_CHUNK, S5_WIDTH
    nslab, tm = yt_ref.shape[0], yt_ref.shape[1]
    for s in range(L):
        for k in range(nslab):
            lo = s * W + k * LANES
            yt_ref[k, pl.ds(s, tm // L, stride=L), :] = ys_ref[:, lo:lo + LANES].astype(F32)
    half = tm // 2
    parts = [pl.ds(0, half), pl.ds(half, half)]
    ys = [_gelu_tanh(jnp.concatenate([yt_ref[k, rows, :] for k in range(nslab)], axis=1)
                     + d_ref[...] * u_ref[rows, :]) for rows in parts]
    ts = [jnp.dot(y.astype(BF16), wg_ref[...], preferred_element_type=F32) for y in ys]
    gated = [(y * _sigmoid(t + bg_ref[...]) * _silu(z_ref[rows, :])).astype(BF16)
             for y, t, rows in zip(ys, ts, parts)]
    branches = [jnp.dot(y, wo_ref[...], preferred_element_type=F32) for y in gated]
    for rows, branch in zip(parts, branches):
        out = _deepnorm_ln(h_ref[rows, :], branch, g_ref[...], b_ref[...])
        of_ref[rows, :] = out
        ob_ref[rows, :] = out.astype(BF16)


def _s5_layer(hf, hb, w_in, lam_re, lam_im, log_step, b_re, b_im, c_re, c_im, d_skip,
              w_glu, b_glu, w_out, g, b, *, B, T, layer=0):
    M, W, G, L = B * T, S5_WIDTH, S5_GROUPS, S5_CHUNK
    nch = T // L
    rows = B * nch
    u = _matmul(hb, w_in, F32, layer=layer, col0=0, ncols=W)
    z = _matmul(hb, w_in, F32, layer=layer, col0=W, ncols=W)
    kd, bd, cd, a1, a2 = _s5_matrices(lam_re, lam_im, log_step, b_re, b_im, c_re, c_im)

    TG, SG = S5_TOEP_GROUPS, S5_STATE_GROUPS
    ntb, nsb = G // TG, G // SG
    tw, sw_in, sw = TG * S5_GROUP, SG * S5_GROUP, 2 * SG * S5_STATE
    nstate = nsb * sw
    s_loc = pl.pallas_call(
        _s5_local_state_kernel,
        grid=(nsb,),
        in_specs=[pl.BlockSpec((M, sw_in), lambda cb: (0, cb)),
                  pl.BlockSpec((L, None, sw_in, sw), lambda cb: (0, cb, 0, 0))],
        out_specs=pl.BlockSpec((rows, sw), lambda cb: (0, cb)),
        out_shape=jax.ShapeDtypeStruct((rows, nstate), F32),
        compiler_params=_cparams("parallel"),
        name="s5_chunk_state",
    )(u, bd)

    scan_w = 2 * sw
    st_in = pl.pallas_call(
        functools.partial(_s5_scan_kernel, B=B, nch=nch),
        grid=(nstate // scan_w,),
        in_specs=[pl.BlockSpec((rows, scan_w), lambda p: (0, p)),
                  pl.BlockSpec((1, scan_w), lambda p: (0, p)),
                  pl.BlockSpec((1, scan_w), lambda p: (0, p))],
        out_specs=pl.BlockSpec((rows, scan_w), lambda p: (0, p)),
        out_shape=jax.ShapeDtypeStruct((rows, nstate), F32),
        compiler_params=_cparams("parallel"),
        name="s5_state_scan",
    )(s_loc, a1, a2)

    per_toep = TG // SG
    y = pl.pallas_call(
        _s5_out_kernel,
        grid=(ntb, L),
        in_specs=[pl.BlockSpec((M, sw_in), lambda cb, r, k=k: (0, per_toep * cb + k))
                  for k in range(per_toep)]
        + [pl.BlockSpec((2 * L - 1, None, tw, tw), lambda cb, r: (0, cb, 0, 0)),
                  pl.BlockSpec((rows, per_toep * sw), lambda cb, r: (0, cb)),
                  pl.BlockSpec((None, per_toep, sw, sw_in), lambda cb, r: (r, cb, 0, 0))],
        out_specs=pl.BlockSpec((rows, tw), lambda cb, r: (0, r * ntb + cb)),
        out_shape=jax.ShapeDtypeStruct((rows, L * W), BF16),
        scratch_shapes=[pltpu.VMEM((L, rows, tw), BF16), pltpu.VMEM((rows, tw), F32)],
        compiler_params=_cparams("parallel", "arbitrary"),
        name="s5_chunk_output",
    )(*([u] * per_toep), kd, st_in, cd)

    tm = min(256, M)
    row = lambda i: (i, 0)
    fixed = lambda i: (0, 0)
    return pl.pallas_call(
        _s5_glu_out_kernel,
        grid=(M // tm,),
        in_specs=[pl.BlockSpec((tm // L, L * W), row),
                  pl.BlockSpec((tm, W), row),
                  pl.BlockSpec((1, W), fixed),
                  _resident((W, W), fixed),
                  pl.BlockSpec((1, W), fixed),
                  pl.BlockSpec((tm, W), row),
                  _resident((W, D_MODEL), fixed),
                  pl.BlockSpec((tm, D_MODEL), row),
                  pl.BlockSpec((1, D_MODEL), fixed),
                  pl.BlockSpec((1, D_MODEL), fixed)],
        out_specs=[pl.BlockSpec((tm, D_MODEL), row), pl.BlockSpec((tm, D_MODEL), row)],
        out_shape=[jax.ShapeDtypeStruct((M, D_MODEL), F32),
                   jax.ShapeDtypeStruct((M, D_MODEL), BF16)],
        scratch_shapes=[pltpu.VMEM((W // LANES, tm, LANES), F32)],
        compiler_params=_cparams("parallel"),
        name="s5_glu_out_ln",
    )(y, u, d_skip.reshape(1, W), w_glu.astype(BF16), b_glu.reshape(1, W), z,
      w_out.astype(BF16), hf, g.reshape(1, D_MODEL), b.reshape(1, D_MODEL))


def kernel(x, positions, ln_g, ln_b, ssd_w_in, ssd_conv_w, ssd_conv_b, ssd_dt_bias, ssd_a_log, ssd_d, ssd_norm_w, ssd_w_out, fox_w_in, fox_f_bias, fox_w_out, mla_w_in, mla_q_norm, mla_kv_norm, mla_w_q_up, mla_w_kv_up, mla_w_out, s5_w_in, s5_lambda_re, s5_lambda_im, s5_log_step, s5_b_re, s5_b_im, s5_c_re, s5_c_im, s5_d, s5_w_glu, s5_b_glu, s5_w_out):
    B, T, D = x.shape
    hf = x.reshape(B * T, D)
    hb = hf.astype(BF16)
    for i in range(DEPTH):
        j = i // 4
        kind = i % 4
        g, b = ln_g[i], ln_b[i]
        if kind == 0:
            hf, hb = _ssd_layer(hf, hb, ssd_w_in, ssd_conv_w[j], ssd_conv_b[j], ssd_dt_bias[j],
                                ssd_a_log[j], ssd_d[j], ssd_norm_w[j], ssd_w_out[j], g, b,
                                B=B, T=T, layer=j)
        elif kind == 1:
            hf, hb = _fox_layer(hf, hb, fox_w_in, fox_f_bias[j], fox_w_out[j], g, b,
                                B=B, T=T, layer=j)
        elif kind == 2:
            hf, hb = _mla_layer(hf, hb, positions, mla_w_in[j], mla_q_norm[j],
                                mla_kv_norm[j], mla_w_q_up[j], mla_w_kv_up[j], mla_w_out[j],
                                g, b, B=B, T=T)
        else:
            hf, hb = _s5_layer(hf, hb, s5_w_in, s5_lambda_re[j], s5_lambda_im[j], s5_log_step[j],
                               s5_b_re[j], s5_b_im[j], s5_c_re[j], s5_c_im[j], s5_d[j],
                               s5_w_glu[j], s5_b_glu[j], s5_w_out[j], g, b, B=B, T=T, layer=j)
    return hf.reshape(B, T, D)
```

```python
import functools
import math

import jax
import jax.numpy as jnp
from jax import lax
from jax.experimental import pallas as pl
from jax.experimental.pallas import tpu as pltpu

F32 = jnp.float32
BF16 = jnp.bfloat16
HIGHEST = lax.Precision.HIGHEST

D_MODEL = 2048
DEPTH = 4
ALPHA = (2.0 * DEPTH) ** 0.25
LN_EPS = 1e-5
RMS_EPS = 1e-6

SSD_D_INNER = 4096
SSD_HEADS = 64
SSD_HEADDIM = 64
SSD_GROUPS = 8
SSD_HEADS_PER_GROUP = SSD_HEADS // SSD_GROUPS
SSD_STATE = 128
SSD_CONV = 4
SSD_CHUNK = 128
SSD_GROUP_WIDTH = SSD_D_INNER // SSD_GROUPS
SSD_BC_WIDTH = SSD_GROUPS * SSD_STATE
SSD_GROUPS_PER_STEP = 4

FOX_HEADS = 16
FOX_HEAD_DIM = 128
FOX_WIDTH = FOX_HEADS * FOX_HEAD_DIM

MLA_HEADS = 16
MLA_Q_RANK = 512
MLA_KV_RANK = 512
MLA_NOPE = 128
MLA_ROPE = 64
MLA_V = 128
MLA_QK = MLA_NOPE + MLA_ROPE
MLA_WIDTH = MLA_HEADS * MLA_V
ROPE_BASE = 10000.0

S5_WIDTH = D_MODEL
S5_GROUP = 16
S5_GROUPS = S5_WIDTH // S5_GROUP
S5_STATE = 64
S5_CHUNK = 8
S5_TOEP_GROUPS = 16
S5_STATE_GROUPS = 8
S5_STEP_GROUP = 4

LANES = 128
VMEM_LIMIT_BYTES = 48 * 1024 * 1024


def _cparams(*sem):
    return pltpu.CompilerParams(dimension_semantics=sem, vmem_limit_bytes=VMEM_LIMIT_BYTES)


def _resident(block_shape, index_map):
    return pl.BlockSpec(block_shape, index_map, pipeline_mode=pl.Buffered(1))


def _sigmoid(x):
    return 0.5 * (jnp.tanh(0.5 * x) + 1.0)


def _silu(x):
    h = 0.5 * x
    return h + h * jnp.tanh(h)


def _softplus(x):
    return jnp.maximum(x, 0.0) + jnp.log1p(jnp.exp(-jnp.abs(x)))


def _deepnorm_ln(h, branch, g, b):
    r = ALPHA * h + branch
    mu = jnp.mean(r, axis=-1, keepdims=True)
    d = r - mu
    var = jnp.mean(d * d, axis=-1, keepdims=True)
    return d * lax.rsqrt(var + LN_EPS) * g + b


def _project(x, wb, wt):
    dims = (((1,), (1,)), ((), ())) if wt else (((1,), (0,)), ((), ()))
    return lax.dot_general(x, wb, dims, preferred_element_type=F32)


def _mm_kernel(x_ref, w_ref, o_ref, wb_ref, *, scale, wt):
    @pl.when(pl.program_id(1) == 0)
    def _():
        wb_ref[...] = w_ref[...].astype(BF16)

    acc = _project(x_ref[...].astype(BF16), wb_ref[...], wt)
    if scale is not None:
        acc = acc * scale
    o_ref[...] = acc.astype(o_ref.dtype)


def _weight_spec(w, layer, K, tn, j0, wt):
    block = (tn, K) if wt else (K, tn)
    index = (lambda j: (j0 + j, 0)) if wt else (lambda j: (0, j0 + j))
    if w.ndim == 2:
        return pl.BlockSpec(block, lambda j, i: index(j))
    return pl.BlockSpec((None,) + block, lambda j, i: (layer,) + index(j))


def _matmul(x, w, out_dtype, *, layer=0, col0=0, ncols=None, scale=None, tn=1024, wt=False):
    M, K = x.shape
    N = w.shape[-2 if wt else -1] - col0 if ncols is None else ncols
    tm = min(1024 if x.dtype == BF16 else 512, M)
    tn = min(tn, N)
    assert M % tm == 0 and N % tn == 0 and col0 % tn == 0, (M, N, col0, tm, tn)
    j0 = col0 // tn
    return pl.pallas_call(
        functools.partial(_mm_kernel, scale=scale, wt=wt),
        grid=(N // tn, M // tm),
        in_specs=[pl.BlockSpec((tm, K), lambda j, i: (i, 0)),
                  _weight_spec(w, layer, K, tn, j0, wt)],
        out_specs=pl.BlockSpec((tm, tn), lambda j, i: (i, j)),
        out_shape=jax.ShapeDtypeStruct((M, N), out_dtype),
        scratch_shapes=[pltpu.VMEM((tn, K) if wt else (K, tn), BF16)],
        compiler_params=_cparams("parallel", "arbitrary"),
        name="proj",
    )(x, w)


CONV_COL_PARTS = 4


def _mm_conv_silu_kernel(x_ref, w_ref, cw_ref, cb_ref, o_ref, wb_ref, ext_ref,
                         *, taps, tiles_per_seq, wt):
    i = pl.program_id(1)
    tm, tn = o_ref.shape

    @pl.when(i == 0)
    def _():
        wb_ref[...] = w_ref[...].astype(BF16)

    @pl.when(i % tiles_per_seq == 0)
    def _():
        ext_ref[pl.ds(0, 8), :] = jnp.zeros((8, tn), F32)

    xb = x_ref[...].astype(BF16)
    slab = tn // CONV_COL_PARTS
    parts = [pl.ds(p * slab, slab) for p in range(CONV_COL_PARTS)]
    raws = [_project(xb, wb_ref[cols, :] if wt else wb_ref[:, cols], wt) for cols in parts]
    for cols, raw in zip(parts, raws):
        ext_ref[pl.ds(8, tm), cols] = raw
        acc = cb_ref[:, cols] + cw_ref[pl.ds(0, 1), cols] * ext_ref[pl.ds(9 - taps, tm), cols]
        for kk in range(1, taps):
            acc = acc + cw_ref[pl.ds(kk, 1), cols] * ext_ref[pl.ds(9 - taps + kk, tm), cols]
        ext_ref[pl.ds(0, 8), cols] = ext_ref[pl.ds(tm, 8), cols]
        o_ref[:, cols] = _silu(acc)


def _matmul_conv_silu(x, w, conv_w, conv_b, *, layer, col0, ncols, seq_len, tn=1024, wt=False):
    M, K = x.shape
    taps = conv_w.shape[0]
    tm = min(1024 if x.dtype == BF16 else 512, seq_len)
    assert M % tm == 0 and seq_len % tm == 0 and ncols % tn == 0 and col0 % tn == 0
    j0 = col0 // tn
    return pl.pallas_call(
        functools.partial(_mm_conv_silu_kernel, taps=taps, tiles_per_seq=seq_len // tm, wt=wt),
        grid=(ncols // tn, M // tm),
        in_specs=[pl.BlockSpec((tm, K), lambda j, i: (i, 0)),
                  _weight_spec(w, layer, K, tn, j0, wt),
                  pl.BlockSpec((taps, tn), lambda j, i: (0, j)),
                  pl.BlockSpec((1, tn), lambda j, i: (0, j))],
        out_specs=pl.BlockSpec((tm, tn), lambda j, i: (i, j)),
        out_shape=jax.ShapeDtypeStruct((M, ncols), F32),
        scratch_shapes=[pltpu.VMEM((tn, K) if wt else (K, tn), BF16),
                        pltpu.VMEM((tm + 8, tn), F32)],
        compiler_params=_cparams("parallel", "arbitrary"),
        name="proj_conv_silu",
    )(x, w, conv_w, conv_b.reshape(1, -1))


def _out_ln_kernel(y_ref, w_ref, h_ref, g_ref, b_ref, of_ref, ob_ref):
    half = y_ref.shape[0] // 2
    parts = [pl.ds(0, half), pl.ds(half, half)]
    branches = [jnp.dot(y_ref[rows, :], w_ref[...], preferred_element_type=F32) for rows in parts]
    for rows, branch in zip(parts, branches):
        out = _deepnorm_ln(h_ref[rows, :], branch, g_ref[...], b_ref[...])
        of_ref[rows, :] = out
        ob_ref[rows, :] = out.astype(BF16)


def _out_proj_ln(y, w, h, g, b):
    M, K = y.shape
    D = w.shape[1]
    tm = min(512 if K * D * w.dtype.itemsize <= VMEM_LIMIT_BYTES // 4 else 256, M)
    assert M % tm == 0
    return pl.pallas_call(
        _out_ln_kernel,
        grid=(M // tm,),
        in_specs=[pl.BlockSpec((tm, K), lambda i: (i, 0)),
                  _resident((K, D), lambda i: (0, 0)),
                  pl.BlockSpec((tm, D), lambda i: (i, 0)),
                  pl.BlockSpec((1, D), lambda i: (0, 0)),
                  pl.BlockSpec((1, D), lambda i: (0, 0))],
        out_specs=[pl.BlockSpec((tm, D), lambda i: (i, 0)),
                   pl.BlockSpec((tm, D), lambda i: (i, 0))],
        out_shape=[jax.ShapeDtypeStruct((M, D), F32), jax.ShapeDtypeStruct((M, D), BF16)],
        compiler_params=_cparams("parallel"),
        name="out_proj_ln",
    )(y, w, h, g.reshape(1, D), b.reshape(1, D))


ATTN_HEADS_PER_STEP = 4
ATTN_WIDE_BLOCKS = 2


def _attn_kernel(*refs, nparts, tq, tk, dv, head_major):
    q_refs = refs[:nparts]
    k_refs = refs[nparts:2 * nparts]
    v_ref, z_ref, o_ref, m_ref, acc_ref = refs[2 * nparts:]
    nh = ATTN_HEADS_PER_STEP
    i = pl.program_id(2)

    def head(ref, hh, rows=slice(None)):
        if head_major:
            return ref[hh, rows, :]
        w = ref.shape[-1] // nh
        return ref[rows, hh * w:(hh + 1) * w]

    def cat(pieces):
        return pieces[0] if len(pieces) == 1 else jnp.concatenate(pieces, axis=1)

    qs = [cat([head(r, hh) for r in q_refs]) for hh in range(nh)]
    m_ref[...] = jnp.full_like(m_ref, -jnp.inf)
    acc_ref[...] = jnp.zeros_like(acc_ref)

    def block(first, width, diag_offset):
        rows = pl.ds(pl.multiple_of(first * tk, tk), width)
        ones_cols = jnp.ones((width, LANES), BF16)
        scores = []
        for hh in range(nh):
            k = cat([head(r, hh, rows) for r in k_refs])
            scores.append(lax.dot_general(qs[hh], k, (((1,), (1,)), ((), ())),
                                          preferred_element_type=F32))
        probs, alphas = [], []
        for hh in range(nh):
            s = scores[hh]
            if diag_offset is not None:
                r = lax.broadcasted_iota(jnp.int32, (tq, width), 0)
                c = lax.broadcasted_iota(jnp.int32, (tq, width), 1)
                s = jnp.where(c + diag_offset <= r, s, -jnp.inf)
            m_prev = m_ref[hh]
            m_new = jnp.maximum(m_prev, jnp.max(s, axis=-1, keepdims=True))
            alpha = jnp.exp(m_prev - m_new)
            p = jnp.exp(s - jnp.concatenate([m_new] * (width // LANES), axis=1))
            m_ref[hh] = m_new
            probs.append(p.astype(BF16))
            alphas.append(jnp.concatenate([alpha] * ((dv + LANES) // LANES), axis=1))
        for hh in range(nh):
            v_aug = jnp.concatenate([head(v_ref, hh, rows), ones_cols], axis=1)
            acc_ref[hh] = alphas[hh] * acc_ref[hh] + jnp.dot(
                probs[hh], v_aug, preferred_element_type=F32)

    n_full = i * (tq // tk)
    wide = ATTN_WIDE_BLOCKS

    def wide_block(j, carry):
        block(j * wide, wide * tk, None)
        return carry

    lax.fori_loop(0, n_full // wide, wide_block, 0)
    for rem in range(wide - 1):
        @pl.when(n_full % wide > rem)
        def _(rem=rem):
            block((n_full // wide) * wide + rem, tk, None)
    for d in range(tq // tk):
        block(n_full + d, tk, d * tk)

    for hh in range(nh):
        cols = slice(hh * dv, (hh + 1) * dv)
        o = acc_ref[hh, :, :dv] / acc_ref[hh, :, dv:]
        o_ref[:, cols] = (o * _silu(z_ref[:, cols])).astype(o_ref.dtype)


def _attention(inputs, in_specs, out_spec, *, B, T, H, dv, tq, tk, nparts, head_major):
    nh = ATTN_HEADS_PER_STEP
    assert tq % tk == 0 and T % tq == 0 and H % nh == 0
    return pl.pallas_call(
        functools.partial(_attn_kernel, nparts=nparts, tq=tq, tk=tk, dv=dv,
                          head_major=head_major),
        grid=(B, H // nh, T // tq),
        in_specs=in_specs,
        out_specs=out_spec,
        out_shape=jax.ShapeDtypeStruct((B * T, H * dv), BF16),
        scratch_shapes=[pltpu.VMEM((nh, tq, LANES), F32),
                        pltpu.VMEM((nh, tq, dv + LANES), F32)],
        compiler_params=_cparams("parallel", "parallel", "arbitrary"),
        name="causal_attention",
    )(*inputs)


def _attn_tiles(T):
    tq = min(512, T)
    tk = min(512, T)
    return tq, tk


def _ssd_kernel(x_ref, bm_ref, cm_ref, z_ref, dtr_ref,
                dtb_ref, al_ref, dsk_ref, nw_ref, spread_ref,
                o_ref,
                state_ref, yz_ref):
    Q, E, P, GW, N = SSD_CHUNK, SSD_HEADS_PER_GROUP, SSD_HEADDIM, SSD_GROUP_WIDTH, SSD_STATE
    NG = SSD_GROUPS_PER_STEP
    groups = range(NG)
    c = pl.program_id(2)

    @pl.when(c == 0)
    def _():
        state_ref[...] = jnp.zeros_like(state_ref)

    row = lax.broadcasted_iota(jnp.int32, (Q, Q), 0)
    col = lax.broadcasted_iota(jnp.int32, (Q, Q), 1)
    lower = row >= col
    tri_u = (row <= col).astype(BF16)
    left = col < P

    def pieces(v):
        hi = v.astype(BF16).astype(F32)
        mid = (v - hi).astype(BF16).astype(F32)
        lo = ((v - hi) - mid).astype(BF16).astype(F32)
        return jnp.concatenate([hi, mid, lo], axis=0).astype(BF16)

    npairs = E // 2
    dt_r, acs_r = [], []
    for gg in groups:
        dt = _softplus(dtr_ref[gg] + dtb_ref[gg])
        cum3 = jnp.dot(pieces(dt * (-jnp.exp(al_ref[gg]))), tri_u, preferred_element_type=F32)
        dt_r.append(dt)
        acs_r.append(cum3[:E] + cum3[E:2 * E] + cum3[2 * E:])

    bm_b, cm_b, bmt_b, cb, carried = [], [], [], [], []
    for gg in groups:
        ncols = pl.ds(gg * N, N)
        bm = bm_ref[:, ncols]
        bm_b.append(bm.astype(BF16))
        cm_b.append(cm_ref[:, ncols].astype(BF16))
        bmt_b.append(bm.T.astype(BF16))
        cb.append(lax.dot_general(cm_b[gg], bm_b[gg], (((1,), (1,)), ((), ())),
                                  preferred_element_type=F32))
        carried.append([jnp.dot(cm_b[gg], state_ref[gg * npairs + kp].astype(BF16),
                                preferred_element_type=F32)
                        for kp in range(npairs)])

    pair_cols = lambda gg, kp: pl.ds(gg * GW + kp * LANES, LANES)
    xps = [[x_ref[:, pair_cols(gg, kp)] for kp in range(npairs)] for gg in groups]
    gates = [[_silu(z_ref[:, pair_cols(gg, kp)]) for kp in range(npairs)] for gg in groups]

    spreads = [lax.dot_general(pieces(jnp.concatenate([dt_r[gg], acs_r[gg]], axis=0)),
                               spread_ref[...], (((0,), (0,)), ((), ())),
                               preferred_element_type=F32) for gg in groups]

    for gg in groups:
        dt_ch = spreads[gg][:, :GW]
        acs_ch = spreads[gg][:, GW:2 * GW]
        acs_col = spreads[gg][:, 2 * GW:]
        ssq = jnp.zeros((Q, 1), F32)
        for kp in range(npairs):
            cols = pair_cols(gg, kp)
            xp = xps[gg][kp]
            dt_p = dt_ch[:, kp * LANES:(kp + 1) * LANES]
            acs_p = acs_ch[:, kp * LANES:(kp + 1) * LANES]
            last_p = acs_p[Q - 1:Q, :]
            xdt = xp * dt_p
            xdt_b = xdt.astype(BF16)

            ys = []
            for e in (2 * kp, 2 * kp + 1):
                seg = acs_col[:, e * Q:(e + 1) * Q] - acs_r[gg][e:e + 1, :]
                dec = jnp.exp(jnp.where(lower, seg, -jnp.inf))
                ys.append(jnp.dot((cb[gg] * dec).astype(BF16), xdt_b,
                                  preferred_element_type=F32))
            y_diag = jnp.where(left, ys[0], ys[1])

            xw = (xdt * jnp.exp(last_p - acs_p)).astype(BF16)
            s_loc = jnp.dot(bmt_b[gg], xw, preferred_element_type=F32)
            y_off = carried[gg][kp] * jnp.exp(acs_p)
            slot = gg * npairs + kp
            state_ref[slot] = state_ref[slot] * jnp.exp(last_p) + s_loc

            y = (y_diag + y_off + dsk_ref[:, cols] * xp) * gates[gg][kp]
            yz_ref[:, cols] = y
            ssq = ssq + jnp.sum(y * y, axis=-1, keepdims=True)

        gcols = pl.ds(gg * GW, GW)
        inv = lax.rsqrt(ssq * (1.0 / GW) + RMS_EPS)
        o_ref[:, gcols] = (yz_ref[:, gcols] * inv * nw_ref[:, gcols]).astype(o_ref.dtype)


def _ssd_core(z, xbc, dt, dt_bias, a_log, d_skip, norm_w, *, B, T):
    G, E, Q = SSD_GROUPS, SSD_HEADS_PER_GROUP, SSD_CHUNK
    NC = T // Q
    GW, N = SSD_GROUP_WIDTH, SSD_STATE
    dtr = jnp.transpose(dt[:, :SSD_HEADS].reshape(B, T, G, E), (0, 2, 3, 1))
    head_rows = lambda v: v.reshape(G, E, 1)
    d_ch = jnp.repeat(d_skip, SSD_HEADDIM).reshape(1, -1)
    heads = jnp.arange(E)
    per_channel = (jnp.arange(GW)[None, :] // SSD_HEADDIM == heads[:, None]).astype(BF16)
    per_head_block = (jnp.arange(E * Q)[None, :] // Q == heads[:, None]).astype(BF16)
    none = lambda n: jnp.zeros((E, n), BF16)
    spread = jnp.concatenate([
        jnp.concatenate([per_channel, none(GW), none(E * Q)], axis=1),
        jnp.concatenate([none(GW), per_channel, per_head_block], axis=1)], axis=0)
    spread = jnp.tile(spread, (3, 1))

    NG = SSD_GROUPS_PER_STEP
    XW, BW = NG * GW, NG * N
    assert G % NG == 0
    bblk = SSD_D_INNER // BW
    cblk = (SSD_D_INNER + SSD_BC_WIDTH) // BW
    rowblk = lambda b, g, c: b * NC + c
    in_specs = [
        pl.BlockSpec((Q, XW), lambda b, g, c: (rowblk(b, g, c), g)),
        pl.BlockSpec((Q, BW), lambda b, g, c: (rowblk(b, g, c), bblk + g)),
        pl.BlockSpec((Q, BW), lambda b, g, c: (rowblk(b, g, c), cblk + g)),
        pl.BlockSpec((Q, XW), lambda b, g, c: (rowblk(b, g, c), g)),
        pl.BlockSpec((None, NG, E, Q), lambda b, g, c: (b, g, 0, c)),
        pl.BlockSpec((NG, E, 1), lambda b, g, c: (g, 0, 0)),
        pl.BlockSpec((NG, E, 1), lambda b, g, c: (g, 0, 0)),
        pl.BlockSpec((1, XW), lambda b, g, c: (0, g)),
        pl.BlockSpec((1, XW), lambda b, g, c: (0, g)),
        pl.BlockSpec(spread.shape, lambda b, g, c: (0, 0)),
    ]
    return pl.pallas_call(
        _ssd_kernel,
        grid=(B, G // NG, NC),
        in_specs=in_specs,
        out_specs=pl.BlockSpec((Q, XW), lambda b, g, c: (rowblk(b, g, c), g)),
        out_shape=jax.ShapeDtypeStruct((B * T, SSD_D_INNER), BF16),
        scratch_shapes=[pltpu.VMEM((NG * E // 2, N, LANES), F32), pltpu.VMEM((Q, XW), F32)],
        compiler_params=_cparams("parallel", "parallel", "arbitrary"),
        name="ssd_chunk_scan",
    )(xbc, xbc, xbc, z, dtr,
      head_rows(dt_bias), head_rows(a_log), d_ch, norm_w.reshape(1, -1), spread)


def _pad_rows(w, n):
    return jnp.pad(w, ((0, n - w.shape[0]), (0, 0)))


def _layer_of(w, layer):
    return w[layer] if w.ndim == 3 else w


def _ssd_layer(hf, hb, w_in, conv_w, conv_b, dt_bias, a_log, d_skip, norm_w, w_out, g, b,
               *, B, T, layer=0):
    di = SSD_D_INNER
    cd = di + 2 * SSD_BC_WIDTH
    w_t = jnp.swapaxes(w_in, -1, -2)
    z = _matmul(hb, w_t, F32, layer=layer, col0=0, ncols=di, wt=True)
    xbc = _matmul_conv_silu(hb, w_t, conv_w, conv_b, layer=layer, col0=di, ncols=cd, seq_len=T,
                            wt=True)
    dt = _matmul(hb, _pad_rows(_layer_of(w_t, layer)[di + cd:], LANES), F32, wt=True)
    y = _ssd_core(z, xbc, dt, dt_bias, a_log, d_skip, norm_w, B=B, T=T)
    return _out_proj_ln(y, w_out.astype(BF16), hf, g, b)


FOX_BIAS_PIECES = 3


def _fox_bias_kernel(f_ref, b_ref, qx_ref, kx_ref, carry_ref, *, nblk):
    blk, W, P = LANES, FOX_WIDTH, FOX_BIAS_PIECES
    row = lax.broadcasted_iota(jnp.int32, (blk, blk), 0)
    col = lax.broadcasted_iota(jnp.int32, (blk, blk), 1)
    tri = (row >= col).astype(F32)
    bias = b_ref[...]
    H = FOX_HEADS
    r = lax.broadcasted_iota(jnp.int32, (blk, W), 0)
    c = lax.broadcasted_iota(jnp.int32, (blk, W), 1)
    head, piece = r % H, r // H
    is_piece = r < P * H
    place_q = jnp.logical_and(is_piece, c == head * blk + piece).astype(BF16)
    place_k = jnp.logical_and(is_piece, c == head * blk + P + piece).astype(BF16)
    is_head = col < H
    cmod = lax.broadcasted_iota(jnp.int32, (1, W), 1) % blk
    ones_q = jnp.logical_and(cmod >= P, cmod < 2 * P).astype(F32)
    ones_k = (cmod < P).astype(F32)

    def body(t, carry):
        rows = pl.ds(pl.multiple_of(t * blk, blk), blk)
        x = f_ref[rows, :] + bias
        logf = jnp.minimum(x, 0.0) - jnp.log1p(jnp.exp(-jnp.abs(x)))
        cum = jnp.dot(tri, logf, precision=HIGHEST, preferred_element_type=F32) + carry
        hi = jnp.where(is_head, cum, 0.0).astype(BF16).astype(F32)
        r1 = jnp.where(is_head, cum, 0.0) - hi
        mid = r1.astype(BF16).astype(F32)
        lo = (r1 - mid).astype(BF16).astype(F32)
        pieces = (hi + pltpu.roll(mid, H, 1) + pltpu.roll(lo, 2 * H, 1)).astype(BF16)
        qx_ref[rows, :] = (jnp.dot(pieces, place_q, preferred_element_type=F32)
                           + ones_q).astype(BF16)
        kx_ref[rows, :] = (ones_k - jnp.dot(pieces, place_k, preferred_element_type=F32)
                           ).astype(BF16)
        return cum[blk - 1:blk, :]

    @pl.when(pl.program_id(1) == 0)
    def _():
        carry_ref[...] = jnp.zeros_like(carry_ref)

    carry_ref[...] = lax.fori_loop(0, nblk, body, carry_ref[...])


def _fox_bias_columns(f, f_bias, *, B, T):
    bias = jnp.pad(f_bias, (0, LANES - FOX_HEADS)).reshape(1, LANES)
    tt = min(512, T)
    out = jax.ShapeDtypeStruct((B, T, FOX_WIDTH), BF16)
    spec = pl.BlockSpec((None, tt, FOX_WIDTH), lambda b, t: (b, t, 0))
    qx, kx = pl.pallas_call(
        functools.partial(_fox_bias_kernel, nblk=tt // LANES),
        grid=(B, T // tt),
        in_specs=[pl.BlockSpec((None, tt, LANES), lambda b, t: (b, t, 0)),
                  pl.BlockSpec((1, LANES), lambda b, t: (0, 0))],
        out_specs=[spec, spec],
        out_shape=[out, out],
        scratch_shapes=[pltpu.VMEM((1, LANES), F32)],
        compiler_params=_cparams("parallel", "arbitrary"),
        name="fox_cum_log_forget",
    )(f.reshape(B, T, LANES), bias)
    return qx.reshape(B * T, FOX_WIDTH), kx.reshape(B * T, FOX_WIDTH)


def _fox_layer(hf, hb, w_in, f_bias, w_out, g, b, *, B, T, layer=0):
    W, H, dh = FOX_WIDTH, FOX_HEADS, FOX_HEAD_DIM
    w_t = jnp.swapaxes(w_in, -1, -2)
    q = _matmul(hb, w_t, BF16, layer=layer, col0=0, ncols=W, scale=dh ** -0.5, wt=True)
    kv = _matmul(hb, w_t, BF16, layer=layer, col0=W, ncols=2 * W, wt=True)
    z = _matmul(hb, w_t, F32, layer=layer, col0=3 * W, ncols=W, wt=True)
    f = _matmul(hb, _pad_rows(_layer_of(w_t, layer)[4 * W:], LANES), F32, wt=True)
    qx, kx = _fox_bias_columns(f, f_bias, B=B, T=T)

    tq, tk = _attn_tiles(T)
    nq = T // tq
    hw = ATTN_HEADS_PER_STEP * dh
    q_tile = pl.BlockSpec((tq, hw), lambda bb, h, i: (bb * nq + i, h))
    seq_k = pl.BlockSpec((T, hw), lambda bb, h, i: (bb, h))
    seq_v = pl.BlockSpec((T, hw), lambda bb, h, i: (bb, W // hw + h))
    in_specs = [q_tile, q_tile, seq_k, seq_k, seq_v, q_tile]
    y = _attention((q, qx, kv, kx, kv, z), in_specs, q_tile,
                   B=B, T=T, H=H, dv=dh, tq=tq, tk=tk, nparts=2, head_major=False)
    return _out_proj_ln(y, w_out.astype(BF16), hf, g, b)


def _rope_table_kernel(pos_ref, freq_ref, sign_ref, cos_ref, sin_ref):
    ang = pos_ref[...].astype(F32) * freq_ref[...]
    cos_ref[...] = jnp.cos(ang)
    sin_ref[...] = jnp.sin(ang) * sign_ref[...]


def _rope_tables(positions, *, B, T):
    half = MLA_ROPE // 2
    inv_freq = ROPE_BASE ** (-jnp.arange(0, MLA_ROPE, 2, dtype=F32) / MLA_ROPE)
    freq = jnp.tile(inv_freq, LANES // half).reshape(1, LANES)
    sign = jnp.tile(jnp.concatenate([-jnp.ones((half,), F32), jnp.ones((half,), F32)]),
                    LANES // MLA_ROPE).reshape(1, LANES)
    M = B * T
    tm = min(1024, M)
    return pl.pallas_call(
        _rope_table_kernel,
        grid=(M // tm,),
        in_specs=[pl.BlockSpec((tm, 1), lambda i: (i, 0)),
                  pl.BlockSpec((1, LANES), lambda i: (0, 0)),
                  pl.BlockSpec((1, LANES), lambda i: (0, 0))],
        out_specs=[pl.BlockSpec((tm, LANES), lambda i: (i, 0)),
                   pl.BlockSpec((tm, LANES), lambda i: (i, 0))],
        out_shape=[jax.ShapeDtypeStruct((M, LANES), F32), jax.ShapeDtypeStruct((M, LANES), F32)],
        compiler_params=_cparams("parallel"),
        name="rope_tables",
    )(positions.reshape(M, 1), freq, sign)


def _rms_to_bf16(x, w):
    y = x * lax.rsqrt(jnp.mean(x * x, axis=-1, keepdims=True) + RMS_EPS)
    return (y * w).astype(BF16)


def _rope_pair(c2, cos, sin):
    return c2 * cos + pltpu.roll(c2, MLA_ROPE, 1) * sin


def _mla_qkv_kernel(x_ref, wl_ref, qn_ref, kn_ref, wq_ref, wkv_ref, cos_ref, sin_ref,
                    q_ref, k_ref, v_ref, *, scale):
    qr, kr = MLA_Q_RANK, MLA_KV_RANK
    cos, sin = cos_ref[...], sin_ref[...]
    lat = _project(x_ref[...], wl_ref[...], True)
    qn = _rms_to_bf16(lat[:, :qr], qn_ref[...])
    kn = _rms_to_bf16(lat[:, qr:qr + kr], kn_ref[...])
    k_pe = _rope_pair(lat[:, qr + kr:], cos, sin)[:, :MLA_ROPE].astype(k_ref.dtype)
    for h in range(MLA_HEADS):
        r = jnp.dot(qn, wq_ref[h], preferred_element_type=F32)
        pe = _rope_pair(r[:, MLA_NOPE:], cos, sin)
        q_ref[h, :, :MLA_NOPE] = (r[:, :MLA_NOPE] * scale).astype(q_ref.dtype)
        q_ref[h, :, MLA_NOPE:] = (pe[:, :MLA_ROPE] * scale).astype(q_ref.dtype)
        r = jnp.dot(kn, wkv_ref[h], preferred_element_type=F32)
        k_ref[h, :, :MLA_NOPE] = r[:, :MLA_NOPE].astype(k_ref.dtype)
        k_ref[h, :, MLA_NOPE:] = k_pe
        v_ref[h] = r[:, MLA_NOPE:].astype(v_ref.dtype)


def _swap_halves(w):
    half = w.shape[-1] // 2
    return jnp.concatenate([w[..., half:], w[..., :half]], axis=-1)


def _mla_layer(hf, hb, positions, w_in, q_norm, kv_norm, w_q_up, w_kv_up, w_out, g, b, *, B, T):
    H, M = MLA_HEADS, B * T
    qr, kr = MLA_Q_RANK, MLA_KV_RANK
    lat_end = qr + kr + MLA_ROPE
    w_t = jnp.swapaxes(w_in, -1, -2)
    w_pe = w_t[qr + kr:lat_end]
    half = MLA_ROPE // 2
    w_lat = jnp.concatenate([w_t[:qr + kr], w_pe, w_pe[half:], w_pe[:half]], axis=0).astype(BF16)
    nlat = w_lat.shape[0]
    z = _matmul(hb, w_t[lat_end:], F32, wt=True)
    cos, sin = _rope_tables(positions, B=B, T=T)

    wq = w_q_up.reshape(qr, H, MLA_QK)
    wq = jnp.concatenate([wq, _swap_halves(wq[..., MLA_NOPE:])], axis=-1)
    wq = jnp.transpose(wq, (1, 0, 2)).astype(BF16)
    wkv = jnp.transpose(w_kv_up.reshape(kr, H, MLA_NOPE + MLA_V), (1, 0, 2)).astype(BF16)

    tm = min(256, M)
    wide = MLA_NOPE + LANES
    row = lambda i: (i, 0)
    q, k, v = pl.pallas_call(
        functools.partial(_mla_qkv_kernel, scale=MLA_QK ** -0.5),
        grid=(M // tm,),
        in_specs=[pl.BlockSpec((tm, D_MODEL), row),
                  _resident((nlat, D_MODEL), lambda i: (0, 0)),
                  pl.BlockSpec((1, qr), lambda i: (0, 0)),
                  pl.BlockSpec((1, kr), lambda i: (0, 0)),
                  _resident((H, qr, wide), lambda i: (0, 0, 0)),
                  _resident((H, kr, MLA_NOPE + MLA_V), lambda i: (0, 0, 0)),
                  pl.BlockSpec((tm, LANES), row),
                  pl.BlockSpec((tm, LANES), row)],
        out_specs=[pl.BlockSpec((H, tm, MLA_QK), lambda i: (0, i, 0)),
                   pl.BlockSpec((H, tm, MLA_QK), lambda i: (0, i, 0)),
                   pl.BlockSpec((H, tm, MLA_V), lambda i: (0, i, 0))],
        out_shape=[jax.ShapeDtypeStruct((H, M, MLA_QK), BF16),
                   jax.ShapeDtypeStruct((H, M, MLA_QK), BF16),
                   jax.ShapeDtypeStruct((H, M, MLA_V), BF16)],
        compiler_params=_cparams("parallel"),
        name="mla_qkv_rope",
    )(hb, w_lat, q_norm.reshape(1, qr), kv_norm.reshape(1, kr), wq, wkv, cos, sin)

    tq, tk = _attn_tiles(T)
    nq = T // tq
    nh = ATTN_HEADS_PER_STEP
    in_specs = [
        pl.BlockSpec((nh, tq, MLA_QK), lambda bb, h, i: (h, bb * nq + i, 0)),
        pl.BlockSpec((nh, T, MLA_QK), lambda bb, h, i: (h, bb, 0)),
        pl.BlockSpec((nh, T, MLA_V), lambda bb, h, i: (h, bb, 0)),
        pl.BlockSpec((tq, nh * MLA_V), lambda bb, h, i: (bb * nq + i, h)),
    ]
    out_spec = pl.BlockSpec((tq, nh * MLA_V), lambda bb, h, i: (bb * nq + i, h))
    y = _attention((q, k, v, z), in_specs, out_spec,
                   B=B, T=T, H=H, dv=MLA_V, tq=tq, tk=tk, nparts=1, head_major=True)
    return _out_proj_ln(y, w_out.astype(BF16), hf, g, b)


def _s5_matrices(lam_re, lam_im, log_step, b_re, b_im, c_re, c_im):
    L = S5_CHUNK
    step = jnp.exp(log_step.astype(F32))[:, None]
    mag = jnp.exp(lam_re * step)
    ar = mag * jnp.cos(lam_im * step)
    ai = mag * jnp.sin(lam_im * step)
    den = lam_re * lam_re + lam_im * lam_im
    fr = ((ar - 1.0) * lam_re + ai * lam_im) / den
    fi = (ai * lam_re - (ar - 1.0) * lam_im) / den
    bbr = fr[..., None] * b_re - fi[..., None] * b_im
    bbi = fr[..., None] * b_im + fi[..., None] * b_re
    pr, pi = [jnp.ones_like(ar)], [jnp.zeros_like(ar)]
    for _ in range(L):
        pr_next = pr[-1] * ar - pi[-1] * ai
        pi_next = pr[-1] * ai + pi[-1] * ar
        pr.append(pr_next)
        pi.append(pi_next)
    pwr = jnp.stack(pr, axis=1)
    pwi = jnp.stack(pi, axis=1)

    G, P, I = S5_GROUPS, S5_STATE, S5_GROUP
    TG, SG = S5_TOEP_GROUPS, S5_STATE_GROUPS
    NTB, NSB = G // TG, G // SG

    c_ji = lambda c: jnp.tile(jnp.transpose(c, (0, 2, 1)), (1, 1, I))
    b_ji = lambda v: jnp.repeat(v, I, axis=2)
    cbr = c_ji(c_re) * b_ji(bbr) - c_ji(c_im) * b_ji(bbi)
    cbi = c_ji(c_re) * b_ji(bbi) + c_ji(c_im) * b_ji(bbr)
    kern = jnp.einsum('gdp,gpq->gdq', jnp.concatenate([pwr[:, :L], -pwi[:, :L]], axis=2),
                      jnp.concatenate([cbr, cbi], axis=1), precision=HIGHEST)
    idx = jnp.arange

    kc = jnp.transpose(kern.reshape(NTB, TG, L, I, I), (2, 0, 1, 3, 4)).reshape(L, NTB, TG * I, I)
    kc = jnp.pad(kc, ((L - 1, 0), (0, 0), (0, 0), (0, 0)))
    repeat_i = (idx(TG * I)[None, :] % I == idx(I)[:, None]).astype(F32)
    same_t = (idx(TG * I)[:, None] // I == idx(TG * I)[None, :] // I)
    kd = jnp.where(same_t, jnp.einsum('dcri,iq->dcrq', kc, repeat_i, precision=HIGHEST), 0.0)

    by_step = lambda w: jnp.transpose(
        w[:, :L][:, ::-1].reshape(NSB, SG, L, P), (2, 0, 1, 3))[:, :, :, None]
    by_chan = lambda v: jnp.transpose(v, (0, 2, 1)).reshape(1, NSB, SG, I, P)
    qr, qi = by_step(pwr), by_step(pwi)
    br, bi = by_chan(bbr), by_chan(bbi)
    twice = lambda m: jnp.concatenate([m, m], axis=-1).reshape(L, NSB, SG * I, 2 * P)
    bp_re, bp_im = twice(qr * br - qi * bi), twice(qr * bi + qi * br)

    chan = lambda c: jnp.transpose(c.reshape(NSB, SG, I, P), (0, 3, 1, 2)).reshape(1, NSB, P, SG * I)
    step_pow = lambda w: jnp.repeat(
        jnp.transpose(w[:, 1:].reshape(NSB, SG, L, P), (2, 0, 3, 1)), I, axis=3)
    cr, ci, wr, wi = chan(c_re), chan(c_im), step_pow(pwr), step_pow(pwi)
    cp = jnp.concatenate([cr * wr - ci * wi, -(cr * wi + ci * wr)], axis=2)

    alr = pwr[:, L].reshape(NSB, SG * P)
    ali = pwi[:, L].reshape(NSB, SG * P)
    a1 = jnp.concatenate([alr, alr], axis=1).reshape(1, -1)
    a2 = jnp.concatenate([-ali, ali], axis=1).reshape(1, -1)
    return kd.astype(BF16), bp_re.astype(BF16), bp_im.astype(BF16), cp.astype(BF16), a1, a2


def _chunk_step(u_ref, s):
    return u_ref[pl.ds(s, u_ref.shape[0] // S5_CHUNK, stride=S5_CHUNK), :]


def _s5_local_state_kernel(u_ref, bpr_ref, bpi_ref, s_ref):
    SG, P, I = S5_STATE_GROUPS, S5_STATE, S5_GROUP
    own = (lax.broadcasted_iota(jnp.int32, (SG * I, SG * P), 0) // I
           == lax.broadcasted_iota(jnp.int32, (SG * I, SG * P), 1) // P)

    def spread(m):
        tiled = jnp.concatenate([m] * (SG // 2), axis=1)
        return jnp.where(own, tiled, jnp.zeros_like(tiled))

    acc = None
    for s in range(S5_CHUNK):
        bd = jnp.concatenate([spread(bpr_ref[s]), spread(bpi_ref[s])], axis=1)
        d = jnp.dot(_chunk_step(u_ref, s).astype(BF16), bd, preferred_element_type=F32)
        acc = d if acc is None else acc + d
    s_ref[...] = acc


def _s5_scan_kernel(s_ref, a1_ref, a2_ref, o_ref, *, B, nch):
    W = s_ref.shape[1]
    a1 = a1_ref[...]
    a2 = a2_ref[...]
    unit = 2 * S5_STATE_GROUPS * S5_STATE

    def swap_re_im(c):
        parts = []
        for q in range(W // unit):
            lo = q * unit
            parts += [c[:, lo + unit // 2:lo + unit], c[:, lo:lo + unit // 2]]
        return jnp.concatenate(parts, axis=1)

    def body(n, carry):
        new = []
        for bb in range(B):
            row = bb * nch + n
            c = carry[bb]
            o_ref[pl.ds(row, 1), :] = c
            new.append(a1 * c + a2 * swap_re_im(c) + s_ref[pl.ds(row, 1), :])
        return tuple(new)

    lax.fori_loop(0, nch, body, tuple(jnp.zeros((1, W), F32) for _ in range(B)))


def _gelu_tanh(y):
    return 0.5 * y * (1.0 + jnp.tanh(math.sqrt(2.0 / math.pi) * (y + 0.044715 * (y * y * y))))


def _s5_out_kernel(*refs):
    L = S5_CHUNK
    nu = S5_TOEP_GROUPS // S5_STATE_GROUPS
    u_refs, (kd_ref, st_ref, cp_ref, o_ref, ub_ref, acc_ref) = refs[:nu], refs[nu:]
    r = pl.program_id(1)

    @pl.when(r == 0)
    def _():
        for s in range(L):
            for k in range(nu):
                ub_ref[s, :, k * LANES:(k + 1) * LANES] = _chunk_step(u_refs[k], s).astype(BF16)

    def lags(first):
        out = None
        for s in range(first, first + S5_STEP_GROUP):
            d = jnp.dot(ub_ref[s], kd_ref[r - s + (L - 1)], preferred_element_type=F32)
            out = d if out is None else out + d
        return out

    SG, P, I = S5_STATE_GROUPS, S5_STATE, S5_GROUP
    col_group = lax.broadcasted_iota(jnp.int32, (P, SG * I), 1) // I

    def spread(c):
        pieces = [jnp.where(col_group == g, c[h * P:(h + 1) * P], jnp.zeros((P, SG * I), BF16))
                  for h in range(2) for g in range(SG)]
        return jnp.concatenate(pieces, axis=0)

    st = st_ref[...].astype(BF16)
    half = st.shape[1] // 2
    carried = jnp.concatenate(
        [jnp.dot(st[:, :half], spread(cp_ref[0]), preferred_element_type=F32),
         jnp.dot(st[:, half:], spread(cp_ref[1]), preferred_element_type=F32)], axis=1)
    acc_ref[...] = carried + lags(0)
    for first in range(S5_STEP_GROUP, L, S5_STEP_GROUP):
        @pl.when(first <= r)
        def _(first=first):
            acc_ref[...] += lags(first)
    o_ref[...] = acc_ref[...].astype(o_ref.dtype)


def _s5_glu_out_kernel(ys_ref, u_ref, d_ref, wg_ref, bg_ref, z_ref, wo_ref, h_ref, g_ref, b_ref,
                       of_ref, ob_ref, yt_ref):
    L, W = S5_CHUNK, S5_WIDTH
    nslab, tm = yt_ref.shape[0], yt_ref.shape[1]
    for s in range(L):
        for k in range(nslab):
            lo = s * W + k * LANES
            yt_ref[k, pl.ds(s, tm // L, stride=L), :] = ys_ref[:, lo:lo + LANES].astype(F32)
    half = tm // 2
    parts = [pl.ds(0, half), pl.ds(half, half)]
    ys = [_gelu_tanh(jnp.concatenate([yt_ref[k, rows, :] for k in range(nslab)], axis=1)
                     + d_ref[...] * u_ref[rows, :]) for rows in parts]
    ts = [jnp.dot(y.astype(BF16), wg_ref[...], preferred_element_type=F32) for y in ys]
    gated = [(y * _sigmoid(t + bg_ref[...]) * _silu(z_ref[rows, :])).astype(BF16)
             for y, t, rows in zip(ys, ts, parts)]
    branches = [jnp.dot(y, wo_ref[...], preferred_element_type=F32) for y in gated]
    for rows, branch in zip(parts, branches):
        out = _deepnorm_ln(h_ref[rows, :], branch, g_ref[...], b_ref[...])
        of_ref[rows, :] = out
        ob_ref[rows, :] = out.astype(BF16)


def _s5_layer(hf, hb, w_in, lam_re, lam_im, log_step, b_re, b_im, c_re, c_im, d_skip,
              w_glu, b_glu, w_out, g, b, *, B, T, layer=0):
    M, W, G, L = B * T, S5_WIDTH, S5_GROUPS, S5_CHUNK
    nch = T // L
    rows = B * nch
    u = _matmul(hb, w_in, F32, layer=layer, col0=0, ncols=W)
    z = _matmul(hb, w_in, F32, layer=layer, col0=W, ncols=W)
    kd, bp_re, bp_im, cp, a1, a2 = _s5_matrices(lam_re, lam_im, log_step, b_re, b_im, c_re, c_im)

    TG, SG = S5_TOEP_GROUPS, S5_STATE_GROUPS
    ntb, nsb = G // TG, G // SG
    tw, sw_in, sw = TG * S5_GROUP, SG * S5_GROUP, 2 * SG * S5_STATE
    nstate = nsb * sw
    s_loc = pl.pallas_call(
        _s5_local_state_kernel,
        grid=(nsb,),
        in_specs=[pl.BlockSpec((M, sw_in), lambda cb: (0, cb)),
                  pl.BlockSpec((L, None, sw_in, LANES), lambda cb: (0, cb, 0, 0)),
                  pl.BlockSpec((L, None, sw_in, LANES), lambda cb: (0, cb, 0, 0))],
        out_specs=pl.BlockSpec((rows, sw), lambda cb: (0, cb)),
        out_shape=jax.ShapeDtypeStruct((rows, nstate), F32),
        compiler_params=_cparams("parallel"),
        name="s5_chunk_state",
    )(u, bp_re, bp_im)

    scan_w = 2 * sw
    st_in = pl.pallas_call(
        functools.partial(_s5_scan_kernel, B=B, nch=nch),
        grid=(nstate // scan_w,),
        in_specs=[pl.BlockSpec((rows, scan_w), lambda p: (0, p)),
                  pl.BlockSpec((1, scan_w), lambda p: (0, p)),
                  pl.BlockSpec((1, scan_w), lambda p: (0, p))],
        out_specs=pl.BlockSpec((rows, scan_w), lambda p: (0, p)),
        out_shape=jax.ShapeDtypeStruct((rows, nstate), F32),
        compiler_params=_cparams("parallel"),
        name="s5_state_scan",
    )(s_loc, a1, a2)

    per_toep = TG // SG
    y = pl.pallas_call(
        _s5_out_kernel,
        grid=(ntb, L),
        in_specs=[pl.BlockSpec((M, sw_in), lambda cb, r, k=k: (0, per_toep * cb + k))
                  for k in range(per_toep)]
        + [pl.BlockSpec((2 * L - 1, None, tw, tw), lambda cb, r: (0, cb, 0, 0)),
                  pl.BlockSpec((rows, per_toep * sw), lambda cb, r: (0, cb)),
                  pl.BlockSpec((None, per_toep, LANES, sw_in), lambda cb, r: (r, cb, 0, 0))],
        out_specs=pl.BlockSpec((rows, tw), lambda cb, r: (0, r * ntb + cb)),
        out_shape=jax.ShapeDtypeStruct((rows, L * W), BF16),
        scratch_shapes=[pltpu.VMEM((L, rows, tw), BF16), pltpu.VMEM((rows, tw), F32)],
        compiler_params=_cparams("parallel", "arbitrary"),
        name="s5_chunk_output",
    )(*([u] * per_toep), kd, st_in, cp)

    tm = min(256, M)
    row = lambda i: (i, 0)
    fixed = lambda i: (0, 0)
    return pl.pallas_call(
        _s5_glu_out_kernel,
        grid=(M // tm,),
        in_specs=[pl.BlockSpec((tm // L, L * W), row),
                  pl.BlockSpec((tm, W), row),
                  pl.BlockSpec((1, W), fixed),
                  _resident((W, W), fixed),
                  pl.BlockSpec((1, W), fixed),
                  pl.BlockSpec((tm, W), row),
                  _resident((W, D_MODEL), fixed),
                  pl.BlockSpec((tm, D_MODEL), row),
                  pl.BlockSpec((1, D_MODEL), fixed),
                  pl.BlockSpec((1, D_MODEL), fixed)],
        out_specs=[pl.BlockSpec((tm, D_MODEL), row), pl.BlockSpec((tm, D_MODEL), row)],
        out_shape=[jax.ShapeDtypeStruct((M, D_MODEL), F32),
                   jax.ShapeDtypeStruct((M, D_MODEL), BF16)],
        scratch_shapes=[pltpu.VMEM((W // LANES, tm, LANES), F32)],
        compiler_params=_cparams("parallel"),
        name="s5_glu_out_ln",
    )(y, u, d_skip.reshape(1, W), w_glu.astype(BF16), b_glu.reshape(1, W), z,
      w_out.astype(BF16), hf, g.reshape(1, D_MODEL), b.reshape(1, D_MODEL))


def kernel(x, positions, ln_g, ln_b, ssd_w_in, ssd_conv_w, ssd_conv_b, ssd_dt_bias, ssd_a_log, ssd_d, ssd_norm_w, ssd_w_out, fox_w_in, fox_f_bias, fox_w_out, mla_w_in, mla_q_norm, mla_kv_norm, mla_w_q_up, mla_w_kv_up, mla_w_out, s5_w_in, s5_lambda_re, s5_lambda_im, s5_log_step, s5_b_re, s5_b_im, s5_c_re, s5_c_im, s5_d, s5_w_glu, s5_b_glu, s5_w_out):
    B, T, D = x.shape
    hf = x.reshape(B * T, D)
    hb = hf.astype(BF16)
    for i in range(DEPTH):
        j = i // 4
        kind = i % 4
        g, b = ln_g[i], ln_b[i]
        if kind == 0:
            hf, hb = _ssd_layer(hf, hb, ssd_w_in, ssd_conv_w[j], ssd_conv_b[j], ssd_dt_bias[j],
                                ssd_a_log[j], ssd_d[j], ssd_norm_w[j], ssd_w_out[j], g, b,
                                B=B, T=T, layer=j)
        elif kind == 1:
            hf, hb = _fox_layer(hf, hb, fox_w_in, fox_f_bias[j], fox_w_out[j], g, b,
                                B=B, T=T, layer=j)
        elif kind == 2:
            hf, hb = _mla_layer(hf, hb, positions, mla_w_in[j], mla_q_norm[j],
                                mla_kv_norm[j], mla_w_q_up[j], mla_w_kv_up[j], mla_w_out[j],
                                g, b, B=B, T=T)
        else:
            hf, hb = _s5_layer(hf, hb, s5_w_in, s5_lambda_re[j], s5_lambda_im[j], s5_log_step[j],
                               s5_b_re[j], s5_b_im[j], s5_c_re[j], s5_c_im[j], s5_d[j],
                               s5_w_glu[j], s5_b_glu[j], s5_w_out[j], g, b, B=B, T=T, layer=j)
    return hf.reshape(B, T, D)
```

```python
import functools
import math

import jax
import jax.numpy as jnp
from jax import lax
from jax.experimental import pallas as pl
from jax.experimental.pallas import tpu as pltpu

F32 = jnp.float32
BF16 = jnp.bfloat16
HIGHEST = lax.Precision.HIGHEST

D_MODEL = 2048
DEPTH = 4
ALPHA = (2.0 * DEPTH) ** 0.25
LN_EPS = 1e-5
RMS_EPS = 1e-6

SSD_D_INNER = 4096
SSD_HEADS = 64
SSD_HEADDIM = 64
SSD_GROUPS = 8
SSD_HEADS_PER_GROUP = SSD_HEADS // SSD_GROUPS
SSD_STATE = 128
SSD_CONV = 4
SSD_CHUNK = 128
SSD_GROUP_WIDTH = SSD_D_INNER // SSD_GROUPS
SSD_BC_WIDTH = SSD_GROUPS * SSD_STATE
SSD_GROUPS_PER_STEP = 8

FOX_HEADS = 16
FOX_HEAD_DIM = 128
FOX_WIDTH = FOX_HEADS * FOX_HEAD_DIM

MLA_HEADS = 16
MLA_Q_RANK = 512
MLA_KV_RANK = 512
MLA_NOPE = 128
MLA_ROPE = 64
MLA_V = 128
MLA_QK = MLA_NOPE + MLA_ROPE
MLA_WIDTH = MLA_HEADS * MLA_V
ROPE_BASE = 10000.0

S5_WIDTH = D_MODEL
S5_GROUP = 16
S5_GROUPS = S5_WIDTH // S5_GROUP
S5_STATE = 64
S5_CHUNK = 8
S5_TOEP_GROUPS = 16
S5_STATE_GROUPS = 8
S5_STEP_GROUP = 4

LANES = 128
VMEM_LIMIT_BYTES = 48 * 1024 * 1024


def _cparams(*sem):
    return pltpu.CompilerParams(dimension_semantics=sem, vmem_limit_bytes=VMEM_LIMIT_BYTES)


def _resident(block_shape, index_map):
    return pl.BlockSpec(block_shape, index_map, pipeline_mode=pl.Buffered(1))


def _sigmoid(x):
    return 0.5 * (jnp.tanh(0.5 * x) + 1.0)


def _silu(x):
    h = 0.5 * x
    return h + h * jnp.tanh(h)


def _softplus(x):
    return jnp.maximum(x, 0.0) + jnp.log1p(jnp.exp(-jnp.abs(x)))


def _deepnorm_ln(h, branch, g, b):
    r = ALPHA * h + branch
    mu = jnp.mean(r, axis=-1, keepdims=True)
    d = r - mu
    var = jnp.mean(d * d, axis=-1, keepdims=True)
    return d * lax.rsqrt(var + LN_EPS) * g + b


def _project(x, wb, wt):
    dims = (((1,), (1,)), ((), ())) if wt else (((1,), (0,)), ((), ()))
    return lax.dot_general(x, wb, dims, preferred_element_type=F32)


def _mm_kernel(x_ref, w_ref, o_ref, wb_ref, *, scale, wt):
    @pl.when(pl.program_id(1) == 0)
    def _():
        wb_ref[...] = w_ref[...].astype(BF16)

    acc = _project(x_ref[...].astype(BF16), wb_ref[...], wt)
    if scale is not None:
        acc = acc * scale
    o_ref[...] = acc.astype(o_ref.dtype)


def _weight_spec(w, layer, K, tn, j0, wt):
    block = (tn, K) if wt else (K, tn)
    index = (lambda j: (j0 + j, 0)) if wt else (lambda j: (0, j0 + j))
    if w.ndim == 2:
        return pl.BlockSpec(block, lambda j, i: index(j))
    return pl.BlockSpec((None,) + block, lambda j, i: (layer,) + index(j))


def _matmul(x, w, out_dtype, *, layer=0, col0=0, ncols=None, scale=None, tn=1024, wt=False):
    M, K = x.shape
    N = w.shape[-2 if wt else -1] - col0 if ncols is None else ncols
    tm = min(1024 if x.dtype == BF16 else 512, M)
    tn = min(tn, N)
    assert M % tm == 0 and N % tn == 0 and col0 % tn == 0, (M, N, col0, tm, tn)
    j0 = col0 // tn
    return pl.pallas_call(
        functools.partial(_mm_kernel, scale=scale, wt=wt),
        grid=(N // tn, M // tm),
        in_specs=[pl.BlockSpec((tm, K), lambda j, i: (i, 0)),
                  _weight_spec(w, layer, K, tn, j0, wt)],
        out_specs=pl.BlockSpec((tm, tn), lambda j, i: (i, j)),
        out_shape=jax.ShapeDtypeStruct((M, N), out_dtype),
        scratch_shapes=[pltpu.VMEM((tn, K) if wt else (K, tn), BF16)],
        compiler_params=_cparams("parallel", "arbitrary"),
        name="proj",
    )(x, w)


CONV_COL_PARTS = 4


def _mm_conv_silu_kernel(x_ref, w_ref, cw_ref, cb_ref, o_ref, wb_ref, ext_ref,
                         *, taps, tiles_per_seq, wt):
    i = pl.program_id(1)
    tm, tn = o_ref.shape

    @pl.when(i == 0)
    def _():
        wb_ref[...] = w_ref[...].astype(BF16)

    @pl.when(i % tiles_per_seq == 0)
    def _():
        ext_ref[pl.ds(0, 8), :] = jnp.zeros((8, tn), F32)

    xb = x_ref[...].astype(BF16)
    slab = tn // CONV_COL_PARTS
    parts = [pl.ds(p * slab, slab) for p in range(CONV_COL_PARTS)]
    raws = [_project(xb, wb_ref[cols, :] if wt else wb_ref[:, cols], wt) for cols in parts]
    for cols, raw in zip(parts, raws):
        ext_ref[pl.ds(8, tm), cols] = raw
        acc = cb_ref[:, cols] + cw_ref[pl.ds(0, 1), cols] * ext_ref[pl.ds(9 - taps, tm), cols]
        for kk in range(1, taps):
            acc = acc + cw_ref[pl.ds(kk, 1), cols] * ext_ref[pl.ds(9 - taps + kk, tm), cols]
        ext_ref[pl.ds(0, 8), cols] = ext_ref[pl.ds(tm, 8), cols]
        o_ref[:, cols] = _silu(acc)


def _matmul_conv_silu(x, w, conv_w, conv_b, *, layer, col0, ncols, seq_len, tn=1024, wt=False):
    M, K = x.shape
    taps = conv_w.shape[0]
    tm = min(1024 if x.dtype == BF16 else 512, seq_len)
    assert M % tm == 0 and seq_len % tm == 0 and ncols % tn == 0 and col0 % tn == 0
    j0 = col0 // tn
    return pl.pallas_call(
        functools.partial(_mm_conv_silu_kernel, taps=taps, tiles_per_seq=seq_len // tm, wt=wt),
        grid=(ncols // tn, M // tm),
        in_specs=[pl.BlockSpec((tm, K), lambda j, i: (i, 0)),
                  _weight_spec(w, layer, K, tn, j0, wt),
                  pl.BlockSpec((taps, tn), lambda j, i: (0, j)),
                  pl.BlockSpec((1, tn), lambda j, i: (0, j))],
        out_specs=pl.BlockSpec((tm, tn), lambda j, i: (i, j)),
        out_shape=jax.ShapeDtypeStruct((M, ncols), F32),
        scratch_shapes=[pltpu.VMEM((tn, K) if wt else (K, tn), BF16),
                        pltpu.VMEM((tm + 8, tn), F32)],
        compiler_params=_cparams("parallel", "arbitrary"),
        name="proj_conv_silu",
    )(x, w, conv_w, conv_b.reshape(1, -1))


def _out_ln_kernel(y_ref, w_ref, h_ref, g_ref, b_ref, of_ref, ob_ref):
    half = y_ref.shape[0] // 2
    parts = [pl.ds(0, half), pl.ds(half, half)]
    branches = [jnp.dot(y_ref[rows, :], w_ref[...], preferred_element_type=F32) for rows in parts]
    for rows, branch in zip(parts, branches):
        out = _deepnorm_ln(h_ref[rows, :], branch, g_ref[...], b_ref[...])
        of_ref[rows, :] = out
        ob_ref[rows, :] = out.astype(BF16)


def _out_proj_ln(y, w, h, g, b):
    M, K = y.shape
    D = w.shape[1]
    tm = min(512 if K * D * w.dtype.itemsize <= VMEM_LIMIT_BYTES // 4 else 256, M)
    assert M % tm == 0
    return pl.pallas_call(
        _out_ln_kernel,
        grid=(M // tm,),
        in_specs=[pl.BlockSpec((tm, K), lambda i: (i, 0)),
                  _resident((K, D), lambda i: (0, 0)),
                  pl.BlockSpec((tm, D), lambda i: (i, 0)),
                  pl.BlockSpec((1, D), lambda i: (0, 0)),
                  pl.BlockSpec((1, D), lambda i: (0, 0))],
        out_specs=[pl.BlockSpec((tm, D), lambda i: (i, 0)),
                   pl.BlockSpec((tm, D), lambda i: (i, 0))],
        out_shape=[jax.ShapeDtypeStruct((M, D), F32), jax.ShapeDtypeStruct((M, D), BF16)],
        compiler_params=_cparams("parallel"),
        name="out_proj_ln",
    )(y, w, h, g.reshape(1, D), b.reshape(1, D))


ATTN_HEADS_PER_STEP = 4
ATTN_WIDE_BLOCKS = 2


def _attn_kernel(*refs, nparts, tq, tk, dv, head_major):
    q_refs = refs[:nparts]
    k_refs = refs[nparts:2 * nparts]
    v_ref, z_ref, o_ref, m_ref, acc_ref = refs[2 * nparts:]
    nh = ATTN_HEADS_PER_STEP
    i = pl.program_id(2)

    def head(ref, hh, rows=slice(None)):
        if head_major:
            return ref[hh, rows, :]
        w = ref.shape[-1] // nh
        return ref[rows, hh * w:(hh + 1) * w]

    def cat(pieces):
        return pieces[0] if len(pieces) == 1 else jnp.concatenate(pieces, axis=1)

    qs = [cat([head(r, hh) for r in q_refs]) for hh in range(nh)]
    m_ref[...] = jnp.full_like(m_ref, -jnp.inf)
    acc_ref[...] = jnp.zeros_like(acc_ref)

    def block(first, width, diag_offset):
        rows = pl.ds(pl.multiple_of(first * tk, tk), width)
        ones_cols = jnp.ones((width, LANES), BF16)
        scores = []
        for hh in range(nh):
            k = cat([head(r, hh, rows) for r in k_refs])
            scores.append(lax.dot_general(qs[hh], k, (((1,), (1,)), ((), ())),
                                          preferred_element_type=F32))
        probs, alphas = [], []
        for hh in range(nh):
            s = scores[hh]
            if diag_offset is not None:
                r = lax.broadcasted_iota(jnp.int32, (tq, width), 0)
                c = lax.broadcasted_iota(jnp.int32, (tq, width), 1)
                s = jnp.where(c + diag_offset <= r, s, -jnp.inf)
            m_prev = m_ref[hh]
            m_new = jnp.maximum(m_prev, jnp.max(s, axis=-1, keepdims=True))
            alpha = jnp.exp(m_prev - m_new)
            p = jnp.exp(s - jnp.concatenate([m_new] * (width // LANES), axis=1))
            m_ref[hh] = m_new
            probs.append(p.astype(BF16))
            alphas.append(jnp.concatenate([alpha] * ((dv + LANES) // LANES), axis=1))
        for hh in range(nh):
            v_aug = jnp.concatenate([head(v_ref, hh, rows), ones_cols], axis=1)
            acc_ref[hh] = alphas[hh] * acc_ref[hh] + jnp.dot(
                probs[hh], v_aug, preferred_element_type=F32)

    n_full = i * (tq // tk)
    wide = ATTN_WIDE_BLOCKS

    def wide_block(j, carry):
        block(j * wide, wide * tk, None)
        return carry

    lax.fori_loop(0, n_full // wide, wide_block, 0)
    if tq == tk and wide == 2:
        @pl.when(n_full % 2 == 1)
        def _():
            block(n_full - 1, 2 * tk, -tk)

        @pl.when(n_full % 2 == 0)
        def _():
            block(n_full, tk, 0)
    else:
        for rem in range(wide - 1):
            @pl.when(n_full % wide > rem)
            def _(rem=rem):
                block((n_full // wide) * wide + rem, tk, None)
        for d in range(tq // tk):
            block(n_full + d, tk, d * tk)

    for hh in range(nh):
        cols = slice(hh * dv, (hh + 1) * dv)
        o = acc_ref[hh, :, :dv] / acc_ref[hh, :, dv:]
        o_ref[:, cols] = (o * _silu(z_ref[:, cols])).astype(o_ref.dtype)


def _attention(inputs, in_specs, out_spec, *, B, T, H, dv, tq, tk, nparts, head_major):
    nh = ATTN_HEADS_PER_STEP
    assert tq % tk == 0 and T % tq == 0 and H % nh == 0
    return pl.pallas_call(
        functools.partial(_attn_kernel, nparts=nparts, tq=tq, tk=tk, dv=dv,
                          head_major=head_major),
        grid=(B, H // nh, T // tq),
        in_specs=in_specs,
        out_specs=out_spec,
        out_shape=jax.ShapeDtypeStruct((B * T, H * dv), BF16),
        scratch_shapes=[pltpu.VMEM((nh, tq, LANES), F32),
                        pltpu.VMEM((nh, tq, dv + LANES), F32)],
        compiler_params=_cparams("parallel", "parallel", "arbitrary"),
        name="causal_attention",
    )(*inputs)


def _attn_tiles(T):
    tq = min(512, T)
    tk = min(512, T)
    return tq, tk


def _ssd_kernel(x_ref, bm_ref, cm_ref, z_ref, dtr_ref,
                dtb_ref, al_ref, dsk_ref, nw_ref, spread_ref,
                o_ref,
                state_ref, yz_ref):
    Q, E, P, GW, N = SSD_CHUNK, SSD_HEADS_PER_GROUP, SSD_HEADDIM, SSD_GROUP_WIDTH, SSD_STATE
    NG = SSD_GROUPS_PER_STEP
    groups = range(NG)
    c = pl.program_id(2)

    @pl.when(c == 0)
    def _():
        state_ref[...] = jnp.zeros_like(state_ref)

    row = lax.broadcasted_iota(jnp.int32, (Q, Q), 0)
    col = lax.broadcasted_iota(jnp.int32, (Q, Q), 1)
    lower = row >= col
    tri_u = (row <= col).astype(BF16)
    left = col < P

    def pieces(v):
        hi = v.astype(BF16).astype(F32)
        mid = (v - hi).astype(BF16).astype(F32)
        lo = ((v - hi) - mid).astype(BF16).astype(F32)
        return jnp.concatenate([hi, mid, lo], axis=0).astype(BF16)

    npairs = E // 2
    dt_r, acs_r = [], []
    for gg in groups:
        dt = _softplus(dtr_ref[gg] + dtb_ref[gg])
        cum3 = jnp.dot(pieces(dt * (-jnp.exp(al_ref[gg]))), tri_u, preferred_element_type=F32)
        dt_r.append(dt)
        acs_r.append(cum3[:E] + cum3[E:2 * E] + cum3[2 * E:])

    bm_b, cm_b, bmt_b, cb, carried = [], [], [], [], []
    for gg in groups:
        ncols = pl.ds(gg * N, N)
        bm = bm_ref[:, ncols]
        bm_b.append(bm.astype(BF16))
        cm_b.append(cm_ref[:, ncols].astype(BF16))
        bmt_b.append(bm.T.astype(BF16))
        cb.append(lax.dot_general(cm_b[gg], bm_b[gg], (((1,), (1,)), ((), ())),
                                  preferred_element_type=F32))
        carried.append([jnp.dot(cm_b[gg], state_ref[gg * npairs + kp].astype(BF16),
                                preferred_element_type=F32)
                        for kp in range(npairs)])

    pair_cols = lambda gg, kp: pl.ds(gg * GW + kp * LANES, LANES)
    xps = [[x_ref[:, pair_cols(gg, kp)] for kp in range(npairs)] for gg in groups]
    gates = [[_silu(z_ref[:, pair_cols(gg, kp)]) for kp in range(npairs)] for gg in groups]

    spreads = [lax.dot_general(pieces(jnp.concatenate([dt_r[gg], acs_r[gg]], axis=0)),
                               spread_ref[...], (((0,), (0,)), ((), ())),
                               preferred_element_type=F32) for gg in groups]

    for gg in groups:
        dt_ch = spreads[gg][:, :GW]
        acs_ch = spreads[gg][:, GW:2 * GW]
        acs_col = spreads[gg][:, 2 * GW:]
        ssq = jnp.zeros((Q, 1), F32)
        for kp in range(npairs):
            cols = pair_cols(gg, kp)
            xp = xps[gg][kp]
            dt_p = dt_ch[:, kp * LANES:(kp + 1) * LANES]
            acs_p = acs_ch[:, kp * LANES:(kp + 1) * LANES]
            last_p = acs_p[Q - 1:Q, :]
            xdt = xp * dt_p
            xdt_b = xdt.astype(BF16)

            ys = []
            for e in (2 * kp, 2 * kp + 1):
                seg = acs_col[:, e * Q:(e + 1) * Q] - acs_r[gg][e:e + 1, :]
                dec = jnp.exp(jnp.where(lower, seg, -jnp.inf))
                ys.append(jnp.dot((cb[gg] * dec).astype(BF16), xdt_b,
                                  preferred_element_type=F32))
            y_diag = jnp.where(left, ys[0], ys[1])

            xw = (xdt * jnp.exp(last_p - acs_p)).astype(BF16)
            s_loc = jnp.dot(bmt_b[gg], xw, preferred_element_type=F32)
            y_off = carried[gg][kp] * jnp.exp(acs_p)
            slot = gg * npairs + kp
            state_ref[slot] = state_ref[slot] * jnp.exp(last_p) + s_loc

            y = (y_diag + y_off + dsk_ref[:, cols] * xp) * gates[gg][kp]
            yz_ref[:, cols] = y
            ssq = ssq + jnp.sum(y * y, axis=-1, keepdims=True)

        gcols = pl.ds(gg * GW, GW)
        inv = lax.rsqrt(ssq * (1.0 / GW) + RMS_EPS)
        o_ref[:, gcols] = (yz_ref[:, gcols] * inv * nw_ref[:, gcols]).astype(o_ref.dtype)


def _ssd_core(z, xbc, dt, dt_bias, a_log, d_skip, norm_w, *, B, T):
    G, E, Q = SSD_GROUPS, SSD_HEADS_PER_GROUP, SSD_CHUNK
    NC = T // Q
    GW, N = SSD_GROUP_WIDTH, SSD_STATE
    dtr = jnp.transpose(dt[:, :SSD_HEADS].reshape(B, T, G, E), (0, 2, 3, 1))
    head_rows = lambda v: v.reshape(G, E, 1)
    d_ch = jnp.repeat(d_skip, SSD_HEADDIM).reshape(1, -1)
    heads = jnp.arange(E)
    per_channel = (jnp.arange(GW)[None, :] // SSD_HEADDIM == heads[:, None]).astype(BF16)
    per_head_block = (jnp.arange(E * Q)[None, :] // Q == heads[:, None]).astype(BF16)
    none = lambda n: jnp.zeros((E, n), BF16)
    spread = jnp.concatenate([
        jnp.concatenate([per_channel, none(GW), none(E * Q)], axis=1),
        jnp.concatenate([none(GW), per_channel, per_head_block], axis=1)], axis=0)
    spread = jnp.tile(spread, (3, 1))

    NG = SSD_GROUPS_PER_STEP
    XW, BW = NG * GW, NG * N
    assert G % NG == 0
    bblk = SSD_D_INNER // BW
    cblk = (SSD_D_INNER + SSD_BC_WIDTH) // BW
    rowblk = lambda b, g, c: b * NC + c
    in_specs = [
        pl.BlockSpec((Q, XW), lambda b, g, c: (rowblk(b, g, c), g)),
        pl.BlockSpec((Q, BW), lambda b, g, c: (rowblk(b, g, c), bblk + g)),
        pl.BlockSpec((Q, BW), lambda b, g, c: (rowblk(b, g, c), cblk + g)),
        pl.BlockSpec((Q, XW), lambda b, g, c: (rowblk(b, g, c), g)),
        pl.BlockSpec((None, NG, E, Q), lambda b, g, c: (b, g, 0, c)),
        pl.BlockSpec((NG, E, 1), lambda b, g, c: (g, 0, 0)),
        pl.BlockSpec((NG, E, 1), lambda b, g, c: (g, 0, 0)),
        pl.BlockSpec((1, XW), lambda b, g, c: (0, g)),
        pl.BlockSpec((1, XW), lambda b, g, c: (0, g)),
        pl.BlockSpec(spread.shape, lambda b, g, c: (0, 0)),
    ]
    return pl.pallas_call(
        _ssd_kernel,
        grid=(B, G // NG, NC),
        in_specs=in_specs,
        out_specs=pl.BlockSpec((Q, XW), lambda b, g, c: (rowblk(b, g, c), g)),
        out_shape=jax.ShapeDtypeStruct((B * T, SSD_D_INNER), BF16),
        scratch_shapes=[pltpu.VMEM((NG * E // 2, N, LANES), F32), pltpu.VMEM((Q, XW), F32)],
        compiler_params=_cparams("parallel", "parallel", "arbitrary"),
        name="ssd_chunk_scan",
    )(xbc, xbc, xbc, z, dtr,
      head_rows(dt_bias), head_rows(a_log), d_ch, norm_w.reshape(1, -1), spread)


def _pad_rows(w, n):
    return jnp.pad(w, ((0, n - w.shape[0]), (0, 0)))


def _layer_of(w, layer):
    return w[layer] if w.ndim == 3 else w


def _ssd_layer(hf, hb, w_in, conv_w, conv_b, dt_bias, a_log, d_skip, norm_w, w_out, g, b,
               *, B, T, layer=0):
    di = SSD_D_INNER
    cd = di + 2 * SSD_BC_WIDTH
    w_t = jnp.swapaxes(w_in, -1, -2)
    z = _matmul(hb, w_t, F32, layer=layer, col0=0, ncols=di, wt=True)
    xbc = _matmul_conv_silu(hb, w_t, conv_w, conv_b, layer=layer, col0=di, ncols=cd, seq_len=T,
                            wt=True)
    dt = _matmul(hb, _pad_rows(_layer_of(w_t, layer)[di + cd:], LANES), F32, wt=True)
    y = _ssd_core(z, xbc, dt, dt_bias, a_log, d_skip, norm_w, B=B, T=T)
    return _out_proj_ln(y, w_out.astype(BF16), hf, g, b)


FOX_BIAS_PIECES = 3


def _fox_bias_kernel(f_ref, b_ref, qx_ref, kx_ref, carry_ref, *, nblk):
    blk, W, P = LANES, FOX_WIDTH, FOX_BIAS_PIECES
    row = lax.broadcasted_iota(jnp.int32, (blk, blk), 0)
    col = lax.broadcasted_iota(jnp.int32, (blk, blk), 1)
    tri = (row >= col).astype(F32)
    bias = b_ref[...]
    H = FOX_HEADS
    r = lax.broadcasted_iota(jnp.int32, (blk, W), 0)
    c = lax.broadcasted_iota(jnp.int32, (blk, W), 1)
    head, piece = r % H, r // H
    is_piece = r < P * H
    place_q = jnp.logical_and(is_piece, c == head * blk + piece).astype(BF16)
    place_k = jnp.logical_and(is_piece, c == head * blk + P + piece).astype(BF16)
    is_head = col < H
    cmod = lax.broadcasted_iota(jnp.int32, (1, W), 1) % blk
    ones_q = jnp.logical_and(cmod >= P, cmod < 2 * P).astype(F32)
    ones_k = (cmod < P).astype(F32)

    def body(t, carry):
        rows = pl.ds(pl.multiple_of(t * blk, blk), blk)
        x = f_ref[rows, :] + bias
        logf = jnp.minimum(x, 0.0) - jnp.log1p(jnp.exp(-jnp.abs(x)))
        cum = jnp.dot(tri, logf, precision=HIGHEST, preferred_element_type=F32) + carry
        hi = jnp.where(is_head, cum, 0.0).astype(BF16).astype(F32)
        r1 = jnp.where(is_head, cum, 0.0) - hi
        mid = r1.astype(BF16).astype(F32)
        lo = (r1 - mid).astype(BF16).astype(F32)
        pieces = (hi + pltpu.roll(mid, H, 1) + pltpu.roll(lo, 2 * H, 1)).astype(BF16)
        qx_ref[rows, :] = (jnp.dot(pieces, place_q, preferred_element_type=F32)
                           + ones_q).astype(BF16)
        kx_ref[rows, :] = (ones_k - jnp.dot(pieces, place_k, preferred_element_type=F32)
                           ).astype(BF16)
        return cum[blk - 1:blk, :]

    @pl.when(pl.program_id(1) == 0)
    def _():
        carry_ref[...] = jnp.zeros_like(carry_ref)

    carry_ref[...] = lax.fori_loop(0, nblk, body, carry_ref[...])


def _fox_bias_columns(f, f_bias, *, B, T):
    bias = jnp.pad(f_bias, (0, LANES - FOX_HEADS)).reshape(1, LANES)
    tt = min(512, T)
    out = jax.ShapeDtypeStruct((B, T, FOX_WIDTH), BF16)
    spec = pl.BlockSpec((None, tt, FOX_WIDTH), lambda b, t: (b, t, 0))
    qx, kx = pl.pallas_call(
        functools.partial(_fox_bias_kernel, nblk=tt // LANES),
        grid=(B, T // tt),
        in_specs=[pl.BlockSpec((None, tt, LANES), lambda b, t: (b, t, 0)),
                  pl.BlockSpec((1, LANES), lambda b, t: (0, 0))],
        out_specs=[spec, spec],
        out_shape=[out, out],
        scratch_shapes=[pltpu.VMEM((1, LANES), F32)],
        compiler_params=_cparams("parallel", "arbitrary"),
        name="fox_cum_log_forget",
    )(f.reshape(B, T, LANES), bias)
    return qx.reshape(B * T, FOX_WIDTH), kx.reshape(B * T, FOX_WIDTH)


def _fox_layer(hf, hb, w_in, f_bias, w_out, g, b, *, B, T, layer=0):
    W, H, dh = FOX_WIDTH, FOX_HEADS, FOX_HEAD_DIM
    w_t = jnp.swapaxes(w_in, -1, -2)
    q = _matmul(hb, w_t, BF16, layer=layer, col0=0, ncols=W, scale=dh ** -0.5, wt=True)
    kv = _matmul(hb, w_t, BF16, layer=layer, col0=W, ncols=2 * W, wt=True)
    z = _matmul(hb, w_t, F32, layer=layer, col0=3 * W, ncols=W, wt=True)
    f = _matmul(hb, _pad_rows(_layer_of(w_t, layer)[4 * W:], LANES), F32, wt=True)
    qx, kx = _fox_bias_columns(f, f_bias, B=B, T=T)

    tq, tk = _attn_tiles(T)
    nq = T // tq
    hw = ATTN_HEADS_PER_STEP * dh
    q_tile = pl.BlockSpec((tq, hw), lambda bb, h, i: (bb * nq + i, h))
    seq_k = pl.BlockSpec((T, hw), lambda bb, h, i: (bb, h))
    seq_v = pl.BlockSpec((T, hw), lambda bb, h, i: (bb, W // hw + h))
    in_specs = [q_tile, q_tile, seq_k, seq_k, seq_v, q_tile]
    y = _attention((q, qx, kv, kx, kv, z), in_specs, q_tile,
                   B=B, T=T, H=H, dv=dh, tq=tq, tk=tk, nparts=2, head_major=False)
    return _out_proj_ln(y, w_out.astype(BF16), hf, g, b)


def _rope_table_kernel(pos_ref, freq_ref, sign_ref, cos_ref, sin_ref):
    ang = pos_ref[...].astype(F32) * freq_ref[...]
    cos_ref[...] = jnp.cos(ang)
    sin_ref[...] = jnp.sin(ang) * sign_ref[...]


def _rope_tables(positions, *, B, T):
    half = MLA_ROPE // 2
    inv_freq = ROPE_BASE ** (-jnp.arange(0, MLA_ROPE, 2, dtype=F32) / MLA_ROPE)
    freq = jnp.tile(inv_freq, LANES // half).reshape(1, LANES)
    sign = jnp.tile(jnp.concatenate([-jnp.ones((half,), F32), jnp.ones((half,), F32)]),
                    LANES // MLA_ROPE).reshape(1, LANES)
    M = B * T
    tm = min(1024, M)
    return pl.pallas_call(
        _rope_table_kernel,
        grid=(M // tm,),
        in_specs=[pl.BlockSpec((tm, 1), lambda i: (i, 0)),
                  pl.BlockSpec((1, LANES), lambda i: (0, 0)),
                  pl.BlockSpec((1, LANES), lambda i: (0, 0))],
        out_specs=[pl.BlockSpec((tm, LANES), lambda i: (i, 0)),
                   pl.BlockSpec((tm, LANES), lambda i: (i, 0))],
        out_shape=[jax.ShapeDtypeStruct((M, LANES), F32), jax.ShapeDtypeStruct((M, LANES), F32)],
        compiler_params=_cparams("parallel"),
        name="rope_tables",
    )(positions.reshape(M, 1), freq, sign)


def _rms_to_bf16(x, w):
    y = x * lax.rsqrt(jnp.mean(x * x, axis=-1, keepdims=True) + RMS_EPS)
    return (y * w).astype(BF16)


def _rope_pair(c2, cos, sin):
    return c2 * cos + pltpu.roll(c2, MLA_ROPE, 1) * sin


def _mla_qkv_kernel(x_ref, wl_ref, qn_ref, kn_ref, wq_ref, wkv_ref, cos_ref, sin_ref,
                    q_ref, k_ref, v_ref, *, scale):
    qr, kr = MLA_Q_RANK, MLA_KV_RANK
    cos, sin = cos_ref[...], sin_ref[...]
    lat = _project(x_ref[...], wl_ref[...], True)
    qn = _rms_to_bf16(lat[:, :qr], qn_ref[...])
    kn = _rms_to_bf16(lat[:, qr:qr + kr], kn_ref[...])
    k_pe = _rope_pair(lat[:, qr + kr:], cos, sin)[:, :MLA_ROPE].astype(k_ref.dtype)
    for h in range(MLA_HEADS):
        r = jnp.dot(qn, wq_ref[h], preferred_element_type=F32)
        pe = _rope_pair(r[:, MLA_NOPE:], cos, sin)
        q_ref[h, :, :MLA_NOPE] = (r[:, :MLA_NOPE] * scale).astype(q_ref.dtype)
        q_ref[h, :, MLA_NOPE:] = (pe[:, :MLA_ROPE] * scale).astype(q_ref.dtype)
        r = jnp.dot(kn, wkv_ref[h], preferred_element_type=F32)
        k_ref[h, :, :MLA_NOPE] = r[:, :MLA_NOPE].astype(k_ref.dtype)
        k_ref[h, :, MLA_NOPE:] = k_pe
        v_ref[h] = r[:, MLA_NOPE:].astype(v_ref.dtype)


def _swap_halves(w):
    half = w.shape[-1] // 2
    return jnp.concatenate([w[..., half:], w[..., :half]], axis=-1)


def _mla_layer(hf, hb, positions, w_in, q_norm, kv_norm, w_q_up, w_kv_up, w_out, g, b, *, B, T):
    H, M = MLA_HEADS, B * T
    qr, kr = MLA_Q_RANK, MLA_KV_RANK
    lat_end = qr + kr + MLA_ROPE
    w_t = jnp.swapaxes(w_in, -1, -2)
    w_pe = w_t[qr + kr:lat_end]
    half = MLA_ROPE // 2
    w_lat = jnp.concatenate([w_t[:qr + kr], w_pe, w_pe[half:], w_pe[:half]], axis=0).astype(BF16)
    nlat = w_lat.shape[0]
    z = _matmul(hb, w_t[lat_end:], F32, wt=True)
    cos, sin = _rope_tables(positions, B=B, T=T)

    wq = w_q_up.reshape(qr, H, MLA_QK)
    wq = jnp.concatenate([wq, _swap_halves(wq[..., MLA_NOPE:])], axis=-1)
    wq = jnp.transpose(wq, (1, 0, 2)).astype(BF16)
    wkv = jnp.transpose(w_kv_up.reshape(kr, H, MLA_NOPE + MLA_V), (1, 0, 2)).astype(BF16)

    tm = min(256, M)
    wide = MLA_NOPE + LANES
    row = lambda i: (i, 0)
    q, k, v = pl.pallas_call(
        functools.partial(_mla_qkv_kernel, scale=MLA_QK ** -0.5),
        grid=(M // tm,),
        in_specs=[pl.BlockSpec((tm, D_MODEL), row),
                  _resident((nlat, D_MODEL), lambda i: (0, 0)),
                  pl.BlockSpec((1, qr), lambda i: (0, 0)),
                  pl.BlockSpec((1, kr), lambda i: (0, 0)),
                  _resident((H, qr, wide), lambda i: (0, 0, 0)),
                  _resident((H, kr, MLA_NOPE + MLA_V), lambda i: (0, 0, 0)),
                  pl.BlockSpec((tm, LANES), row),
                  pl.BlockSpec((tm, LANES), row)],
        out_specs=[pl.BlockSpec((H, tm, MLA_QK), lambda i: (0, i, 0)),
                   pl.BlockSpec((H, tm, MLA_QK), lambda i: (0, i, 0)),
                   pl.BlockSpec((H, tm, MLA_V), lambda i: (0, i, 0))],
        out_shape=[jax.ShapeDtypeStruct((H, M, MLA_QK), BF16),
                   jax.ShapeDtypeStruct((H, M, MLA_QK), BF16),
                   jax.ShapeDtypeStruct((H, M, MLA_V), BF16)],
        compiler_params=_cparams("parallel"),
        name="mla_qkv_rope",
    )(hb, w_lat, q_norm.reshape(1, qr), kv_norm.reshape(1, kr), wq, wkv, cos, sin)

    tq, tk = _attn_tiles(T)
    nq = T // tq
    nh = ATTN_HEADS_PER_STEP
    in_specs = [
        pl.BlockSpec((nh, tq, MLA_QK), lambda bb, h, i: (h, bb * nq + i, 0)),
        pl.BlockSpec((nh, T, MLA_QK), lambda bb, h, i: (h, bb, 0)),
        pl.BlockSpec((nh, T, MLA_V), lambda bb, h, i: (h, bb, 0)),
        pl.BlockSpec((tq, nh * MLA_V), lambda bb, h, i: (bb * nq + i, h)),
    ]
    out_spec = pl.BlockSpec((tq, nh * MLA_V), lambda bb, h, i: (bb * nq + i, h))
    y = _attention((q, k, v, z), in_specs, out_spec,
                   B=B, T=T, H=H, dv=MLA_V, tq=tq, tk=tk, nparts=1, head_major=True)
    return _out_proj_ln(y, w_out.astype(BF16), hf, g, b)


def _s5_matrices(lam_re, lam_im, log_step, b_re, b_im, c_re, c_im):
    L = S5_CHUNK
    step = jnp.exp(log_step.astype(F32))[:, None]
    mag = jnp.exp(lam_re * step)
    ar = mag * jnp.cos(lam_im * step)
    ai = mag * jnp.sin(lam_im * step)
    den = lam_re * lam_re + lam_im * lam_im
    fr = ((ar - 1.0) * lam_re + ai * lam_im) / den
    fi = (ai * lam_re - (ar - 1.0) * lam_im) / den
    bbr = fr[..., None] * b_re - fi[..., None] * b_im
    bbi = fr[..., None] * b_im + fi[..., None] * b_re
    pr, pi = [jnp.ones_like(ar)], [jnp.zeros_like(ar)]
    for _ in range(L):
        pr_next = pr[-1] * ar - pi[-1] * ai
        pi_next = pr[-1] * ai + pi[-1] * ar
        pr.append(pr_next)
        pi.append(pi_next)
    pwr = jnp.stack(pr, axis=1)
    pwi = jnp.stack(pi, axis=1)

    G, P, I = S5_GROUPS, S5_STATE, S5_GROUP
    TG, SG = S5_TOEP_GROUPS, S5_STATE_GROUPS
    NTB, NSB = G // TG, G // SG

    c_ji = lambda c: jnp.tile(jnp.transpose(c, (0, 2, 1)), (1, 1, I))
    b_ji = lambda v: jnp.repeat(v, I, axis=2)
    cbr = c_ji(c_re) * b_ji(bbr) - c_ji(c_im) * b_ji(bbi)
    cbi = c_ji(c_re) * b_ji(bbi) + c_ji(c_im) * b_ji(bbr)
    kern = jnp.einsum('gdp,gpq->gdq', jnp.concatenate([pwr[:, :L], -pwi[:, :L]], axis=2),
                      jnp.concatenate([cbr, cbi], axis=1), precision=HIGHEST)
    idx = jnp.arange

    kc = jnp.transpose(kern.reshape(NTB, TG, L, I, I), (2, 0, 1, 3, 4)).reshape(L, NTB, TG * I, I)
    kc = jnp.pad(kc, ((L - 1, 0), (0, 0), (0, 0), (0, 0)))
    repeat_i = (idx(TG * I)[None, :] % I == idx(I)[:, None]).astype(F32)
    same_t = (idx(TG * I)[:, None] // I == idx(TG * I)[None, :] // I)
    kd = jnp.where(same_t, jnp.einsum('dcri,iq->dcrq', kc, repeat_i, precision=HIGHEST), 0.0)

    by_step = lambda w: jnp.transpose(
        w[:, :L][:, ::-1].reshape(NSB, SG, L, P), (2, 0, 1, 3))[:, :, :, None]
    by_chan = lambda v: jnp.transpose(v, (0, 2, 1)).reshape(1, NSB, SG, I, P)
    qr, qi = by_step(pwr), by_step(pwi)
    br, bi = by_chan(bbr), by_chan(bbi)
    twice = lambda m: jnp.concatenate([m, m], axis=-1).reshape(L, NSB, SG * I, 2 * P)
    bp_re, bp_im = twice(qr * br - qi * bi), twice(qr * bi + qi * br)

    chan = lambda c: jnp.transpose(c.reshape(NSB, SG, I, P), (0, 3, 1, 2)).reshape(1, NSB, P, SG * I)
    step_pow = lambda w: jnp.repeat(
        jnp.transpose(w[:, 1:].reshape(NSB, SG, L, P), (2, 0, 3, 1)), I, axis=3)
    cr, ci, wr, wi = chan(c_re), chan(c_im), step_pow(pwr), step_pow(pwi)
    cp = jnp.concatenate([cr * wr - ci * wi, -(cr * wi + ci * wr)], axis=2)

    alr = pwr[:, L].reshape(NSB, SG * P)
    ali = pwi[:, L].reshape(NSB, SG * P)
    a1 = jnp.concatenate([alr, alr], axis=1).reshape(1, -1)
    a2 = jnp.concatenate([-ali, ali], axis=1).reshape(1, -1)
    return kd.astype(BF16), bp_re.astype(BF16), bp_im.astype(BF16), cp.astype(BF16), a1, a2


def _chunk_step(u_ref, s):
    return u_ref[pl.ds(s, u_ref.shape[0] // S5_CHUNK, stride=S5_CHUNK), :]


def _s5_local_state_kernel(u_ref, bpr_ref, bpi_ref, s_ref):
    SG, P, I = S5_STATE_GROUPS, S5_STATE, S5_GROUP
    own = (lax.broadcasted_iota(jnp.int32, (SG * I, SG * P), 0) // I
           == lax.broadcasted_iota(jnp.int32, (SG * I, SG * P), 1) // P)

    def spread(m):
        tiled = jnp.concatenate([m] * (SG // 2), axis=1)
        return jnp.where(own, tiled, jnp.zeros_like(tiled))

    acc = None
    for s in range(S5_CHUNK):
        bd = jnp.concatenate([spread(bpr_ref[s]), spread(bpi_ref[s])], axis=1)
        d = jnp.dot(_chunk_step(u_ref, s).astype(BF16), bd, preferred_element_type=F32)
        acc = d if acc is None else acc + d
    s_ref[...] = acc


def _s5_scan_kernel(s_ref, a1_ref, a2_ref, o_ref, *, B, nch):
    W = s_ref.shape[1]
    a1 = a1_ref[...]
    a2 = a2_ref[...]
    unit = 2 * S5_STATE_GROUPS * S5_STATE

    def swap_re_im(c):
        parts = []
        for q in range(W // unit):
            lo = q * unit
            parts += [c[:, lo + unit // 2:lo + unit], c[:, lo:lo + unit // 2]]
        return jnp.concatenate(parts, axis=1)

    def body(n, carry):
        new = []
        for bb in range(B):
            row = bb * nch + n
            c = carry[bb]
            o_ref[pl.ds(row, 1), :] = c
            new.append(a1 * c + a2 * swap_re_im(c) + s_ref[pl.ds(row, 1), :])
        return tuple(new)

    lax.fori_loop(0, nch, body, tuple(jnp.zeros((1, W), F32) for _ in range(B)))


def _gelu_tanh(y):
    return 0.5 * y * (1.0 + jnp.tanh(math.sqrt(2.0 / math.pi) * (y + 0.044715 * (y * y * y))))


def _s5_out_kernel(*refs):
    L = S5_CHUNK
    nu = S5_TOEP_GROUPS // S5_STATE_GROUPS
    u_refs, (kd_ref, st_ref, cp_ref, o_ref, ub_ref, acc_ref) = refs[:nu], refs[nu:]
    r = pl.program_id(1)

    @pl.when(r == 0)
    def _():
        for s in range(L):
            for k in range(nu):
                ub_ref[s, :, k * LANES:(k + 1) * LANES] = _chunk_step(u_refs[k], s).astype(BF16)

    def lags(first):
        out = None
        for s in range(first, first + S5_STEP_GROUP):
            d = jnp.dot(ub_ref[s], kd_ref[r - s + (L - 1)], preferred_element_type=F32)
            out = d if out is None else out + d
        return out

    SG, P, I = S5_STATE_GROUPS, S5_STATE, S5_GROUP
    col_group = lax.broadcasted_iota(jnp.int32, (P, SG * I), 1) // I

    def spread(c):
        pieces = [jnp.where(col_group == g, c[h * P:(h + 1) * P], jnp.zeros((P, SG * I), BF16))
                  for h in range(2) for g in range(SG)]
        return jnp.concatenate(pieces, axis=0)

    st = st_ref[...].astype(BF16)
    half = st.shape[1] // 2
    carried = jnp.concatenate(
        [jnp.dot(st[:, :half], spread(cp_ref[0]), preferred_element_type=F32),
         jnp.dot(st[:, half:], spread(cp_ref[1]), preferred_element_type=F32)], axis=1)
    acc_ref[...] = carried + lags(0)
    for first in range(S5_STEP_GROUP, L, S5_STEP_GROUP):
        @pl.when(first <= r)
        def _(first=first):
            acc_ref[...] += lags(first)
    o_ref[...] = acc_ref[...].astype(o_ref.dtype)


def _s5_glu_out_kernel(ys_ref, u_ref, d_ref, wg_ref, bg_ref, z_ref, wo_ref, h_ref, g_ref, b_ref,
                       of_ref, ob_ref, yt_ref):
    L, W = S5_CHUNK, S5_WIDTH
    nslab, tm = yt_ref.shape[0], yt_ref.shape[1]
    for s in range(L):
        for k in range(nslab):
            lo = s * W + k * LANES
            yt_ref[k, pl.ds(s, tm // L, stride=L), :] = ys_ref[:, lo:lo + LANES].astype(F32)
    half = tm // 2
    parts = [pl.ds(0, half), pl.ds(half, half)]
    ys = [_gelu_tanh(jnp.concatenate([yt_ref[k, rows, :] for k in range(nslab)], axis=1)
                     + d_ref[...] * u_ref[rows, :]) for rows in parts]
    ts = [jnp.dot(y.astype(BF16), wg_ref[...], preferred_element_type=F32) for y in ys]
    gated = [(y * _sigmoid(t + bg_ref[...]) * _silu(z_ref[rows, :])).astype(BF16)
             for y, t, rows in zip(ys, ts, parts)]
    branches = [jnp.dot(y, wo_ref[...], preferred_element_type=F32) for y in gated]
    for rows, branch in zip(parts, branches):
        out = _deepnorm_ln(h_ref[rows, :], branch, g_ref[...], b_ref[...])
        of_ref[rows, :] = out
        ob_ref[rows, :] = out.astype(BF16)


def _s5_layer(hf, hb, w_in, lam_re, lam_im, log_step, b_re, b_im, c_re, c_im, d_skip,
              w_glu, b_glu, w_out, g, b, *, B, T, layer=0):
    M, W, G, L = B * T, S5_WIDTH, S5_GROUPS, S5_CHUNK
    nch = T // L
    rows = B * nch
    u = _matmul(hb, w_in, F32, layer=layer, col0=0, ncols=W)
    z = _matmul(hb, w_in, F32, layer=layer, col0=W, ncols=W)
    kd, bp_re, bp_im, cp, a1, a2 = _s5_matrices(lam_re, lam_im, log_step, b_re, b_im, c_re, c_im)

    TG, SG = S5_TOEP_GROUPS, S5_STATE_GROUPS
    ntb, nsb = G // TG, G // SG
    tw, sw_in, sw = TG * S5_GROUP, SG * S5_GROUP, 2 * SG * S5_STATE
    nstate = nsb * sw
    s_loc = pl.pallas_call(
        _s5_local_state_kernel,
        grid=(nsb,),
        in_specs=[pl.BlockSpec((M, sw_in), lambda cb: (0, cb)),
                  pl.BlockSpec((L, None, sw_in, LANES), lambda cb: (0, cb, 0, 0)),
                  pl.BlockSpec((L, None, sw_in, LANES), lambda cb: (0, cb, 0, 0))],
        out_specs=pl.BlockSpec((rows, sw), lambda cb: (0, cb)),
        out_shape=jax.ShapeDtypeStruct((rows, nstate), F32),
        compiler_params=_cparams("parallel"),
        name="s5_chunk_state",
    )(u, bp_re, bp_im)

    scan_w = 2 * sw
    st_in = pl.pallas_call(
        functools.partial(_s5_scan_kernel, B=B, nch=nch),
        grid=(nstate // scan_w,),
        in_specs=[pl.BlockSpec((rows, scan_w), lambda p: (0, p)),
                  pl.BlockSpec((1, scan_w), lambda p: (0, p)),
                  pl.BlockSpec((1, scan_w), lambda p: (0, p))],
        out_specs=pl.BlockSpec((rows, scan_w), lambda p: (0, p)),
        out_shape=jax.ShapeDtypeStruct((rows, nstate), F32),
        compiler_params=_cparams("parallel"),
        name="s5_state_scan",
    )(s_loc, a1, a2)

    per_toep = TG // SG
    y = pl.pallas_call(
        _s5_out_kernel,
        grid=(ntb, L),
        in_specs=[pl.BlockSpec((M, sw_in), lambda cb, r, k=k: (0, per_toep * cb + k))
                  for k in range(per_toep)]
        + [pl.BlockSpec((2 * L - 1, None, tw, tw), lambda cb, r: (0, cb, 0, 0)),
                  pl.BlockSpec((rows, per_toep * sw), lambda cb, r: (0, cb)),
                  pl.BlockSpec((None, per_toep, LANES, sw_in), lambda cb, r: (r, cb, 0, 0))],
        out_specs=pl.BlockSpec((rows, tw), lambda cb, r: (0, r * ntb + cb)),
        out_shape=jax.ShapeDtypeStruct((rows, L * W), BF16),
        scratch_shapes=[pltpu.VMEM((L, rows, tw), BF16), pltpu.VMEM((rows, tw), F32)],
        compiler_params=_cparams("parallel", "arbitrary"),
        name="s5_chunk_output",
    )(*([u] * per_toep), kd, st_in, cp)

    tm = min(256, M)
    row = lambda i: (i, 0)
    fixed = lambda i: (0, 0)
    return pl.pallas_call(
        _s5_glu_out_kernel,
        grid=(M // tm,),
        in_specs=[pl.BlockSpec((tm // L, L * W), row),
                  pl.BlockSpec((tm, W), row),
                  pl.BlockSpec((1, W), fixed),
                  _resident((W, W), fixed),
                  pl.BlockSpec((1, W), fixed),
                  pl.BlockSpec((tm, W), row),
                  _resident((W, D_MODEL), fixed),
                  pl.BlockSpec((tm, D_MODEL), row),
                  pl.BlockSpec((1, D_MODEL), fixed),
                  pl.BlockSpec((1, D_MODEL), fixed)],
        out_specs=[pl.BlockSpec((tm, D_MODEL), row), pl.BlockSpec((tm, D_MODEL), row)],
        out_shape=[jax.ShapeDtypeStruct((M, D_MODEL), F32),
                   jax.ShapeDtypeStruct((M, D_MODEL), BF16)],
        scratch_shapes=[pltpu.VMEM((W // LANES, tm, LANES), F32)],
        compiler_params=_cparams("parallel"),
        name="s5_glu_out_ln",
    )(y, u, d_skip.reshape(1, W), w_glu.astype(BF16), b_glu.reshape(1, W), z,
      w_out.astype(BF16), hf, g.reshape(1, D_MODEL), b.reshape(1, D_MODEL))


def kernel(x, positions, ln_g, ln_b, ssd_w_in, ssd_conv_w, ssd_conv_b, ssd_dt_bias, ssd_a_log, ssd_d, ssd_norm_w, ssd_w_out, fox_w_in, fox_f_bias, fox_w_out, mla_w_in, mla_q_norm, mla_kv_norm, mla_w_q_up, mla_w_kv_up, mla_w_out, s5_w_in, s5_lambda_re, s5_lambda_im, s5_log_step, s5_b_re, s5_b_im, s5_c_re, s5_c_im, s5_d, s5_w_glu, s5_b_glu, s5_w_out):
    B, T, D = x.shape
    hf = x.reshape(B * T, D)
    hb = hf.astype(BF16)
    for i in range(DEPTH):
        j = i // 4
        kind = i % 4
        g, b = ln_g[i], ln_b[i]
        if kind == 0:
            hf, hb = _ssd_layer(hf, hb, ssd_w_in, ssd_conv_w[j], ssd_conv_b[j], ssd_dt_bias[j],
                                ssd_a_log[j], ssd_d[j], ssd_norm_w[j], ssd_w_out[j], g, b,
                                B=B, T=T, layer=j)
        elif kind == 1:
            hf, hb = _fox_layer(hf, hb, fox_w_in, fox_f_bias[j], fox_w_out[j], g, b,
                                B=B, T=T, layer=j)
        elif kind == 2:
            hf, hb = _mla_layer(hf, hb, positions, mla_w_in[j], mla_q_norm[j],
                                mla_kv_norm[j], mla_w_q_up[j], mla_w_kv_up[j], mla_w_out[j],
                                g, b, B=B, T=T)
        else:
            hf, hb = _s5_layer(hf, hb, s5_w_in, s5_lambda_re[j], s5_lambda_im[j], s5_log_step[j],
                               s5_b_re[j], s5_b_im[j], s5_c_re[j], s5_c_im[j], s5_d[j],
                               s5_w_glu[j], s5_b_glu[j], s5_w_out[j], g, b, B=B, T=T, layer=j)
    return hf.reshape(B, T, D)
```

```python
import functools
import math

import jax
import jax.numpy as jnp
from jax import lax
from jax.experimental import pallas as pl
from jax.experimental.pallas import tpu as pltpu

F32 = jnp.float32
BF16 = jnp.bfloat16
HIGHEST = lax.Precision.HIGHEST

D_MODEL = 2048
DEPTH = 4
ALPHA = (2.0 * DEPTH) ** 0.25
LN_EPS = 1e-5
RMS_EPS = 1e-6

SSD_D_INNER = 4096
SSD_HEADS = 64
SSD_HEADDIM = 64
SSD_GROUPS = 8
SSD_HEADS_PER_GROUP = SSD_HEADS // SSD_GROUPS
SSD_STATE = 128
SSD_CONV = 4
SSD_CHUNK = 128
SSD_GROUP_WIDTH = SSD_D_INNER // SSD_GROUPS
SSD_BC_WIDTH = SSD_GROUPS * SSD_STATE
SSD_GROUPS_PER_STEP = 8

FOX_HEADS = 16
FOX_HEAD_DIM = 128
FOX_WIDTH = FOX_HEADS * FOX_HEAD_DIM

MLA_HEADS = 16
MLA_Q_RANK = 512
MLA_KV_RANK = 512
MLA_NOPE = 128
MLA_ROPE = 64
MLA_V = 128
MLA_QK = MLA_NOPE + MLA_ROPE
MLA_WIDTH = MLA_HEADS * MLA_V
ROPE_BASE = 10000.0

S5_WIDTH = D_MODEL
S5_GROUP = 16
S5_GROUPS = S5_WIDTH // S5_GROUP
S5_STATE = 64
S5_CHUNK = 8
S5_TOEP_GROUPS = 16
S5_STATE_GROUPS = 8
S5_STEP_GROUP = 4

LANES = 128
VMEM_LIMIT_BYTES = 48 * 1024 * 1024


def _cparams(*sem):
    return pltpu.CompilerParams(dimension_semantics=sem, vmem_limit_bytes=VMEM_LIMIT_BYTES)


def _resident(block_shape, index_map):
    return pl.BlockSpec(block_shape, index_map, pipeline_mode=pl.Buffered(1))


def _sigmoid(x):
    return 0.5 * (jnp.tanh(0.5 * x) + 1.0)


def _silu(x):
    h = 0.5 * x
    return h + h * jnp.tanh(h)


def _softplus(x):
    return jnp.maximum(x, 0.0) + jnp.log1p(jnp.exp(-jnp.abs(x)))


def _deepnorm_ln(h, branch, g, b):
    r = ALPHA * h + branch
    mu = jnp.mean(r, axis=-1, keepdims=True)
    d = r - mu
    var = jnp.mean(d * d, axis=-1, keepdims=True)
    return d * lax.rsqrt(var + LN_EPS) * g + b


def _project(x, wb, wt):
    dims = (((1,), (1,)), ((), ())) if wt else (((1,), (0,)), ((), ()))
    return lax.dot_general(x, wb, dims, preferred_element_type=F32)


def _mm_kernel(x_ref, w_ref, o_ref, wb_ref, *, scale, wt):
    @pl.when(pl.program_id(1) == 0)
    def _():
        wb_ref[...] = w_ref[...].astype(BF16)

    acc = _project(x_ref[...].astype(BF16), wb_ref[...], wt)
    if scale is not None:
        acc = acc * scale
    o_ref[...] = acc.astype(o_ref.dtype)


def _weight_spec(w, layer, K, tn, j0, wt):
    block = (tn, K) if wt else (K, tn)
    index = (lambda j: (j0 + j, 0)) if wt else (lambda j: (0, j0 + j))
    if w.ndim == 2:
        return pl.BlockSpec(block, lambda j, i: index(j))
    return pl.BlockSpec((None,) + block, lambda j, i: (layer,) + index(j))


def _matmul(x, w, out_dtype, *, layer=0, col0=0, ncols=None, scale=None, tn=1024, wt=False):
    M, K = x.shape
    N = w.shape[-2 if wt else -1] - col0 if ncols is None else ncols
    tm = min(1024 if x.dtype == BF16 else 512, M)
    tn = min(tn, N)
    assert M % tm == 0 and N % tn == 0 and col0 % tn == 0, (M, N, col0, tm, tn)
    j0 = col0 // tn
    return pl.pallas_call(
        functools.partial(_mm_kernel, scale=scale, wt=wt),
        grid=(N // tn, M // tm),
        in_specs=[pl.BlockSpec((tm, K), lambda j, i: (i, 0)),
                  _weight_spec(w, layer, K, tn, j0, wt)],
        out_specs=pl.BlockSpec((tm, tn), lambda j, i: (i, j)),
        out_shape=jax.ShapeDtypeStruct((M, N), out_dtype),
        scratch_shapes=[pltpu.VMEM((tn, K) if wt else (K, tn), BF16)],
        compiler_params=_cparams("parallel", "arbitrary"),
        name="proj",
    )(x, w)


CONV_COL_PARTS = 4


def _mm_conv_silu_kernel(x_ref, w_ref, cw_ref, cb_ref, o_ref, wb_ref, ext_ref,
                         *, taps, tiles_per_seq, wt):
    i = pl.program_id(1)
    tm, tn = o_ref.shape

    @pl.when(i == 0)
    def _():
        wb_ref[...] = w_ref[...].astype(BF16)

    @pl.when(i % tiles_per_seq == 0)
    def _():
        ext_ref[pl.ds(0, 8), :] = jnp.zeros((8, tn), F32)

    xb = x_ref[...].astype(BF16)
    slab = tn // CONV_COL_PARTS
    parts = [pl.ds(p * slab, slab) for p in range(CONV_COL_PARTS)]
    raws = [_project(xb, wb_ref[cols, :] if wt else wb_ref[:, cols], wt) for cols in parts]
    for cols, raw in zip(parts, raws):
        ext_ref[pl.ds(8, tm), cols] = raw
        acc = cb_ref[:, cols] + cw_ref[pl.ds(0, 1), cols] * ext_ref[pl.ds(9 - taps, tm), cols]
        for kk in range(1, taps):
            acc = acc + cw_ref[pl.ds(kk, 1), cols] * ext_ref[pl.ds(9 - taps + kk, tm), cols]
        ext_ref[pl.ds(0, 8), cols] = ext_ref[pl.ds(tm, 8), cols]
        o_ref[:, cols] = _silu(acc)


def _matmul_conv_silu(x, w, conv_w, conv_b, *, layer, col0, ncols, seq_len, tn=1024, wt=False):
    M, K = x.shape
    taps = conv_w.shape[0]
    tm = min(1024 if x.dtype == BF16 else 512, seq_len)
    assert M % tm == 0 and seq_len % tm == 0 and ncols % tn == 0 and col0 % tn == 0
    j0 = col0 // tn
    return pl.pallas_call(
        functools.partial(_mm_conv_silu_kernel, taps=taps, tiles_per_seq=seq_len // tm, wt=wt),
        grid=(ncols // tn, M // tm),
        in_specs=[pl.BlockSpec((tm, K), lambda j, i: (i, 0)),
                  _weight_spec(w, layer, K, tn, j0, wt),
                  pl.BlockSpec((taps, tn), lambda j, i: (0, j)),
                  pl.BlockSpec((1, tn), lambda j, i: (0, j))],
        out_specs=pl.BlockSpec((tm, tn), lambda j, i: (i, j)),
        out_shape=jax.ShapeDtypeStruct((M, ncols), F32),
        scratch_shapes=[pltpu.VMEM((tn, K) if wt else (K, tn), BF16),
                        pltpu.VMEM((tm + 8, tn), F32)],
        compiler_params=_cparams("parallel", "arbitrary"),
        name="proj_conv_silu",
    )(x, w, conv_w, conv_b.reshape(1, -1))


def _out_ln_kernel(y_ref, w_ref, h_ref, g_ref, b_ref, of_ref, ob_ref):
    half = y_ref.shape[0] // 2
    parts = [pl.ds(0, half), pl.ds(half, half)]
    branches = [jnp.dot(y_ref[rows, :], w_ref[...], preferred_element_type=F32) for rows in parts]
    for rows, branch in zip(parts, branches):
        out = _deepnorm_ln(h_ref[rows, :], branch, g_ref[...], b_ref[...])
        of_ref[rows, :] = out
        ob_ref[rows, :] = out.astype(BF16)


def _out_proj_ln(y, w, h, g, b):
    M, K = y.shape
    D = w.shape[1]
    tm = min(512 if K * D * w.dtype.itemsize <= VMEM_LIMIT_BYTES // 4 else 256, M)
    assert M % tm == 0
    return pl.pallas_call(
        _out_ln_kernel,
        grid=(M // tm,),
        in_specs=[pl.BlockSpec((tm, K), lambda i: (i, 0)),
                  _resident((K, D), lambda i: (0, 0)),
                  pl.BlockSpec((tm, D), lambda i: (i, 0)),
                  pl.BlockSpec((1, D), lambda i: (0, 0)),
                  pl.BlockSpec((1, D), lambda i: (0, 0))],
        out_specs=[pl.BlockSpec((tm, D), lambda i: (i, 0)),
                   pl.BlockSpec((tm, D), lambda i: (i, 0))],
        out_shape=[jax.ShapeDtypeStruct((M, D), F32), jax.ShapeDtypeStruct((M, D), BF16)],
        compiler_params=_cparams("parallel"),
        name="out_proj_ln",
    )(y, w, h, g.reshape(1, D), b.reshape(1, D))


ATTN_HEADS_PER_STEP = 4
ATTN_WIDE_BLOCKS = 2


def _attn_kernel(*refs, nparts, tq, tk, dv, head_major):
    q_refs = refs[:nparts]
    k_refs = refs[nparts:2 * nparts]
    v_ref, z_ref, o_ref, m_ref, acc_ref = refs[2 * nparts:]
    nh = ATTN_HEADS_PER_STEP
    i = pl.program_id(2)

    def head(ref, hh, rows=slice(None)):
        if head_major:
            return ref[hh, rows, :]
        w = ref.shape[-1] // nh
        return ref[rows, hh * w:(hh + 1) * w]

    def cat(pieces):
        return pieces[0] if len(pieces) == 1 else jnp.concatenate(pieces, axis=1)

    qs = [cat([head(r, hh) for r in q_refs]) for hh in range(nh)]
    m_ref[...] = jnp.full_like(m_ref, -jnp.inf)
    acc_ref[...] = jnp.zeros_like(acc_ref)

    def block(first, width, diag_offset):
        rows = pl.ds(pl.multiple_of(first * tk, tk), width)
        ones_cols = jnp.ones((width, LANES), BF16)
        scores = []
        for hh in range(nh):
            k = cat([head(r, hh, rows) for r in k_refs])
            scores.append(lax.dot_general(qs[hh], k, (((1,), (1,)), ((), ())),
                                          preferred_element_type=F32))
        probs, alphas = [], []
        for hh in range(nh):
            s = scores[hh]
            if diag_offset is not None:
                r = lax.broadcasted_iota(jnp.int32, (tq, width), 0)
                c = lax.broadcasted_iota(jnp.int32, (tq, width), 1)
                s = jnp.where(c + diag_offset <= r, s, -jnp.inf)
            m_prev = m_ref[hh]
            m_new = jnp.maximum(m_prev, jnp.max(s, axis=-1, keepdims=True))
            alpha = jnp.exp(m_prev - m_new)
            p = jnp.exp(s - jnp.concatenate([m_new] * (width // LANES), axis=1))
            m_ref[hh] = m_new
            probs.append(p.astype(BF16))
            alphas.append(jnp.concatenate([alpha] * ((dv + LANES) // LANES), axis=1))
        for hh in range(nh):
            v_aug = jnp.concatenate([head(v_ref, hh, rows), ones_cols], axis=1)
            acc_ref[hh] = alphas[hh] * acc_ref[hh] + jnp.dot(
                probs[hh], v_aug, preferred_element_type=F32)

    n_full = i * (tq // tk)
    wide = ATTN_WIDE_BLOCKS

    def wide_block(j, carry):
        block(j * wide, wide * tk, None)
        return carry

    lax.fori_loop(0, n_full // wide, wide_block, 0)
    if tq == tk and wide == 2:
        @pl.when(n_full % 2 == 1)
        def _():
            block(n_full - 1, 2 * tk, -tk)

        @pl.when(n_full % 2 == 0)
        def _():
            block(n_full, tk, 0)
    else:
        for rem in range(wide - 1):
            @pl.when(n_full % wide > rem)
            def _(rem=rem):
                block((n_full // wide) * wide + rem, tk, None)
        for d in range(tq // tk):
            block(n_full + d, tk, d * tk)

    for hh in range(nh):
        cols = slice(hh * dv, (hh + 1) * dv)
        o = acc_ref[hh, :, :dv] / acc_ref[hh, :, dv:]
        o_ref[:, cols] = (o * _silu(z_ref[:, cols])).astype(o_ref.dtype)


def _attention(inputs, in_specs, out_spec, *, B, T, H, dv, tq, tk, nparts, head_major):
    nh = ATTN_HEADS_PER_STEP
    assert tq % tk == 0 and T % tq == 0 and H % nh == 0
    return pl.pallas_call(
        functools.partial(_attn_kernel, nparts=nparts, tq=tq, tk=tk, dv=dv,
                          head_major=head_major),
        grid=(B, H // nh, T // tq),
        in_specs=in_specs,
        out_specs=out_spec,
        out_shape=jax.ShapeDtypeStruct((B * T, H * dv), BF16),
        scratch_shapes=[pltpu.VMEM((nh, tq, LANES), F32),
                        pltpu.VMEM((nh, tq, dv + LANES), F32)],
        compiler_params=_cparams("parallel", "parallel", "arbitrary"),
        name="causal_attention",
    )(*inputs)


def _attn_tiles(T):
    tq = min(512, T)
    tk = min(512, T)
    return tq, tk


def _ssd_kernel(x_ref, bm_ref, cm_ref, z_ref, dtr_ref,
                dtb_ref, al_ref, dsk_ref, nw_ref, spread_ref,
                o_ref,
                state_ref, yz_ref):
    Q, E, P, GW, N = SSD_CHUNK, SSD_HEADS_PER_GROUP, SSD_HEADDIM, SSD_GROUP_WIDTH, SSD_STATE
    NG = SSD_GROUPS_PER_STEP
    groups = range(NG)
    c = pl.program_id(2)

    @pl.when(c == 0)
    def _():
        state_ref[...] = jnp.zeros_like(state_ref)

    row = lax.broadcasted_iota(jnp.int32, (Q, Q), 0)
    col = lax.broadcasted_iota(jnp.int32, (Q, Q), 1)
    lower = row >= col
    tri_u = (row <= col).astype(BF16)
    left = col < P

    def pieces(v):
        hi = v.astype(BF16).astype(F32)
        mid = (v - hi).astype(BF16).astype(F32)
        lo = ((v - hi) - mid).astype(BF16).astype(F32)
        return jnp.concatenate([hi, mid, lo], axis=0).astype(BF16)

    npairs = E // 2
    dt_r, acs_r = [], []
    for gg in groups:
        dt = _softplus(dtr_ref[gg] + dtb_ref[gg])
        cum3 = jnp.dot(pieces(dt * (-jnp.exp(al_ref[gg]))), tri_u, preferred_element_type=F32)
        dt_r.append(dt)
        acs_r.append(cum3[:E] + cum3[E:2 * E] + cum3[2 * E:])

    bm_b, cm_b, bmt_b, cb, carried = [], [], [], [], []
    for gg in groups:
        ncols = pl.ds(gg * N, N)
        bm = bm_ref[:, ncols]
        bm_b.append(bm.astype(BF16))
        cm_b.append(cm_ref[:, ncols].astype(BF16))
        bmt_b.append(bm.T.astype(BF16))
        cb.append(lax.dot_general(cm_b[gg], bm_b[gg], (((1,), (1,)), ((), ())),
                                  preferred_element_type=F32))
        carried.append([jnp.dot(cm_b[gg], state_ref[gg * npairs + kp].astype(BF16),
                                preferred_element_type=F32)
                        for kp in range(npairs)])

    pair_cols = lambda gg, kp: pl.ds(gg * GW + kp * LANES, LANES)
    xps = [[x_ref[:, pair_cols(gg, kp)] for kp in range(npairs)] for gg in groups]
    gates = [[_silu(z_ref[:, pair_cols(gg, kp)]) for kp in range(npairs)] for gg in groups]

    spreads = [lax.dot_general(pieces(jnp.concatenate([dt_r[gg], acs_r[gg]], axis=0)),
                               spread_ref[...], (((0,), (0,)), ((), ())),
                               preferred_element_type=F32) for gg in groups]

    for gg in groups:
        dt_ch = spreads[gg][:, :GW]
        acs_ch = spreads[gg][:, GW:2 * GW]
        acs_col = spreads[gg][:, 2 * GW:]
        ssq = jnp.zeros((Q, 1), F32)
        for kp in range(npairs):
            cols = pair_cols(gg, kp)
            xp = xps[gg][kp]
            dt_p = dt_ch[:, kp * LANES:(kp + 1) * LANES]
            acs_p = acs_ch[:, kp * LANES:(kp + 1) * LANES]
            last_p = acs_p[Q - 1:Q, :]
            xdt = xp * dt_p
            xdt_b = xdt.astype(BF16)

            ys = []
            for e in (2 * kp, 2 * kp + 1):
                seg = acs_col[:, e * Q:(e + 1) * Q] - acs_r[gg][e:e + 1, :]
                dec = jnp.exp(jnp.where(lower, seg, -jnp.inf))
                ys.append(jnp.dot((cb[gg] * dec).astype(BF16), xdt_b,
                                  preferred_element_type=F32))
            y_diag = jnp.where(left, ys[0], ys[1])

            xw = (xdt * jnp.exp(last_p - acs_p)).astype(BF16)
            s_loc = jnp.dot(bmt_b[gg], xw, preferred_element_type=F32)
            y_off = carried[gg][kp] * jnp.exp(acs_p)
            slot = gg * npairs + kp
            state_ref[slot] = state_ref[slot] * jnp.exp(last_p) + s_loc

            y = (y_diag + y_off + dsk_ref[:, cols] * xp) * gates[gg][kp]
            yz_ref[:, cols] = y
            ssq = ssq + jnp.sum(y * y, axis=-1, keepdims=True)

        gcols = pl.ds(gg * GW, GW)
        inv = lax.rsqrt(ssq * (1.0 / GW) + RMS_EPS)
        o_ref[:, gcols] = (yz_ref[:, gcols] * inv * nw_ref[:, gcols]).astype(o_ref.dtype)


def _ssd_core(z, xbc, dt, dt_bias, a_log, d_skip, norm_w, *, B, T):
    G, E, Q = SSD_GROUPS, SSD_HEADS_PER_GROUP, SSD_CHUNK
    NC = T // Q
    GW, N = SSD_GROUP_WIDTH, SSD_STATE
    dtr = jnp.transpose(dt[:, :SSD_HEADS].reshape(B, T, G, E), (0, 2, 3, 1))
    head_rows = lambda v: v.reshape(G, E, 1)
    d_ch = jnp.repeat(d_skip, SSD_HEADDIM).reshape(1, -1)
    heads = jnp.arange(E)
    per_channel = (jnp.arange(GW)[None, :] // SSD_HEADDIM == heads[:, None]).astype(BF16)
    per_head_block = (jnp.arange(E * Q)[None, :] // Q == heads[:, None]).astype(BF16)
    none = lambda n: jnp.zeros((E, n), BF16)
    spread = jnp.concatenate([
        jnp.concatenate([per_channel, none(GW), none(E * Q)], axis=1),
        jnp.concatenate([none(GW), per_channel, per_head_block], axis=1)], axis=0)
    spread = jnp.tile(spread, (3, 1))

    NG = SSD_GROUPS_PER_STEP
    XW, BW = NG * GW, NG * N
    assert G % NG == 0
    bblk = SSD_D_INNER // BW
    cblk = (SSD_D_INNER + SSD_BC_WIDTH) // BW
    rowblk = lambda b, g, c: b * NC + c
    in_specs = [
        pl.BlockSpec((Q, XW), lambda b, g, c: (rowblk(b, g, c), g)),
        pl.BlockSpec((Q, BW), lambda b, g, c: (rowblk(b, g, c), bblk + g)),
        pl.BlockSpec((Q, BW), lambda b, g, c: (rowblk(b, g, c), cblk + g)),
        pl.BlockSpec((Q, XW), lambda b, g, c: (rowblk(b, g, c), g)),
        pl.BlockSpec((None, NG, E, Q), lambda b, g, c: (b, g, 0, c)),
        pl.BlockSpec((NG, E, 1), lambda b, g, c: (g, 0, 0)),
        pl.BlockSpec((NG, E, 1), lambda b, g, c: (g, 0, 0)),
        pl.BlockSpec((1, XW), lambda b, g, c: (0, g)),
        pl.BlockSpec((1, XW), lambda b, g, c: (0, g)),
        pl.BlockSpec(spread.shape, lambda b, g, c: (0, 0)),
    ]
    return pl.pallas_call(
        _ssd_kernel,
        grid=(B, G // NG, NC),
        in_specs=in_specs,
        out_specs=pl.BlockSpec((Q, XW), lambda b, g, c: (rowblk(b, g, c), g)),
        out_shape=jax.ShapeDtypeStruct((B * T, SSD_D_INNER), BF16),
        scratch_shapes=[pltpu.VMEM((NG * E // 2, N, LANES), F32), pltpu.VMEM((Q, XW), F32)],
        compiler_params=_cparams("parallel", "parallel", "arbitrary"),
        name="ssd_chunk_scan",
    )(xbc, xbc, xbc, z, dtr,
      head_rows(dt_bias), head_rows(a_log), d_ch, norm_w.reshape(1, -1), spread)


def _pad_rows(w, n):
    return jnp.pad(w, ((0, n - w.shape[0]), (0, 0)))


def _layer_of(w, layer):
    return w[layer] if w.ndim == 3 else w


def _ssd_layer(hf, hb, w_in, conv_w, conv_b, dt_bias, a_log, d_skip, norm_w, w_out, g, b,
               *, B, T, layer=0):
    di = SSD_D_INNER
    cd = di + 2 * SSD_BC_WIDTH
    w_t = jnp.swapaxes(w_in, -1, -2)
    z = _matmul(hb, w_t, F32, layer=layer, col0=0, ncols=di, wt=True)
    xbc = _matmul_conv_silu(hb, w_t, conv_w, conv_b, layer=layer, col0=di, ncols=cd, seq_len=T,
                            wt=True)
    dt = _matmul(hb, _pad_rows(_layer_of(w_t, layer)[di + cd:], LANES), F32, wt=True)
    y = _ssd_core(z, xbc, dt, dt_bias, a_log, d_skip, norm_w, B=B, T=T)
    return _out_proj_ln(y, w_out.astype(BF16), hf, g, b)


FOX_BIAS_PIECES = 3


def _fox_bias_kernel(f_ref, b_ref, qx_ref, kx_ref, carry_ref, *, nblk):
    blk, W, P = LANES, FOX_WIDTH, FOX_BIAS_PIECES
    row = lax.broadcasted_iota(jnp.int32, (blk, blk), 0)
    col = lax.broadcasted_iota(jnp.int32, (blk, blk), 1)
    tri = (row >= col).astype(F32)
    bias = b_ref[...]
    H = FOX_HEADS
    r = lax.broadcasted_iota(jnp.int32, (blk, W), 0)
    c = lax.broadcasted_iota(jnp.int32, (blk, W), 1)
    head, piece = r % H, r // H
    is_piece = r < P * H
    place_q = jnp.logical_and(is_piece, c == head * blk + piece).astype(BF16)
    place_k = jnp.logical_and(is_piece, c == head * blk + P + piece).astype(BF16)
    is_head = col < H
    cmod = lax.broadcasted_iota(jnp.int32, (1, W), 1) % blk
    ones_q = jnp.logical_and(cmod >= P, cmod < 2 * P).astype(F32)
    ones_k = (cmod < P).astype(F32)

    def body(t, carry):
        rows = pl.ds(t * blk, blk)
        x = f_ref[rows, :] + bias
        logf = jnp.minimum(x, 0.0) - jnp.log1p(jnp.exp(-jnp.abs(x)))
        cum = jnp.dot(tri, logf, precision=HIGHEST, preferred_element_type=F32) + carry
        hi = jnp.where(is_head, cum, 0.0).astype(BF16).astype(F32)
        r1 = jnp.where(is_head, cum, 0.0) - hi
        mid = r1.astype(BF16).astype(F32)
        lo = (r1 - mid).astype(BF16).astype(F32)
        pieces = (hi + pltpu.roll(mid, H, 1) + pltpu.roll(lo, 2 * H, 1)).astype(BF16)
        qx_ref[rows, :] = (jnp.dot(pieces, place_q, preferred_element_type=F32)
                           + ones_q).astype(BF16)
        kx_ref[rows, :] = (ones_k - jnp.dot(pieces, place_k, preferred_element_type=F32)
                           ).astype(BF16)
        return cum[blk - 1:blk, :]

    @pl.when(pl.program_id(1) == 0)
    def _():
        carry_ref[...] = jnp.zeros_like(carry_ref)

    carry = carry_ref[...]
    for t in range(nblk):
        carry = body(t, carry)
    carry_ref[...] = carry


def _fox_bias_columns(f, f_bias, *, B, T):
    bias = jnp.pad(f_bias, (0, LANES - FOX_HEADS)).reshape(1, LANES)
    tt = min(512, T)
    out = jax.ShapeDtypeStruct((B, T, FOX_WIDTH), BF16)
    spec = pl.BlockSpec((None, tt, FOX_WIDTH), lambda b, t: (b, t, 0))
    qx, kx = pl.pallas_call(
        functools.partial(_fox_bias_kernel, nblk=tt // LANES),
        grid=(B, T // tt),
        in_specs=[pl.BlockSpec((None, tt, LANES), lambda b, t: (b, t, 0)),
                  pl.BlockSpec((1, LANES), lambda b, t: (0, 0))],
        out_specs=[spec, spec],
        out_shape=[out, out],
        scratch_shapes=[pltpu.VMEM((1, LANES), F32)],
        compiler_params=_cparams("parallel", "arbitrary"),
        name="fox_cum_log_forget",
    )(f.reshape(B, T, LANES), bias)
    return qx.reshape(B * T, FOX_WIDTH), kx.reshape(B * T, FOX_WIDTH)


def _fox_layer(hf, hb, w_in, f_bias, w_out, g, b, *, B, T, layer=0):
    W, H, dh = FOX_WIDTH, FOX_HEADS, FOX_HEAD_DIM
    w_t = jnp.swapaxes(w_in, -1, -2)
    q = _matmul(hb, w_t, BF16, layer=layer, col0=0, ncols=W, scale=dh ** -0.5, wt=True)
    kv = _matmul(hb, w_t, BF16, layer=layer, col0=W, ncols=2 * W, wt=True)
    z = _matmul(hb, w_t, F32, layer=layer, col0=3 * W, ncols=W, wt=True)
    f = _matmul(hb, _pad_rows(_layer_of(w_t, layer)[4 * W:], LANES), F32, wt=True)
    qx, kx = _fox_bias_columns(f, f_bias, B=B, T=T)

    tq, tk = _attn_tiles(T)
    nq = T // tq
    hw = ATTN_HEADS_PER_STEP * dh
    q_tile = pl.BlockSpec((tq, hw), lambda bb, h, i: (bb * nq + i, h))
    seq_k = pl.BlockSpec((T, hw), lambda bb, h, i: (bb, h))
    seq_v = pl.BlockSpec((T, hw), lambda bb, h, i: (bb, W // hw + h))
    in_specs = [q_tile, q_tile, seq_k, seq_k, seq_v, q_tile]
    y = _attention((q, qx, kv, kx, kv, z), in_specs, q_tile,
                   B=B, T=T, H=H, dv=dh, tq=tq, tk=tk, nparts=2, head_major=False)
    return _out_proj_ln(y, w_out.astype(BF16), hf, g, b)


def _rope_table_kernel(pos_ref, freq_ref, sign_ref, cos_ref, sin_ref):
    ang = pos_ref[...].astype(F32) * freq_ref[...]
    cos_ref[...] = jnp.cos(ang)
    sin_ref[...] = jnp.sin(ang) * sign_ref[...]


def _rope_tables(positions, *, B, T):
    half = MLA_ROPE // 2
    inv_freq = ROPE_BASE ** (-jnp.arange(0, MLA_ROPE, 2, dtype=F32) / MLA_ROPE)
    freq = jnp.tile(inv_freq, LANES // half).reshape(1, LANES)
    sign = jnp.tile(jnp.concatenate([-jnp.ones((half,), F32), jnp.ones((half,), F32)]),
                    LANES // MLA_ROPE).reshape(1, LANES)
    M = B * T
    tm = min(1024, M)
    return pl.pallas_call(
        _rope_table_kernel,
        grid=(M // tm,),
        in_specs=[pl.BlockSpec((tm, 1), lambda i: (i, 0)),
                  pl.BlockSpec((1, LANES), lambda i: (0, 0)),
                  pl.BlockSpec((1, LANES), lambda i: (0, 0))],
        out_specs=[pl.BlockSpec((tm, LANES), lambda i: (i, 0)),
                   pl.BlockSpec((tm, LANES), lambda i: (i, 0))],
        out_shape=[jax.ShapeDtypeStruct((M, LANES), F32), jax.ShapeDtypeStruct((M, LANES), F32)],
        compiler_params=_cparams("parallel"),
        name="rope_tables",
    )(positions.reshape(M, 1), freq, sign)


def _rms_to_bf16(x, w):
    y = x * lax.rsqrt(jnp.mean(x * x, axis=-1, keepdims=True) + RMS_EPS)
    return (y * w).astype(BF16)


def _rope_pair(c2, cos, sin):
    return c2 * cos + pltpu.roll(c2, MLA_ROPE, 1) * sin


def _mla_qkv_kernel(x_ref, wl_ref, qn_ref, kn_ref, wq_ref, wkv_ref, cos_ref, sin_ref,
                    q_ref, k_ref, v_ref, *, scale):
    qr, kr = MLA_Q_RANK, MLA_KV_RANK
    cos, sin = cos_ref[...], sin_ref[...]
    lat = _project(x_ref[...], wl_ref[...], True)
    qn = _rms_to_bf16(lat[:, :qr], qn_ref[...])
    kn = _rms_to_bf16(lat[:, qr:qr + kr], kn_ref[...])
    k_pe = _rope_pair(lat[:, qr + kr:], cos, sin)[:, :MLA_ROPE].astype(k_ref.dtype)
    for h in range(MLA_HEADS):
        r = jnp.dot(qn, wq_ref[h], preferred_element_type=F32)
        pe = _rope_pair(r[:, MLA_NOPE:], cos, sin)
        q_ref[h, :, :MLA_NOPE] = (r[:, :MLA_NOPE] * scale).astype(q_ref.dtype)
        q_ref[h, :, MLA_NOPE:] = (pe[:, :MLA_ROPE] * scale).astype(q_ref.dtype)
        r = jnp.dot(kn, wkv_ref[h], preferred_element_type=F32)
        k_ref[h, :, :MLA_NOPE] = r[:, :MLA_NOPE].astype(k_ref.dtype)
        k_ref[h, :, MLA_NOPE:] = k_pe
        v_ref[h] = r[:, MLA_NOPE:].astype(v_ref.dtype)


def _swap_halves(w):
    half = w.shape[-1] // 2
    return jnp.concatenate([w[..., half:], w[..., :half]], axis=-1)


def _mla_layer(hf, hb, positions, w_in, q_norm, kv_norm, w_q_up, w_kv_up, w_out, g, b, *, B, T):
    H, M = MLA_HEADS, B * T
    qr, kr = MLA_Q_RANK, MLA_KV_RANK
    lat_end = qr + kr + MLA_ROPE
    w_t = jnp.swapaxes(w_in, -1, -2)
    w_pe = w_t[qr + kr:lat_end]
    half = MLA_ROPE // 2
    w_lat = jnp.concatenate([w_t[:qr + kr], w_pe, w_pe[half:], w_pe[:half]], axis=0).astype(BF16)
    nlat = w_lat.shape[0]
    z = _matmul(hb, w_t[lat_end:], F32, wt=True)
    cos, sin = _rope_tables(positions, B=B, T=T)

    wq = w_q_up.reshape(qr, H, MLA_QK)
    wq = jnp.concatenate([wq, _swap_halves(wq[..., MLA_NOPE:])], axis=-1)
    wq = jnp.transpose(wq, (1, 0, 2)).astype(BF16)
    wkv = jnp.transpose(w_kv_up.reshape(kr, H, MLA_NOPE + MLA_V), (1, 0, 2)).astype(BF16)

    tm = min(256, M)
    wide = MLA_NOPE + LANES
    row = lambda i: (i, 0)
    q, k, v = pl.pallas_call(
        functools.partial(_mla_qkv_kernel, scale=MLA_QK ** -0.5),
        grid=(M // tm,),
        in_specs=[pl.BlockSpec((tm, D_MODEL), row),
                  _resident((nlat, D_MODEL), lambda i: (0, 0)),
                  pl.BlockSpec((1, qr), lambda i: (0, 0)),
                  pl.BlockSpec((1, kr), lambda i: (0, 0)),
                  _resident((H, qr, wide), lambda i: (0, 0, 0)),
                  _resident((H, kr, MLA_NOPE + MLA_V), lambda i: (0, 0, 0)),
                  pl.BlockSpec((tm, LANES), row),
                  pl.BlockSpec((tm, LANES), row)],
        out_specs=[pl.BlockSpec((H, tm, MLA_QK), lambda i: (0, i, 0)),
                   pl.BlockSpec((H, tm, MLA_QK), lambda i: (0, i, 0)),
                   pl.BlockSpec((H, tm, MLA_V), lambda i: (0, i, 0))],
        out_shape=[jax.ShapeDtypeStruct((H, M, MLA_QK), BF16),
                   jax.ShapeDtypeStruct((H, M, MLA_QK), BF16),
                   jax.ShapeDtypeStruct((H, M, MLA_V), BF16)],
        compiler_params=_cparams("parallel"),
        name="mla_qkv_rope",
    )(hb, w_lat, q_norm.reshape(1, qr), kv_norm.reshape(1, kr), wq, wkv, cos, sin)

    tq, tk = _attn_tiles(T)
    nq = T // tq
    nh = ATTN_HEADS_PER_STEP
    in_specs = [
        pl.BlockSpec((nh, tq, MLA_QK), lambda bb, h, i: (h, bb * nq + i, 0)),
        pl.BlockSpec((nh, T, MLA_QK), lambda bb, h, i: (h, bb, 0)),
        pl.BlockSpec((nh, T, MLA_V), lambda bb, h, i: (h, bb, 0)),
        pl.BlockSpec((tq, nh * MLA_V), lambda bb, h, i: (bb * nq + i, h)),
    ]
    out_spec = pl.BlockSpec((tq, nh * MLA_V), lambda bb, h, i: (bb * nq + i, h))
    y = _attention((q, k, v, z), in_specs, out_spec,
                   B=B, T=T, H=H, dv=MLA_V, tq=tq, tk=tk, nparts=1, head_major=True)
    return _out_proj_ln(y, w_out.astype(BF16), hf, g, b)


def _s5_matrices(lam_re, lam_im, log_step, b_re, b_im, c_re, c_im):
    L = S5_CHUNK
    step = jnp.exp(log_step.astype(F32))[:, None]
    mag = jnp.exp(lam_re * step)
    ar = mag * jnp.cos(lam_im * step)
    ai = mag * jnp.sin(lam_im * step)
    den = lam_re * lam_re + lam_im * lam_im
    fr = ((ar - 1.0) * lam_re + ai * lam_im) / den
    fi = (ai * lam_re - (ar - 1.0) * lam_im) / den
    bbr = fr[..., None] * b_re - fi[..., None] * b_im
    bbi = fr[..., None] * b_im + fi[..., None] * b_re
    pr, pi = [jnp.ones_like(ar)], [jnp.zeros_like(ar)]
    for _ in range(L):
        pr_next = pr[-1] * ar - pi[-1] * ai
        pi_next = pr[-1] * ai + pi[-1] * ar
        pr.append(pr_next)
        pi.append(pi_next)
    pwr = jnp.stack(pr, axis=1)
    pwi = jnp.stack(pi, axis=1)

    G, P, I = S5_GROUPS, S5_STATE, S5_GROUP
    TG, SG = S5_TOEP_GROUPS, S5_STATE_GROUPS
    NTB, NSB = G // TG, G // SG

    c_ji = lambda c: jnp.tile(jnp.transpose(c, (0, 2, 1)), (1, 1, I))
    b_ji = lambda v: jnp.repeat(v, I, axis=2)
    cbr = c_ji(c_re) * b_ji(bbr) - c_ji(c_im) * b_ji(bbi)
    cbi = c_ji(c_re) * b_ji(bbi) + c_ji(c_im) * b_ji(bbr)
    kern = jnp.einsum('gdp,gpq->gdq', jnp.concatenate([pwr[:, :L], -pwi[:, :L]], axis=2),
                      jnp.concatenate([cbr, cbi], axis=1), precision=HIGHEST)
    idx = jnp.arange

    kc = jnp.transpose(kern.reshape(NTB, TG, L, I, I), (2, 0, 1, 3, 4)).reshape(L, NTB, TG * I, I)
    kc = jnp.pad(kc, ((L - 1, 0), (0, 0), (0, 0), (0, 0)))
    repeat_i = (idx(TG * I)[None, :] % I == idx(I)[:, None]).astype(F32)
    same_t = (idx(TG * I)[:, None] // I == idx(TG * I)[None, :] // I)
    kd = jnp.where(same_t, jnp.einsum('dcri,iq->dcrq', kc, repeat_i, precision=HIGHEST), 0.0)

    by_step = lambda w: jnp.transpose(
        w[:, :L][:, ::-1].reshape(NSB, SG, L, P), (2, 0, 1, 3))[:, :, :, None]
    by_chan = lambda v: jnp.transpose(v, (0, 2, 1)).reshape(1, NSB, SG, I, P)
    qr, qi = by_step(pwr), by_step(pwi)
    br, bi = by_chan(bbr), by_chan(bbi)
    twice = lambda m: jnp.concatenate([m, m], axis=-1).reshape(L, NSB, SG * I, 2 * P)
    bp_re, bp_im = twice(qr * br - qi * bi), twice(qr * bi + qi * br)

    chan = lambda c: jnp.transpose(c.reshape(NSB, SG, I, P), (0, 3, 1, 2)).reshape(1, NSB, P, SG * I)
    step_pow = lambda w: jnp.repeat(
        jnp.transpose(w[:, 1:].reshape(NSB, SG, L, P), (2, 0, 3, 1)), I, axis=3)
    cr, ci, wr, wi = chan(c_re), chan(c_im), step_pow(pwr), step_pow(pwi)
    cp = jnp.concatenate([cr * wr - ci * wi, -(cr * wi + ci * wr)], axis=2)

    alr = pwr[:, L].reshape(NSB, SG * P)
    ali = pwi[:, L].reshape(NSB, SG * P)
    a1 = jnp.concatenate([alr, alr], axis=1).reshape(1, -1)
    a2 = jnp.concatenate([-ali, ali], axis=1).reshape(1, -1)
    return kd.astype(BF16), bp_re.astype(BF16), bp_im.astype(BF16), cp.astype(BF16), a1, a2


def _chunk_step(u_ref, s):
    return u_ref[pl.ds(s, u_ref.shape[0] // S5_CHUNK, stride=S5_CHUNK), :]


def _s5_local_state_kernel(u_ref, bpr_ref, bpi_ref, s_ref):
    SG, P, I = S5_STATE_GROUPS, S5_STATE, S5_GROUP
    own = (lax.broadcasted_iota(jnp.int32, (SG * I, SG * P), 0) // I
           == lax.broadcasted_iota(jnp.int32, (SG * I, SG * P), 1) // P)

    def spread(m):
        tiled = jnp.concatenate([m] * (SG // 2), axis=1)
        return jnp.where(own, tiled, jnp.zeros_like(tiled))

    acc = None
    for s in range(S5_CHUNK):
        bd = jnp.concatenate([spread(bpr_ref[s]), spread(bpi_ref[s])], axis=1)
        d = jnp.dot(_chunk_step(u_ref, s).astype(BF16), bd, preferred_element_type=F32)
        acc = d if acc is None else acc + d
    s_ref[...] = acc


def _s5_scan_kernel(s_ref, a1_ref, a2_ref, o_ref, *, B, nch):
    W = s_ref.shape[1]
    a1 = a1_ref[...]
    a2 = a2_ref[...]
    unit = 2 * S5_STATE_GROUPS * S5_STATE

    def swap_re_im(c):
        parts = []
        for q in range(W // unit):
            lo = q * unit
            parts += [c[:, lo + unit // 2:lo + unit], c[:, lo:lo + unit // 2]]
        return jnp.concatenate(parts, axis=1)

    def body(n, carry):
        new = []
        for bb in range(B):
            row = bb * nch + n
            c = carry[bb]
            o_ref[pl.ds(row, 1), :] = c
            new.append(a1 * c + a2 * swap_re_im(c) + s_ref[pl.ds(row, 1), :])
        return tuple(new)

    lax.fori_loop(0, nch, body, tuple(jnp.zeros((1, W), F32) for _ in range(B)), unroll=4)


def _gelu_tanh(y):
    return 0.5 * y * (1.0 + jnp.tanh(math.sqrt(2.0 / math.pi) * (y + 0.044715 * (y * y * y))))


def _s5_out_kernel(*refs):
    L = S5_CHUNK
    nu = S5_TOEP_GROUPS // S5_STATE_GROUPS
    u_refs, (kd_ref, st_ref, cp_ref, o_ref, ub_ref, acc_ref) = refs[:nu], refs[nu:]
    r = pl.program_id(1)

    @pl.when(r == 0)
    def _():
        for s in range(L):
            for k in range(nu):
                ub_ref[s, :, k * LANES:(k + 1) * LANES] = _chunk_step(u_refs[k], s).astype(BF16)

    def lags(first):
        out = None
        for s in range(first, first + S5_STEP_GROUP):
            d = jnp.dot(ub_ref[s], kd_ref[r - s + (L - 1)], preferred_element_type=F32)
            out = d if out is None else out + d
        return out

    SG, P, I = S5_STATE_GROUPS, S5_STATE, S5_GROUP
    col_group = lax.broadcasted_iota(jnp.int32, (P, SG * I), 1) // I

    def spread(c):
        pieces = [jnp.where(col_group == g, c[h * P:(h + 1) * P], jnp.zeros((P, SG * I), BF16))
                  for h in range(2) for g in range(SG)]
        return jnp.concatenate(pieces, axis=0)

    st = st_ref[...].astype(BF16)
    half = st.shape[1] // 2
    carried = jnp.concatenate(
        [jnp.dot(st[:, :half], spread(cp_ref[0]), preferred_element_type=F32),
         jnp.dot(st[:, half:], spread(cp_ref[1]), preferred_element_type=F32)], axis=1)
    acc_ref[...] = carried + lags(0)
    for first in range(S5_STEP_GROUP, L, S5_STEP_GROUP):
        @pl.when(first <= r)
        def _(first=first):
            acc_ref[...] += lags(first)
    o_ref[...] = acc_ref[...].astype(o_ref.dtype)


def _s5_glu_out_kernel(ys_ref, u_ref, d_ref, wg_ref, bg_ref, z_ref, wo_ref, h_ref, g_ref, b_ref,
                       of_ref, ob_ref, yt_ref):
    L, W = S5_CHUNK, S5_WIDTH
    nslab, tm = yt_ref.shape[0], yt_ref.shape[1]
    for s in range(L):
        for k in range(nslab):
            lo = s * W + k * LANES
            yt_ref[k, pl.ds(s, tm // L, stride=L), :] = ys_ref[:, lo:lo + LANES].astype(F32)
    half = tm // 2
    parts = [pl.ds(0, half), pl.ds(half, half)]
    ys = [_gelu_tanh(jnp.concatenate([yt_ref[k, rows, :] for k in range(nslab)], axis=1)
                     + d_ref[...] * u_ref[rows, :]) for rows in parts]
    ts = [jnp.dot(y.astype(BF16), wg_ref[...], preferred_element_type=F32) for y in ys]
    gated = [(y * _sigmoid(t + bg_ref[...]) * _silu(z_ref[rows, :])).astype(BF16)
             for y, t, rows in zip(ys, ts, parts)]
    branches = [jnp.dot(y, wo_ref[...], preferred_element_type=F32) for y in gated]
    for rows, branch in zip(parts, branches):
        out = _deepnorm_ln(h_ref[rows, :], branch, g_ref[...], b_ref[...])
        of_ref[rows, :] = out
        ob_ref[rows, :] = out.astype(BF16)


def _s5_layer(hf, hb, w_in, lam_re, lam_im, log_step, b_re, b_im, c_re, c_im, d_skip,
              w_glu, b_glu, w_out, g, b, *, B, T, layer=0):
    M, W, G, L = B * T, S5_WIDTH, S5_GROUPS, S5_CHUNK
    nch = T // L
    rows = B * nch
    u = _matmul(hb, w_in, F32, layer=layer, col0=0, ncols=W)
    z = _matmul(hb, w_in, F32, layer=layer, col0=W, ncols=W)
    kd, bp_re, bp_im, cp, a1, a2 = _s5_matrices(lam_re, lam_im, log_step, b_re, b_im, c_re, c_im)

    TG, SG = S5_TOEP_GROUPS, S5_STATE_GROUPS
    ntb, nsb = G // TG, G // SG
    tw, sw_in, sw = TG * S5_GROUP, SG * S5_GROUP, 2 * SG * S5_STATE
    nstate = nsb * sw
    s_loc = pl.pallas_call(
        _s5_local_state_kernel,
        grid=(nsb,),
        in_specs=[pl.BlockSpec((M, sw_in), lambda cb: (0, cb)),
                  pl.BlockSpec((L, None, sw_in, LANES), lambda cb: (0, cb, 0, 0)),
                  pl.BlockSpec((L, None, sw_in, LANES), lambda cb: (0, cb, 0, 0))],
        out_specs=pl.BlockSpec((rows, sw), lambda cb: (0, cb)),
        out_shape=jax.ShapeDtypeStruct((rows, nstate), F32),
        compiler_params=_cparams("parallel"),
        name="s5_chunk_state",
    )(u, bp_re, bp_im)

    scan_w = 2 * sw
    st_in = pl.pallas_call(
        functools.partial(_s5_scan_kernel, B=B, nch=nch),
        grid=(nstate // scan_w,),
        in_specs=[pl.BlockSpec((rows, scan_w), lambda p: (0, p)),
                  pl.BlockSpec((1, scan_w), lambda p: (0, p)),
                  pl.BlockSpec((1, scan_w), lambda p: (0, p))],
        out_specs=pl.BlockSpec((rows, scan_w), lambda p: (0, p)),
        out_shape=jax.ShapeDtypeStruct((rows, nstate), F32),
        compiler_params=_cparams("parallel"),
        name="s5_state_scan",
    )(s_loc, a1, a2)

    per_toep = TG // SG
    y = pl.pallas_call(
        _s5_out_kernel,
        grid=(ntb, L),
        in_specs=[pl.BlockSpec((M, sw_in), lambda cb, r, k=k: (0, per_toep * cb + k))
                  for k in range(per_toep)]
        + [pl.BlockSpec((2 * L - 1, None, tw, tw), lambda cb, r: (0, cb, 0, 0)),
                  pl.BlockSpec((rows, per_toep * sw), lambda cb, r: (0, cb)),
                  pl.BlockSpec((None, per_toep, LANES, sw_in), lambda cb, r: (r, cb, 0, 0))],
        out_specs=pl.BlockSpec((rows, tw), lambda cb, r: (0, r * ntb + cb)),
        out_shape=jax.ShapeDtypeStruct((rows, L * W), BF16),
        scratch_shapes=[pltpu.VMEM((L, rows, tw), BF16), pltpu.VMEM((rows, tw), F32)],
        compiler_params=_cparams("parallel", "arbitrary"),
        name="s5_chunk_output",
    )(*([u] * per_toep), kd, st_in, cp)

    tm = min(256, M)
    row = lambda i: (i, 0)
    fixed = lambda i: (0, 0)
    return pl.pallas_call(
        _s5_glu_out_kernel,
        grid=(M // tm,),
        in_specs=[pl.BlockSpec((tm // L, L * W), row),
                  pl.BlockSpec((tm, W), row),
                  pl.BlockSpec((1, W), fixed),
                  _resident((W, W), fixed),
                  pl.BlockSpec((1, W), fixed),
                  pl.BlockSpec((tm, W), row),
                  _resident((W, D_MODEL), fixed),
                  pl.BlockSpec((tm, D_MODEL), row),
                  pl.BlockSpec((1, D_MODEL), fixed),
                  pl.BlockSpec((1, D_MODEL), fixed)],
        out_specs=[pl.BlockSpec((tm, D_MODEL), row), pl.BlockSpec((tm, D_MODEL), row)],
        out_shape=[jax.ShapeDtypeStruct((M, D_MODEL), F32),
                   jax.ShapeDtypeStruct((M, D_MODEL), BF16)],
        scratch_shapes=[pltpu.VMEM((W // LANES, tm, LANES), F32)],
        compiler_params=_cparams("parallel"),
        name="s5_glu_out_ln",
    )(y, u, d_skip.reshape(1, W), w_glu.astype(BF16), b_glu.reshape(1, W), z,
      w_out.astype(BF16), hf, g.reshape(1, D_MODEL), b.reshape(1, D_MODEL))


def kernel(x, positions, ln_g, ln_b, ssd_w_in, ssd_conv_w, ssd_conv_b, ssd_dt_bias, ssd_a_log, ssd_d, ssd_norm_w, ssd_w_out, fox_w_in, fox_f_bias, fox_w_out, mla_w_in, mla_q_norm, mla_kv_norm, mla_w_q_up, mla_w_kv_up, mla_w_out, s5_w_in, s5_lambda_re, s5_lambda_im, s5_log_step, s5_b_re, s5_b_im, s5_c_re, s5_c_im, s5_d, s5_w_glu, s5_b_glu, s5_w_out):
    B, T, D = x.shape
    hf = x.reshape(B * T, D)
    hb = hf.astype(BF16)
    for i in range(DEPTH):
        j = i // 4
        kind = i % 4
        g, b = ln_g[i], ln_b[i]
        if kind == 0:
            hf, hb = _ssd_layer(hf, hb, ssd_w_in, ssd_conv_w[j], ssd_conv_b[j], ssd_dt_bias[j],
                                ssd_a_log[j], ssd_d[j], ssd_norm_w[j], ssd_w_out[j], g, b,
                                B=B, T=T, layer=j)
        elif kind == 1:
            hf, hb = _fox_layer(hf, hb, fox_w_in, fox_f_bias[j], fox_w_out[j], g, b,
                                B=B, T=T, layer=j)
        elif kind == 2:
            hf, hb = _mla_layer(hf, hb, positions, mla_w_in[j], mla_q_norm[j],
                                mla_kv_norm[j], mla_w_q_up[j], mla_w_kv_up[j], mla_w_out[j],
                                g, b, B=B, T=T)
        else:
            hf, hb = _s5_layer(hf, hb, s5_w_in, s5_lambda_re[j], s5_lambda_im[j], s5_log_step[j],
                               s5_b_re[j], s5_b_im[j], s5_c_re[j], s5_c_im[j], s5_d[j],
                               s5_w_glu[j], s5_b_glu[j], s5_w_out[j], g, b, B=B, T=T, layer=j)
    return hf.reshape(B, T, D)
```

```python
import functools
import math

import jax
import jax.numpy as jnp
from jax import lax
from jax.experimental import pallas as pl
from jax.experimental.pallas import tpu as pltpu

F32 = jnp.float32
BF16 = jnp.bfloat16
HIGHEST = lax.Precision.HIGHEST

D_MODEL = 2048
DEPTH = 4
ALPHA = (2.0 * DEPTH) ** 0.25
LN_EPS = 1e-5
RMS_EPS = 1e-6

SSD_D_INNER = 4096
SSD_HEADS = 64
SSD_HEADDIM = 64
SSD_GROUPS = 8
SSD_HEADS_PER_GROUP = SSD_HEADS // SSD_GROUPS
SSD_STATE = 128
SSD_CONV = 4
SSD_CHUNK = 128
SSD_GROUP_WIDTH = SSD_D_INNER // SSD_GROUPS
SSD_BC_WIDTH = SSD_GROUPS * SSD_STATE
SSD_GROUPS_PER_STEP = 8

FOX_HEADS = 16
FOX_HEAD_DIM = 128
FOX_WIDTH = FOX_HEADS * FOX_HEAD_DIM

MLA_HEADS = 16
MLA_Q_RANK = 512
MLA_KV_RANK = 512
MLA_NOPE = 128
MLA_ROPE = 64
MLA_V = 128
MLA_QK = MLA_NOPE + MLA_ROPE
MLA_WIDTH = MLA_HEADS * MLA_V
ROPE_BASE = 10000.0

S5_WIDTH = D_MODEL
S5_GROUP = 16
S5_GROUPS = S5_WIDTH // S5_GROUP
S5_STATE = 64
S5_CHUNK = 8
S5_TOEP_GROUPS = 16
S5_STATE_GROUPS = 8
S5_STEP_GROUP = 4

LANES = 128
VMEM_LIMIT_BYTES = 48 * 1024 * 1024


def _cparams(*sem):
    return pltpu.CompilerParams(dimension_semantics=sem, vmem_limit_bytes=VMEM_LIMIT_BYTES)


def _resident(block_shape, index_map):
    return pl.BlockSpec(block_shape, index_map, pipeline_mode=pl.Buffered(1))


def _sigmoid(x):
    return 0.5 * (jnp.tanh(0.5 * x) + 1.0)


def _silu(x):
    h = 0.5 * x
    return h + h * jnp.tanh(h)


def _softplus(x):
    return jnp.maximum(x, 0.0) + jnp.log1p(jnp.exp(-jnp.abs(x)))


def _deepnorm_ln(h, branch, g, b):
    r = ALPHA * h + branch
    mu = jnp.mean(r, axis=-1, keepdims=True)
    d = r - mu
    var = jnp.mean(d * d, axis=-1, keepdims=True)
    return d * lax.rsqrt(var + LN_EPS) * g + b


def _project(x, wb, wt):
    dims = (((1,), (1,)), ((), ())) if wt else (((1,), (0,)), ((), ()))
    return lax.dot_general(x, wb, dims, preferred_element_type=F32)


def _mm_kernel(x_ref, w_ref, o_ref, wb_ref, *, scale, wt):
    @pl.when(pl.program_id(1) == 0)
    def _():
        wb_ref[...] = w_ref[...].astype(BF16)

    acc = _project(x_ref[...].astype(BF16), wb_ref[...], wt)
    if scale is not None:
        acc = acc * scale
    o_ref[...] = acc.astype(o_ref.dtype)


def _weight_spec(w, layer, K, tn, j0, wt):
    block = (tn, K) if wt else (K, tn)
    index = (lambda j: (j0 + j, 0)) if wt else (lambda j: (0, j0 + j))
    if w.ndim == 2:
        return pl.BlockSpec(block, lambda j, i: index(j))
    return pl.BlockSpec((None,) + block, lambda j, i: (layer,) + index(j))


def _matmul(x, w, out_dtype, *, layer=0, col0=0, ncols=None, scale=None, tn=1024, wt=False):
    M, K = x.shape
    N = w.shape[-2 if wt else -1] - col0 if ncols is None else ncols
    tm = min(1024 if x.dtype == BF16 else 512, M)
    tn = min(tn, N)
    assert M % tm == 0 and N % tn == 0 and col0 % tn == 0, (M, N, col0, tm, tn)
    j0 = col0 // tn
    return pl.pallas_call(
        functools.partial(_mm_kernel, scale=scale, wt=wt),
        grid=(N // tn, M // tm),
        in_specs=[pl.BlockSpec((tm, K), lambda j, i: (i, 0)),
                  _weight_spec(w, layer, K, tn, j0, wt)],
        out_specs=pl.BlockSpec((tm, tn), lambda j, i: (i, j)),
        out_shape=jax.ShapeDtypeStruct((M, N), out_dtype),
        scratch_shapes=[pltpu.VMEM((tn, K) if wt else (K, tn), BF16)],
        compiler_params=_cparams("parallel", "arbitrary"),
        name="proj",
    )(x, w)


def _mm_cast_kernel(x_ref, w_ref, o_ref, xb_ref, *, wt):
    xb = x_ref[...].astype(BF16)
    xb_ref[...] = xb
    o_ref[...] = _project(xb, w_ref[...].astype(BF16), wt)


def _matmul_and_cast(x, w, *, wt):
    M, K = x.shape
    N = w.shape[0 if wt else 1]
    tm = min(512, M)
    assert M % tm == 0
    return pl.pallas_call(
        functools.partial(_mm_cast_kernel, wt=wt),
        grid=(M // tm,),
        in_specs=[pl.BlockSpec((tm, K), lambda i: (i, 0)),
                  pl.BlockSpec(w.shape, lambda i: (0, 0))],
        out_specs=[pl.BlockSpec((tm, N), lambda i: (i, 0)),
                   pl.BlockSpec((tm, K), lambda i: (i, 0))],
        out_shape=[jax.ShapeDtypeStruct((M, N), F32), jax.ShapeDtypeStruct((M, K), BF16)],
        compiler_params=_cparams("parallel"),
        name="proj_and_cast",
    )(x, w)


CONV_COL_PARTS = 4


def _mm_conv_silu_kernel(x_ref, w_ref, cw_ref, cb_ref, o_ref, wb_ref, ext_ref,
                         *, taps, tiles_per_seq, wt):
    i = pl.program_id(1)
    tm, tn = o_ref.shape

    @pl.when(i == 0)
    def _():
        wb_ref[...] = w_ref[...].astype(BF16)

    @pl.when(i % tiles_per_seq == 0)
    def _():
        ext_ref[pl.ds(0, 8), :] = jnp.zeros((8, tn), F32)

    xb = x_ref[...].astype(BF16)
    slab = tn // CONV_COL_PARTS
    parts = [pl.ds(p * slab, slab) for p in range(CONV_COL_PARTS)]
    raws = [_project(xb, wb_ref[cols, :] if wt else wb_ref[:, cols], wt) for cols in parts]
    for cols, raw in zip(parts, raws):
        ext_ref[pl.ds(8, tm), cols] = raw
        acc = cb_ref[:, cols] + cw_ref[pl.ds(0, 1), cols] * ext_ref[pl.ds(9 - taps, tm), cols]
        for kk in range(1, taps):
            acc = acc + cw_ref[pl.ds(kk, 1), cols] * ext_ref[pl.ds(9 - taps + kk, tm), cols]
        ext_ref[pl.ds(0, 8), cols] = ext_ref[pl.ds(tm, 8), cols]
        o_ref[:, cols] = _silu(acc)


def _matmul_conv_silu(x, w, conv_w, conv_b, *, layer, col0, ncols, seq_len, tn=1024, wt=False):
    M, K = x.shape
    taps = conv_w.shape[0]
    tm = min(1024 if x.dtype == BF16 else 512, seq_len)
    assert M % tm == 0 and seq_len % tm == 0 and ncols % tn == 0 and col0 % tn == 0
    j0 = col0 // tn
    return pl.pallas_call(
        functools.partial(_mm_conv_silu_kernel, taps=taps, tiles_per_seq=seq_len // tm, wt=wt),
        grid=(ncols // tn, M // tm),
        in_specs=[pl.BlockSpec((tm, K), lambda j, i: (i, 0)),
                  _weight_spec(w, layer, K, tn, j0, wt),
                  pl.BlockSpec((taps, tn), lambda j, i: (0, j)),
                  pl.BlockSpec((1, tn), lambda j, i: (0, j))],
        out_specs=pl.BlockSpec((tm, tn), lambda j, i: (i, j)),
        out_shape=jax.ShapeDtypeStruct((M, ncols), F32),
        scratch_shapes=[pltpu.VMEM((tn, K) if wt else (K, tn), BF16),
                        pltpu.VMEM((tm + 8, tn), F32)],
        compiler_params=_cparams("parallel", "arbitrary"),
        name="proj_conv_silu",
    )(x, w, conv_w, conv_b.reshape(1, -1))


def _out_ln_kernel(y_ref, w_ref, h_ref, g_ref, b_ref, of_ref, ob_ref):
    half = y_ref.shape[0] // 2
    parts = [pl.ds(0, half), pl.ds(half, half)]
    branches = [jnp.dot(y_ref[rows, :], w_ref[...], preferred_element_type=F32) for rows in parts]
    for rows, branch in zip(parts, branches):
        out = _deepnorm_ln(h_ref[rows, :], branch, g_ref[...], b_ref[...])
        of_ref[rows, :] = out
        ob_ref[rows, :] = out.astype(BF16)


def _out_proj_ln(y, w, h, g, b):
    M, K = y.shape
    D = w.shape[1]
    tm = min(512 if K * D * w.dtype.itemsize <= VMEM_LIMIT_BYTES // 4 else 256, M)
    assert M % tm == 0
    return pl.pallas_call(
        _out_ln_kernel,
        grid=(M // tm,),
        in_specs=[pl.BlockSpec((tm, K), lambda i: (i, 0)),
                  _resident((K, D), lambda i: (0, 0)),
                  pl.BlockSpec((tm, D), lambda i: (i, 0)),
                  pl.BlockSpec((1, D), lambda i: (0, 0)),
                  pl.BlockSpec((1, D), lambda i: (0, 0))],
        out_specs=[pl.BlockSpec((tm, D), lambda i: (i, 0)),
                   pl.BlockSpec((tm, D), lambda i: (i, 0))],
        out_shape=[jax.ShapeDtypeStruct((M, D), F32), jax.ShapeDtypeStruct((M, D), BF16)],
        compiler_params=_cparams("parallel"),
        name="out_proj_ln",
    )(y, w, h, g.reshape(1, D), b.reshape(1, D))


ATTN_HEADS_PER_STEP = 4
ATTN_WIDE_BLOCKS = 2


def _attn_kernel(*refs, nparts, tq, tk, dv, head_major):
    q_refs = refs[:nparts]
    k_refs = refs[nparts:2 * nparts]
    v_ref, z_ref, o_ref, m_ref, acc_ref = refs[2 * nparts:]
    nh = ATTN_HEADS_PER_STEP
    i = pl.program_id(2)

    def head(ref, hh, rows=slice(None)):
        if head_major:
            return ref[hh, rows, :]
        w = ref.shape[-1] // nh
        return ref[rows, hh * w:(hh + 1) * w]

    def cat(pieces):
        return pieces[0] if len(pieces) == 1 else jnp.concatenate(pieces, axis=1)

    qs = [cat([head(r, hh) for r in q_refs]) for hh in range(nh)]
    m_ref[...] = jnp.full_like(m_ref, -jnp.inf)
    acc_ref[...] = jnp.zeros_like(acc_ref)

    def block(first, width, diag_offset):
        rows = pl.ds(pl.multiple_of(first * tk, tk), width)
        ones_cols = jnp.ones((width, LANES), BF16)
        scores = []
        for hh in range(nh):
            k = cat([head(r, hh, rows) for r in k_refs])
            scores.append(lax.dot_general(qs[hh], k, (((1,), (1,)), ((), ())),
                                          preferred_element_type=F32))
        probs, alphas = [], []
        for hh in range(nh):
            s = scores[hh]
            if diag_offset is not None:
                r = lax.broadcasted_iota(jnp.int32, (tq, width), 0)
                c = lax.broadcasted_iota(jnp.int32, (tq, width), 1)
                s = jnp.where(c + diag_offset <= r, s, -jnp.inf)
            m_prev = m_ref[hh]
            m_new = jnp.maximum(m_prev, jnp.max(s, axis=-1, keepdims=True))
            alpha = jnp.exp(m_prev - m_new)
            p = jnp.exp(s - jnp.concatenate([m_new] * (width // LANES), axis=1))
            m_ref[hh] = m_new
            probs.append(p.astype(BF16))
            alphas.append(jnp.concatenate([alpha] * ((dv + LANES) // LANES), axis=1))
        for hh in range(nh):
            v_aug = jnp.concatenate([head(v_ref, hh, rows), ones_cols], axis=1)
            acc_ref[hh] = alphas[hh] * acc_ref[hh] + jnp.dot(
                probs[hh], v_aug, preferred_element_type=F32)

    n_full = i * (tq // tk)
    wide = ATTN_WIDE_BLOCKS

    def wide_block(j, carry):
        block(j * wide, wide * tk, None)
        return carry

    lax.fori_loop(0, n_full // wide, wide_block, 0)
    if tq == tk and wide == 2:
        @pl.when(n_full % 2 == 1)
        def _():
            block(n_full - 1, 2 * tk, -tk)

        @pl.when(n_full % 2 == 0)
        def _():
            block(n_full, tk, 0)
    else:
        for rem in range(wide - 1):
            @pl.when(n_full % wide > rem)
            def _(rem=rem):
                block((n_full // wide) * wide + rem, tk, None)
        for d in range(tq // tk):
            block(n_full + d, tk, d * tk)

    for hh in range(nh):
        cols = slice(hh * dv, (hh + 1) * dv)
        o = acc_ref[hh, :, :dv] / acc_ref[hh, :, dv:]
        o_ref[:, cols] = (o * _silu(z_ref[:, cols])).astype(o_ref.dtype)


def _attention(inputs, in_specs, out_spec, *, B, T, H, dv, tq, tk, nparts, head_major):
    nh = ATTN_HEADS_PER_STEP
    assert tq % tk == 0 and T % tq == 0 and H % nh == 0
    return pl.pallas_call(
        functools.partial(_attn_kernel, nparts=nparts, tq=tq, tk=tk, dv=dv,
                          head_major=head_major),
        grid=(B, H // nh, T // tq),
        in_specs=in_specs,
        out_specs=out_spec,
        out_shape=jax.ShapeDtypeStruct((B * T, H * dv), BF16),
        scratch_shapes=[pltpu.VMEM((nh, tq, LANES), F32),
                        pltpu.VMEM((nh, tq, dv + LANES), F32)],
        compiler_params=_cparams("parallel", "parallel", "arbitrary"),
        name="causal_attention",
    )(*inputs)


def _attn_tiles(T):
    tq = min(512, T)
    tk = min(512, T)
    return tq, tk


def _ssd_kernel(x_ref, bm_ref, cm_ref, z_ref, dtr_ref,
                dtb_ref, al_ref, dsk_ref, nw_ref, spread_ref,
                o_ref,
                state_ref, yz_ref):
    Q, E, P, GW, N = SSD_CHUNK, SSD_HEADS_PER_GROUP, SSD_HEADDIM, SSD_GROUP_WIDTH, SSD_STATE
    NG = SSD_GROUPS_PER_STEP
    groups = range(NG)
    c = pl.program_id(2)

    @pl.when(c == 0)
    def _():
        state_ref[...] = jnp.zeros_like(state_ref)

    row = lax.broadcasted_iota(jnp.int32, (Q, Q), 0)
    col = lax.broadcasted_iota(jnp.int32, (Q, Q), 1)
    lower = row >= col
    tri_u = (row <= col).astype(BF16)
    left = col < P

    def pieces(v):
        hi = v.astype(BF16).astype(F32)
        mid = (v - hi).astype(BF16).astype(F32)
        lo = ((v - hi) - mid).astype(BF16).astype(F32)
        return jnp.concatenate([hi, mid, lo], axis=0).astype(BF16)

    npairs = E // 2
    dt_r, acs_r = [], []
    for gg in groups:
        dt = _softplus(dtr_ref[gg] + dtb_ref[gg])
        cum3 = jnp.dot(pieces(dt * (-jnp.exp(al_ref[gg]))), tri_u, preferred_element_type=F32)
        dt_r.append(dt)
        acs_r.append(cum3[:E] + cum3[E:2 * E] + cum3[2 * E:])

    bm_b, cm_b, bmt_b, cb, carried = [], [], [], [], []
    for gg in groups:
        ncols = pl.ds(gg * N, N)
        bm = bm_ref[:, ncols]
        bm_b.append(bm.astype(BF16))
        cm_b.append(cm_ref[:, ncols].astype(BF16))
        bmt_b.append(bm.T.astype(BF16))
        cb.append(lax.dot_general(cm_b[gg], bm_b[gg], (((1,), (1,)), ((), ())),
                                  preferred_element_type=F32))
        carried.append([jnp.dot(cm_b[gg], state_ref[gg * npairs + kp].astype(BF16),
                                preferred_element_type=F32)
                        for kp in range(npairs)])

    pair_cols = lambda gg, kp: pl.ds(gg * GW + kp * LANES, LANES)
    xps = [[x_ref[:, pair_cols(gg, kp)] for kp in range(npairs)] for gg in groups]
    gates = [[_silu(z_ref[:, pair_cols(gg, kp)]) for kp in range(npairs)] for gg in groups]

    spreads = [lax.dot_general(pieces(jnp.concatenate([dt_r[gg], acs_r[gg]], axis=0)),
                               spread_ref[...], (((0,), (0,)), ((), ())),
                               preferred_element_type=F32) for gg in groups]

    for gg in groups:
        dt_ch = spreads[gg][:, :GW]
        acs_ch = spreads[gg][:, GW:2 * GW]
        acs_col = spreads[gg][:, 2 * GW:]
        ssq = jnp.zeros((Q, 1), F32)
        for kp in range(npairs):
            cols = pair_cols(gg, kp)
            xp = xps[gg][kp]
            dt_p = dt_ch[:, kp * LANES:(kp + 1) * LANES]
            acs_p = acs_ch[:, kp * LANES:(kp + 1) * LANES]
            last_p = acs_p[Q - 1:Q, :]
            xdt = xp * dt_p
            xdt_b = xdt.astype(BF16)

            ys = []
            for e in (2 * kp, 2 * kp + 1):
                seg = acs_col[:, e * Q:(e + 1) * Q] - acs_r[gg][e:e + 1, :]
                dec = jnp.exp(jnp.where(lower, seg, -jnp.inf))
                ys.append(jnp.dot((cb[gg] * dec).astype(BF16), xdt_b,
                                  preferred_element_type=F32))
            y_diag = jnp.where(left, ys[0], ys[1])

            xw = (xdt * jnp.exp(last_p - acs_p)).astype(BF16)
            s_loc = jnp.dot(bmt_b[gg], xw, preferred_element_type=F32)
            y_off = carried[gg][kp] * jnp.exp(acs_p)
            slot = gg * npairs + kp
            state_ref[slot] = state_ref[slot] * jnp.exp(last_p) + s_loc

            y = (y_diag + y_off + dsk_ref[:, cols] * xp) * gates[gg][kp]
            yz_ref[:, cols] = y
            ssq = ssq + jnp.sum(y * y, axis=-1, keepdims=True)

        gcols = pl.ds(gg * GW, GW)
        inv = lax.rsqrt(ssq * (1.0 / GW) + RMS_EPS)
        o_ref[:, gcols] = (yz_ref[:, gcols] * inv * nw_ref[:, gcols]).astype(o_ref.dtype)


def _ssd_core(z, xbc, dt, dt_bias, a_log, d_skip, norm_w, *, B, T):
    G, E, Q = SSD_GROUPS, SSD_HEADS_PER_GROUP, SSD_CHUNK
    NC = T // Q
    GW, N = SSD_GROUP_WIDTH, SSD_STATE
    dtr = jnp.transpose(dt[:, :SSD_HEADS].reshape(B, T, G, E), (0, 2, 3, 1))
    head_rows = lambda v: v.reshape(G, E, 1)
    d_ch = jnp.repeat(d_skip, SSD_HEADDIM).reshape(1, -1)
    heads = jnp.arange(E)
    per_channel = (jnp.arange(GW)[None, :] // SSD_HEADDIM == heads[:, None]).astype(BF16)
    per_head_block = (jnp.arange(E * Q)[None, :] // Q == heads[:, None]).astype(BF16)
    none = lambda n: jnp.zeros((E, n), BF16)
    spread = jnp.concatenate([
        jnp.concatenate([per_channel, none(GW), none(E * Q)], axis=1),
        jnp.concatenate([none(GW), per_channel, per_head_block], axis=1)], axis=0)
    spread = jnp.tile(spread, (3, 1))

    NG = SSD_GROUPS_PER_STEP
    XW, BW = NG * GW, NG * N
    assert G % NG == 0
    bblk = SSD_D_INNER // BW
    cblk = (SSD_D_INNER + SSD_BC_WIDTH) // BW
    rowblk = lambda b, g, c: b * NC + c
    in_specs = [
        pl.BlockSpec((Q, XW), lambda b, g, c: (rowblk(b, g, c), g)),
        pl.BlockSpec((Q, BW), lambda b, g, c: (rowblk(b, g, c), bblk + g)),
        pl.BlockSpec((Q, BW), lambda b, g, c: (rowblk(b, g, c), cblk + g)),
        pl.BlockSpec((Q, XW), lambda b, g, c: (rowblk(b, g, c), g)),
        pl.BlockSpec((None, NG, E, Q), lambda b, g, c: (b, g, 0, c)),
        pl.BlockSpec((NG, E, 1), lambda b, g, c: (g, 0, 0)),
        pl.BlockSpec((NG, E, 1), lambda b, g, c: (g, 0, 0)),
        pl.BlockSpec((1, XW), lambda b, g, c: (0, g)),
        pl.BlockSpec((1, XW), lambda b, g, c: (0, g)),
        pl.BlockSpec(spread.shape, lambda b, g, c: (0, 0)),
    ]
    return pl.pallas_call(
        _ssd_kernel,
        grid=(B, G // NG, NC),
        in_specs=in_specs,
        out_specs=pl.BlockSpec((Q, XW), lambda b, g, c: (rowblk(b, g, c), g)),
        out_shape=jax.ShapeDtypeStruct((B * T, SSD_D_INNER), BF16),
        scratch_shapes=[pltpu.VMEM((NG * E // 2, N, LANES), F32), pltpu.VMEM((Q, XW), F32)],
        compiler_params=_cparams("parallel", "parallel", "arbitrary"),
        name="ssd_chunk_scan",
    )(xbc, xbc, xbc, z, dtr,
      head_rows(dt_bias), head_rows(a_log), d_ch, norm_w.reshape(1, -1), spread)


def _pad_rows(w, n):
    return jnp.pad(w, ((0, n - w.shape[0]), (0, 0)))


def _layer_of(w, layer):
    return w[layer] if w.ndim == 3 else w


def _ssd_layer(hf, hb, w_in, conv_w, conv_b, dt_bias, a_log, d_skip, norm_w, w_out, g, b,
               *, B, T, layer=0):
    di = SSD_D_INNER
    cd = di + 2 * SSD_BC_WIDTH
    w_t = jnp.swapaxes(w_in, -1, -2)
    w_dt = _pad_rows(_layer_of(w_t, layer)[di + cd:], LANES)
    if hb is None:
        dt, hb = _matmul_and_cast(hf, w_dt, wt=True)
    else:
        dt = _matmul(hb, w_dt, F32, wt=True)
    z = _matmul(hb, w_t, F32, layer=layer, col0=0, ncols=di, wt=True)
    xbc = _matmul_conv_silu(hb, w_t, conv_w, conv_b, layer=layer, col0=di, ncols=cd, seq_len=T,
                            wt=True)
    y = _ssd_core(z, xbc, dt, dt_bias, a_log, d_skip, norm_w, B=B, T=T)
    return _out_proj_ln(y, w_out.astype(BF16), hf, g, b)


FOX_BIAS_PIECES = 3


def _fox_bias_kernel(f_ref, b_ref, qx_ref, kx_ref, carry_ref, *, nblk):
    blk, W, P = LANES, FOX_WIDTH, FOX_BIAS_PIECES
    row = lax.broadcasted_iota(jnp.int32, (blk, blk), 0)
    col = lax.broadcasted_iota(jnp.int32, (blk, blk), 1)
    tri = (row >= col).astype(F32)
    bias = b_ref[...]
    H = FOX_HEADS
    r = lax.broadcasted_iota(jnp.int32, (blk, W), 0)
    c = lax.broadcasted_iota(jnp.int32, (blk, W), 1)
    head, piece = r % H, r // H
    is_piece = r < P * H
    place_q = jnp.logical_and(is_piece, c == head * blk + piece).astype(BF16)
    place_k = jnp.logical_and(is_piece, c == head * blk + P + piece).astype(BF16)
    is_head = col < H
    cmod = lax.broadcasted_iota(jnp.int32, (1, W), 1) % blk
    ones_q = jnp.logical_and(cmod >= P, cmod < 2 * P).astype(F32)
    ones_k = (cmod < P).astype(F32)

    def body(t, carry):
        rows = pl.ds(t * blk, blk)
        x = f_ref[rows, :] + bias
        logf = jnp.minimum(x, 0.0) - jnp.log1p(jnp.exp(-jnp.abs(x)))
        cum = jnp.dot(tri, logf, precision=HIGHEST, preferred_element_type=F32) + carry
        hi = jnp.where(is_head, cum, 0.0).astype(BF16).astype(F32)
        r1 = jnp.where(is_head, cum, 0.0) - hi
        mid = r1.astype(BF16).astype(F32)
        lo = (r1 - mid).astype(BF16).astype(F32)
        pieces = (hi + pltpu.roll(mid, H, 1) + pltpu.roll(lo, 2 * H, 1)).astype(BF16)
        qx_ref[rows, :] = (jnp.dot(pieces, place_q, preferred_element_type=F32)
                           + ones_q).astype(BF16)
        kx_ref[rows, :] = (ones_k - jnp.dot(pieces, place_k, preferred_element_type=F32)
                           ).astype(BF16)
        return cum[blk - 1:blk, :]

    @pl.when(pl.program_id(1) == 0)
    def _():
        carry_ref[...] = jnp.zeros_like(carry_ref)

    carry = carry_ref[...]
    for t in range(nblk):
        carry = body(t, carry)
    carry_ref[...] = carry


def _fox_bias_columns(f, f_bias, *, B, T):
    bias = jnp.pad(f_bias, (0, LANES - FOX_HEADS)).reshape(1, LANES)
    tt = min(512, T)
    out = jax.ShapeDtypeStruct((B, T, FOX_WIDTH), BF16)
    spec = pl.BlockSpec((None, tt, FOX_WIDTH), lambda b, t: (b, t, 0))
    qx, kx = pl.pallas_call(
        functools.partial(_fox_bias_kernel, nblk=tt // LANES),
        grid=(B, T // tt),
        in_specs=[pl.BlockSpec((None, tt, LANES), lambda b, t: (b, t, 0)),
                  pl.BlockSpec((1, LANES), lambda b, t: (0, 0))],
        out_specs=[spec, spec],
        out_shape=[out, out],
        scratch_shapes=[pltpu.VMEM((1, LANES), F32)],
        compiler_params=_cparams("parallel", "arbitrary"),
        name="fox_cum_log_forget",
    )(f.reshape(B, T, LANES), bias)
    return qx.reshape(B * T, FOX_WIDTH), kx.reshape(B * T, FOX_WIDTH)


def _fox_layer(hf, hb, w_in, f_bias, w_out, g, b, *, B, T, layer=0):
    W, H, dh = FOX_WIDTH, FOX_HEADS, FOX_HEAD_DIM
    w_t = jnp.swapaxes(w_in, -1, -2)
    q = _matmul(hb, w_t, BF16, layer=layer, col0=0, ncols=W, scale=dh ** -0.5, wt=True)
    kv = _matmul(hb, w_t, BF16, layer=layer, col0=W, ncols=2 * W, wt=True)
    z = _matmul(hb, w_t, F32, layer=layer, col0=3 * W, ncols=W, wt=True)
    f = _matmul(hb, _pad_rows(_layer_of(w_t, layer)[4 * W:], LANES), F32, wt=True)
    qx, kx = _fox_bias_columns(f, f_bias, B=B, T=T)

    tq, tk = _attn_tiles(T)
    nq = T // tq
    hw = ATTN_HEADS_PER_STEP * dh
    q_tile = pl.BlockSpec((tq, hw), lambda bb, h, i: (bb * nq + i, h))
    seq_k = pl.BlockSpec((T, hw), lambda bb, h, i: (bb, h))
    seq_v = pl.BlockSpec((T, hw), lambda bb, h, i: (bb, W // hw + h))
    in_specs = [q_tile, q_tile, seq_k, seq_k, seq_v, q_tile]
    y = _attention((q, qx, kv, kx, kv, z), in_specs, q_tile,
                   B=B, T=T, H=H, dv=dh, tq=tq, tk=tk, nparts=2, head_major=False)
    return _out_proj_ln(y, w_out.astype(BF16), hf, g, b)


def _rope_table_kernel(pos_ref, freq_ref, sign_ref, cos_ref, sin_ref):
    ang = pos_ref[...].astype(F32) * freq_ref[...]
    cos_ref[...] = jnp.cos(ang)
    sin_ref[...] = jnp.sin(ang) * sign_ref[...]


def _rope_tables(positions, *, B, T):
    half = MLA_ROPE // 2
    inv_freq = ROPE_BASE ** (-jnp.arange(0, MLA_ROPE, 2, dtype=F32) / MLA_ROPE)
    freq = jnp.tile(inv_freq, LANES // half).reshape(1, LANES)
    sign = jnp.tile(jnp.concatenate([-jnp.ones((half,), F32), jnp.ones((half,), F32)]),
                    LANES // MLA_ROPE).reshape(1, LANES)
    M = B * T
    tm = min(1024, M)
    return pl.pallas_call(
        _rope_table_kernel,
        grid=(M // tm,),
        in_specs=[pl.BlockSpec((tm, 1), lambda i: (i, 0)),
                  pl.BlockSpec((1, LANES), lambda i: (0, 0)),
                  pl.BlockSpec((1, LANES), lambda i: (0, 0))],
        out_specs=[pl.BlockSpec((tm, LANES), lambda i: (i, 0)),
                   pl.BlockSpec((tm, LANES), lambda i: (i, 0))],
        out_shape=[jax.ShapeDtypeStruct((M, LANES), F32), jax.ShapeDtypeStruct((M, LANES), F32)],
        compiler_params=_cparams("parallel"),
        name="rope_tables",
    )(positions.reshape(M, 1), freq, sign)


def _rms_to_bf16(x, w):
    y = x * lax.rsqrt(jnp.mean(x * x, axis=-1, keepdims=True) + RMS_EPS)
    return (y * w).astype(BF16)


def _rope_pair(c2, cos, sin):
    return c2 * cos + pltpu.roll(c2, MLA_ROPE, 1) * sin


def _mla_qkv_kernel(x_ref, wl_ref, qn_ref, kn_ref, wq_ref, wkv_ref, cos_ref, sin_ref,
                    q_ref, k_ref, v_ref, *, scale):
    qr, kr = MLA_Q_RANK, MLA_KV_RANK
    cos, sin = cos_ref[...], sin_ref[...]
    lat = _project(x_ref[...], wl_ref[...], True)
    qn = _rms_to_bf16(lat[:, :qr], qn_ref[...])
    kn = _rms_to_bf16(lat[:, qr:qr + kr], kn_ref[...])
    k_pe = _rope_pair(lat[:, qr + kr:], cos, sin)[:, :MLA_ROPE].astype(k_ref.dtype)
    for h in range(MLA_HEADS):
        r = jnp.dot(qn, wq_ref[h], preferred_element_type=F32)
        pe = _rope_pair(r[:, MLA_NOPE:], cos, sin)
        q_ref[h, :, :MLA_NOPE] = (r[:, :MLA_NOPE] * scale).astype(q_ref.dtype)
        q_ref[h, :, MLA_NOPE:] = (pe[:, :MLA_ROPE] * scale).astype(q_ref.dtype)
        r = jnp.dot(kn, wkv_ref[h], preferred_element_type=F32)
        k_ref[h, :, :MLA_NOPE] = r[:, :MLA_NOPE].astype(k_ref.dtype)
        k_ref[h, :, MLA_NOPE:] = k_pe
        v_ref[h] = r[:, MLA_NOPE:].astype(v_ref.dtype)


def _swap_halves(w):
    half = w.shape[-1] // 2
    return jnp.concatenate([w[..., half:], w[..., :half]], axis=-1)


def _mla_layer(hf, hb, positions, w_in, q_norm, kv_norm, w_q_up, w_kv_up, w_out, g, b, *, B, T):
    H, M = MLA_HEADS, B * T
    qr, kr = MLA_Q_RANK, MLA_KV_RANK
    lat_end = qr + kr + MLA_ROPE
    w_t = jnp.swapaxes(w_in, -1, -2)
    w_pe = w_t[qr + kr:lat_end]
    half = MLA_ROPE // 2
    w_lat = jnp.concatenate([w_t[:qr + kr], w_pe, w_pe[half:], w_pe[:half]], axis=0).astype(BF16)
    nlat = w_lat.shape[0]
    z = _matmul(hb, w_t[lat_end:], F32, wt=True)
    cos, sin = _rope_tables(positions, B=B, T=T)

    wq = w_q_up.reshape(qr, H, MLA_QK)
    wq = jnp.concatenate([wq, _swap_halves(wq[..., MLA_NOPE:])], axis=-1)
    wq = jnp.transpose(wq, (1, 0, 2)).astype(BF16)
    wkv = jnp.transpose(w_kv_up.reshape(kr, H, MLA_NOPE + MLA_V), (1, 0, 2)).astype(BF16)

    tm = min(256, M)
    wide = MLA_NOPE + LANES
    row = lambda i: (i, 0)
    q, k, v = pl.pallas_call(
        functools.partial(_mla_qkv_kernel, scale=MLA_QK ** -0.5),
        grid=(M // tm,),
        in_specs=[pl.BlockSpec((tm, D_MODEL), row),
                  _resident((nlat, D_MODEL), lambda i: (0, 0)),
                  pl.BlockSpec((1, qr), lambda i: (0, 0)),
                  pl.BlockSpec((1, kr), lambda i: (0, 0)),
                  _resident((H, qr, wide), lambda i: (0, 0, 0)),
                  _resident((H, kr, MLA_NOPE + MLA_V), lambda i: (0, 0, 0)),
                  pl.BlockSpec((tm, LANES), row),
                  pl.BlockSpec((tm, LANES), row)],
        out_specs=[pl.BlockSpec((H, tm, MLA_QK), lambda i: (0, i, 0)),
                   pl.BlockSpec((H, tm, MLA_QK), lambda i: (0, i, 0)),
                   pl.BlockSpec((H, tm, MLA_V), lambda i: (0, i, 0))],
        out_shape=[jax.ShapeDtypeStruct((H, M, MLA_QK), BF16),
                   jax.ShapeDtypeStruct((H, M, MLA_QK), BF16),
                   jax.ShapeDtypeStruct((H, M, MLA_V), BF16)],
        compiler_params=_cparams("parallel"),
        name="mla_qkv_rope",
    )(hb, w_lat, q_norm.reshape(1, qr), kv_norm.reshape(1, kr), wq, wkv, cos, sin)

    tq, tk = _attn_tiles(T)
    nq = T // tq
    nh = ATTN_HEADS_PER_STEP
    in_specs = [
        pl.BlockSpec((nh, tq, MLA_QK), lambda bb, h, i: (h, bb * nq + i, 0)),
        pl.BlockSpec((nh, T, MLA_QK), lambda bb, h, i: (h, bb, 0)),
        pl.BlockSpec((nh, T, MLA_V), lambda bb, h, i: (h, bb, 0)),
        pl.BlockSpec((tq, nh * MLA_V), lambda bb, h, i: (bb * nq + i, h)),
    ]
    out_spec = pl.BlockSpec((tq, nh * MLA_V), lambda bb, h, i: (bb * nq + i, h))
    y = _attention((q, k, v, z), in_specs, out_spec,
                   B=B, T=T, H=H, dv=MLA_V, tq=tq, tk=tk, nparts=1, head_major=True)
    return _out_proj_ln(y, w_out.astype(BF16), hf, g, b)


def _s5_matrices(lam_re, lam_im, log_step, b_re, b_im, c_re, c_im):
    L = S5_CHUNK
    step = jnp.exp(log_step.astype(F32))[:, None]
    mag = jnp.exp(lam_re * step)
    ar = mag * jnp.cos(lam_im * step)
    ai = mag * jnp.sin(lam_im * step)
    den = lam_re * lam_re + lam_im * lam_im
    fr = ((ar - 1.0) * lam_re + ai * lam_im) / den
    fi = (ai * lam_re - (ar - 1.0) * lam_im) / den
    bbr = fr[..., None] * b_re - fi[..., None] * b_im
    bbi = fr[..., None] * b_im + fi[..., None] * b_re
    pr, pi = [jnp.ones_like(ar)], [jnp.zeros_like(ar)]
    for _ in range(L):
        pr_next = pr[-1] * ar - pi[-1] * ai
        pi_next = pr[-1] * ai + pi[-1] * ar
        pr.append(pr_next)
        pi.append(pi_next)
    pwr = jnp.stack(pr, axis=1)
    pwi = jnp.stack(pi, axis=1)

    G, P, I = S5_GROUPS, S5_STATE, S5_GROUP
    TG, SG = S5_TOEP_GROUPS, S5_STATE_GROUPS
    NTB, NSB = G // TG, G // SG

    c_ji = lambda c: jnp.tile(jnp.transpose(c, (0, 2, 1)), (1, 1, I))
    b_ji = lambda v: jnp.repeat(v, I, axis=2)
    cbr = c_ji(c_re) * b_ji(bbr) - c_ji(c_im) * b_ji(bbi)
    cbi = c_ji(c_re) * b_ji(bbi) + c_ji(c_im) * b_ji(bbr)
    kern = jnp.einsum('gdp,gpq->gdq', jnp.concatenate([pwr[:, :L], -pwi[:, :L]], axis=2),
                      jnp.concatenate([cbr, cbi], axis=1), precision=HIGHEST)
    idx = jnp.arange

    kc = jnp.transpose(kern.reshape(NTB, TG, L, I, I), (2, 0, 1, 3, 4)).reshape(L, NTB, TG * I, I)
    kc = jnp.pad(kc, ((L - 1, 0), (0, 0), (0, 0), (0, 0)))
    repeat_i = (idx(TG * I)[None, :] % I == idx(I)[:, None]).astype(F32)
    same_t = (idx(TG * I)[:, None] // I == idx(TG * I)[None, :] // I)
    kd = jnp.where(same_t, jnp.einsum('dcri,iq->dcrq', kc, repeat_i, precision=HIGHEST), 0.0)

    by_step = lambda w: jnp.transpose(
        w[:, :L][:, ::-1].reshape(NSB, SG, L, P), (2, 0, 1, 3))[:, :, :, None]
    by_chan = lambda v: jnp.transpose(v, (0, 2, 1)).reshape(1, NSB, SG, I, P)
    qr, qi = by_step(pwr), by_step(pwi)
    br, bi = by_chan(bbr), by_chan(bbi)
    twice = lambda m: jnp.concatenate([m, m], axis=-1).reshape(L, NSB, SG * I, 2 * P)
    bp_re, bp_im = twice(qr * br - qi * bi), twice(qr * bi + qi * br)

    chan = lambda c: jnp.transpose(c.reshape(NSB, SG, I, P), (0, 3, 1, 2)).reshape(1, NSB, P, SG * I)
    step_pow = lambda w: jnp.repeat(
        jnp.transpose(w[:, 1:].reshape(NSB, SG, L, P), (2, 0, 3, 1)), I, axis=3)
    cr, ci, wr, wi = chan(c_re), chan(c_im), step_pow(pwr), step_pow(pwi)
    cp = jnp.concatenate([cr * wr - ci * wi, -(cr * wi + ci * wr)], axis=2)

    alr = pwr[:, L].reshape(NSB, SG * P)
    ali = pwi[:, L].reshape(NSB, SG * P)
    a1 = jnp.concatenate([alr, alr], axis=1).reshape(1, -1)
    a2 = jnp.concatenate([-ali, ali], axis=1).reshape(1, -1)
    return kd.astype(BF16), bp_re.astype(BF16), bp_im.astype(BF16), cp.astype(BF16), a1, a2


def _chunk_step(u_ref, s):
    return u_ref[pl.ds(s, u_ref.shape[0] // S5_CHUNK, stride=S5_CHUNK), :]


def _s5_local_state_kernel(u_ref, bpr_ref, bpi_ref, s_ref):
    SG, P, I = S5_STATE_GROUPS, S5_STATE, S5_GROUP
    own = (lax.broadcasted_iota(jnp.int32, (SG * I, SG * P), 0) // I
           == lax.broadcasted_iota(jnp.int32, (SG * I, SG * P), 1) // P)

    def spread(m):
        tiled = jnp.concatenate([m] * (SG // 2), axis=1)
        return jnp.where(own, tiled, jnp.zeros_like(tiled))

    acc = None
    for s in range(S5_CHUNK):
        bd = jnp.concatenate([spread(bpr_ref[s]), spread(bpi_ref[s])], axis=1)
        d = jnp.dot(_chunk_step(u_ref, s).astype(BF16), bd, preferred_element_type=F32)
        acc = d if acc is None else acc + d
    s_ref[...] = acc


def _s5_scan_kernel(s_ref, a1_ref, a2_ref, o_ref, *, B, nch):
    W = s_ref.shape[1]
    a1 = a1_ref[...]
    a2 = a2_ref[...]
    unit = 2 * S5_STATE_GROUPS * S5_STATE

    def swap_re_im(c):
        parts = []
        for q in range(W // unit):
            lo = q * unit
            parts += [c[:, lo + unit // 2:lo + unit], c[:, lo:lo + unit // 2]]
        return jnp.concatenate(parts, axis=1)

    def body(n, carry):
        new = []
        for bb in range(B):
            row = bb * nch + n
            c = carry[bb]
            o_ref[pl.ds(row, 1), :] = c
            new.append(a1 * c + a2 * swap_re_im(c) + s_ref[pl.ds(row, 1), :])
        return tuple(new)

    lax.fori_loop(0, nch, body, tuple(jnp.zeros((1, W), F32) for _ in range(B)), unroll=4)


def _gelu_tanh(y):
    return 0.5 * y * (1.0 + jnp.tanh(math.sqrt(2.0 / math.pi) * (y + 0.044715 * (y * y * y))))


def _s5_out_kernel(*refs):
    L = S5_CHUNK
    nu = S5_TOEP_GROUPS // S5_STATE_GROUPS
    u_refs, (kd_ref, st_ref, cp_ref, o_ref, ub_ref, acc_ref) = refs[:nu], refs[nu:]
    r = pl.program_id(1)

    @pl.when(r == 0)
    def _():
        for s in range(L):
            for k in range(nu):
                ub_ref[s, :, k * LANES:(k + 1) * LANES] = _chunk_step(u_refs[k], s).astype(BF16)

    def lags(first):
        out = None
        for s in range(first, first + S5_STEP_GROUP):
            d = jnp.dot(ub_ref[s], kd_ref[r - s + (L - 1)], preferred_element_type=F32)
            out = d if out is None else out + d
        return out

    SG, P, I = S5_STATE_GROUPS, S5_STATE, S5_GROUP
    col_group = lax.broadcasted_iota(jnp.int32, (P, SG * I), 1) // I

    def spread(c):
        pieces = [jnp.where(col_group == g, c[h * P:(h + 1) * P], jnp.zeros((P, SG * I), BF16))
                  for h in range(2) for g in range(SG)]
        return jnp.concatenate(pieces, axis=0)

    st = st_ref[...].astype(BF16)
    half = st.shape[1] // 2
    carried = jnp.concatenate(
        [jnp.dot(st[:, :half], spread(cp_ref[0]), preferred_element_type=F32),
         jnp.dot(st[:, half:], spread(cp_ref[1]), preferred_element_type=F32)], axis=1)
    acc_ref[...] = carried + lags(0)
    for first in range(S5_STEP_GROUP, L, S5_STEP_GROUP):
        @pl.when(first <= r)
        def _(first=first):
            acc_ref[...] += lags(first)
    o_ref[...] = acc_ref[...].astype(o_ref.dtype)


def _s5_glu_out_kernel(ys_ref, u_ref, d_ref, wg_ref, bg_ref, z_ref, wo_ref, h_ref, g_ref, b_ref,
                       of_ref, ob_ref, yt_ref):
    L, W = S5_CHUNK, S5_WIDTH
    nslab, tm = yt_ref.shape[0], yt_ref.shape[1]
    for s in range(L):
        for k in range(nslab):
            lo = s * W + k * LANES
            yt_ref[k, pl.ds(s, tm // L, stride=L), :] = ys_ref[:, lo:lo + LANES].astype(F32)
    half = tm // 2
    parts = [pl.ds(0, half), pl.ds(half, half)]
    ys = [_gelu_tanh(jnp.concatenate([yt_ref[k, rows, :] for k in range(nslab)], axis=1)
                     + d_ref[...] * u_ref[rows, :]) for rows in parts]
    ts = [jnp.dot(y.astype(BF16), wg_ref[...], preferred_element_type=F32) for y in ys]
    gated = [(y * _sigmoid(t + bg_ref[...]) * _silu(z_ref[rows, :])).astype(BF16)
             for y, t, rows in zip(ys, ts, parts)]
    branches = [jnp.dot(y, wo_ref[...], preferred_element_type=F32) for y in gated]
    for rows, branch in zip(parts, branches):
        out = _deepnorm_ln(h_ref[rows, :], branch, g_ref[...], b_ref[...])
        of_ref[rows, :] = out
        ob_ref[rows, :] = out.astype(BF16)


def _s5_layer(hf, hb, w_in, lam_re, lam_im, log_step, b_re, b_im, c_re, c_im, d_skip,
              w_glu, b_glu, w_out, g, b, *, B, T, layer=0):
    M, W, G, L = B * T, S5_WIDTH, S5_GROUPS, S5_CHUNK
    nch = T // L
    rows = B * nch
    u = _matmul(hb, w_in, F32, layer=layer, col0=0, ncols=W)
    z = _matmul(hb, w_in, F32, layer=layer, col0=W, ncols=W)
    kd, bp_re, bp_im, cp, a1, a2 = _s5_matrices(lam_re, lam_im, log_step, b_re, b_im, c_re, c_im)

    TG, SG = S5_TOEP_GROUPS, S5_STATE_GROUPS
    ntb, nsb = G // TG, G // SG
    tw, sw_in, sw = TG * S5_GROUP, SG * S5_GROUP, 2 * SG * S5_STATE
    nstate = nsb * sw
    s_loc = pl.pallas_call(
        _s5_local_state_kernel,
        grid=(nsb,),
        in_specs=[pl.BlockSpec((M, sw_in), lambda cb: (0, cb)),
                  pl.BlockSpec((L, None, sw_in, LANES), lambda cb: (0, cb, 0, 0)),
                  pl.BlockSpec((L, None, sw_in, LANES), lambda cb: (0, cb, 0, 0))],
        out_specs=pl.BlockSpec((rows, sw), lambda cb: (0, cb)),
        out_shape=jax.ShapeDtypeStruct((rows, nstate), F32),
        compiler_params=_cparams("parallel"),
        name="s5_chunk_state",
    )(u, bp_re, bp_im)

    scan_w = 2 * sw
    st_in = pl.pallas_call(
        functools.partial(_s5_scan_kernel, B=B, nch=nch),
        grid=(nstate // scan_w,),
        in_specs=[pl.BlockSpec((rows, scan_w), lambda p: (0, p)),
                  pl.BlockSpec((1, scan_w), lambda p: (0, p)),
                  pl.BlockSpec((1, scan_w), lambda p: (0, p))],
        out_specs=pl.BlockSpec((rows, scan_w), lambda p: (0, p)),
        out_shape=jax.ShapeDtypeStruct((rows, nstate), F32),
        compiler_params=_cparams("parallel"),
        name="s5_state_scan",
    )(s_loc, a1, a2)

    per_toep = TG // SG
    y = pl.pallas_call(
        _s5_out_kernel,
        grid=(ntb, L),
        in_specs=[pl.BlockSpec((M, sw_in), lambda cb, r, k=k: (0, per_toep * cb + k))
                  for k in range(per_toep)]
        + [pl.BlockSpec((2 * L - 1, None, tw, tw), lambda cb, r: (0, cb, 0, 0)),
                  pl.BlockSpec((rows, per_toep * sw), lambda cb, r: (0, cb)),
                  pl.BlockSpec((None, per_toep, LANES, sw_in), lambda cb, r: (r, cb, 0, 0))],
        out_specs=pl.BlockSpec((rows, tw), lambda cb, r: (0, r * ntb + cb)),
        out_shape=jax.ShapeDtypeStruct((rows, L * W), BF16),
        scratch_shapes=[pltpu.VMEM((L, rows, tw), BF16), pltpu.VMEM((rows, tw), F32)],
        compiler_params=_cparams("parallel", "arbitrary"),
        name="s5_chunk_output",
    )(*([u] * per_toep), kd, st_in, cp)

    tm = min(256, M)
    row = lambda i: (i, 0)
    fixed = lambda i: (0, 0)
    return pl.pallas_call(
        _s5_glu_out_kernel,
        grid=(M // tm,),
        in_specs=[pl.BlockSpec((tm // L, L * W), row),
                  pl.BlockSpec((tm, W), row),
                  pl.BlockSpec((1, W), fixed),
                  _resident((W, W), fixed),
                  pl.BlockSpec((1, W), fixed),
                  pl.BlockSpec((tm, W), row),
                  _resident((W, D_MODEL), fixed),
                  pl.BlockSpec((tm, D_MODEL), row),
                  pl.BlockSpec((1, D_MODEL), fixed),
                  pl.BlockSpec((1, D_MODEL), fixed)],
        out_specs=[pl.BlockSpec((tm, D_MODEL), row), pl.BlockSpec((tm, D_MODEL), row)],
        out_shape=[jax.ShapeDtypeStruct((M, D_MODEL), F32),
                   jax.ShapeDtypeStruct((M, D_MODEL), BF16)],
        scratch_shapes=[pltpu.VMEM((W // LANES, tm, LANES), F32)],
        compiler_params=_cparams("parallel"),
        name="s5_glu_out_ln",
    )(y, u, d_skip.reshape(1, W), w_glu.astype(BF16), b_glu.reshape(1, W), z,
      w_out.astype(BF16), hf, g.reshape(1, D_MODEL), b.reshape(1, D_MODEL))


def kernel(x, positions, ln_g, ln_b, ssd_w_in, ssd_conv_w, ssd_conv_b, ssd_dt_bias, ssd_a_log, ssd_d, ssd_norm_w, ssd_w_out, fox_w_in, fox_f_bias, fox_w_out, mla_w_in, mla_q_norm, mla_kv_norm, mla_w_q_up, mla_w_kv_up, mla_w_out, s5_w_in, s5_lambda_re, s5_lambda_im, s5_log_step, s5_b_re, s5_b_im, s5_c_re, s5_c_im, s5_d, s5_w_glu, s5_b_glu, s5_w_out):
    B, T, D = x.shape
    hf = x.reshape(B * T, D)
    hb = None
    for i in range(DEPTH):
        j = i // 4
        kind = i % 4
        g, b = ln_g[i], ln_b[i]
        if hb is None and kind != 0:
            hb = hf.astype(BF16)
        if kind == 0:
            hf, hb = _ssd_layer(hf, hb, ssd_w_in, ssd_conv_w[j], ssd_conv_b[j], ssd_dt_bias[j],
                                ssd_a_log[j], ssd_d[j], ssd_norm_w[j], ssd_w_out[j], g, b,
                                B=B, T=T, layer=j)
        elif kind == 1:
            hf, hb = _fox_layer(hf, hb, fox_w_in, fox_f_bias[j], fox_w_out[j], g, b,
                                B=B, T=T, layer=j)
        elif kind == 2:
            hf, hb = _mla_layer(hf, hb, positions, mla_w_in[j], mla_q_norm[j],
                                mla_kv_norm[j], mla_w_q_up[j], mla_w_kv_up[j], mla_w_out[j],
                                g, b, B=B, T=T)
        else:
            hf, hb = _s5_layer(hf, hb, s5_w_in, s5_lambda_re[j], s5_lambda_im[j], s5_log_step[j],
                               s5_b_re[j], s5_b_im[j], s5_c_re[j], s5_c_im[j], s5_d[j],
                               s5_w_glu[j], s5_b_glu[j], s5_w_out[j], g, b, B=B, T=T, layer=j)
    return hf.reshape(B, T, D)
```

```python
import functools
import math

import jax
import jax.numpy as jnp
from jax import lax
from jax.experimental import pallas as pl
from jax.experimental.pallas import tpu as pltpu

F32 = jnp.float32
BF16 = jnp.bfloat16
HIGHEST = lax.Precision.HIGHEST

D_MODEL = 2048
DEPTH = 4
ALPHA = (2.0 * DEPTH) ** 0.25
LN_EPS = 1e-5
RMS_EPS = 1e-6

SSD_D_INNER = 4096
SSD_HEADS = 64
SSD_HEADDIM = 64
SSD_GROUPS = 8
SSD_HEADS_PER_GROUP = SSD_HEADS // SSD_GROUPS
SSD_STATE = 128
SSD_CONV = 4
SSD_CHUNK = 128
SSD_GROUP_WIDTH = SSD_D_INNER // SSD_GROUPS
SSD_BC_WIDTH = SSD_GROUPS * SSD_STATE
SSD_GROUPS_PER_STEP = 8

FOX_HEADS = 16
FOX_HEAD_DIM = 128
FOX_WIDTH = FOX_HEADS * FOX_HEAD_DIM

MLA_HEADS = 16
MLA_Q_RANK = 512
MLA_KV_RANK = 512
MLA_NOPE = 128
MLA_ROPE = 64
MLA_V = 128
MLA_QK = MLA_NOPE + MLA_ROPE
MLA_WIDTH = MLA_HEADS * MLA_V
ROPE_BASE = 10000.0

S5_WIDTH = D_MODEL
S5_GROUP = 16
S5_GROUPS = S5_WIDTH // S5_GROUP
S5_STATE = 64
S5_CHUNK = 8
S5_TOEP_GROUPS = 16
S5_STATE_GROUPS = 8
S5_STEP_GROUP = 4

LANES = 128
VMEM_LIMIT_BYTES = 48 * 1024 * 1024


def _cparams(*sem):
    return pltpu.CompilerParams(dimension_semantics=sem, vmem_limit_bytes=VMEM_LIMIT_BYTES)


def _resident(block_shape, index_map):
    return pl.BlockSpec(block_shape, index_map, pipeline_mode=pl.Buffered(1))


def _sigmoid(x):
    return 0.5 * (jnp.tanh(0.5 * x) + 1.0)


def _silu(x):
    h = 0.5 * x
    return h + h * jnp.tanh(h)


def _softplus(x):
    return jnp.maximum(x, 0.0) + jnp.log1p(jnp.exp(-jnp.abs(x)))


def _deepnorm_ln(h, branch, g, b):
    r = ALPHA * h + branch
    mu = jnp.mean(r, axis=-1, keepdims=True)
    d = r - mu
    var = jnp.mean(d * d, axis=-1, keepdims=True)
    return d * lax.rsqrt(var + LN_EPS) * g + b


def _project(x, wb, wt):
    dims = (((1,), (1,)), ((), ())) if wt else (((1,), (0,)), ((), ()))
    return lax.dot_general(x, wb, dims, preferred_element_type=F32)


def _mm_kernel(x_ref, w_ref, o_ref, wb_ref, *, scale, wt):
    @pl.when(pl.program_id(1) == 0)
    def _():
        wb_ref[...] = w_ref[...].astype(BF16)

    acc = _project(x_ref[...].astype(BF16), wb_ref[...], wt)
    if scale is not None:
        acc = acc * scale
    o_ref[...] = acc.astype(o_ref.dtype)


def _weight_spec(w, layer, K, tn, j0, wt):
    block = (tn, K) if wt else (K, tn)
    index = (lambda j: (j0 + j, 0)) if wt else (lambda j: (0, j0 + j))
    if w.ndim == 2:
        return pl.BlockSpec(block, lambda j, i: index(j))
    return pl.BlockSpec((None,) + block, lambda j, i: (layer,) + index(j))


def _matmul(x, w, out_dtype, *, layer=0, col0=0, ncols=None, scale=None, tn=1024, wt=False):
    M, K = x.shape
    N = w.shape[-2 if wt else -1] - col0 if ncols is None else ncols
    tm = min(1024 if x.dtype == BF16 else 512, M)
    tn = min(tn, N)
    assert M % tm == 0 and N % tn == 0 and col0 % tn == 0, (M, N, col0, tm, tn)
    j0 = col0 // tn
    return pl.pallas_call(
        functools.partial(_mm_kernel, scale=scale, wt=wt),
        grid=(N // tn, M // tm),
        in_specs=[pl.BlockSpec((tm, K), lambda j, i: (i, 0)),
                  _weight_spec(w, layer, K, tn, j0, wt)],
        out_specs=pl.BlockSpec((tm, tn), lambda j, i: (i, j)),
        out_shape=jax.ShapeDtypeStruct((M, N), out_dtype),
        scratch_shapes=[pltpu.VMEM((tn, K) if wt else (K, tn), BF16)],
        compiler_params=_cparams("parallel", "arbitrary"),
        name="proj",
    )(x, w)


def _mm_cast_kernel(x_ref, w_ref, o_ref, xb_ref, *, wt):
    xb = x_ref[...].astype(BF16)
    xb_ref[...] = xb
    o_ref[...] = _project(xb, w_ref[...].astype(BF16), wt)


def _matmul_and_cast(x, w, *, wt):
    M, K = x.shape
    N = w.shape[0 if wt else 1]
    tm = min(512, M)
    assert M % tm == 0
    return pl.pallas_call(
        functools.partial(_mm_cast_kernel, wt=wt),
        grid=(M // tm,),
        in_specs=[pl.BlockSpec((tm, K), lambda i: (i, 0)),
                  pl.BlockSpec(w.shape, lambda i: (0, 0))],
        out_specs=[pl.BlockSpec((tm, N), lambda i: (i, 0)),
                   pl.BlockSpec((tm, K), lambda i: (i, 0))],
        out_shape=[jax.ShapeDtypeStruct((M, N), F32), jax.ShapeDtypeStruct((M, K), BF16)],
        compiler_params=_cparams("parallel"),
        name="proj_and_cast",
    )(x, w)


CONV_COL_PARTS = 4


def _mm_conv_silu_kernel(x_ref, w_ref, cw_ref, cb_ref, o_ref, wb_ref, ext_ref,
                         *, taps, tiles_per_seq, wt):
    i = pl.program_id(1)
    tm, tn = o_ref.shape

    @pl.when(i == 0)
    def _():
        wb_ref[...] = w_ref[...].astype(BF16)

    @pl.when(i % tiles_per_seq == 0)
    def _():
        ext_ref[pl.ds(0, 8), :] = jnp.zeros((8, tn), F32)

    xb = x_ref[...].astype(BF16)
    slab = tn // CONV_COL_PARTS
    parts = [pl.ds(p * slab, slab) for p in range(CONV_COL_PARTS)]
    raws = [_project(xb, wb_ref[cols, :] if wt else wb_ref[:, cols], wt) for cols in parts]
    for cols, raw in zip(parts, raws):
        ext_ref[pl.ds(8, tm), cols] = raw
        acc = cb_ref[:, cols] + cw_ref[pl.ds(0, 1), cols] * ext_ref[pl.ds(9 - taps, tm), cols]
        for kk in range(1, taps):
            acc = acc + cw_ref[pl.ds(kk, 1), cols] * ext_ref[pl.ds(9 - taps + kk, tm), cols]
        ext_ref[pl.ds(0, 8), cols] = ext_ref[pl.ds(tm, 8), cols]
        o_ref[:, cols] = _silu(acc)


def _matmul_conv_silu(x, w, conv_w, conv_b, *, layer, col0, ncols, seq_len, tn=1024, wt=False):
    M, K = x.shape
    taps = conv_w.shape[0]
    tm = min(1024 if x.dtype == BF16 else 512, seq_len)
    assert M % tm == 0 and seq_len % tm == 0 and ncols % tn == 0 and col0 % tn == 0
    j0 = col0 // tn
    return pl.pallas_call(
        functools.partial(_mm_conv_silu_kernel, taps=taps, tiles_per_seq=seq_len // tm, wt=wt),
        grid=(ncols // tn, M // tm),
        in_specs=[pl.BlockSpec((tm, K), lambda j, i: (i, 0)),
                  _weight_spec(w, layer, K, tn, j0, wt),
                  pl.BlockSpec((taps, tn), lambda j, i: (0, j)),
                  pl.BlockSpec((1, tn), lambda j, i: (0, j))],
        out_specs=pl.BlockSpec((tm, tn), lambda j, i: (i, j)),
        out_shape=jax.ShapeDtypeStruct((M, ncols), F32),
        scratch_shapes=[pltpu.VMEM((tn, K) if wt else (K, tn), BF16),
                        pltpu.VMEM((tm + 8, tn), F32)],
        compiler_params=_cparams("parallel", "arbitrary"),
        name="proj_conv_silu",
    )(x, w, conv_w, conv_b.reshape(1, -1))


def _out_ln_kernel(y_ref, w_ref, h_ref, g_ref, b_ref, of_ref, ob_ref):
    half = y_ref.shape[0] // 2
    parts = [pl.ds(0, half), pl.ds(half, half)]
    branches = [jnp.dot(y_ref[rows, :], w_ref[...], preferred_element_type=F32) for rows in parts]
    for rows, branch in zip(parts, branches):
        out = _deepnorm_ln(h_ref[rows, :], branch, g_ref[...], b_ref[...])
        of_ref[rows, :] = out
        ob_ref[rows, :] = out.astype(BF16)


def _out_proj_ln(y, w, h, g, b):
    M, K = y.shape
    D = w.shape[1]
    tm = min(512 if K * D * w.dtype.itemsize <= VMEM_LIMIT_BYTES // 4 else 256, M)
    assert M % tm == 0
    return pl.pallas_call(
        _out_ln_kernel,
        grid=(M // tm,),
        in_specs=[pl.BlockSpec((tm, K), lambda i: (i, 0)),
                  _resident((K, D), lambda i: (0, 0)),
                  pl.BlockSpec((tm, D), lambda i: (i, 0)),
                  pl.BlockSpec((1, D), lambda i: (0, 0)),
                  pl.BlockSpec((1, D), lambda i: (0, 0))],
        out_specs=[pl.BlockSpec((tm, D), lambda i: (i, 0)),
                   pl.BlockSpec((tm, D), lambda i: (i, 0))],
        out_shape=[jax.ShapeDtypeStruct((M, D), F32), jax.ShapeDtypeStruct((M, D), BF16)],
        compiler_params=_cparams("parallel"),
        name="out_proj_ln",
    )(y, w, h, g.reshape(1, D), b.reshape(1, D))


ATTN_HEADS_PER_STEP = 4
ATTN_WIDE_BLOCKS = 2


def _attn_kernel(*refs, nparts, tq, tk, dv, head_major):
    q_refs = refs[:nparts]
    k_refs = refs[nparts:2 * nparts]
    v_ref, z_ref, o_ref, m_ref, acc_ref = refs[2 * nparts:]
    nh = ATTN_HEADS_PER_STEP
    i = pl.program_id(2)

    def head(ref, hh, rows=slice(None)):
        if head_major:
            return ref[hh, rows, :]
        w = ref.shape[-1] // nh
        return ref[rows, hh * w:(hh + 1) * w]

    def cat(pieces):
        return pieces[0] if len(pieces) == 1 else jnp.concatenate(pieces, axis=1)

    qs = [cat([head(r, hh) for r in q_refs]) for hh in range(nh)]
    m_ref[...] = jnp.full_like(m_ref, -jnp.inf)
    acc_ref[...] = jnp.zeros_like(acc_ref)

    def block(first, width, diag_offset, q_rows=slice(0, tq)):
        nq = q_rows.stop - q_rows.start
        rows = pl.ds(pl.multiple_of(first * tk, tk), width)
        ones_cols = jnp.ones((width, LANES), BF16)
        scores = []
        for hh in range(nh):
            k = cat([head(r, hh, rows) for r in k_refs])
            scores.append(lax.dot_general(qs[hh][q_rows], k, (((1,), (1,)), ((), ())),
                                          preferred_element_type=F32))
        probs, alphas = [], []
        for hh in range(nh):
            s = scores[hh]
            if diag_offset is not None:
                r = lax.broadcasted_iota(jnp.int32, (nq, width), 0) + q_rows.start
                c = lax.broadcasted_iota(jnp.int32, (nq, width), 1)
                s = jnp.where(c + diag_offset <= r, s, -jnp.inf)
            m_prev = m_ref[hh, q_rows]
            m_new = jnp.maximum(m_prev, jnp.max(s, axis=-1, keepdims=True))
            alpha = jnp.exp(m_prev - m_new)
            p = jnp.exp(s - jnp.concatenate([m_new] * (width // LANES), axis=1))
            m_ref[hh, q_rows] = m_new
            probs.append(p.astype(BF16))
            alphas.append(jnp.concatenate([alpha] * ((dv + LANES) // LANES), axis=1))
        for hh in range(nh):
            v_aug = jnp.concatenate([head(v_ref, hh, rows), ones_cols], axis=1)
            acc_ref[hh, q_rows] = alphas[hh] * acc_ref[hh, q_rows] + jnp.dot(
                probs[hh], v_aug, preferred_element_type=F32)

    n_full = i * (tq // tk)
    wide = ATTN_WIDE_BLOCKS

    def wide_block(j, carry):
        block(j * wide, wide * tk, None)
        return carry

    lax.fori_loop(0, n_full // wide, wide_block, 0)
    if tq == tk and wide == 2:
        half = tq // 2

        @pl.when(n_full % 2 == 1)
        def _():
            block(n_full - 1, 2 * tk - half, -tk, slice(0, half))
            block(n_full - 1, 2 * tk, -tk, slice(half, tq))

        @pl.when(n_full % 2 == 0)
        def _():
            block(n_full, tk - half, 0, slice(0, half))
            block(n_full, tk, 0, slice(half, tq))
    else:
        for rem in range(wide - 1):
            @pl.when(n_full % wide > rem)
            def _(rem=rem):
                block((n_full // wide) * wide + rem, tk, None)
        for d in range(tq // tk):
            block(n_full + d, tk, d * tk)

    for hh in range(nh):
        cols = slice(hh * dv, (hh + 1) * dv)
        o = acc_ref[hh, :, :dv] / acc_ref[hh, :, dv:]
        o_ref[:, cols] = (o * _silu(z_ref[:, cols])).astype(o_ref.dtype)


def _attention(inputs, in_specs, out_spec, *, B, T, H, dv, tq, tk, nparts, head_major):
    nh = ATTN_HEADS_PER_STEP
    assert tq % tk == 0 and T % tq == 0 and H % nh == 0
    return pl.pallas_call(
        functools.partial(_attn_kernel, nparts=nparts, tq=tq, tk=tk, dv=dv,
                          head_major=head_major),
        grid=(B, H // nh, T // tq),
        in_specs=in_specs,
        out_specs=out_spec,
        out_shape=jax.ShapeDtypeStruct((B * T, H * dv), BF16),
        scratch_shapes=[pltpu.VMEM((nh, tq, LANES), F32),
                        pltpu.VMEM((nh, tq, dv + LANES), F32)],
        compiler_params=_cparams("parallel", "parallel", "arbitrary"),
        name="causal_attention",
    )(*inputs)


def _attn_tiles(T):
    tq = min(512, T)
    tk = min(512, T)
    return tq, tk


def _ssd_kernel(x_ref, bm_ref, cm_ref, z_ref, dtr_ref,
                dtb_ref, al_ref, dsk_ref, nw_ref, spread_ref,
                o_ref,
                state_ref, yz_ref):
    Q, E, P, GW, N = SSD_CHUNK, SSD_HEADS_PER_GROUP, SSD_HEADDIM, SSD_GROUP_WIDTH, SSD_STATE
    NG = SSD_GROUPS_PER_STEP
    groups = range(NG)
    c = pl.program_id(2)

    @pl.when(c == 0)
    def _():
        state_ref[...] = jnp.zeros_like(state_ref)

    row = lax.broadcasted_iota(jnp.int32, (Q, Q), 0)
    col = lax.broadcasted_iota(jnp.int32, (Q, Q), 1)
    lower = row >= col
    tri_u = (row <= col).astype(BF16)
    left = col < P

    def pieces(v):
        hi = v.astype(BF16).astype(F32)
        mid = (v - hi).astype(BF16).astype(F32)
        lo = ((v - hi) - mid).astype(BF16).astype(F32)
        return jnp.concatenate([hi, mid, lo], axis=0).astype(BF16)

    npairs = E // 2
    dt_r, acs_r = [], []
    for gg in groups:
        dt = _softplus(dtr_ref[gg] + dtb_ref[gg])
        cum3 = jnp.dot(pieces(dt * (-jnp.exp(al_ref[gg]))), tri_u, preferred_element_type=F32)
        dt_r.append(dt)
        acs_r.append(cum3[:E] + cum3[E:2 * E] + cum3[2 * E:])

    bm_b, cm_b, bmt_b, cb, carried = [], [], [], [], []
    for gg in groups:
        ncols = pl.ds(gg * N, N)
        bm = bm_ref[:, ncols]
        bm_b.append(bm.astype(BF16))
        cm_b.append(cm_ref[:, ncols].astype(BF16))
        bmt_b.append(bm.T.astype(BF16))
        cb.append(lax.dot_general(cm_b[gg], bm_b[gg], (((1,), (1,)), ((), ())),
                                  preferred_element_type=F32))
        carried.append([jnp.dot(cm_b[gg], state_ref[gg * npairs + kp].astype(BF16),
                                preferred_element_type=F32)
                        for kp in range(npairs)])

    pair_cols = lambda gg, kp: pl.ds(gg * GW + kp * LANES, LANES)
    xps = [[x_ref[:, pair_cols(gg, kp)] for kp in range(npairs)] for gg in groups]
    gates = [[_silu(z_ref[:, pair_cols(gg, kp)]) for kp in range(npairs)] for gg in groups]

    spreads = [lax.dot_general(pieces(jnp.concatenate([dt_r[gg], acs_r[gg]], axis=0)),
                               spread_ref[...], (((0,), (0,)), ((), ())),
                               preferred_element_type=F32) for gg in groups]

    for gg in groups:
        dt_ch = spreads[gg][:, :GW]
        acs_ch = spreads[gg][:, GW:2 * GW]
        acs_col = spreads[gg][:, 2 * GW:]
        ssq = jnp.zeros((Q, 1), F32)
        for kp in range(npairs):
            cols = pair_cols(gg, kp)
            xp = xps[gg][kp]
            dt_p = dt_ch[:, kp * LANES:(kp + 1) * LANES]
            acs_p = acs_ch[:, kp * LANES:(kp + 1) * LANES]
            last_p = acs_p[Q - 1:Q, :]
            xdt = xp * dt_p
            xdt_b = xdt.astype(BF16)

            ys = []
            for e in (2 * kp, 2 * kp + 1):
                seg = acs_col[:, e * Q:(e + 1) * Q] - acs_r[gg][e:e + 1, :]
                dec = jnp.exp(jnp.where(lower, seg, -jnp.inf))
                ys.append(jnp.dot((cb[gg] * dec).astype(BF16), xdt_b,
                                  preferred_element_type=F32))
            y_diag = jnp.where(left, ys[0], ys[1])

            xw = (xdt * jnp.exp(last_p - acs_p)).astype(BF16)
            s_loc = jnp.dot(bmt_b[gg], xw, preferred_element_type=F32)
            y_off = carried[gg][kp] * jnp.exp(acs_p)
            slot = gg * npairs + kp
            state_ref[slot] = state_ref[slot] * jnp.exp(last_p) + s_loc

            y = (y_diag + y_off + dsk_ref[:, cols] * xp) * gates[gg][kp]
            yz_ref[:, cols] = y
            ssq = ssq + jnp.sum(y * y, axis=-1, keepdims=True)

        gcols = pl.ds(gg * GW, GW)
        inv = lax.rsqrt(ssq * (1.0 / GW) + RMS_EPS)
        o_ref[:, gcols] = (yz_ref[:, gcols] * inv * nw_ref[:, gcols]).astype(o_ref.dtype)


def _ssd_core(z, xbc, dt, dt_bias, a_log, d_skip, norm_w, *, B, T):
    G, E, Q = SSD_GROUPS, SSD_HEADS_PER_GROUP, SSD_CHUNK
    NC = T // Q
    GW, N = SSD_GROUP_WIDTH, SSD_STATE
    dtr = jnp.transpose(dt[:, :SSD_HEADS].reshape(B, T, G, E), (0, 2, 3, 1))
    head_rows = lambda v: v.reshape(G, E, 1)
    d_ch = jnp.repeat(d_skip, SSD_HEADDIM).reshape(1, -1)
    heads = jnp.arange(E)
    per_channel = (jnp.arange(GW)[None, :] // SSD_HEADDIM == heads[:, None]).astype(BF16)
    per_head_block = (jnp.arange(E * Q)[None, :] // Q == heads[:, None]).astype(BF16)
    none = lambda n: jnp.zeros((E, n), BF16)
    spread = jnp.concatenate([
        jnp.concatenate([per_channel, none(GW), none(E * Q)], axis=1),
        jnp.concatenate([none(GW), per_channel, per_head_block], axis=1)], axis=0)
    spread = jnp.tile(spread, (3, 1))

    NG = SSD_GROUPS_PER_STEP
    XW, BW = NG * GW, NG * N
    assert G % NG == 0
    bblk = SSD_D_INNER // BW
    cblk = (SSD_D_INNER + SSD_BC_WIDTH) // BW
    rowblk = lambda b, g, c: b * NC + c
    in_specs = [
        pl.BlockSpec((Q, XW), lambda b, g, c: (rowblk(b, g, c), g)),
        pl.BlockSpec((Q, BW), lambda b, g, c: (rowblk(b, g, c), bblk + g)),
        pl.BlockSpec((Q, BW), lambda b, g, c: (rowblk(b, g, c), cblk + g)),
        pl.BlockSpec((Q, XW), lambda b, g, c: (rowblk(b, g, c), g)),
        pl.BlockSpec((None, NG, E, Q), lambda b, g, c: (b, g, 0, c)),
        pl.BlockSpec((NG, E, 1), lambda b, g, c: (g, 0, 0)),
        pl.BlockSpec((NG, E, 1), lambda b, g, c: (g, 0, 0)),
        pl.BlockSpec((1, XW), lambda b, g, c: (0, g)),
        pl.BlockSpec((1, XW), lambda b, g, c: (0, g)),
        pl.BlockSpec(spread.shape, lambda b, g, c: (0, 0)),
    ]
    return pl.pallas_call(
        _ssd_kernel,
        grid=(B, G // NG, NC),
        in_specs=in_specs,
        out_specs=pl.BlockSpec((Q, XW), lambda b, g, c: (rowblk(b, g, c), g)),
        out_shape=jax.ShapeDtypeStruct((B * T, SSD_D_INNER), BF16),
        scratch_shapes=[pltpu.VMEM((NG * E // 2, N, LANES), F32), pltpu.VMEM((Q, XW), F32)],
        compiler_params=_cparams("parallel", "parallel", "arbitrary"),
        name="ssd_chunk_scan",
    )(xbc, xbc, xbc, z, dtr,
      head_rows(dt_bias), head_rows(a_log), d_ch, norm_w.reshape(1, -1), spread)


def _pad_rows(w, n):
    return jnp.pad(w, ((0, n - w.shape[0]), (0, 0)))


def _layer_of(w, layer):
    return w[layer] if w.ndim == 3 else w


def _ssd_layer(hf, hb, w_in, conv_w, conv_b, dt_bias, a_log, d_skip, norm_w, w_out, g, b,
               *, B, T, layer=0):
    di = SSD_D_INNER
    cd = di + 2 * SSD_BC_WIDTH
    w_t = jnp.swapaxes(w_in, -1, -2)
    w_dt = _pad_rows(_layer_of(w_t, layer)[di + cd:], LANES)
    if hb is None:
        dt, hb = _matmul_and_cast(hf, w_dt, wt=True)
    else:
        dt = _matmul(hb, w_dt, F32, wt=True)
    z = _matmul(hb, w_t, F32, layer=layer, col0=0, ncols=di, wt=True)
    xbc = _matmul_conv_silu(hb, w_t, conv_w, conv_b, layer=layer, col0=di, ncols=cd, seq_len=T,
                            wt=True)
    y = _ssd_core(z, xbc, dt, dt_bias, a_log, d_skip, norm_w, B=B, T=T)
    return _out_proj_ln(y, w_out.astype(BF16), hf, g, b)


FOX_BIAS_PIECES = 3


def _fox_bias_kernel(f_ref, b_ref, qx_ref, kx_ref, carry_ref, *, nblk):
    blk, W, P = LANES, FOX_WIDTH, FOX_BIAS_PIECES
    row = lax.broadcasted_iota(jnp.int32, (blk, blk), 0)
    col = lax.broadcasted_iota(jnp.int32, (blk, blk), 1)
    tri = (row >= col).astype(F32)
    bias = b_ref[...]
    H = FOX_HEADS
    r = lax.broadcasted_iota(jnp.int32, (blk, W), 0)
    c = lax.broadcasted_iota(jnp.int32, (blk, W), 1)
    head, piece = r % H, r // H
    is_piece = r < P * H
    place_q = jnp.logical_and(is_piece, c == head * blk + piece).astype(BF16)
    place_k = jnp.logical_and(is_piece, c == head * blk + P + piece).astype(BF16)
    is_head = col < H
    cmod = lax.broadcasted_iota(jnp.int32, (1, W), 1) % blk
    ones_q = jnp.logical_and(cmod >= P, cmod < 2 * P).astype(F32)
    ones_k = (cmod < P).astype(F32)

    def body(t, carry):
        rows = pl.ds(t * blk, blk)
        x = f_ref[rows, :] + bias
        logf = jnp.minimum(x, 0.0) - jnp.log1p(jnp.exp(-jnp.abs(x)))
        cum = jnp.dot(tri, logf, precision=HIGHEST, preferred_element_type=F32) + carry
        hi = jnp.where(is_head, cum, 0.0).astype(BF16).astype(F32)
        r1 = jnp.where(is_head, cum, 0.0) - hi
        mid = r1.astype(BF16).astype(F32)
        lo = (r1 - mid).astype(BF16).astype(F32)
        pieces = (hi + pltpu.roll(mid, H, 1) + pltpu.roll(lo, 2 * H, 1)).astype(BF16)
        qx_ref[rows, :] = (jnp.dot(pieces, place_q, preferred_element_type=F32)
                           + ones_q).astype(BF16)
        kx_ref[rows, :] = (ones_k - jnp.dot(pieces, place_k, preferred_element_type=F32)
                           ).astype(BF16)
        return cum[blk - 1:blk, :]

    @pl.when(pl.program_id(1) == 0)
    def _():
        carry_ref[...] = jnp.zeros_like(carry_ref)

    carry = carry_ref[...]
    for t in range(nblk):
        carry = body(t, carry)
    carry_ref[...] = carry


def _fox_bias_columns(f, f_bias, *, B, T):
    bias = jnp.pad(f_bias, (0, LANES - FOX_HEADS)).reshape(1, LANES)
    tt = min(512, T)
    out = jax.ShapeDtypeStruct((B, T, FOX_WIDTH), BF16)
    spec = pl.BlockSpec((None, tt, FOX_WIDTH), lambda b, t: (b, t, 0))
    qx, kx = pl.pallas_call(
        functools.partial(_fox_bias_kernel, nblk=tt // LANES),
        grid=(B, T // tt),
        in_specs=[pl.BlockSpec((None, tt, LANES), lambda b, t: (b, t, 0)),
                  pl.BlockSpec((1, LANES), lambda b, t: (0, 0))],
        out_specs=[spec, spec],
        out_shape=[out, out],
        scratch_shapes=[pltpu.VMEM((1, LANES), F32)],
        compiler_params=_cparams("parallel", "arbitrary"),
        name="fox_cum_log_forget",
    )(f.reshape(B, T, LANES), bias)
    return qx.reshape(B * T, FOX_WIDTH), kx.reshape(B * T, FOX_WIDTH)


def _fox_layer(hf, hb, w_in, f_bias, w_out, g, b, *, B, T, layer=0):
    W, H, dh = FOX_WIDTH, FOX_HEADS, FOX_HEAD_DIM
    w_t = jnp.swapaxes(w_in, -1, -2)
    q = _matmul(hb, w_t, BF16, layer=layer, col0=0, ncols=W, scale=dh ** -0.5, wt=True)
    kv = _matmul(hb, w_t, BF16, layer=layer, col0=W, ncols=2 * W, wt=True)
    z = _matmul(hb, w_t, F32, layer=layer, col0=3 * W, ncols=W, wt=True)
    f = _matmul(hb, _pad_rows(_layer_of(w_t, layer)[4 * W:], LANES), F32, wt=True)
    qx, kx = _fox_bias_columns(f, f_bias, B=B, T=T)

    tq, tk = _attn_tiles(T)
    nq = T // tq
    hw = ATTN_HEADS_PER_STEP * dh
    q_tile = pl.BlockSpec((tq, hw), lambda bb, h, i: (bb * nq + i, h))
    seq_k = pl.BlockSpec((T, hw), lambda bb, h, i: (bb, h))
    seq_v = pl.BlockSpec((T, hw), lambda bb, h, i: (bb, W // hw + h))
    in_specs = [q_tile, q_tile, seq_k, seq_k, seq_v, q_tile]
    y = _attention((q, qx, kv, kx, kv, z), in_specs, q_tile,
                   B=B, T=T, H=H, dv=dh, tq=tq, tk=tk, nparts=2, head_major=False)
    return _out_proj_ln(y, w_out.astype(BF16), hf, g, b)


def _rope_table_kernel(pos_ref, freq_ref, sign_ref, cos_ref, sin_ref):
    ang = pos_ref[...].astype(F32) * freq_ref[...]
    cos_ref[...] = jnp.cos(ang)
    sin_ref[...] = jnp.sin(ang) * sign_ref[...]


def _rope_tables(positions, *, B, T):
    half = MLA_ROPE // 2
    inv_freq = ROPE_BASE ** (-jnp.arange(0, MLA_ROPE, 2, dtype=F32) / MLA_ROPE)
    freq = jnp.tile(inv_freq, LANES // half).reshape(1, LANES)
    sign = jnp.tile(jnp.concatenate([-jnp.ones((half,), F32), jnp.ones((half,), F32)]),
                    LANES // MLA_ROPE).reshape(1, LANES)
    M = B * T
    tm = min(1024, M)
    return pl.pallas_call(
        _rope_table_kernel,
        grid=(M // tm,),
        in_specs=[pl.BlockSpec((tm, 1), lambda i: (i, 0)),
                  pl.BlockSpec((1, LANES), lambda i: (0, 0)),
                  pl.BlockSpec((1, LANES), lambda i: (0, 0))],
        out_specs=[pl.BlockSpec((tm, LANES), lambda i: (i, 0)),
                   pl.BlockSpec((tm, LANES), lambda i: (i, 0))],
        out_shape=[jax.ShapeDtypeStruct((M, LANES), F32), jax.ShapeDtypeStruct((M, LANES), F32)],
        compiler_params=_cparams("parallel"),
        name="rope_tables",
    )(positions.reshape(M, 1), freq, sign)


def _rms_to_bf16(x, w):
    y = x * lax.rsqrt(jnp.mean(x * x, axis=-1, keepdims=True) + RMS_EPS)
    return (y * w).astype(BF16)


def _rope_pair(c2, cos, sin):
    return c2 * cos + pltpu.roll(c2, MLA_ROPE, 1) * sin


def _mla_qkv_kernel(x_ref, wl_ref, qn_ref, kn_ref, wq_ref, wkv_ref, cos_ref, sin_ref,
                    q_ref, k_ref, v_ref, *, scale):
    qr, kr = MLA_Q_RANK, MLA_KV_RANK
    cos, sin = cos_ref[...], sin_ref[...]
    lat = _project(x_ref[...], wl_ref[...], True)
    qn = _rms_to_bf16(lat[:, :qr], qn_ref[...])
    kn = _rms_to_bf16(lat[:, qr:qr + kr], kn_ref[...])
    k_pe = _rope_pair(lat[:, qr + kr:], cos, sin)[:, :MLA_ROPE].astype(k_ref.dtype)
    for h in range(MLA_HEADS):
        r = jnp.dot(qn, wq_ref[h], preferred_element_type=F32)
        pe = _rope_pair(r[:, MLA_NOPE:], cos, sin)
        q_ref[h, :, :MLA_NOPE] = (r[:, :MLA_NOPE] * scale).astype(q_ref.dtype)
        q_ref[h, :, MLA_NOPE:] = (pe[:, :MLA_ROPE] * scale).astype(q_ref.dtype)
        r = jnp.dot(kn, wkv_ref[h], preferred_element_type=F32)
        k_ref[h, :, :MLA_NOPE] = r[:, :MLA_NOPE].astype(k_ref.dtype)
        k_ref[h, :, MLA_NOPE:] = k_pe
        v_ref[h] = r[:, MLA_NOPE:].astype(v_ref.dtype)


def _swap_halves(w):
    half = w.shape[-1] // 2
    return jnp.concatenate([w[..., half:], w[..., :half]], axis=-1)


def _mla_layer(hf, hb, positions, w_in, q_norm, kv_norm, w_q_up, w_kv_up, w_out, g, b, *, B, T):
    H, M = MLA_HEADS, B * T
    qr, kr = MLA_Q_RANK, MLA_KV_RANK
    lat_end = qr + kr + MLA_ROPE
    w_t = jnp.swapaxes(w_in, -1, -2)
    w_pe = w_t[qr + kr:lat_end]
    half = MLA_ROPE // 2
    w_lat = jnp.concatenate([w_t[:qr + kr], w_pe, w_pe[half:], w_pe[:half]], axis=0).astype(BF16)
    nlat = w_lat.shape[0]
    z = _matmul(hb, w_t[lat_end:], F32, wt=True)
    cos, sin = _rope_tables(positions, B=B, T=T)

    wq = w_q_up.reshape(qr, H, MLA_QK)
    wq = jnp.concatenate([wq, _swap_halves(wq[..., MLA_NOPE:])], axis=-1)
    wq = jnp.transpose(wq, (1, 0, 2)).astype(BF16)
    wkv = jnp.transpose(w_kv_up.reshape(kr, H, MLA_NOPE + MLA_V), (1, 0, 2)).astype(BF16)

    tm = min(256, M)
    wide = MLA_NOPE + LANES
    row = lambda i: (i, 0)
    q, k, v = pl.pallas_call(
        functools.partial(_mla_qkv_kernel, scale=MLA_QK ** -0.5),
        grid=(M // tm,),
        in_specs=[pl.BlockSpec((tm, D_MODEL), row),
                  _resident((nlat, D_MODEL), lambda i: (0, 0)),
                  pl.BlockSpec((1, qr), lambda i: (0, 0)),
                  pl.BlockSpec((1, kr), lambda i: (0, 0)),
                  _resident((H, qr, wide), lambda i: (0, 0, 0)),
                  _resident((H, kr, MLA_NOPE + MLA_V), lambda i: (0, 0, 0)),
                  pl.BlockSpec((tm, LANES), row),
                  pl.BlockSpec((tm, LANES), row)],
        out_specs=[pl.BlockSpec((H, tm, MLA_QK), lambda i: (0, i, 0)),
                   pl.BlockSpec((H, tm, MLA_QK), lambda i: (0, i, 0)),
                   pl.BlockSpec((H, tm, MLA_V), lambda i: (0, i, 0))],
        out_shape=[jax.ShapeDtypeStruct((H, M, MLA_QK), BF16),
                   jax.ShapeDtypeStruct((H, M, MLA_QK), BF16),
                   jax.ShapeDtypeStruct((H, M, MLA_V), BF16)],
        compiler_params=_cparams("parallel"),
        name="mla_qkv_rope",
    )(hb, w_lat, q_norm.reshape(1, qr), kv_norm.reshape(1, kr), wq, wkv, cos, sin)

    tq, tk = _attn_tiles(T)
    nq = T // tq
    nh = ATTN_HEADS_PER_STEP
    in_specs = [
        pl.BlockSpec((nh, tq, MLA_QK), lambda bb, h, i: (h, bb * nq + i, 0)),
        pl.BlockSpec((nh, T, MLA_QK), lambda bb, h, i: (h, bb, 0)),
        pl.BlockSpec((nh, T, MLA_V), lambda bb, h, i: (h, bb, 0)),
        pl.BlockSpec((tq, nh * MLA_V), lambda bb, h, i: (bb * nq + i, h)),
    ]
    out_spec = pl.BlockSpec((tq, nh * MLA_V), lambda bb, h, i: (bb * nq + i, h))
    y = _attention((q, k, v, z), in_specs, out_spec,
                   B=B, T=T, H=H, dv=MLA_V, tq=tq, tk=tk, nparts=1, head_major=True)
    return _out_proj_ln(y, w_out.astype(BF16), hf, g, b)


def _s5_matrices(lam_re, lam_im, log_step, b_re, b_im, c_re, c_im):
    L = S5_CHUNK
    step = jnp.exp(log_step.astype(F32))[:, None]
    mag = jnp.exp(lam_re * step)
    ar = mag * jnp.cos(lam_im * step)
    ai = mag * jnp.sin(lam_im * step)
    den = lam_re * lam_re + lam_im * lam_im
    fr = ((ar - 1.0) * lam_re + ai * lam_im) / den
    fi = (ai * lam_re - (ar - 1.0) * lam_im) / den
    bbr = fr[..., None] * b_re - fi[..., None] * b_im
    bbi = fr[..., None] * b_im + fi[..., None] * b_re
    pr, pi = [jnp.ones_like(ar)], [jnp.zeros_like(ar)]
    for _ in range(L):
        pr_next = pr[-1] * ar - pi[-1] * ai
        pi_next = pr[-1] * ai + pi[-1] * ar
        pr.append(pr_next)
        pi.append(pi_next)
    pwr = jnp.stack(pr, axis=1)
    pwi = jnp.stack(pi, axis=1)

    G, P, I = S5_GROUPS, S5_STATE, S5_GROUP
    TG, SG = S5_TOEP_GROUPS, S5_STATE_GROUPS
    NTB, NSB = G // TG, G // SG

    c_ji = lambda c: jnp.tile(jnp.transpose(c, (0, 2, 1)), (1, 1, I))
    b_ji = lambda v: jnp.repeat(v, I, axis=2)
    cbr = c_ji(c_re) * b_ji(bbr) - c_ji(c_im) * b_ji(bbi)
    cbi = c_ji(c_re) * b_ji(bbi) + c_ji(c_im) * b_ji(bbr)
    kern = jnp.einsum('gdp,gpq->gdq', jnp.concatenate([pwr[:, :L], -pwi[:, :L]], axis=2),
                      jnp.concatenate([cbr, cbi], axis=1), precision=HIGHEST)
    idx = jnp.arange

    kc = jnp.transpose(kern.reshape(NTB, TG, L, I, I), (2, 0, 1, 3, 4)).reshape(L, NTB, TG * I, I)
    kc = jnp.pad(kc, ((L - 1, 0), (0, 0), (0, 0), (0, 0)))
    repeat_i = (idx(TG * I)[None, :] % I == idx(I)[:, None]).astype(F32)
    same_t = (idx(TG * I)[:, None] // I == idx(TG * I)[None, :] // I)
    kd = jnp.where(same_t, jnp.einsum('dcri,iq->dcrq', kc, repeat_i, precision=HIGHEST), 0.0)

    by_step = lambda w: jnp.transpose(
        w[:, :L][:, ::-1].reshape(NSB, SG, L, P), (2, 0, 1, 3))[:, :, :, None]
    by_chan = lambda v: jnp.transpose(v, (0, 2, 1)).reshape(1, NSB, SG, I, P)
    qr, qi = by_step(pwr), by_step(pwi)
    br, bi = by_chan(bbr), by_chan(bbi)
    twice = lambda m: jnp.concatenate([m, m], axis=-1).reshape(L, NSB, SG * I, 2 * P)
    bp_re, bp_im = twice(qr * br - qi * bi), twice(qr * bi + qi * br)

    chan = lambda c: jnp.transpose(c.reshape(NSB, SG, I, P), (0, 3, 1, 2)).reshape(1, NSB, P, SG * I)
    step_pow = lambda w: jnp.repeat(
        jnp.transpose(w[:, 1:].reshape(NSB, SG, L, P), (2, 0, 3, 1)), I, axis=3)
    cr, ci, wr, wi = chan(c_re), chan(c_im), step_pow(pwr), step_pow(pwi)
    cp = jnp.concatenate([cr * wr - ci * wi, -(cr * wi + ci * wr)], axis=2)

    alr = pwr[:, L].reshape(NSB, SG * P)
    ali = pwi[:, L].reshape(NSB, SG * P)
    a1 = jnp.concatenate([alr, alr], axis=1).reshape(1, -1)
    a2 = jnp.concatenate([-ali, ali], axis=1).reshape(1, -1)
    return kd.astype(BF16), bp_re.astype(BF16), bp_im.astype(BF16), cp.astype(BF16), a1, a2


def _chunk_step(u_ref, s):
    return u_ref[pl.ds(s, u_ref.shape[0] // S5_CHUNK, stride=S5_CHUNK), :]


def _s5_local_state_kernel(u_ref, bpr_ref, bpi_ref, s_ref):
    SG, P, I = S5_STATE_GROUPS, S5_STATE, S5_GROUP
    own = (lax.broadcasted_iota(jnp.int32, (SG * I, SG * P), 0) // I
           == lax.broadcasted_iota(jnp.int32, (SG * I, SG * P), 1) // P)

    def spread(m):
        tiled = jnp.concatenate([m] * (SG // 2), axis=1)
        return jnp.where(own, tiled, jnp.zeros_like(tiled))

    acc = None
    for s in range(S5_CHUNK):
        bd = jnp.concatenate([spread(bpr_ref[s]), spread(bpi_ref[s])], axis=1)
        d = jnp.dot(_chunk_step(u_ref, s).astype(BF16), bd, preferred_element_type=F32)
        acc = d if acc is None else acc + d
    s_ref[...] = acc


def _s5_scan_kernel(s_ref, a1_ref, a2_ref, o_ref, *, B, nch):
    W = s_ref.shape[1]
    a1 = a1_ref[...]
    a2 = a2_ref[...]
    unit = 2 * S5_STATE_GROUPS * S5_STATE

    def swap_re_im(c):
        parts = []
        for q in range(W // unit):
            lo = q * unit
            parts += [c[:, lo + unit // 2:lo + unit], c[:, lo:lo + unit // 2]]
        return jnp.concatenate(parts, axis=1)

    def body(n, carry):
        new = []
        for bb in range(B):
            row = bb * nch + n
            c = carry[bb]
            o_ref[pl.ds(row, 1), :] = c
            new.append(a1 * c + a2 * swap_re_im(c) + s_ref[pl.ds(row, 1), :])
        return tuple(new)

    lax.fori_loop(0, nch, body, tuple(jnp.zeros((1, W), F32) for _ in range(B)), unroll=4)


def _gelu_tanh(y):
    return 0.5 * y * (1.0 + jnp.tanh(math.sqrt(2.0 / math.pi) * (y + 0.044715 * (y * y * y))))


def _s5_out_kernel(*refs):
    L = S5_CHUNK
    nu = S5_TOEP_GROUPS // S5_STATE_GROUPS
    u_refs, (kd_ref, st_ref, cp_ref, o_ref, ub_ref, acc_ref) = refs[:nu], refs[nu:]
    r = pl.program_id(1)

    @pl.when(r == 0)
    def _():
        for s in range(L):
            for k in range(nu):
                ub_ref[s, :, k * LANES:(k + 1) * LANES] = _chunk_step(u_refs[k], s).astype(BF16)

    def lags(first):
        out = None
        for s in range(first, first + S5_STEP_GROUP):
            d = jnp.dot(ub_ref[s], kd_ref[r - s + (L - 1)], preferred_element_type=F32)
            out = d if out is None else out + d
        return out

    SG, P, I = S5_STATE_GROUPS, S5_STATE, S5_GROUP
    col_group = lax.broadcasted_iota(jnp.int32, (P, SG * I), 1) // I

    def spread(c):
        pieces = [jnp.where(col_group == g, c[h * P:(h + 1) * P], jnp.zeros((P, SG * I), BF16))
                  for h in range(2) for g in range(SG)]
        return jnp.concatenate(pieces, axis=0)

    st = st_ref[...].astype(BF16)
    half = st.shape[1] // 2
    carried = jnp.concatenate(
        [jnp.dot(st[:, :half], spread(cp_ref[0]), preferred_element_type=F32),
         jnp.dot(st[:, half:], spread(cp_ref[1]), preferred_element_type=F32)], axis=1)
    acc_ref[...] = carried + lags(0)
    for first in range(S5_STEP_GROUP, L, S5_STEP_GROUP):
        @pl.when(first <= r)
        def _(first=first):
            acc_ref[...] += lags(first)
    o_ref[...] = acc_ref[...].astype(o_ref.dtype)


def _s5_glu_out_kernel(ys_ref, u_ref, d_ref, wg_ref, bg_ref, z_ref, wo_ref, h_ref, g_ref, b_ref,
                       of_ref, ob_ref, yt_ref):
    L, W = S5_CHUNK, S5_WIDTH
    nslab, tm = yt_ref.shape[0], yt_ref.shape[1]
    for s in range(L):
        for k in range(nslab):
            lo = s * W + k * LANES
            yt_ref[k, pl.ds(s, tm // L, stride=L), :] = ys_ref[:, lo:lo + LANES].astype(F32)
    half = tm // 2
    parts = [pl.ds(0, half), pl.ds(half, half)]
    ys = [_gelu_tanh(jnp.concatenate([yt_ref[k, rows, :] for k in range(nslab)], axis=1)
                     + d_ref[...] * u_ref[rows, :]) for rows in parts]
    ts = [jnp.dot(y.astype(BF16), wg_ref[...], preferred_element_type=F32) for y in ys]
    gated = [(y * _sigmoid(t + bg_ref[...]) * _silu(z_ref[rows, :])).astype(BF16)
             for y, t, rows in zip(ys, ts, parts)]
    branches = [jnp.dot(y, wo_ref[...], preferred_element_type=F32) for y in gated]
    for rows, branch in zip(parts, branches):
        out = _deepnorm_ln(h_ref[rows, :], branch, g_ref[...], b_ref[...])
        of_ref[rows, :] = out
        ob_ref[rows, :] = out.astype(BF16)


def _s5_layer(hf, hb, w_in, lam_re, lam_im, log_step, b_re, b_im, c_re, c_im, d_skip,
              w_glu, b_glu, w_out, g, b, *, B, T, layer=0):
    M, W, G, L = B * T, S5_WIDTH, S5_GROUPS, S5_CHUNK
    nch = T // L
    rows = B * nch
    u = _matmul(hb, w_in, F32, layer=layer, col0=0, ncols=W)
    z = _matmul(hb, w_in, F32, layer=layer, col0=W, ncols=W)
    kd, bp_re, bp_im, cp, a1, a2 = _s5_matrices(lam_re, lam_im, log_step, b_re, b_im, c_re, c_im)

    TG, SG = S5_TOEP_GROUPS, S5_STATE_GROUPS
    ntb, nsb = G // TG, G // SG
    tw, sw_in, sw = TG * S5_GROUP, SG * S5_GROUP, 2 * SG * S5_STATE
    nstate = nsb * sw
    s_loc = pl.pallas_call(
        _s5_local_state_kernel,
        grid=(nsb,),
        in_specs=[pl.BlockSpec((M, sw_in), lambda cb: (0, cb)),
                  pl.BlockSpec((L, None, sw_in, LANES), lambda cb: (0, cb, 0, 0)),
                  pl.BlockSpec((L, None, sw_in, LANES), lambda cb: (0, cb, 0, 0))],
        out_specs=pl.BlockSpec((rows, sw), lambda cb: (0, cb)),
        out_shape=jax.ShapeDtypeStruct((rows, nstate), F32),
        compiler_params=_cparams("parallel"),
        name="s5_chunk_state",
    )(u, bp_re, bp_im)

    scan_w = 2 * sw
    st_in = pl.pallas_call(
        functools.partial(_s5_scan_kernel, B=B, nch=nch),
        grid=(nstate // scan_w,),
        in_specs=[pl.BlockSpec((rows, scan_w), lambda p: (0, p)),
                  pl.BlockSpec((1, scan_w), lambda p: (0, p)),
                  pl.BlockSpec((1, scan_w), lambda p: (0, p))],
        out_specs=pl.BlockSpec((rows, scan_w), lambda p: (0, p)),
        out_shape=jax.ShapeDtypeStruct((rows, nstate), F32),
        compiler_params=_cparams("parallel"),
        name="s5_state_scan",
    )(s_loc, a1, a2)

    per_toep = TG // SG
    y = pl.pallas_call(
        _s5_out_kernel,
        grid=(ntb, L),
        in_specs=[pl.BlockSpec((M, sw_in), lambda cb, r, k=k: (0, per_toep * cb + k))
                  for k in range(per_toep)]
        + [pl.BlockSpec((2 * L - 1, None, tw, tw), lambda cb, r: (0, cb, 0, 0)),
                  pl.BlockSpec((rows, per_toep * sw), lambda cb, r: (0, cb)),
                  pl.BlockSpec((None, per_toep, LANES, sw_in), lambda cb, r: (r, cb, 0, 0))],
        out_specs=pl.BlockSpec((rows, tw), lambda cb, r: (0, r * ntb + cb)),
        out_shape=jax.ShapeDtypeStruct((rows, L * W), BF16),
        scratch_shapes=[pltpu.VMEM((L, rows, tw), BF16), pltpu.VMEM((rows, tw), F32)],
        compiler_params=_cparams("parallel", "arbitrary"),
        name="s5_chunk_output",
    )(*([u] * per_toep), kd, st_in, cp)

    tm = min(256, M)
    row = lambda i: (i, 0)
    fixed = lambda i: (0, 0)
    return pl.pallas_call(
        _s5_glu_out_kernel,
        grid=(M // tm,),
        in_specs=[pl.BlockSpec((tm // L, L * W), row),
                  pl.BlockSpec((tm, W), row),
                  pl.BlockSpec((1, W), fixed),
                  _resident((W, W), fixed),
                  pl.BlockSpec((1, W), fixed),
                  pl.BlockSpec((tm, W), row),
                  _resident((W, D_MODEL), fixed),
                  pl.BlockSpec((tm, D_MODEL), row),
                  pl.BlockSpec((1, D_MODEL), fixed),
                  pl.BlockSpec((1, D_MODEL), fixed)],
        out_specs=[pl.BlockSpec((tm, D_MODEL), row), pl.BlockSpec((tm, D_MODEL), row)],
        out_shape=[jax.ShapeDtypeStruct((M, D_MODEL), F32),
                   jax.ShapeDtypeStruct((M, D_MODEL), BF16)],
        scratch_shapes=[pltpu.VMEM((W // LANES, tm, LANES), F32)],
        compiler_params=_cparams("parallel"),
        name="s5_glu_out_ln",
    )(y, u, d_skip.reshape(1, W), w_glu.astype(BF16), b_glu.reshape(1, W), z,
      w_out.astype(BF16), hf, g.reshape(1, D_MODEL), b.reshape(1, D_MODEL))


def kernel(x, positions, ln_g, ln_b, ssd_w_in, ssd_conv_w, ssd_conv_b, ssd_dt_bias, ssd_a_log, ssd_d, ssd_norm_w, ssd_w_out, fox_w_in, fox_f_bias, fox_w_out, mla_w_in, mla_q_norm, mla_kv_norm, mla_w_q_up, mla_w_kv_up, mla_w_out, s5_w_in, s5_lambda_re, s5_lambda_im, s5_log_step, s5_b_re, s5_b_im, s5_c_re, s5_c_im, s5_d, s5_w_glu, s5_b_glu, s5_w_out):
    B, T, D = x.shape
    hf = x.reshape(B * T, D)
    hb = None
    for i in range(DEPTH):
        j = i // 4
        kind = i % 4
        g, b = ln_g[i], ln_b[i]
        if hb is None and kind != 0:
            hb = hf.astype(BF16)
        if kind == 0:
            hf, hb = _ssd_layer(hf, hb, ssd_w_in, ssd_conv_w[j], ssd_conv_b[j], ssd_dt_bias[j],
                                ssd_a_log[j], ssd_d[j], ssd_norm_w[j], ssd_w_out[j], g, b,
                                B=B, T=T, layer=j)
        elif kind == 1:
            hf, hb = _fox_layer(hf, hb, fox_w_in, fox_f_bias[j], fox_w_out[j], g, b,
                                B=B, T=T, layer=j)
        elif kind == 2:
            hf, hb = _mla_layer(hf, hb, positions, mla_w_in[j], mla_q_norm[j],
                                mla_kv_norm[j], mla_w_q_up[j], mla_w_kv_up[j], mla_w_out[j],
                                g, b, B=B, T=T)
        else:
            hf, hb = _s5_layer(hf, hb, s5_w_in, s5_lambda_re[j], s5_lambda_im[j], s5_log_step[j],
                               s5_b_re[j], s5_b_im[j], s5_c_re[j], s5_c_im[j], s5_d[j],
                               s5_w_glu[j], s5_b_glu[j], s5_w_out[j], g, b, B=B, T=T, layer=j)
    return hf.reshape(B, T, D)
```
